```python
import math
import jax, jax.numpy as jnp
from jax import lax
import numpy as np

D_MODEL = 2048
BATCH = 16
SEQ = 2048
DEPTH = 1

SWA_HEADS = 16
SWA_KV_HEADS = 2
SWA_HEAD_DIM = 64
WINDOW = 128
SWA_BLOCK = WINDOW
MLA_HEADS = 8
MLA_Q_RANK = 512
MLA_KV_RANK = 256
MLA_NOPE_DIM = 128
MLA_ROPE_DIM = 64
MLA_V_DIM = 128
MLA_QBLOCK = 128
ROPE_THETA = 10000.0
MAX_POS_OFFSET = 4096
N_BRANCH = 2
N_EXPERTS = 32
TOP_K = 4
D_EXPERT = D_MODEL
SWIGLU_LIMIT = 7.0
SWIGLU_ALPHA = 1.702
MOE_BLOCK = 256
EPS = 1e-6
NEG_INF = -1e30

IN_WIDTHS = (
    SWA_HEADS * SWA_HEAD_DIM,
    SWA_KV_HEADS * SWA_HEAD_DIM,
    SWA_KV_HEADS * SWA_HEAD_DIM,
    MLA_Q_RANK,
    MLA_KV_RANK,
    MLA_ROPE_DIM,
    N_BRANCH * D_MODEL,
)
IN_COLS = sum(IN_WIDTHS)

kernel_name = 'hybrid_swa_mla_gated_moe'


def rms_norm(x, g):
    xf = x.astype(jnp.float32)
    y = xf * lax.rsqrt(jnp.mean(xf * xf, axis=-1, keepdims=True) + EPS)
    return (y * g.astype(jnp.float32)).astype(x.dtype)


def rope(x, pos):
    d = x.shape[-1]
    half = d // 2
    inv_freq = jnp.power(ROPE_THETA, -jnp.arange(half, dtype=jnp.float32) * 2.0 / d)
    ang = pos.astype(jnp.float32)[..., None] * inv_freq
    cos = jnp.cos(ang)[:, :, None, :]
    sin = jnp.sin(ang)[:, :, None, :]
    xf = x.astype(jnp.float32)
    x1, x2 = xf[..., :half], xf[..., half:]
    return jnp.concatenate([x1 * cos - x2 * sin, x2 * cos + x1 * sin], axis=-1).astype(x.dtype)


def sliding_window_attention(q, k, v, sinks):
    B, S, Hq, Dh = q.shape
    Hkv = k.shape[2]
    G = Hq // Hkv
    nb = S // SWA_BLOCK
    qb = q.reshape(B, nb, SWA_BLOCK, Hkv, G, Dh)

    def band(t):
        prev = jnp.pad(t, ((0, 0), (SWA_BLOCK, 0), (0, 0), (0, 0)))[:, :S]
        return jnp.concatenate([prev.reshape(B, nb, SWA_BLOCK, Hkv, Dh),
                                t.reshape(B, nb, SWA_BLOCK, Hkv, Dh)], axis=2)

    kb, vb = band(k), band(v)
    s = jnp.einsum('bnqhgd,bnkhd->bnhgqk', qb, kb,
                   preferred_element_type=jnp.float32) * (Dh ** -0.5)
    blk = jnp.arange(nb)[:, None, None] * SWA_BLOCK
    q_abs = blk + jnp.arange(SWA_BLOCK)[None, :, None]
    k_abs = blk - SWA_BLOCK + jnp.arange(2 * SWA_BLOCK)[None, None, :]
    dist = q_abs - k_abs
    valid = (dist >= 0) & (dist < WINDOW) & (k_abs >= 0)
    s = jnp.where(valid[None, :, None, None], s, NEG_INF)
    sink = jnp.broadcast_to(
        sinks.astype(jnp.float32).reshape(Hkv, G)[None, None, :, :, None, None],
        s.shape[:-1] + (1,))
    p = jax.nn.softmax(jnp.concatenate([s, sink], axis=-1), axis=-1)[..., :-1]
    o = jnp.einsum('bnhgqk,bnkhd->bnqhgd', p.astype(v.dtype), vb)
    return o.reshape(B, S, Hq * Dh)


def mla_attention(q_nope, q_rope, k_nope, k_rope, v):
    B, S, H, _ = q_nope.shape
    dv = v.shape[-1]
    nb = S // MLA_QBLOCK
    scale = (MLA_NOPE_DIM + MLA_ROPE_DIM) ** -0.5
    k_pos = jnp.arange(S)

    def to_blocks(t):
        return jnp.moveaxis(t.reshape((B, nb, MLA_QBLOCK) + t.shape[2:]), 1, 0)

    def one_block(args):
        qn, qr, n = args
        s = (jnp.einsum('bqhd,bkhd->bhqk', qn, k_nope, preferred_element_type=jnp.float32)
             + jnp.einsum('bqhd,bkd->bhqk', qr, k_rope, preferred_element_type=jnp.float32)) * scale
        q_pos = n * MLA_QBLOCK + jnp.arange(MLA_QBLOCK)
        s = jnp.where((k_pos[None, :] <= q_pos[:, None])[None, None], s, NEG_INF)
        p = jax.nn.softmax(s, axis=-1)
        return jnp.einsum('bhqk,bkhd->bqhd', p.astype(v.dtype), v)

    o = lax.map(one_block, (to_blocks(q_nope), to_blocks(q_rope), jnp.arange(nb)))
    return jnp.moveaxis(o, 0, 1).reshape(B, S, H * dv)


def clamped_swiglu(gu):
    gate, up = gu[..., ::2], gu[..., 1::2]
    gate = jnp.minimum(gate, SWIGLU_LIMIT)
    up = jnp.clip(up, -SWIGLU_LIMIT, SWIGLU_LIMIT)
    return gate * jax.nn.sigmoid(SWIGLU_ALPHA * gate) * (up + 1.0)


def moe(xt, w_router, b_router, w_gate_up, b_gate_up, w_down, b_down):
    N, D = xt.shape
    logits = jnp.dot(xt, w_router, preferred_element_type=jnp.float32) + b_router.astype(jnp.float32)
    top_v, top_e = lax.top_k(logits, TOP_K)
    top_w = jax.nn.softmax(top_v, axis=-1)
    A = N * TOP_K
    flat_e = top_e.reshape(-1)
    flat_tok = jnp.repeat(jnp.arange(N, dtype=jnp.int32), TOP_K)
    order = jnp.argsort(flat_e, stable=True)
    se = flat_e[order]
    stok = flat_tok[order]
    counts = jnp.bincount(flat_e, length=N_EXPERTS)
    padded = (counts + MOE_BLOCK - 1) // MOE_BLOCK * MOE_BLOCK
    start = jnp.cumsum(counts) - counts
    pend = jnp.cumsum(padded)
    pstart = pend - padded
    dest = pstart[se] + jnp.arange(A) - start[se]
    n_blocks = -(-A // MOE_BLOCK) + N_EXPERTS
    P = n_blocks * MOE_BLOCK
    buf = jnp.zeros((P, D), xt.dtype).at[dest].set(xt[stok])
    block_e = jnp.minimum(
        jnp.searchsorted(pend, jnp.arange(n_blocks) * MOE_BLOCK, side='right'), N_EXPERTS - 1)

    def expert_block(args):
        xb, e = args
        gu = xb @ w_gate_up[e] + b_gate_up[e]
        return clamped_swiglu(gu) @ w_down[e] + b_down[e]

    ybuf = lax.map(expert_block, (buf.reshape(n_blocks, MOE_BLOCK, D), block_e)).reshape(P, D)
    y_assign = ybuf[dest] * top_w.reshape(-1)[order][:, None].astype(ybuf.dtype)
    return jax.ops.segment_sum(y_assign, stok, num_segments=N).astype(xt.dtype)


def setup_inputs(seed: int = 0) -> dict:
    key = jax.random.key(seed)
    ks = jax.random.split(key, 24)
    f32 = jnp.float32
    L, D = DEPTH, D_MODEL

    def nrm(k, shape, scale):
        return jax.random.normal(k, shape, f32) * scale

    def gain(k, shape):
        return 1.0 + 0.01 * jax.random.normal(k, shape, f32)

    x = nrm(ks[0], (BATCH, SEQ, D), 1.0)
    positions = (jnp.arange(SEQ, dtype=jnp.int32)[None, :]
                 + jax.random.randint(ks[1], (BATCH, 1), 0, MAX_POS_OFFSET, dtype=jnp.int32))
    return {
        'x': x,
        'positions': positions,
        'g_attn': gain(ks[2], (L, D)),
        'w_in': nrm(ks[3], (L, D, IN_COLS), D ** -0.5),
        'b_gate': nrm(ks[4], (L, N_BRANCH * D), 0.1),
        'sinks': nrm(ks[5], (L, SWA_HEADS), 1.0),
        'g_q': gain(ks[6], (L, MLA_Q_RANK)),
        'w_uq': nrm(ks[7], (L, MLA_Q_RANK, MLA_HEADS * (MLA_NOPE_DIM + MLA_ROPE_DIM)), MLA_Q_RANK ** -0.5),
        'g_kv': gain(ks[8], (L, MLA_KV_RANK)),
        'w_ukv': nrm(ks[9], (L, MLA_KV_RANK, MLA_HEADS * (MLA_NOPE_DIM + MLA_V_DIM)), MLA_KV_RANK ** -0.5),
        'w_o_swa': nrm(ks[10], (L, SWA_HEADS * SWA_HEAD_DIM, D), (SWA_HEADS * SWA_HEAD_DIM) ** -0.5),
        'w_o_mla': nrm(ks[11], (L, MLA_HEADS * MLA_V_DIM, D), (MLA_HEADS * MLA_V_DIM) ** -0.5),
        'w_out': nrm(ks[12], (L, D, D), D ** -0.5),
        'g_ffn': gain(ks[13], (L, D)),
        'w_router': nrm(ks[14], (L, D, N_EXPERTS), D ** -0.5),
        'b_router': nrm(ks[15], (L, N_EXPERTS), 0.01),
        'w_gate_up': nrm(ks[16], (L, N_EXPERTS, D, 2 * D_EXPERT), D ** -0.5),
        'b_gate_up': nrm(ks[17], (L, N_EXPERTS, 2 * D_EXPERT), 0.01),
        'w_down': nrm(ks[18], (L, N_EXPERTS, D_EXPERT, D), D_EXPERT ** -0.5),
        'b_down': nrm(ks[19], (L, N_EXPERTS, D), 0.01),
        'g_final': gain(ks[20], (D,)),
    }


def reference(x, positions, g_attn, w_in, b_gate, sinks, g_q, w_uq, g_kv, w_ukv,
              w_o_swa, w_o_mla, w_out, g_ffn, w_router, b_router, w_gate_up,
              b_gate_up, w_down, b_down, g_final):
    B, S, D = x.shape
    splits = [int(s) for s in np.cumsum(IN_WIDTHS)[:-1]]
    h = x
    for l in range(DEPTH):
        xn = rms_norm(h, g_attn[l])
        proj = xn @ w_in[l]
        q_a, k_a, v_a, dq, dkv, kr, gates = jnp.split(proj, splits, axis=-1)

        qa = rope(q_a.reshape(B, S, SWA_HEADS, SWA_HEAD_DIM), positions)
        ka = rope(k_a.reshape(B, S, SWA_KV_HEADS, SWA_HEAD_DIM), positions)
        va = v_a.reshape(B, S, SWA_KV_HEADS, SWA_HEAD_DIM)
        out_a = sliding_window_attention(qa, ka, va, sinks[l])

        c_q = rms_norm(dq, g_q[l])
        qb = (c_q @ w_uq[l]).reshape(B, S, MLA_HEADS, MLA_NOPE_DIM + MLA_ROPE_DIM)
        q_nope = qb[..., :MLA_NOPE_DIM]
        q_rope = rope(qb[..., MLA_NOPE_DIM:], positions)
        c_kv = rms_norm(dkv, g_kv[l])
        kvb = (c_kv @ w_ukv[l]).reshape(B, S, MLA_HEADS, MLA_NOPE_DIM + MLA_V_DIM)
        k_nope = kvb[..., :MLA_NOPE_DIM]
        vb = kvb[..., MLA_NOPE_DIM:]
        k_rope = rope(kr[:, :, None, :], positions)[:, :, 0, :]
        out_b = mla_attention(q_nope, q_rope, k_nope, k_rope, vb)

        g = jax.nn.sigmoid((gates + b_gate[l]).astype(jnp.float32)).reshape(B, S, N_BRANCH, D).astype(h.dtype)
        mixed = g[:, :, 0] * (out_a @ w_o_swa[l]) + g[:, :, 1] * (out_b @ w_o_mla[l])
        h = h + mixed @ w_out[l]

        hn = rms_norm(h, g_ffn[l]).reshape(B * S, D)
        h = h + moe(hn, w_router[l], b_router[l], w_gate_up[l], b_gate_up[l],
                    w_down[l], b_down[l]).reshape(B, S, D)
    return rms_norm(h, g_final)
```

```python
import functools

import jax
import jax.numpy as jnp
from jax import lax
from jax.experimental import pallas as pl
from jax.experimental.pallas import tpu as pltpu

F32 = jnp.float32
BF16 = jnp.bfloat16

D = 2048
HQ, HKV, DH, WIN = 16, 2, 64, 128
G = HQ // HKV
MH, QR, KVR, DN, DR, DV = 8, 512, 256, 128, 64, 128
THETA = 10000.0
E, TOPK, DE = 32, 4, 2048
LIMIT, ALPHA = 7.0, 1.702
EPS = 1e-6
NEG = -1e30

LANES = 128
ROW_TILES = D // LANES
VMEM_LIMIT = 56 * 1024 * 1024

PROJ_COLS = 6400
COL_GATES, COL_Q, COL_DQ, COL_DKV, COL_K, COL_V, COL_KR = 0, 4096, 5120, 5632, 5888, 6016, 6144

NT = (((1,), (1,)), ((), ()))


def _params(sem, vmem=VMEM_LIMIT):
    return pltpu.CompilerParams(dimension_semantics=sem, vmem_limit_bytes=vmem)


def _rms(x, g):
    return x * lax.rsqrt(jnp.mean(x * x, axis=-1, keepdims=True) + EPS) * g


def _rope128(v, cos, sin):
    lane = lax.broadcasted_iota(jnp.int32, v.shape, 1)
    rot = jnp.where((lane % DH) < (DH // 2), -pltpu.roll(v, LANES - DH // 2, 1), pltpu.roll(v, DH // 2, 1))
    return v * cos + rot * sin


def _rope_table_kernel(pos_ref, inv_ref, cos_ref, sin_ref):
    ang = pos_ref[...].astype(F32) * inv_ref[...]
    cos_ref[...] = jnp.cos(ang)
    sin_ref[...] = jnp.sin(ang)


def _rope_tables(positions, n):
    half = DH // 2
    per_row = LANES // half
    inv = jnp.power(THETA, -jnp.arange(half, dtype=F32) * 2.0 / DH)
    inv_row = jnp.tile(inv, per_row)[None, :]
    pos_rows = jnp.repeat(positions.reshape(n // per_row, per_row), half, axis=1)
    rows = n // per_row
    tr = min(rows, 1024)
    cos, sin = pl.pallas_call(
        _rope_table_kernel,
        grid=(rows // tr,),
        in_specs=[pl.BlockSpec((tr, LANES), lambda i: (i, 0)), pl.BlockSpec((1, LANES), lambda i: (0, 0))],
        out_specs=[pl.BlockSpec((tr, LANES), lambda i: (i, 0))] * 2,
        out_shape=[jax.ShapeDtypeStruct((rows, LANES), F32)] * 2,
        name="rope_tables",
    )(pos_rows, inv_row)
    cos = jnp.tile(cos.reshape(n, half), (1, per_row))
    sin = jnp.tile(sin.reshape(n, half), (1, per_row))
    return cos, sin


def _in_proj_kernel(x_ref, g_ref, w_ref, o_ref, xn_ref):
    @pl.when(pl.program_id(1) == 0)
    def _():
        xn_ref[...] = _rms(x_ref[...], g_ref[...]).astype(BF16)

    o_ref[...] = jnp.dot(xn_ref[...], w_ref[...], preferred_element_type=F32).astype(o_ref.dtype)


def _in_proj(x2, g_attn, w_in_p, n):
    tm = min(n, 1024)
    tn = 1280
    return pl.pallas_call(
        _in_proj_kernel,
        grid=(n // tm, PROJ_COLS // tn),
        in_specs=[
            pl.BlockSpec((tm, D), lambda i, j: (i, 0)),
            pl.BlockSpec((1, D), lambda i, j: (0, 0)),
            pl.BlockSpec((D, tn), lambda i, j: (0, j)),
        ],
        out_specs=pl.BlockSpec((tm, tn), lambda i, j: (i, j)),
        out_shape=jax.ShapeDtypeStruct((n, PROJ_COLS), BF16),
        scratch_shapes=[pltpu.VMEM((tm, D), BF16)],
        compiler_params=_params(("arbitrary", "arbitrary")),
        name="in_proj",
    )(x2, g_attn, w_in_p)


def _prep_kernel(q_ref, dq_ref, dkv_ref, k_ref, kr_ref, cos_ref, sin_ref, gq_ref, gkv_ref,
                 wq_ref, wk_ref, wv_ref, qa_ref, ka_ref, qm_ref, km_ref, vm_ref):
    cos = cos_ref[...]
    sin = sin_ref[...]
    swa_scale = DH ** -0.5
    for c in range(HQ * DH // LANES):
        sl = slice(c * LANES, (c + 1) * LANES)
        qa_ref[:, sl] = (_rope128(q_ref[:, sl].astype(F32), cos, sin) * swa_scale).astype(BF16)
    ka_ref[...] = _rope128(k_ref[...].astype(F32), cos, sin).astype(BF16)

    mla_scale = (DN + DR) ** -0.5
    cq = _rms(dq_ref[...].astype(F32), gq_ref[...]).astype(BF16)
    qb = jnp.dot(cq, wq_ref[...], preferred_element_type=F32)
    ckv = _rms(dkv_ref[...].astype(F32), gkv_ref[...]).astype(BF16)
    kn = jnp.dot(ckv, wk_ref[...], preferred_element_type=F32)
    vm_ref[...] = jnp.dot(ckv, wv_ref[...], preferred_element_type=F32).astype(BF16)
    kr = _rope128(kr_ref[...].astype(F32), cos, sin).astype(BF16)
    for h in range(MH):
        lo = h * 2 * LANES
        qm_ref[:, lo:lo + LANES] = (qb[:, lo:lo + LANES] * mla_scale).astype(BF16)
        qm_ref[:, lo + LANES:lo + 2 * LANES] = (
            _rope128(qb[:, lo + LANES:lo + 2 * LANES], cos, sin) * mla_scale).astype(BF16)
        km_ref[:, lo:lo + LANES] = kn[:, h * LANES:(h + 1) * LANES].astype(BF16)
        km_ref[:, lo + LANES:lo + 2 * LANES] = kr


def _prep(proj, cos, sin, g_q, g_kv, wq_p, wk_p, wv_p, n):
    tm = min(n, 512)
    row = lambda w, cb: pl.BlockSpec((tm, w), lambda i: (i, cb))
    full = lambda a: pl.BlockSpec(a.shape, lambda i: (0, 0))
    return pl.pallas_call(
        _prep_kernel,
        grid=(n // tm,),
        in_specs=[
            row(HQ * DH, COL_Q // (HQ * DH)), row(QR, COL_DQ // QR), row(KVR, COL_DKV // KVR),
            row(LANES, COL_K // LANES), row(LANES, COL_KR // LANES),
            row(LANES, 0), row(LANES, 0), full(g_q), full(g_kv), full(wq_p), full(wk_p), full(wv_p),
        ],
        out_specs=[row(HQ * DH, 0), row(LANES, 0), row(MH * 2 * LANES, 0), row(MH * 2 * LANES, 0),
                   row(MH * DV, 0)],
        out_shape=[
            jax.ShapeDtypeStruct((n, HQ * DH), BF16), jax.ShapeDtypeStruct((n, LANES), BF16),
            jax.ShapeDtypeStruct((n, MH * 2 * LANES), BF16), jax.ShapeDtypeStruct((n, MH * 2 * LANES), BF16),
            jax.ShapeDtypeStruct((n, MH * DV), BF16),
        ],
        compiler_params=_params(("arbitrary",)),
        name="rope_mla_prep",
    )(proj, proj, proj, proj, proj, cos, sin, g_q, g_kv, wq_p, wk_p, wv_p)


def _swa_kernel(sinks_ref, q_ref, kc_ref, kp_ref, vc_ref, vp_ref, o_ref):
    blk = pl.program_id(1)
    q2 = jnp.concatenate([q_ref[:, g * LANES:(g + 1) * LANES] for g in range(G)], axis=0)
    k2 = jnp.concatenate([kp_ref[...], kc_ref[...]], axis=0)
    v2 = jnp.concatenate([vp_ref[...], vc_ref[...]], axis=0)
    lane = lax.broadcasted_iota(jnp.int32, (2 * WIN, LANES), 1)
    qi = lax.broadcasted_iota(jnp.int32, (WIN, 2 * WIN), 0)
    kj = lax.broadcasted_iota(jnp.int32, (WIN, 2 * WIN), 1)
    valid = (kj > qi) & (kj <= qi + WIN) & ((kj >= WIN) | (blk > 0))
    acc = jnp.zeros((G * WIN, LANES), F32)
    for h in range(HKV):
        head = (lane >= h * DH) & (lane < (h + 1) * DH)
        kz = jnp.where(head, k2, jnp.zeros_like(k2))
        vz = jnp.where(head, v2, jnp.zeros_like(v2))
        s = lax.dot_general(q2, kz, NT, preferred_element_type=F32)
        ps, invs = [], []
        for g in range(G):
            sg = jnp.where(valid, s[g * WIN:(g + 1) * WIN], NEG)
            sink = sinks_ref[h * G + g]
            m = jnp.maximum(jnp.max(sg, axis=-1, keepdims=True), sink)
            p = jnp.exp(sg - m)
            l = jnp.sum(p, axis=-1, keepdims=True) + jnp.exp(sink - m)
            ps.append(p.astype(BF16))
            invs.append(1.0 / l)
        o_h = jnp.dot(jnp.concatenate(ps, axis=0), vz, preferred_element_type=F32)
        acc = acc + o_h * jnp.concatenate(invs, axis=0)
    for g in range(G):
        o_ref[:, g * LANES:(g + 1) * LANES] = acc[g * WIN:(g + 1) * WIN].astype(BF16)


def _swa(sinks, qa, ka, proj, b, s):
    nb = s // WIN
    cur = lambda cb: pl.BlockSpec((WIN, LANES), lambda bi, n, sk: (bi * nb + n, cb))
    prev = lambda cb: pl.BlockSpec((WIN, LANES), lambda bi, n, sk: (bi * nb + jnp.maximum(n - 1, 0), cb))
    grid_spec = pltpu.PrefetchScalarGridSpec(
        num_scalar_prefetch=1,
        grid=(b, nb),
        in_specs=[
            pl.BlockSpec((WIN, HQ * DH), lambda bi, n, sk: (bi * nb + n, 0)),
            cur(0), prev(0), cur(COL_V // LANES), prev(COL_V // LANES),
        ],
        out_specs=pl.BlockSpec((WIN, HQ * DH), lambda bi, n, sk: (bi * nb + n, 0)),
    )
    return pl.pallas_call(
        _swa_kernel,
        grid_spec=grid_spec,
        out_shape=jax.ShapeDtypeStruct((b * s, HQ * DH), BF16),
        compiler_params=_params(("arbitrary", "arbitrary")),
        name="swa_attention",
    )(sinks, qa, ka, ka, proj, proj)


def _mla_kernel(q_ref, k_ref, v_ref, o_ref, *, seq, tq):
    for i in range(seq // tq):
        kv = (i + 1) * tq
        q = q_ref[i * tq:(i + 1) * tq, :]
        s = lax.dot_general(q, k_ref[0:kv, :], NT, preferred_element_type=F32)
        row = lax.broadcasted_iota(jnp.int32, (tq, kv), 0) + i * tq
        col = lax.broadcasted_iota(jnp.int32, (tq, kv), 1)
        s = jnp.where(col <= row, s, NEG)
        m = jnp.max(s, axis=-1, keepdims=True)
        p = jnp.exp(s - m)
        l = jnp.sum(p, axis=-1, keepdims=True)
        o = jnp.dot(p.astype(BF16), v_ref[0:kv, :], preferred_element_type=F32) * (1.0 / l)
        o_ref[i * tq:(i + 1) * tq, :] = o.astype(BF16)


def _mla(qm, km, vm, b, s):
    tq = min(s, 256)
    return pl.pallas_call(
        functools.partial(_mla_kernel, seq=s, tq=tq),
        grid=(b, MH),
        in_specs=[
            pl.BlockSpec((s, 2 * LANES), lambda bi, h: (bi, h)),
            pl.BlockSpec((s, 2 * LANES), lambda bi, h: (bi, h)),
            pl.BlockSpec((s, DV), lambda bi, h: (bi, h)),
        ],
        out_specs=pl.BlockSpec((s, DV), lambda bi, h: (bi, h)),
        out_shape=jax.ShapeDtypeStruct((b * s, MH * DV), BF16),
        compiler_params=_params(("arbitrary", "arbitrary")),
        name="mla_attention",
    )(qm, km, vm)


def _merge_kernel(oa_ref, ob_ref, ga_ref, gb_ref, x_ref, wa_ref, wb_ref, wo_ref, bg_ref, gf_ref,
                  wr_ref, br_ref, h_ref, hn3_ref, te_ref, tw_ref, rk_ref, cnt_ref, base_ref, *, tm):
    @pl.when(pl.program_id(0) == 0)
    def _():
        base_ref[...] = jnp.zeros_like(base_ref)

    ya = jnp.dot(oa_ref[...], wa_ref[...], preferred_element_type=F32)
    yb = jnp.dot(ob_ref[...], wb_ref[...], preferred_element_type=F32)
    ga = jax.nn.sigmoid(ga_ref[...].astype(F32) + bg_ref[:, 0:D])
    gb = jax.nn.sigmoid(gb_ref[...].astype(F32) + bg_ref[:, D:2 * D])
    mixed = (ga * ya + gb * yb).astype(BF16)
    h = x_ref[...] + jnp.dot(mixed, wo_ref[...], preferred_element_type=F32)
    h_ref[...] = h
    hn = _rms(h, gf_ref[...])
    for c in range(ROW_TILES):
        hn3_ref[pl.ds(c, tm, stride=ROW_TILES), :] = hn[:, c * LANES:(c + 1) * LANES]

    logits = lax.dot_general(wr_ref[...], hn, NT, preferred_element_type=F32,
                             precision=lax.Precision.HIGHEST) + br_ref[...]
    eidx = lax.broadcasted_iota(jnp.int32, (E, tm), 0)
    vals, idxs = [], []
    l = logits
    for _ in range(TOPK):
        m = jnp.max(l, axis=0, keepdims=True)
        idx = jnp.min(jnp.where(l == m, eidx, E), axis=0, keepdims=True)
        vals.append(m)
        idxs.append(idx)
        l = jnp.where(eidx == idx, -jnp.inf, l)
    ex = [jnp.exp(v - vals[0]) for v in vals]
    tot = ex[0] + ex[1] + ex[2] + ex[3]
    tw_ref[...] = jnp.concatenate([e / tot for e in ex], axis=0)
    te_ref[...] = jnp.concatenate(idxs, axis=0)

    onehot = jnp.zeros((E, tm), F32)
    for idx in idxs:
        onehot = onehot + jnp.where(eidx == idx, 1.0, 0.0)
    r = lax.broadcasted_iota(jnp.int32, (tm, tm), 0)
    c = lax.broadcasted_iota(jnp.int32, (tm, tm), 1)
    earlier = jnp.where(r < c, 1.0, 0.0).astype(BF16)
    before = jnp.dot(onehot.astype(BF16), earlier, preferred_element_type=F32) + base_ref[:, 0:1]
    rk_ref[...] = jnp.concatenate(
        [jnp.sum(jnp.where(eidx == idx, before, 0.0), axis=0, keepdims=True) for idx in idxs],
        axis=0).astype(jnp.int32)
    base_ref[...] = base_ref[...] + jnp.sum(onehot, axis=1, keepdims=True)
    cnt_ref[...] = base_ref[...]


def _merge(out_a, out_b, proj, x2, wa, wb, wo, b_gate, g_ffn, wr_t, b_router, n):
    tm = min(n, 256)
    row = lambda w, cb: pl.BlockSpec((tm, w), lambda i: (i, cb))
    full = lambda a: pl.BlockSpec(a.shape, lambda i: (0, 0), pipeline_mode=pl.Buffered(1))
    tok = pl.BlockSpec((TOPK, tm), lambda i: (0, i))
    return pl.pallas_call(
        functools.partial(_merge_kernel, tm=tm),
        grid=(n // tm,),
        in_specs=[
            row(HQ * DH, 0), row(MH * DV, 0), row(D, 0), row(D, 1), row(D, 0),
            full(wa), full(wb), full(wo), full(b_gate), full(g_ffn), full(wr_t), full(b_router),
        ],
        out_specs=[
            row(D, 0), pl.BlockSpec((tm * ROW_TILES, LANES), lambda i: (i, 0)), tok, tok, tok,
            pl.BlockSpec((E, LANES), lambda i: (0, 0)),
        ],
        out_shape=[
            jax.ShapeDtypeStruct((n, D), F32), jax.ShapeDtypeStruct((n * ROW_TILES, LANES), F32),
            jax.ShapeDtypeStruct((TOPK, n), jnp.int32), jax.ShapeDtypeStruct((TOPK, n), F32),
            jax.ShapeDtypeStruct((TOPK, n), jnp.int32), jax.ShapeDtypeStruct((E, LANES), F32),
        ],
        scratch_shapes=[pltpu.VMEM((E, LANES), F32)],
        compiler_params=_params(("arbitrary",)),
        name="merge_outproj_router",
    )(out_a, out_b, proj, proj, x2, wa, wb, wo, b_gate, g_ffn, wr_t, b_router)


def _dest_kernel(pstart_ref, te_ref, rk_ref, dest_ref):
    te = te_ref[...]
    d = rk_ref[...]
    for e in range(E):
        d = d + jnp.where(te == e, pstart_ref[e], 0)
    dest_ref[...] = d


def _dest(pstart, top_e, rank, n):
    tn = min(n, 8192)
    grid_spec = pltpu.PrefetchScalarGridSpec(
        num_scalar_prefetch=1, grid=(n // tn,),
        in_specs=[pl.BlockSpec((TOPK, tn), lambda i, ps: (0, i))] * 2,
        out_specs=pl.BlockSpec((TOPK, tn), lambda i, ps: (0, i)))
    return pl.pallas_call(
        _dest_kernel, grid_spec=grid_spec,
        out_shape=jax.ShapeDtypeStruct((TOPK, n), jnp.int32), name="dest_rows",
    )(pstart, top_e, rank)


def _dispatch_kernel(cnt_ref, pad_ref, pst_ref, dest_ref, hn3_ref, xg_ref, zrow_ref, sem, zsem, *, tt):
    i = pl.program_id(0)

    def row_copy(src_row, dst_row):
        return pltpu.make_async_copy(
            hn3_ref.at[pl.ds(pl.multiple_of(src_row * ROW_TILES, ROW_TILES), ROW_TILES)],
            xg_ref.at[pl.ds(pl.multiple_of(dst_row * ROW_TILES, ROW_TILES), ROW_TILES)], sem)

    def zero_copy(dst_row):
        return pltpu.make_async_copy(
            zrow_ref, xg_ref.at[pl.ds(pl.multiple_of(dst_row * ROW_TILES, ROW_TILES), ROW_TILES)], zsem)

    @pl.when(i == 0)
    def _():
        zrow_ref[...] = jnp.zeros_like(zrow_ref)

        def per_expert(e, carry):
            base = pst_ref[e]

            def start(r, c):
                zero_copy(base + r).start()
                return c

            def wait(r, c):
                zero_copy(base + r).wait()
                return c

            lax.fori_loop(cnt_ref[e], pad_ref[e], start, 0)
            lax.fori_loop(cnt_ref[e], pad_ref[e], wait, 0)
            return carry

        lax.fori_loop(0, E, per_expert, 0)

    def start(t, c):
        for k in range(TOPK):
            row_copy(i * tt + t, dest_ref[k, t]).start()
        return c

    def wait(t, c):
        for k in range(TOPK):
            row_copy(i * tt + t, dest_ref[k, t]).wait()
        return c

    lax.fori_loop(0, tt, start, 0)
    lax.fori_loop(0, tt, wait, 0)


def _dispatch(counts, padded, pstart, dest, hn3, n, p_rows):
    tt = min(n, 512)
    grid_spec = pltpu.PrefetchScalarGridSpec(
        num_scalar_prefetch=3, grid=(n // tt,),
        in_specs=[
            pl.BlockSpec((TOPK, tt), lambda i, a, b, c: (0, i), memory_space=pltpu.SMEM),
            pl.BlockSpec(memory_space=pl.ANY),
        ],
        out_specs=pl.BlockSpec(memory_space=pl.ANY),
        scratch_shapes=[pltpu.VMEM((ROW_TILES, LANES), F32), pltpu.SemaphoreType.DMA(()),
                        pltpu.SemaphoreType.DMA(())],
    )
    return pl.pallas_call(
        functools.partial(_dispatch_kernel, tt=tt), grid_spec=grid_spec,
        out_shape=jax.ShapeDtypeStruct((p_rows * ROW_TILES, LANES), F32),
        compiler_params=pltpu.CompilerParams(dimension_semantics=("arbitrary",), has_side_effects=True),
        name="dispatch_rows",
    )(counts, padded, pstart, dest, hn3)


def _moe_kernel(be_ref, nu_ref, x3_ref, wg_ref, wu_ref, wd_ref, bg_ref, bu_ref, bd_ref, y3_ref,
                x2_ref, acc_ref, *, bm, nj):
    blk = pl.program_id(0)
    j = pl.program_id(1)

    @pl.when(blk < nu_ref[0])
    def _():
        @pl.when(j == 0)
        def _():
            for c in range(ROW_TILES):
                x2_ref[:, c * LANES:(c + 1) * LANES] = x3_ref[pl.ds(c, bm, stride=ROW_TILES), :].astype(BF16)

        x = x2_ref[...]
        gate = jnp.dot(x, wg_ref[...], preferred_element_type=F32) + bg_ref[...]
        up = jnp.dot(x, wu_ref[...], preferred_element_type=F32) + bu_ref[...]
        gate = jnp.minimum(gate, LIMIT)
        up = jnp.clip(up, -LIMIT, LIMIT)
        act = (gate * jax.nn.sigmoid(ALPHA * gate) * (up + 1.0)).astype(BF16)
        part = jnp.dot(act, wd_ref[...], preferred_element_type=F32)

        @pl.when(j == 0)
        def _():
            acc_ref[...] = part + bd_ref[...]

        @pl.when(j > 0)
        def _():
            acc_ref[...] += part

        @pl.when(j == nj - 1)
        def _():
            for c in range(ROW_TILES):
                y3_ref[pl.ds(c, bm, stride=ROW_TILES), :] = acc_ref[:, c * LANES:(c + 1) * LANES]


def _moe(block_e, n_used, xg, wg, wu, wd, bg, bu, bd, n_blocks, bm):
    th = 512
    nj = DE // th

    def wcol(blk, j, be, nu):
        return (be[blk], 0, jnp.where(blk < nu[0], j, nj - 1))

    def wrow(blk, j, be, nu):
        return (be[blk], jnp.where(blk < nu[0], j, nj - 1), 0)

    def xrow(blk, j, be, nu):
        return (jnp.minimum(blk, nu[0] - 1), 0)

    grid_spec = pltpu.PrefetchScalarGridSpec(
        num_scalar_prefetch=2, grid=(n_blocks, nj),
        in_specs=[
            pl.BlockSpec((bm * ROW_TILES, LANES), xrow),
            pl.BlockSpec((None, D, th), wcol), pl.BlockSpec((None, D, th), wcol),
            pl.BlockSpec((None, th, D), wrow),
            pl.BlockSpec((None, 1, th), wcol), pl.BlockSpec((None, 1, th), wcol),
            pl.BlockSpec((None, 1, D), lambda blk, j, be, nu: (be[blk], 0, 0)),
        ],
        out_specs=pl.BlockSpec((bm * ROW_TILES, LANES), xrow),
        scratch_shapes=[pltpu.VMEM((bm, D), BF16), pltpu.VMEM((bm, D), F32)],
    )
    return pl.pallas_call(
        functools.partial(_moe_kernel, bm=bm, nj=nj), grid_spec=grid_spec,
        out_shape=jax.ShapeDtypeStruct((n_blocks * bm * ROW_TILES, LANES), F32),
        compiler_params=_params(("arbitrary", "arbitrary")),
        name="moe_experts",
    )(block_e, n_used, xg, wg, wu, wd, bg, bu, bd)


def _combine_kernel(dcur_ref, dnxt_ref, y3_ref, h_ref, tw_ref, gf_ref, o_ref, buf_ref, sem, *, tt, nt):
    i = pl.program_id(0)
    slot = i % 2

    def row_copy(d_ref, k, t, s):
        return pltpu.make_async_copy(
            y3_ref.at[pl.ds(pl.multiple_of(d_ref[k, t] * ROW_TILES, ROW_TILES), ROW_TILES)],
            buf_ref.at[s, pl.ds((k * tt + t) * ROW_TILES, ROW_TILES)], sem.at[s])

    def issue(d_ref, s):
        def body(t, c):
            for k in range(TOPK):
                row_copy(d_ref, k, t, s).start()
            return c
        lax.fori_loop(0, tt, body, 0)

    @pl.when(i == 0)
    def _():
        issue(dcur_ref, 0)

    @pl.when(i + 1 < nt)
    def _():
        issue(dnxt_ref, 1 - slot)

    def wait(t, c):
        for k in range(TOPK):
            row_copy(dcur_ref, k, t, slot).wait()
        return c

    lax.fori_loop(0, tt, wait, 0)

    tw = tw_ref[...]
    for c in range(ROW_TILES):
        acc = h_ref[:, c * LANES:(c + 1) * LANES]
        for k in range(TOPK):
            rows = buf_ref[slot, pl.ds(k * tt * ROW_TILES + c, tt, stride=ROW_TILES), :]
            acc = acc + rows * tw[:, k:k + 1]
        o_ref[:, c * LANES:(c + 1) * LANES] = acc
    hf = o_ref[...]
    o_ref[...] = _rms(hf, gf_ref[...])


def _combine(dest, y3, h, tw_t, g_final, n):
    tt = min(n, 256)
    nt = n // tt
    grid_spec = pltpu.PrefetchScalarGridSpec(
        num_scalar_prefetch=0, grid=(nt,),
        in_specs=[
            pl.BlockSpec((TOPK, tt), lambda i: (0, i), memory_space=pltpu.SMEM),
            pl.BlockSpec((TOPK, tt), lambda i: (0, jnp.minimum(i + 1, nt - 1)), memory_space=pltpu.SMEM),
            pl.BlockSpec(memory_space=pl.ANY),
            pl.BlockSpec((tt, D), lambda i: (i, 0)),
            pl.BlockSpec((tt, TOPK), lambda i: (i, 0)),
            pl.BlockSpec((1, D), lambda i: (0, 0)),
        ],
        out_specs=pl.BlockSpec((tt, D), lambda i: (i, 0)),
        scratch_shapes=[pltpu.VMEM((2, TOPK * tt * ROW_TILES, LANES), F32), pltpu.SemaphoreType.DMA((2,))],
    )
    return pl.pallas_call(
        functools.partial(_combine_kernel, tt=tt, nt=nt), grid_spec=grid_spec,
        out_shape=jax.ShapeDtypeStruct((n, D), F32),
        compiler_params=_params(("arbitrary",)),
        name="combine_final_norm",
    )(dest, dest, y3, h, tw_t, g_final)


def _permute_weights(w_in, w_uq, w_ukv, w_o_swa):
    widths = (HQ * DH, HKV * DH, HKV * DH, QR, KVR, DR, 2 * D)
    offs = [0]
    for w in widths:
        offs.append(offs[-1] + w)
    wq, wk, wv, wdq, wdkv, wkr, wgates = [w_in[:, offs[i]:offs[i + 1]] for i in range(7)]
    wq = wq.reshape(D, HKV, G, DH).transpose(0, 2, 1, 3).reshape(D, HQ * DH)
    pad = jnp.zeros((D, PROJ_COLS - COL_KR - DR), w_in.dtype)
    w_in_p = jnp.concatenate([wgates, wq, wdq, wdkv, wk, wv, wkr, pad], axis=1).astype(BF16)
    wo_a = w_o_swa.reshape(HKV, G, DH, D).transpose(1, 0, 2, 3).reshape(HQ * DH, D).astype(BF16)
    wuq = w_uq.reshape(QR, MH, DN + DR)
    wq_p = jnp.concatenate([wuq, jnp.zeros((QR, MH, 2 * LANES - DN - DR), w_uq.dtype)], axis=2)
    wq_p = wq_p.reshape(QR, MH * 2 * LANES).astype(BF16)
    wukv = w_ukv.reshape(KVR, MH, DN + DV)
    wk_p = wukv[:, :, :DN].reshape(KVR, MH * DN).astype(BF16)
    wv_p = wukv[:, :, DN:].reshape(KVR, MH * DV).astype(BF16)
    return w_in_p, wo_a, wq_p, wk_p, wv_p


def kernel(x, positions, g_attn, w_in, b_gate, sinks, g_q, w_uq, g_kv, w_ukv, w_o_swa, w_o_mla, w_out,
           g_ffn, w_router, b_router, w_gate_up, b_gate_up, w_down, b_down, g_final):
    b, s, _ = x.shape
    n = b * s
    assert w_in.shape[0] == 1, "single-layer block"
    x2 = x.reshape(n, D)

    w_in_p, wo_a, wq_p, wk_p, wv_p = _permute_weights(w_in[0], w_uq[0], w_ukv[0], w_o_swa[0])
    cos, sin = _rope_tables(positions, n)
    proj = _in_proj(x2, g_attn, w_in_p, n)
    qa, ka, qm, km, vm = _prep(proj, cos, sin, g_q, g_kv, wq_p, wk_p, wv_p, n)
    out_a = _swa(sinks[0], qa, ka, proj, b, s)
    out_b = _mla(qm, km, vm, b, s)

    h, hn3, top_e, top_w, rank, cnt = _merge(
        out_a, out_b, proj, x2, wo_a, w_o_mla[0].astype(BF16), w_out[0].astype(BF16), b_gate, g_ffn,
        w_router[0].T, b_router[0][:, None], n)

    bm = 512
    n_blocks = -(-(n * TOPK) // bm) + E
    counts = cnt[:, 0].astype(jnp.int32)
    padded = (counts + bm - 1) // bm * bm
    pend = jnp.cumsum(padded)
    pstart = pend - padded
    n_used = pend[-1] // bm
    blocks = jnp.arange(n_blocks, dtype=jnp.int32)
    block_e = jnp.minimum(jnp.searchsorted(pend, blocks * bm, side="right"), E - 1).astype(jnp.int32)
    block_e = jnp.where(blocks < n_used, block_e, block_e[jnp.maximum(n_used - 1, 0)])

    dest = _dest(pstart, top_e, rank, n)
    xg = _dispatch(counts, padded, pstart, dest, hn3, n, n_blocks * bm)

    wgu = w_gate_up[0]
    wg = wgu[:, :, 0::2].astype(BF16)
    wu = wgu[:, :, 1::2].astype(BF16)
    bgu = b_gate_up[0]
    y3 = _moe(block_e, n_used.reshape(1).astype(jnp.int32), xg, wg, wu, w_down[0].astype(BF16),
              bgu[:, None, 0::2], bgu[:, None, 1::2], b_down[0][:, None, :], n_blocks, bm)

    out = _combine(dest, y3, h, top_w.T, g_final[None, :], n)
    return out.reshape(b, s, D)
```

```python
import functools

import jax
import jax.numpy as jnp
from jax import lax
from jax.experimental import pallas as pl
from jax.experimental.pallas import tpu as pltpu

F32 = jnp.float32
BF16 = jnp.bfloat16

D = 2048
HQ, HKV, DH, WIN = 16, 2, 64, 128
G = HQ // HKV
MH, QR, KVR, DN, DR, DV = 8, 512, 256, 128, 64, 128
THETA = 10000.0
E, TOPK, DE = 32, 4, 2048
LIMIT, ALPHA = 7.0, 1.702
EPS = 1e-6
NEG = -1e30

LANES = 128
ROW_TILES = D // LANES
VMEM_LIMIT = 56 * 1024 * 1024

PROJ_COLS = 6400
COL_GATES, COL_Q, COL_DQ, COL_DKV, COL_K, COL_V, COL_KR = 0, 4096, 5120, 5632, 5888, 6016, 6144

NT = (((1,), (1,)), ((), ()))


def _params(sem, vmem=VMEM_LIMIT):
    return pltpu.CompilerParams(dimension_semantics=sem, vmem_limit_bytes=vmem)


def _rms(x, g):
    return x * lax.rsqrt(jnp.mean(x * x, axis=-1, keepdims=True) + EPS) * g


def _rope128(v, cos, sin):
    lane = lax.broadcasted_iota(jnp.int32, v.shape, 1)
    rot = jnp.where((lane % DH) < (DH // 2), -pltpu.roll(v, LANES - DH // 2, 1), pltpu.roll(v, DH // 2, 1))
    return v * cos + rot * sin


def _rope_table_kernel(pos_ref, inv_ref, cos_ref, sin_ref):
    ang = pos_ref[...].astype(F32) * inv_ref[...]
    cos_ref[...] = jnp.cos(ang)
    sin_ref[...] = jnp.sin(ang)


def _rope_tables(positions, n):
    half = DH // 2
    per_row = LANES // half
    inv = jnp.power(THETA, -jnp.arange(half, dtype=F32) * 2.0 / DH)
    inv_row = jnp.tile(inv, per_row)[None, :]
    pos_rows = jnp.repeat(positions.reshape(n // per_row, per_row), half, axis=1)
    rows = n // per_row
    tr = min(rows, 1024)
    cos, sin = pl.pallas_call(
        _rope_table_kernel,
        grid=(rows // tr,),
        in_specs=[pl.BlockSpec((tr, LANES), lambda i: (i, 0)), pl.BlockSpec((1, LANES), lambda i: (0, 0))],
        out_specs=[pl.BlockSpec((tr, LANES), lambda i: (i, 0))] * 2,
        out_shape=[jax.ShapeDtypeStruct((rows, LANES), F32)] * 2,
        name="rope_tables",
    )(pos_rows, inv_row)
    cos = jnp.tile(cos.reshape(n, half), (1, per_row))
    sin = jnp.tile(sin.reshape(n, half), (1, per_row))
    return cos, sin


def _in_proj_kernel(x_ref, g_ref, w_ref, o_ref, xn_ref):
    @pl.when(pl.program_id(1) == 0)
    def _():
        xn_ref[...] = _rms(x_ref[...], g_ref[...]).astype(BF16)

    o_ref[...] = jnp.dot(xn_ref[...], w_ref[...], preferred_element_type=F32).astype(o_ref.dtype)


def _in_proj(x2, g_attn, w_in_p, n):
    tm = min(n, 1024)
    tn = 1280
    return pl.pallas_call(
        _in_proj_kernel,
        grid=(n // tm, PROJ_COLS // tn),
        in_specs=[
            pl.BlockSpec((tm, D), lambda i, j: (i, 0)),
            pl.BlockSpec((1, D), lambda i, j: (0, 0)),
            pl.BlockSpec((D, tn), lambda i, j: (0, j)),
        ],
        out_specs=pl.BlockSpec((tm, tn), lambda i, j: (i, j)),
        out_shape=jax.ShapeDtypeStruct((n, PROJ_COLS), BF16),
        scratch_shapes=[pltpu.VMEM((tm, D), BF16)],
        compiler_params=_params(("arbitrary", "arbitrary")),
        name="in_proj",
    )(x2, g_attn, w_in_p)


def _prep_kernel(q_ref, dq_ref, dkv_ref, k_ref, kr_ref, cos_ref, sin_ref, gq_ref, gkv_ref,
                 wq_ref, wk_ref, wv_ref, qa_ref, ka_ref, qm_ref, km_ref, vm_ref):
    cos = cos_ref[...]
    sin = sin_ref[...]
    swa_scale = DH ** -0.5
    for c in range(HQ * DH // LANES):
        sl = slice(c * LANES, (c + 1) * LANES)
        qa_ref[:, sl] = (_rope128(q_ref[:, sl].astype(F32), cos, sin) * swa_scale).astype(BF16)
    ka_ref[...] = _rope128(k_ref[...].astype(F32), cos, sin).astype(BF16)

    mla_scale = (DN + DR) ** -0.5
    cq = _rms(dq_ref[...].astype(F32), gq_ref[...]).astype(BF16)
    qb = jnp.dot(cq, wq_ref[...], preferred_element_type=F32)
    ckv = _rms(dkv_ref[...].astype(F32), gkv_ref[...]).astype(BF16)
    kn = jnp.dot(ckv, wk_ref[...], preferred_element_type=F32)
    vm_ref[...] = jnp.dot(ckv, wv_ref[...], preferred_element_type=F32).astype(BF16)
    kr = _rope128(kr_ref[...].astype(F32), cos, sin).astype(BF16)
    for h in range(MH):
        lo = h * 2 * LANES
        qm_ref[:, lo:lo + LANES] = (qb[:, lo:lo + LANES] * mla_scale).astype(BF16)
        qm_ref[:, lo + LANES:lo + 2 * LANES] = (
            _rope128(qb[:, lo + LANES:lo + 2 * LANES], cos, sin) * mla_scale).astype(BF16)
        km_ref[:, lo:lo + LANES] = kn[:, h * LANES:(h + 1) * LANES].astype(BF16)
        km_ref[:, lo + LANES:lo + 2 * LANES] = kr


def _prep(proj, cos, sin, g_q, g_kv, wq_p, wk_p, wv_p, n):
    tm = min(n, 512)
    row = lambda w, cb: pl.BlockSpec((tm, w), lambda i: (i, cb))
    full = lambda a: pl.BlockSpec(a.shape, lambda i: (0, 0))
    return pl.pallas_call(
        _prep_kernel,
        grid=(n // tm,),
        in_specs=[
            row(HQ * DH, COL_Q // (HQ * DH)), row(QR, COL_DQ // QR), row(KVR, COL_DKV // KVR),
            row(LANES, COL_K // LANES), row(LANES, COL_KR // LANES),
            row(LANES, 0), row(LANES, 0), full(g_q), full(g_kv), full(wq_p), full(wk_p), full(wv_p),
        ],
        out_specs=[row(HQ * DH, 0), row(LANES, 0), row(MH * 2 * LANES, 0), row(MH * 2 * LANES, 0),
                   row(MH * DV, 0)],
        out_shape=[
            jax.ShapeDtypeStruct((n, HQ * DH), BF16), jax.ShapeDtypeStruct((n, LANES), BF16),
            jax.ShapeDtypeStruct((n, MH * 2 * LANES), BF16), jax.ShapeDtypeStruct((n, MH * 2 * LANES), BF16),
            jax.ShapeDtypeStruct((n, MH * DV), BF16),
        ],
        compiler_params=_params(("arbitrary",)),
        name="rope_mla_prep",
    )(proj, proj, proj, proj, proj, cos, sin, g_q, g_kv, wq_p, wk_p, wv_p)


def _swa_kernel(sinks_ref, q_ref, kc_ref, kp_ref, vc_ref, vp_ref, o_ref):
    blk = pl.program_id(1)
    q2 = jnp.concatenate([q_ref[:, g * LANES:(g + 1) * LANES] for g in range(G)], axis=0)
    k2 = jnp.concatenate([kp_ref[...], kc_ref[...]], axis=0)
    v2 = jnp.concatenate([vp_ref[...], vc_ref[...]], axis=0)
    lane = lax.broadcasted_iota(jnp.int32, (2 * WIN, LANES), 1)
    qi = lax.broadcasted_iota(jnp.int32, (WIN, 2 * WIN), 0)
    kj = lax.broadcasted_iota(jnp.int32, (WIN, 2 * WIN), 1)
    valid = (kj > qi) & (kj <= qi + WIN) & ((kj >= WIN) | (blk > 0))
    acc = jnp.zeros((G * WIN, LANES), F32)
    for h in range(HKV):
        head = (lane >= h * DH) & (lane < (h + 1) * DH)
        kz = jnp.where(head, k2, jnp.zeros_like(k2))
        vz = jnp.where(head, v2, jnp.zeros_like(v2))
        s = lax.dot_general(q2, kz, NT, preferred_element_type=F32)
        ps, invs = [], []
        for g in range(G):
            sg = jnp.where(valid, s[g * WIN:(g + 1) * WIN], NEG)
            sink = sinks_ref[h * G + g]
            m = jnp.maximum(jnp.max(sg, axis=-1, keepdims=True), sink)
            p = jnp.exp(sg - m)
            l = jnp.sum(p, axis=-1, keepdims=True) + jnp.exp(sink - m)
            ps.append(p.astype(BF16))
            invs.append(1.0 / l)
        o_h = jnp.dot(jnp.concatenate(ps, axis=0), vz, preferred_element_type=F32)
        acc = acc + o_h * jnp.concatenate(invs, axis=0)
    for g in range(G):
        o_ref[:, g * LANES:(g + 1) * LANES] = acc[g * WIN:(g + 1) * WIN].astype(BF16)


def _swa(sinks, qa, ka, proj, b, s):
    nb = s // WIN
    cur = lambda cb: pl.BlockSpec((WIN, LANES), lambda bi, n, sk: (bi * nb + n, cb))
    prev = lambda cb: pl.BlockSpec((WIN, LANES), lambda bi, n, sk: (bi * nb + jnp.maximum(n - 1, 0), cb))
    grid_spec = pltpu.PrefetchScalarGridSpec(
        num_scalar_prefetch=1,
        grid=(b, nb),
        in_specs=[
            pl.BlockSpec((WIN, HQ * DH), lambda bi, n, sk: (bi * nb + n, 0)),
            cur(0), prev(0), cur(COL_V // LANES), prev(COL_V // LANES),
        ],
        out_specs=pl.BlockSpec((WIN, HQ * DH), lambda bi, n, sk: (bi * nb + n, 0)),
    )
    return pl.pallas_call(
        _swa_kernel,
        grid_spec=grid_spec,
        out_shape=jax.ShapeDtypeStruct((b * s, HQ * DH), BF16),
        compiler_params=_params(("arbitrary", "arbitrary")),
        name="swa_attention",
    )(sinks, qa, ka, ka, proj, proj)


def _mla_kernel(q_ref, k_ref, v_ref, o_ref, *, seq, tq):
    for i in range(seq // tq):
        kv = (i + 1) * tq
        q = q_ref[i * tq:(i + 1) * tq, :]
        s = lax.dot_general(q, k_ref[0:kv, :], NT, preferred_element_type=F32)
        row = lax.broadcasted_iota(jnp.int32, (tq, kv), 0) + i * tq
        col = lax.broadcasted_iota(jnp.int32, (tq, kv), 1)
        s = jnp.where(col <= row, s, NEG)
        m = jnp.max(s, axis=-1, keepdims=True)
        p = jnp.exp(s - m)
        l = jnp.sum(p, axis=-1, keepdims=True)
        o = jnp.dot(p.astype(BF16), v_ref[0:kv, :], preferred_element_type=F32) * (1.0 / l)
        o_ref[i * tq:(i + 1) * tq, :] = o.astype(BF16)


def _mla(qm, km, vm, b, s):
    tq = min(s, 256)
    return pl.pallas_call(
        functools.partial(_mla_kernel, seq=s, tq=tq),
        grid=(b, MH),
        in_specs=[
            pl.BlockSpec((s, 2 * LANES), lambda bi, h: (bi, h)),
            pl.BlockSpec((s, 2 * LANES), lambda bi, h: (bi, h)),
            pl.BlockSpec((s, DV), lambda bi, h: (bi, h)),
        ],
        out_specs=pl.BlockSpec((s, DV), lambda bi, h: (bi, h)),
        out_shape=jax.ShapeDtypeStruct((b * s, MH * DV), BF16),
        compiler_params=_params(("arbitrary", "arbitrary")),
        name="mla_attention",
    )(qm, km, vm)


def _merge_kernel(oa_ref, ob_ref, ga_ref, gb_ref, x_ref, wa_ref, wb_ref, wo_ref, bg_ref, gf_ref,
                  wr_ref, br_ref, h_ref, hn3_ref, te_ref, tw_ref, rk_ref, cnt_ref, base_ref, *, tm):
    @pl.when(pl.program_id(0) == 0)
    def _():
        base_ref[...] = jnp.zeros_like(base_ref)

    ya = jnp.dot(oa_ref[...], wa_ref[...], preferred_element_type=F32)
    yb = jnp.dot(ob_ref[...], wb_ref[...], preferred_element_type=F32)
    ga = jax.nn.sigmoid(ga_ref[...].astype(F32) + bg_ref[:, 0:D])
    gb = jax.nn.sigmoid(gb_ref[...].astype(F32) + bg_ref[:, D:2 * D])
    mixed = (ga * ya + gb * yb).astype(BF16)
    h = x_ref[...] + jnp.dot(mixed, wo_ref[...], preferred_element_type=F32)
    h_ref[...] = h
    hn = _rms(h, gf_ref[...])
    for c in range(ROW_TILES):
        hn3_ref[pl.ds(c, tm, stride=ROW_TILES), :] = hn[:, c * LANES:(c + 1) * LANES]

    logits = lax.dot_general(wr_ref[...], hn, NT, preferred_element_type=F32,
                             precision=lax.Precision.HIGHEST) + br_ref[...]
    eidx = lax.broadcasted_iota(jnp.int32, (E, tm), 0)
    vals, idxs = [], []
    l = logits
    for _ in range(TOPK):
        m = jnp.max(l, axis=0, keepdims=True)
        idx = jnp.min(jnp.where(l == m, eidx, E), axis=0, keepdims=True)
        vals.append(m)
        idxs.append(idx)
        l = jnp.where(eidx == idx, -jnp.inf, l)
    ex = [jnp.exp(v - vals[0]) for v in vals]
    tot = ex[0] + ex[1] + ex[2] + ex[3]
    tw_ref[...] = jnp.concatenate([e / tot for e in ex], axis=0)
    te_ref[...] = jnp.concatenate(idxs, axis=0)

    onehot = jnp.zeros((E, tm), F32)
    for idx in idxs:
        onehot = onehot + jnp.where(eidx == idx, 1.0, 0.0)
    r = lax.broadcasted_iota(jnp.int32, (tm, tm), 0)
    c = lax.broadcasted_iota(jnp.int32, (tm, tm), 1)
    earlier = jnp.where(r < c, 1.0, 0.0).astype(BF16)
    before = jnp.dot(onehot.astype(BF16), earlier, preferred_element_type=F32) + base_ref[:, 0:1]
    rk_ref[...] = jnp.concatenate(
        [jnp.sum(jnp.where(eidx == idx, before, 0.0), axis=0, keepdims=True) for idx in idxs],
        axis=0).astype(jnp.int32)
    base_ref[...] = base_ref[...] + jnp.sum(onehot, axis=1, keepdims=True)
    cnt_ref[...] = base_ref[...]


def _merge(out_a, out_b, proj, x2, wa, wb, wo, b_gate, g_ffn, wr_t, b_router, n):
    tm = min(n, 256)
    row = lambda w, cb: pl.BlockSpec((tm, w), lambda i: (i, cb))
    full = lambda a: pl.BlockSpec(a.shape, lambda i: (0, 0), pipeline_mode=pl.Buffered(1))
    tok = pl.BlockSpec((TOPK, tm), lambda i: (0, i))
    return pl.pallas_call(
        functools.partial(_merge_kernel, tm=tm),
        grid=(n // tm,),
        in_specs=[
            row(HQ * DH, 0), row(MH * DV, 0), row(D, 0), row(D, 1), row(D, 0),
            full(wa), full(wb), full(wo), full(b_gate), full(g_ffn), full(wr_t), full(b_router),
        ],
        out_specs=[
            row(D, 0), pl.BlockSpec((tm * ROW_TILES, LANES), lambda i: (i, 0)), tok, tok, tok,
            pl.BlockSpec((E, LANES), lambda i: (0, 0)),
        ],
        out_shape=[
            jax.ShapeDtypeStruct((n, D), F32), jax.ShapeDtypeStruct((n * ROW_TILES, LANES), F32),
            jax.ShapeDtypeStruct((TOPK, n), jnp.int32), jax.ShapeDtypeStruct((TOPK, n), F32),
            jax.ShapeDtypeStruct((TOPK, n), jnp.int32), jax.ShapeDtypeStruct((E, LANES), F32),
        ],
        scratch_shapes=[pltpu.VMEM((E, LANES), F32)],
        compiler_params=_params(("arbitrary",)),
        name="merge_outproj_router",
    )(out_a, out_b, proj, proj, x2, wa, wb, wo, b_gate, g_ffn, wr_t, b_router)


def _dest_kernel(pstart_ref, te_ref, rk_ref, dest_ref):
    te = te_ref[...]
    d = rk_ref[...]
    for e in range(E):
        d = d + jnp.where(te == e, pstart_ref[e], 0)
    dest_ref[...] = d


def _dest(pstart, top_e, rank, n):
    tn = min(n, 8192)
    grid_spec = pltpu.PrefetchScalarGridSpec(
        num_scalar_prefetch=1, grid=(n // tn,),
        in_specs=[pl.BlockSpec((TOPK, tn), lambda i, ps: (0, i))] * 2,
        out_specs=pl.BlockSpec((TOPK, tn), lambda i, ps: (0, i)))
    return pl.pallas_call(
        _dest_kernel, grid_spec=grid_spec,
        out_shape=jax.ShapeDtypeStruct((TOPK, n), jnp.int32), name="dest_rows",
    )(pstart, top_e, rank)


def _dispatch_kernel(cnt_ref, pad_ref, pst_ref, dest_ref, hn3_ref, xg_ref, zrow_ref, sem, zsem, *, tt):
    i = pl.program_id(0)

    def row_copy(t, dst_row):
        return pltpu.make_async_copy(
            hn3_ref.at[pl.ds(pl.multiple_of(t * ROW_TILES, ROW_TILES), ROW_TILES)],
            xg_ref.at[pl.ds(pl.multiple_of(dst_row * ROW_TILES, ROW_TILES), ROW_TILES)], sem)

    def zero_copy(dst_row):
        return pltpu.make_async_copy(
            zrow_ref, xg_ref.at[pl.ds(pl.multiple_of(dst_row * ROW_TILES, ROW_TILES), ROW_TILES)], zsem)

    @pl.when(i == 0)
    def _():
        zrow_ref[...] = jnp.zeros_like(zrow_ref)

        def per_expert(e, carry):
            base = pst_ref[e]

            def start(r, c):
                zero_copy(base + r).start()
                return c

            def wait(r, c):
                zero_copy(base + r).wait()
                return c

            lax.fori_loop(cnt_ref[e], pad_ref[e], start, 0)
            lax.fori_loop(cnt_ref[e], pad_ref[e], wait, 0)
            return carry

        lax.fori_loop(0, E, per_expert, 0)

    def start(t, c):
        for k in range(TOPK):
            row_copy(t, dest_ref[k, t]).start()
        return c

    def wait(t, c):
        for k in range(TOPK):
            row_copy(t, dest_ref[k, t]).wait()
        return c

    lax.fori_loop(0, tt, start, 0, unroll=8)
    lax.fori_loop(0, tt, wait, 0, unroll=8)


def _dispatch(counts, padded, pstart, dest, hn3, n, p_rows):
    tt = min(n, 512)
    grid_spec = pltpu.PrefetchScalarGridSpec(
        num_scalar_prefetch=3, grid=(n // tt,),
        in_specs=[
            pl.BlockSpec((TOPK, tt), lambda i, a, b, c: (0, i), memory_space=pltpu.SMEM),
            pl.BlockSpec((tt * ROW_TILES, LANES), lambda i, a, b, c: (i, 0)),
        ],
        out_specs=pl.BlockSpec(memory_space=pl.ANY),
        scratch_shapes=[pltpu.VMEM((ROW_TILES, LANES), F32), pltpu.SemaphoreType.DMA(()),
                        pltpu.SemaphoreType.DMA(())],
    )
    return pl.pallas_call(
        functools.partial(_dispatch_kernel, tt=tt), grid_spec=grid_spec,
        out_shape=jax.ShapeDtypeStruct((p_rows * ROW_TILES, LANES), F32),
        compiler_params=pltpu.CompilerParams(dimension_semantics=("arbitrary",), has_side_effects=True),
        name="dispatch_rows",
    )(counts, padded, pstart, dest, hn3)


def _deinterleave_kernel(w_ref, g_ref, u_ref):
    chunk = 2 * LANES
    r = lax.broadcasted_iota(jnp.int32, (chunk, chunk), 0)
    c = lax.broadcasted_iota(jnp.int32, (chunk, chunk), 1)
    src = jnp.where(c < LANES, 2 * c, 2 * (c - LANES) + 1)
    perm = jnp.where(r == src, 1.0, 0.0).astype(BF16)
    for k in range(w_ref.shape[1] // chunk):
        w = w_ref[:, k * chunk:(k + 1) * chunk].astype(BF16)
        o = jnp.dot(w, perm, preferred_element_type=F32)
        g_ref[:, k * LANES:(k + 1) * LANES] = o[:, :LANES].astype(BF16)
        u_ref[:, k * LANES:(k + 1) * LANES] = o[:, LANES:].astype(BF16)


def _deinterleave(w_gate_up):
    tr = 512
    out = pl.BlockSpec((None, tr, DE), lambda e, i: (e, i, 0))
    return pl.pallas_call(
        _deinterleave_kernel,
        grid=(E, D // tr),
        in_specs=[pl.BlockSpec((None, tr, 2 * DE), lambda e, i: (e, i, 0))],
        out_specs=[out, out],
        out_shape=[jax.ShapeDtypeStruct((E, D, DE), BF16)] * 2,
        compiler_params=_params(("arbitrary", "arbitrary")),
        name="split_gate_up",
    )(w_gate_up)


def _moe_kernel(be_ref, nu_ref, x3_ref, wg_ref, wu_ref, wd_ref, bg_ref, bu_ref, bd_ref, y3_ref,
                x2_ref, acc_ref, *, bm, nj):
    blk = pl.program_id(0)
    j = pl.program_id(1)

    @pl.when(blk < nu_ref[0])
    def _():
        @pl.when(j == 0)
        def _():
            for c in range(ROW_TILES):
                x2_ref[:, c * LANES:(c + 1) * LANES] = x3_ref[pl.ds(c, bm, stride=ROW_TILES), :].astype(BF16)

        x = x2_ref[...]
        gate = jnp.dot(x, wg_ref[...], preferred_element_type=F32) + bg_ref[...]
        up = jnp.dot(x, wu_ref[...], preferred_element_type=F32) + bu_ref[...]
        gate = jnp.minimum(gate, LIMIT)
        up = jnp.clip(up, -LIMIT, LIMIT)
        act = (gate * jax.nn.sigmoid(ALPHA * gate) * (up + 1.0)).astype(BF16)
        part = jnp.dot(act, wd_ref[...], preferred_element_type=F32)

        @pl.when(j == 0)
        def _():
            acc_ref[...] = part + bd_ref[...]

        @pl.when(j > 0)
        def _():
            acc_ref[...] += part

        @pl.when(j == nj - 1)
        def _():
            for c in range(ROW_TILES):
                y3_ref[pl.ds(c, bm, stride=ROW_TILES), :] = acc_ref[:, c * LANES:(c + 1) * LANES]


def _moe(block_e, n_used, xg, wg, wu, wd, bg, bu, bd, n_blocks, bm):
    th = 512
    nj = DE // th

    def wcol(blk, j, be, nu):
        return (be[blk], 0, jnp.where(blk < nu[0], j, nj - 1))

    def wrow(blk, j, be, nu):
        return (be[blk], jnp.where(blk < nu[0], j, nj - 1), 0)

    def xrow(blk, j, be, nu):
        return (jnp.minimum(blk, nu[0] - 1), 0)

    grid_spec = pltpu.PrefetchScalarGridSpec(
        num_scalar_prefetch=2, grid=(n_blocks, nj),
        in_specs=[
            pl.BlockSpec((bm * ROW_TILES, LANES), xrow),
            pl.BlockSpec((None, D, th), wcol), pl.BlockSpec((None, D, th), wcol),
            pl.BlockSpec((None, th, D), wrow),
            pl.BlockSpec((None, 1, th), wcol), pl.BlockSpec((None, 1, th), wcol),
            pl.BlockSpec((None, 1, D), lambda blk, j, be, nu: (be[blk], 0, 0)),
        ],
        out_specs=pl.BlockSpec((bm * ROW_TILES, LANES), xrow),
        scratch_shapes=[pltpu.VMEM((bm, D), BF16), pltpu.VMEM((bm, D), F32)],
    )
    return pl.pallas_call(
        functools.partial(_moe_kernel, bm=bm, nj=nj), grid_spec=grid_spec,
        out_shape=jax.ShapeDtypeStruct((n_blocks * bm * ROW_TILES, LANES), F32),
        compiler_params=_params(("arbitrary", "arbitrary")),
        name="moe_experts",
    )(block_e, n_used, xg, wg, wu, wd, bg, bu, bd)


def _combine_kernel(dcur_ref, dnxt_ref, y3_ref, h_ref, tw_ref, gf_ref, o_ref, buf_ref, sem, *, tt, nt):
    i = pl.program_id(0)
    slot = i % 2

    def row_copy(d_ref, k, t, s):
        return pltpu.make_async_copy(
            y3_ref.at[pl.ds(pl.multiple_of(d_ref[k, t] * ROW_TILES, ROW_TILES), ROW_TILES)],
            buf_ref.at[s, pl.ds((k * tt + t) * ROW_TILES, ROW_TILES)], sem.at[s])

    def issue(d_ref, s):
        def body(t, c):
            for k in range(TOPK):
                row_copy(d_ref, k, t, s).start()
            return c
        lax.fori_loop(0, tt, body, 0)

    @pl.when(i == 0)
    def _():
        issue(dcur_ref, 0)

    @pl.when(i + 1 < nt)
    def _():
        issue(dnxt_ref, 1 - slot)

    def wait(t, c):
        for k in range(TOPK):
            row_copy(dcur_ref, k, t, slot).wait()
        return c

    lax.fori_loop(0, tt, wait, 0)

    tw = tw_ref[...]
    for c in range(ROW_TILES):
        acc = h_ref[:, c * LANES:(c + 1) * LANES]
        for k in range(TOPK):
            rows = buf_ref[slot, pl.ds(k * tt * ROW_TILES + c, tt, stride=ROW_TILES), :]
            acc = acc + rows * tw[:, k:k + 1]
        o_ref[:, c * LANES:(c + 1) * LANES] = acc
    hf = o_ref[...]
    o_ref[...] = _rms(hf, gf_ref[...])


def _combine(dest, y3, h, tw_t, g_final, n):
    tt = min(n, 256)
    nt = n // tt
    grid_spec = pltpu.PrefetchScalarGridSpec(
        num_scalar_prefetch=0, grid=(nt,),
        in_specs=[
            pl.BlockSpec((TOPK, tt), lambda i: (0, i), memory_space=pltpu.SMEM),
            pl.BlockSpec((TOPK, tt), lambda i: (0, jnp.minimum(i + 1, nt - 1)), memory_space=pltpu.SMEM),
            pl.BlockSpec(memory_space=pl.ANY),
            pl.BlockSpec((tt, D), lambda i: (i, 0)),
            pl.BlockSpec((tt, TOPK), lambda i: (i, 0)),
            pl.BlockSpec((1, D), lambda i: (0, 0)),
        ],
        out_specs=pl.BlockSpec((tt, D), lambda i: (i, 0)),
        scratch_shapes=[pltpu.VMEM((2, TOPK * tt * ROW_TILES, LANES), F32), pltpu.SemaphoreType.DMA((2,))],
    )
    return pl.pallas_call(
        functools.partial(_combine_kernel, tt=tt, nt=nt), grid_spec=grid_spec,
        out_shape=jax.ShapeDtypeStruct((n, D), F32),
        compiler_params=_params(("arbitrary",)),
        name="combine_final_norm",
    )(dest, dest, y3, h, tw_t, g_final)


def _permute_weights(w_in, w_uq, w_ukv, w_o_swa):
    widths = (HQ * DH, HKV * DH, HKV * DH, QR, KVR, DR, 2 * D)
    offs = [0]
    for w in widths:
        offs.append(offs[-1] + w)
    wq, wk, wv, wdq, wdkv, wkr, wgates = [w_in[:, offs[i]:offs[i + 1]] for i in range(7)]
    wq = wq.reshape(D, HKV, G, DH).transpose(0, 2, 1, 3).reshape(D, HQ * DH)
    pad = jnp.zeros((D, PROJ_COLS - COL_KR - DR), w_in.dtype)
    w_in_p = jnp.concatenate([wgates, wq, wdq, wdkv, wk, wv, wkr, pad], axis=1).astype(BF16)
    wo_a = w_o_swa.reshape(HKV, G, DH, D).transpose(1, 0, 2, 3).reshape(HQ * DH, D).astype(BF16)
    wuq = w_uq.reshape(QR, MH, DN + DR)
    wq_p = jnp.concatenate([wuq, jnp.zeros((QR, MH, 2 * LANES - DN - DR), w_uq.dtype)], axis=2)
    wq_p = wq_p.reshape(QR, MH * 2 * LANES).astype(BF16)
    wukv = w_ukv.reshape(KVR, MH, DN + DV)
    wk_p = wukv[:, :, :DN].reshape(KVR, MH * DN).astype(BF16)
    wv_p = wukv[:, :, DN:].reshape(KVR, MH * DV).astype(BF16)
    return w_in_p, wo_a, wq_p, wk_p, wv_p


def kernel(x, positions, g_attn, w_in, b_gate, sinks, g_q, w_uq, g_kv, w_ukv, w_o_swa, w_o_mla, w_out,
           g_ffn, w_router, b_router, w_gate_up, b_gate_up, w_down, b_down, g_final):
    b, s, _ = x.shape
    n = b * s
    assert w_in.shape[0] == 1, "single-layer block"
    x2 = x.reshape(n, D)

    w_in_p, wo_a, wq_p, wk_p, wv_p = _permute_weights(w_in[0], w_uq[0], w_ukv[0], w_o_swa[0])
    cos, sin = _rope_tables(positions, n)
    proj = _in_proj(x2, g_attn, w_in_p, n)
    qa, ka, qm, km, vm = _prep(proj, cos, sin, g_q, g_kv, wq_p, wk_p, wv_p, n)
    out_a = _swa(sinks[0], qa, ka, proj, b, s)
    out_b = _mla(qm, km, vm, b, s)

    h, hn3, top_e, top_w, rank, cnt = _merge(
        out_a, out_b, proj, x2, wo_a, w_o_mla[0].astype(BF16), w_out[0].astype(BF16), b_gate, g_ffn,
        w_router[0].T, b_router[0][:, None], n)

    bm = 512
    n_blocks = -(-(n * TOPK) // bm) + E
    counts = cnt[:, 0].astype(jnp.int32)
    padded = (counts + bm - 1) // bm * bm
    pend = jnp.cumsum(padded)
    pstart = pend - padded
    n_used = pend[-1] // bm
    blocks = jnp.arange(n_blocks, dtype=jnp.int32)
    block_e = jnp.minimum(jnp.sum(pend[None, :] <= (blocks * bm)[:, None], axis=1), E - 1).astype(jnp.int32)
    block_e = jnp.where(blocks < n_used, block_e, block_e[jnp.maximum(n_used - 1, 0)])

    dest = _dest(pstart, top_e, rank, n)
    xg = _dispatch(counts, padded, pstart, dest, hn3, n, n_blocks * bm)

    wg, wu = _deinterleave(w_gate_up[0])
    bgu = b_gate_up[0]
    y3 = _moe(block_e, n_used.reshape(1).astype(jnp.int32), xg, wg, wu, w_down[0].astype(BF16),
              bgu[:, None, 0::2], bgu[:, None, 1::2], b_down[0][:, None, :], n_blocks, bm)

    out = _combine(dest, y3, h, top_w.T, g_final[None, :], n)
    return out.reshape(b, s, D)
```

```python
import functools

import jax
import jax.numpy as jnp
from jax import lax
from jax.experimental import pallas as pl
from jax.experimental.pallas import tpu as pltpu

F32 = jnp.float32
BF16 = jnp.bfloat16
U32 = jnp.uint32

D = 2048
HQ, HKV, DH, WIN = 16, 2, 64, 128
G = HQ // HKV
MH, QR, KVR, DN, DR, DV = 8, 512, 256, 128, 64, 128
THETA = 10000.0
E, TOPK, DE = 32, 4, 2048
LIMIT, ALPHA = 7.0, 1.702
EPS = 1e-6
NEG = -1e30

LANES = 128
CHUNKS = D // LANES
PACK_ROWS = CHUNKS // 2
VMEM_LIMIT = 56 * 1024 * 1024

PROJ_COLS = 6400
COL_GATES, COL_Q, COL_DQ, COL_DKV, COL_K, COL_V, COL_KR = 0, 4096, 5120, 5632, 5888, 6016, 6144

NT = (((1,), (1,)), ((), ()))


def _params(sem, vmem=VMEM_LIMIT):
    return pltpu.CompilerParams(dimension_semantics=sem, vmem_limit_bytes=vmem)


def _rms(x, g):
    return x * lax.rsqrt(jnp.mean(x * x, axis=-1, keepdims=True) + EPS) * g


def _pack_pair(lo, hi):
    lo_bits = lax.bitcast_convert_type(lo.astype(BF16).astype(F32), U32)
    hi_bits = lax.bitcast_convert_type(hi.astype(BF16).astype(F32), U32)
    return (lo_bits >> 16) | hi_bits


def _unpack_pair(w):
    lo = lax.bitcast_convert_type(w << 16, F32)
    hi = lax.bitcast_convert_type(w & jnp.uint32(0xFFFF0000), F32)
    return lo, hi


def _rope128(v, cos, sin):
    lane = lax.broadcasted_iota(jnp.int32, v.shape, 1)
    rot = jnp.where((lane % DH) < (DH // 2), -pltpu.roll(v, LANES - DH // 2, 1), pltpu.roll(v, DH // 2, 1))
    return v * cos + rot * sin


def _rope_table_kernel(pos_ref, inv_ref, cos_ref, sin_ref):
    ang = pos_ref[...].astype(F32) * inv_ref[...]
    cos_ref[...] = jnp.cos(ang)
    sin_ref[...] = jnp.sin(ang)


def _rope_tables(positions, n):
    half = DH // 2
    per_row = LANES // half
    inv = jnp.power(THETA, -jnp.arange(half, dtype=F32) * 2.0 / DH)
    inv_row = jnp.tile(inv, per_row)[None, :]
    pos_rows = jnp.repeat(positions.reshape(n // per_row, per_row), half, axis=1)
    rows = n // per_row
    tr = min(rows, 1024)
    cos, sin = pl.pallas_call(
        _rope_table_kernel,
        grid=(rows // tr,),
        in_specs=[pl.BlockSpec((tr, LANES), lambda i: (i, 0)), pl.BlockSpec((1, LANES), lambda i: (0, 0))],
        out_specs=[pl.BlockSpec((tr, LANES), lambda i: (i, 0))] * 2,
        out_shape=[jax.ShapeDtypeStruct((rows, LANES), F32)] * 2,
        name="rope_tables",
    )(pos_rows, inv_row)
    cos = jnp.tile(cos.reshape(n, half), (1, per_row))
    sin = jnp.tile(sin.reshape(n, half), (1, per_row))
    return cos, sin


def _in_proj_kernel(x_ref, g_ref, w_ref, o_ref, xn_ref):
    @pl.when(pl.program_id(1) == 0)
    def _():
        xn_ref[...] = _rms(x_ref[...], g_ref[...]).astype(BF16)

    o_ref[...] = jnp.dot(xn_ref[...], w_ref[...], preferred_element_type=F32).astype(o_ref.dtype)


def _in_proj(x2, g_attn, w_in_p, n):
    tm = min(n, 1024)
    tn = 1280
    return pl.pallas_call(
        _in_proj_kernel,
        grid=(n // tm, PROJ_COLS // tn),
        in_specs=[
            pl.BlockSpec((tm, D), lambda i, j: (i, 0)),
            pl.BlockSpec((1, D), lambda i, j: (0, 0)),
            pl.BlockSpec((D, tn), lambda i, j: (0, j)),
        ],
        out_specs=pl.BlockSpec((tm, tn), lambda i, j: (i, j)),
        out_shape=jax.ShapeDtypeStruct((n, PROJ_COLS), BF16),
        scratch_shapes=[pltpu.VMEM((tm, D), BF16)],
        compiler_params=_params(("arbitrary", "arbitrary")),
        name="in_proj",
    )(x2, g_attn, w_in_p)


def _prep_kernel(q_ref, dq_ref, dkv_ref, k_ref, kr_ref, cos_ref, sin_ref, gq_ref, gkv_ref,
                 wq_ref, wk_ref, wv_ref, qa_ref, ka_ref, qm_ref, km_ref, vm_ref):
    cos = cos_ref[...]
    sin = sin_ref[...]
    swa_scale = DH ** -0.5
    for c in range(HQ * DH // LANES):
        sl = slice(c * LANES, (c + 1) * LANES)
        qa_ref[:, sl] = (_rope128(q_ref[:, sl].astype(F32), cos, sin) * swa_scale).astype(BF16)
    ka_ref[...] = _rope128(k_ref[...].astype(F32), cos, sin).astype(BF16)

    mla_scale = (DN + DR) ** -0.5
    cq = _rms(dq_ref[...].astype(F32), gq_ref[...]).astype(BF16)
    qb = jnp.dot(cq, wq_ref[...], preferred_element_type=F32)
    ckv = _rms(dkv_ref[...].astype(F32), gkv_ref[...]).astype(BF16)
    kn = jnp.dot(ckv, wk_ref[...], preferred_element_type=F32)
    vm_ref[...] = jnp.dot(ckv, wv_ref[...], preferred_element_type=F32).astype(BF16)
    kr = _rope128(kr_ref[...].astype(F32), cos, sin).astype(BF16)
    for h in range(MH):
        lo = h * 2 * LANES
        qm_ref[:, lo:lo + LANES] = (qb[:, lo:lo + LANES] * mla_scale).astype(BF16)
        qm_ref[:, lo + LANES:lo + 2 * LANES] = (
            _rope128(qb[:, lo + LANES:lo + 2 * LANES], cos, sin) * mla_scale).astype(BF16)
        km_ref[:, lo:lo + LANES] = kn[:, h * LANES:(h + 1) * LANES].astype(BF16)
        km_ref[:, lo + LANES:lo + 2 * LANES] = kr


def _prep(proj, cos, sin, g_q, g_kv, wq_p, wk_p, wv_p, n):
    tm = min(n, 512)
    row = lambda w, cb: pl.BlockSpec((tm, w), lambda i: (i, cb))
    full = lambda a: pl.BlockSpec(a.shape, lambda i: (0, 0))
    return pl.pallas_call(
        _prep_kernel,
        grid=(n // tm,),
        in_specs=[
            row(HQ * DH, COL_Q // (HQ * DH)), row(QR, COL_DQ // QR), row(KVR, COL_DKV // KVR),
            row(LANES, COL_K // LANES), row(LANES, COL_KR // LANES),
            row(LANES, 0), row(LANES, 0), full(g_q), full(g_kv), full(wq_p), full(wk_p), full(wv_p),
        ],
        out_specs=[row(HQ * DH, 0), row(LANES, 0), row(MH * 2 * LANES, 0), row(MH * 2 * LANES, 0),
                   row(MH * DV, 0)],
        out_shape=[
            jax.ShapeDtypeStruct((n, HQ * DH), BF16), jax.ShapeDtypeStruct((n, LANES), BF16),
            jax.ShapeDtypeStruct((n, MH * 2 * LANES), BF16), jax.ShapeDtypeStruct((n, MH * 2 * LANES), BF16),
            jax.ShapeDtypeStruct((n, MH * DV), BF16),
        ],
        compiler_params=_params(("arbitrary",)),
        name="rope_mla_prep",
    )(proj, proj, proj, proj, proj, cos, sin, g_q, g_kv, wq_p, wk_p, wv_p)


def _swa_kernel(sinks_ref, q_ref, kc_ref, kp_ref, vc_ref, vp_ref, o_ref):
    blk = pl.program_id(1)
    q2 = jnp.concatenate([q_ref[:, g * LANES:(g + 1) * LANES] for g in range(G)], axis=0)
    k2 = jnp.concatenate([kp_ref[...], kc_ref[...]], axis=0)
    v2 = jnp.concatenate([vp_ref[...], vc_ref[...]], axis=0)
    lane = lax.broadcasted_iota(jnp.int32, (2 * WIN, LANES), 1)
    qi = lax.broadcasted_iota(jnp.int32, (WIN, 2 * WIN), 0)
    kj = lax.broadcasted_iota(jnp.int32, (WIN, 2 * WIN), 1)
    valid = (kj > qi) & (kj <= qi + WIN) & ((kj >= WIN) | (blk > 0))
    acc = jnp.zeros((G * WIN, LANES), F32)
    for h in range(HKV):
        head = (lane >= h * DH) & (lane < (h + 1) * DH)
        kz = jnp.where(head, k2, jnp.zeros_like(k2))
        vz = jnp.where(head, v2, jnp.zeros_like(v2))
        s = lax.dot_general(q2, kz, NT, preferred_element_type=F32)
        ps, invs = [], []
        for g in range(G):
            sg = jnp.where(valid, s[g * WIN:(g + 1) * WIN], NEG)
            sink = sinks_ref[h * G + g]
            m = jnp.maximum(jnp.max(sg, axis=-1, keepdims=True), sink)
            p = jnp.exp(sg - m)
            l = jnp.sum(p, axis=-1, keepdims=True) + jnp.exp(sink - m)
            ps.append(p.astype(BF16))
            invs.append(1.0 / l)
        o_h = jnp.dot(jnp.concatenate(ps, axis=0), vz, preferred_element_type=F32)
        acc = acc + o_h * jnp.concatenate(invs, axis=0)
    for g in range(G):
        o_ref[:, g * LANES:(g + 1) * LANES] = acc[g * WIN:(g + 1) * WIN].astype(BF16)


def _swa(sinks, qa, ka, proj, b, s):
    nb = s // WIN
    cur = lambda cb: pl.BlockSpec((WIN, LANES), lambda bi, n, sk: (bi * nb + n, cb))
    prev = lambda cb: pl.BlockSpec((WIN, LANES), lambda bi, n, sk: (bi * nb + jnp.maximum(n - 1, 0), cb))
    grid_spec = pltpu.PrefetchScalarGridSpec(
        num_scalar_prefetch=1,
        grid=(b, nb),
        in_specs=[
            pl.BlockSpec((WIN, HQ * DH), lambda bi, n, sk: (bi * nb + n, 0)),
            cur(0), prev(0), cur(COL_V // LANES), prev(COL_V // LANES),
        ],
        out_specs=pl.BlockSpec((WIN, HQ * DH), lambda bi, n, sk: (bi * nb + n, 0)),
    )
    return pl.pallas_call(
        _swa_kernel,
        grid_spec=grid_spec,
        out_shape=jax.ShapeDtypeStruct((b * s, HQ * DH), BF16),
        compiler_params=_params(("arbitrary", "arbitrary")),
        name="swa_attention",
    )(sinks, qa, ka, ka, proj, proj)


def _mla_kernel(q_ref, k_ref, v_ref, o_ref, *, seq, tq):
    for i in range(seq // tq):
        kv = (i + 1) * tq
        q = q_ref[i * tq:(i + 1) * tq, :]
        s = lax.dot_general(q, k_ref[0:kv, :], NT, preferred_element_type=F32)
        row = lax.broadcasted_iota(jnp.int32, (tq, kv), 0) + i * tq
        col = lax.broadcasted_iota(jnp.int32, (tq, kv), 1)
        s = jnp.where(col <= row, s, NEG)
        m = jnp.max(s, axis=-1, keepdims=True)
        p = jnp.exp(s - m)
        l = jnp.sum(p, axis=-1, keepdims=True)
        o = jnp.dot(p.astype(BF16), v_ref[0:kv, :], preferred_element_type=F32) * (1.0 / l)
        o_ref[i * tq:(i + 1) * tq, :] = o.astype(BF16)


def _mla(qm, km, vm, b, s):
    tq = min(s, 256)
    return pl.pallas_call(
        functools.partial(_mla_kernel, seq=s, tq=tq),
        grid=(b, MH),
        in_specs=[
            pl.BlockSpec((s, 2 * LANES), lambda bi, h: (bi, h)),
            pl.BlockSpec((s, 2 * LANES), lambda bi, h: (bi, h)),
            pl.BlockSpec((s, DV), lambda bi, h: (bi, h)),
        ],
        out_specs=pl.BlockSpec((s, DV), lambda bi, h: (bi, h)),
        out_shape=jax.ShapeDtypeStruct((b * s, MH * DV), BF16),
        compiler_params=_params(("arbitrary", "arbitrary")),
        name="mla_attention",
    )(qm, km, vm)


def _mix_kernel(oa_ref, ob_ref, ga_ref, gb_ref, wa_ref, wb_ref, ba_ref, bb_ref, o_ref):
    ya = jnp.dot(oa_ref[...], wa_ref[...], preferred_element_type=F32)
    yb = jnp.dot(ob_ref[...], wb_ref[...], preferred_element_type=F32)
    ga = jax.nn.sigmoid(ga_ref[...].astype(F32) + ba_ref[...])
    gb = jax.nn.sigmoid(gb_ref[...].astype(F32) + bb_ref[...])
    o_ref[...] = (ga * ya + gb * yb).astype(BF16)


def _mix(out_a, out_b, proj, wa, wb, b_gate, n):
    tm = min(n, 1024)
    tn = 1024
    nj = D // tn
    return pl.pallas_call(
        _mix_kernel,
        grid=(n // tm, nj),
        in_specs=[
            pl.BlockSpec((tm, HQ * DH), lambda i, j: (i, 0)),
            pl.BlockSpec((tm, MH * DV), lambda i, j: (i, 0)),
            pl.BlockSpec((tm, tn), lambda i, j: (i, j)),
            pl.BlockSpec((tm, tn), lambda i, j: (i, nj + j)),
            pl.BlockSpec((HQ * DH, tn), lambda i, j: (0, j)),
            pl.BlockSpec((MH * DV, tn), lambda i, j: (0, j)),
            pl.BlockSpec((1, tn), lambda i, j: (0, j)),
            pl.BlockSpec((1, tn), lambda i, j: (0, nj + j)),
        ],
        out_specs=pl.BlockSpec((tm, tn), lambda i, j: (i, j)),
        out_shape=jax.ShapeDtypeStruct((n, D), BF16),
        compiler_params=_params(("arbitrary", "arbitrary")),
        name="gated_mix",
    )(out_a, out_b, proj, proj, wa, wb, b_gate, b_gate)


def _router_kernel(mx_ref, x_ref, wo_ref, gf_ref, wr_ref, br_ref,
                   h_ref, hnp_ref, te_ref, tw_ref, rk_ref, cnt_ref, base_ref, *, tm):
    @pl.when(pl.program_id(0) == 0)
    def _():
        base_ref[...] = jnp.zeros_like(base_ref)

    h = x_ref[...] + jnp.dot(mx_ref[...], wo_ref[...], preferred_element_type=F32)
    h_ref[...] = h
    hn = _rms(h, gf_ref[...])
    for c in range(PACK_ROWS):
        hi = c + PACK_ROWS
        hnp_ref[pl.ds(c, tm, stride=PACK_ROWS), :] = _pack_pair(
            hn[:, c * LANES:(c + 1) * LANES], hn[:, hi * LANES:(hi + 1) * LANES])

    logits = lax.dot_general(wr_ref[...], hn, NT, preferred_element_type=F32,
                             precision=lax.Precision.HIGHEST) + br_ref[...]
    eidx = lax.broadcasted_iota(jnp.int32, (E, tm), 0)
    vals, idxs = [], []
    l = logits
    for _ in range(TOPK):
        m = jnp.max(l, axis=0, keepdims=True)
        idx = jnp.min(jnp.where(l == m, eidx, E), axis=0, keepdims=True)
        vals.append(m)
        idxs.append(idx)
        l = jnp.where(eidx == idx, -jnp.inf, l)
    ex = [jnp.exp(v - vals[0]) for v in vals]
    tot = ex[0] + ex[1] + ex[2] + ex[3]
    tw_ref[...] = jnp.concatenate([e / tot for e in ex], axis=0)
    te_ref[...] = jnp.concatenate(idxs, axis=0)

    onehot = jnp.zeros((E, tm), F32)
    for idx in idxs:
        onehot = onehot + jnp.where(eidx == idx, 1.0, 0.0)
    r = lax.broadcasted_iota(jnp.int32, (tm, tm), 0)
    c = lax.broadcasted_iota(jnp.int32, (tm, tm), 1)
    earlier = jnp.where(r < c, 1.0, 0.0).astype(BF16)
    before = jnp.dot(onehot.astype(BF16), earlier, preferred_element_type=F32) + base_ref[:, 0:1]
    rk_ref[...] = jnp.concatenate(
        [jnp.sum(jnp.where(eidx == idx, before, 0.0), axis=0, keepdims=True) for idx in idxs],
        axis=0).astype(jnp.int32)
    base_ref[...] = base_ref[...] + jnp.sum(onehot, axis=1, keepdims=True)
    cnt_ref[...] = base_ref[...]


def _router(mixed, x2, wo, g_ffn, wr_t, b_router, n):
    tm = min(n, 512)
    row = lambda w: pl.BlockSpec((tm, w), lambda i: (i, 0))
    full = lambda a: pl.BlockSpec(a.shape, lambda i: (0, 0), pipeline_mode=pl.Buffered(1))
    tok = pl.BlockSpec((TOPK, tm), lambda i: (0, i))
    return pl.pallas_call(
        functools.partial(_router_kernel, tm=tm),
        grid=(n // tm,),
        in_specs=[row(D), row(D), full(wo), full(g_ffn), full(wr_t), full(b_router)],
        out_specs=[
            row(D), pl.BlockSpec((tm * PACK_ROWS, LANES), lambda i: (i, 0)), tok, tok, tok,
            pl.BlockSpec((E, LANES), lambda i: (0, 0)),
        ],
        out_shape=[
            jax.ShapeDtypeStruct((n, D), F32), jax.ShapeDtypeStruct((n * PACK_ROWS, LANES), U32),
            jax.ShapeDtypeStruct((TOPK, n), jnp.int32), jax.ShapeDtypeStruct((TOPK, n), F32),
            jax.ShapeDtypeStruct((TOPK, n), jnp.int32), jax.ShapeDtypeStruct((E, LANES), F32),
        ],
        scratch_shapes=[pltpu.VMEM((E, LANES), F32)],
        compiler_params=_params(("arbitrary",)),
        name="outproj_router",
    )(mixed, x2, wo, g_ffn, wr_t, b_router)


def _dest_kernel(pstart_ref, te_ref, rk_ref, dest_ref):
    te = te_ref[...]
    d = rk_ref[...]
    for e in range(E):
        d = d + jnp.where(te == e, pstart_ref[e], 0)
    dest_ref[...] = d


def _dest(pstart, top_e, rank, n):
    tn = min(n, 8192)
    grid_spec = pltpu.PrefetchScalarGridSpec(
        num_scalar_prefetch=1, grid=(n // tn,),
        in_specs=[pl.BlockSpec((TOPK, tn), lambda i, ps: (0, i))] * 2,
        out_specs=pl.BlockSpec((TOPK, tn), lambda i, ps: (0, i)))
    return pl.pallas_call(
        _dest_kernel, grid_spec=grid_spec,
        out_shape=jax.ShapeDtypeStruct((TOPK, n), jnp.int32), name="dest_rows",
    )(pstart, top_e, rank)


def _dispatch_kernel(cnt_ref, pad_ref, pst_ref, dest_ref, hnp_ref, xg_ref, zrow_ref, sem, zsem, *, tt):
    i = pl.program_id(0)

    def row_copy(t, dst_row):
        return pltpu.make_async_copy(
            hnp_ref.at[pl.ds(pl.multiple_of(t * PACK_ROWS, PACK_ROWS), PACK_ROWS)],
            xg_ref.at[pl.ds(pl.multiple_of(dst_row * PACK_ROWS, PACK_ROWS), PACK_ROWS)], sem)

    def zero_copy(dst_row):
        return pltpu.make_async_copy(
            zrow_ref, xg_ref.at[pl.ds(pl.multiple_of(dst_row * PACK_ROWS, PACK_ROWS), PACK_ROWS)], zsem)

    @pl.when(i == 0)
    def _():
        zrow_ref[...] = jnp.zeros_like(zrow_ref)

        def per_expert(e, carry):
            base = pst_ref[e]

            def start(r, c):
                zero_copy(base + r).start()
                return c

            def wait(r, c):
                zero_copy(base + r).wait()
                return c

            lax.fori_loop(cnt_ref[e], pad_ref[e], start, 0)
            lax.fori_loop(cnt_ref[e], pad_ref[e], wait, 0)
            return carry

        lax.fori_loop(0, E, per_expert, 0)

    def start(t, c):
        for k in range(TOPK):
            row_copy(t, dest_ref[k, t]).start()
        return c

    def wait(t, c):
        for k in range(TOPK):
            row_copy(t, dest_ref[k, t]).wait()
        return c

    lax.fori_loop(0, tt, start, 0, unroll=8)
    lax.fori_loop(0, tt, wait, 0, unroll=8)


def _dispatch(counts, padded, pstart, dest, hnp, n, p_rows):
    tt = min(n, 512)
    grid_spec = pltpu.PrefetchScalarGridSpec(
        num_scalar_prefetch=3, grid=(n // tt,),
        in_specs=[
            pl.BlockSpec((TOPK, tt), lambda i, a, b, c: (0, i), memory_space=pltpu.SMEM),
            pl.BlockSpec((tt * PACK_ROWS, LANES), lambda i, a, b, c: (i, 0)),
        ],
        out_specs=pl.BlockSpec(memory_space=pl.ANY),
        scratch_shapes=[pltpu.VMEM((PACK_ROWS, LANES), U32), pltpu.SemaphoreType.DMA(()),
                        pltpu.SemaphoreType.DMA(())],
    )
    return pl.pallas_call(
        functools.partial(_dispatch_kernel, tt=tt), grid_spec=grid_spec,
        out_shape=jax.ShapeDtypeStruct((p_rows * PACK_ROWS, LANES), U32),
        compiler_params=pltpu.CompilerParams(dimension_semantics=("arbitrary",), has_side_effects=True),
        name="dispatch_rows",
    )(counts, padded, pstart, dest, hnp)


def _deinterleave_kernel(w_ref, g_ref, u_ref):
    chunk = 2 * LANES
    r = lax.broadcasted_iota(jnp.int32, (chunk, chunk), 0)
    c = lax.broadcasted_iota(jnp.int32, (chunk, chunk), 1)
    src = jnp.where(c < LANES, 2 * c, 2 * (c - LANES) + 1)
    perm = jnp.where(r == src, 1.0, 0.0).astype(BF16)
    for k in range(w_ref.shape[1] // chunk):
        w = w_ref[:, k * chunk:(k + 1) * chunk].astype(BF16)
        o = jnp.dot(w, perm, preferred_element_type=F32)
        g_ref[:, k * LANES:(k + 1) * LANES] = o[:, :LANES].astype(BF16)
        u_ref[:, k * LANES:(k + 1) * LANES] = o[:, LANES:].astype(BF16)


def _deinterleave(w_gate_up):
    tr = 512
    out = pl.BlockSpec((None, tr, DE), lambda e, i: (e, i, 0))
    return pl.pallas_call(
        _deinterleave_kernel,
        grid=(E, D // tr),
        in_specs=[pl.BlockSpec((None, tr, 2 * DE), lambda e, i: (e, i, 0))],
        out_specs=[out, out],
        out_shape=[jax.ShapeDtypeStruct((E, D, DE), BF16)] * 2,
        compiler_params=_params(("arbitrary", "arbitrary")),
        name="split_gate_up",
    )(w_gate_up)


def _moe_kernel(be_ref, nu_ref, xp_ref, wg_ref, wu_ref, wd_ref, bg_ref, bu_ref, bd_ref, yp_ref,
                x2_ref, acc_ref, *, bm, nj):
    blk = pl.program_id(0)
    j = pl.program_id(1)

    @pl.when(blk < nu_ref[0])
    def _():
        @pl.when(j == 0)
        def _():
            for c in range(PACK_ROWS):
                lo, hi = _unpack_pair(xp_ref[pl.ds(c, bm, stride=PACK_ROWS), :])
                x2_ref[:, c * LANES:(c + 1) * LANES] = lo.astype(BF16)
                x2_ref[:, (c + PACK_ROWS) * LANES:(c + PACK_ROWS + 1) * LANES] = hi.astype(BF16)

        x = x2_ref[...]
        gate = jnp.dot(x, wg_ref[...], preferred_element_type=F32) + bg_ref[...]
        up = jnp.dot(x, wu_ref[...], preferred_element_type=F32) + bu_ref[...]
        gate = jnp.minimum(gate, LIMIT)
        up = jnp.clip(up, -LIMIT, LIMIT)
        act = (gate * jax.nn.sigmoid(ALPHA * gate) * (up + 1.0)).astype(BF16)
        part = jnp.dot(act, wd_ref[...], preferred_element_type=F32)

        @pl.when(j == 0)
        def _():
            acc_ref[...] = part + bd_ref[...]

        @pl.when((j > 0) & (j < nj - 1))
        def _():
            acc_ref[...] += part

        @pl.when(j == nj - 1)
        def _():
            for c in range(PACK_ROWS):
                hi = c + PACK_ROWS
                yp_ref[pl.ds(c, bm, stride=PACK_ROWS), :] = _pack_pair(
                    acc_ref[:, c * LANES:(c + 1) * LANES] + part[:, c * LANES:(c + 1) * LANES],
                    acc_ref[:, hi * LANES:(hi + 1) * LANES] + part[:, hi * LANES:(hi + 1) * LANES])


def _moe(block_e, n_used, xg, wg, wu, wd, bg, bu, bd, n_blocks, bm):
    th = 1024
    nj = DE // th

    def wcol(blk, j, be, nu):
        return (be[blk], 0, jnp.where(blk < nu[0], j, nj - 1))

    def wrow(blk, j, be, nu):
        return (be[blk], jnp.where(blk < nu[0], j, nj - 1), 0)

    def xrow(blk, j, be, nu):
        return (jnp.minimum(blk, nu[0] - 1), 0)

    grid_spec = pltpu.PrefetchScalarGridSpec(
        num_scalar_prefetch=2, grid=(n_blocks, nj),
        in_specs=[
            pl.BlockSpec((bm * PACK_ROWS, LANES), xrow),
            pl.BlockSpec((None, D, th), wcol), pl.BlockSpec((None, D, th), wcol),
            pl.BlockSpec((None, th, D), wrow),
            pl.BlockSpec((None, 1, th), wcol), pl.BlockSpec((None, 1, th), wcol),
            pl.BlockSpec((None, 1, D), lambda blk, j, be, nu: (be[blk], 0, 0)),
        ],
        out_specs=pl.BlockSpec((bm * PACK_ROWS, LANES), xrow),
        scratch_shapes=[pltpu.VMEM((bm, D), BF16), pltpu.VMEM((bm, D), F32)],
    )
    return pl.pallas_call(
        functools.partial(_moe_kernel, bm=bm, nj=nj), grid_spec=grid_spec,
        out_shape=jax.ShapeDtypeStruct((n_blocks * bm * PACK_ROWS, LANES), U32),
        compiler_params=_params(("arbitrary", "arbitrary")),
        name="moe_experts",
    )(block_e, n_used, xg, wg, wu, wd, bg, bu, bd)


def _combine_kernel(dcur_ref, dnxt_ref, yp_ref, h_ref, tw_ref, gf_ref, o_ref, buf_ref, sum_ref, sem, *, tt, nt):
    i = pl.program_id(0)
    slot = i % 2
    rows = tt * PACK_ROWS

    def row_copy(d_ref, k, t, s):
        return pltpu.make_async_copy(
            yp_ref.at[pl.ds(pl.multiple_of(d_ref[k, t] * PACK_ROWS, PACK_ROWS), PACK_ROWS)],
            buf_ref.at[s, pl.ds((k * tt + t) * PACK_ROWS, PACK_ROWS)], sem.at[s])

    def issue(d_ref, s):
        def body(t, c):
            for k in range(TOPK):
                row_copy(d_ref, k, t, s).start()
            return c
        lax.fori_loop(0, tt, body, 0, unroll=8)

    @pl.when(i == 0)
    def _():
        issue(dcur_ref, 0)

    @pl.when(i + 1 < nt)
    def _():
        issue(dnxt_ref, 1 - slot)

    def wait(t, c):
        for k in range(TOPK):
            row_copy(dcur_ref, k, t, slot).wait()
        return c

    lax.fori_loop(0, tt, wait, 0, unroll=8)

    tw = tw_ref[...]
    lo_sum = jnp.zeros((rows, LANES), F32)
    hi_sum = jnp.zeros((rows, LANES), F32)
    for k in range(TOPK):
        lo, hi = _unpack_pair(buf_ref[slot, k * rows:(k + 1) * rows, :])
        lo_sum = lo_sum + lo * tw[:, k:k + 1]
        hi_sum = hi_sum + hi * tw[:, k:k + 1]
    sum_ref[0] = lo_sum
    sum_ref[1] = hi_sum
    for c in range(PACK_ROWS):
        hi = c + PACK_ROWS
        o_ref[:, c * LANES:(c + 1) * LANES] = (
            h_ref[:, c * LANES:(c + 1) * LANES] + sum_ref[0, pl.ds(c, tt, stride=PACK_ROWS), :])
        o_ref[:, hi * LANES:(hi + 1) * LANES] = (
            h_ref[:, hi * LANES:(hi + 1) * LANES] + sum_ref[1, pl.ds(c, tt, stride=PACK_ROWS), :])
    o_ref[...] = _rms(o_ref[...], gf_ref[...])


def _combine(dest, yp, h, tw_rows, g_final, n):
    tt = min(n, 256)
    nt = n // tt
    grid_spec = pltpu.PrefetchScalarGridSpec(
        num_scalar_prefetch=0, grid=(nt,),
        in_specs=[
            pl.BlockSpec((TOPK, tt), lambda i: (0, i), memory_space=pltpu.SMEM),
            pl.BlockSpec((TOPK, tt), lambda i: (0, jnp.minimum(i + 1, nt - 1)), memory_space=pltpu.SMEM),
            pl.BlockSpec(memory_space=pl.ANY),
            pl.BlockSpec((tt, D), lambda i: (i, 0)),
            pl.BlockSpec((tt * PACK_ROWS, TOPK), lambda i: (i, 0)),
            pl.BlockSpec((1, D), lambda i: (0, 0)),
        ],
        out_specs=pl.BlockSpec((tt, D), lambda i: (i, 0)),
        scratch_shapes=[pltpu.VMEM((2, TOPK * tt * PACK_ROWS, LANES), U32),
                        pltpu.VMEM((2, tt * PACK_ROWS, LANES), F32), pltpu.SemaphoreType.DMA((2,))],
    )
    return pl.pallas_call(
        functools.partial(_combine_kernel, tt=tt, nt=nt), grid_spec=grid_spec,
        out_shape=jax.ShapeDtypeStruct((n, D), F32),
        compiler_params=_params(("arbitrary",)),
        name="combine_final_norm",
    )(dest, dest, yp, h, tw_rows, g_final)


def _permute_weights(w_in, w_uq, w_ukv, w_o_swa):
    widths = (HQ * DH, HKV * DH, HKV * DH, QR, KVR, DR, 2 * D)
    offs = [0]
    for w in widths:
        offs.append(offs[-1] + w)
    wq, wk, wv, wdq, wdkv, wkr, wgates = [w_in[:, offs[i]:offs[i + 1]] for i in range(7)]
    wq = wq.reshape(D, HKV, G, DH).transpose(0, 2, 1, 3).reshape(D, HQ * DH)
    pad = jnp.zeros((D, PROJ_COLS - COL_KR - DR), w_in.dtype)
    w_in_p = jnp.concatenate([wgates, wq, wdq, wdkv, wk, wv, wkr, pad], axis=1).astype(BF16)
    wo_a = w_o_swa.reshape(HKV, G, DH, D).transpose(1, 0, 2, 3).reshape(HQ * DH, D).astype(BF16)
    wuq = w_uq.reshape(QR, MH, DN + DR)
    wq_p = jnp.concatenate([wuq, jnp.zeros((QR, MH, 2 * LANES - DN - DR), w_uq.dtype)], axis=2)
    wq_p = wq_p.reshape(QR, MH * 2 * LANES).astype(BF16)
    wukv = w_ukv.reshape(KVR, MH, DN + DV)
    wk_p = wukv[:, :, :DN].reshape(KVR, MH * DN).astype(BF16)
    wv_p = wukv[:, :, DN:].reshape(KVR, MH * DV).astype(BF16)
    return w_in_p, wo_a, wq_p, wk_p, wv_p


def kernel(x, positions, g_attn, w_in, b_gate, sinks, g_q, w_uq, g_kv, w_ukv, w_o_swa, w_o_mla, w_out,
           g_ffn, w_router, b_router, w_gate_up, b_gate_up, w_down, b_down, g_final):
    b, s, _ = x.shape
    n = b * s
    assert w_in.shape[0] == 1, "single-layer block"
    x2 = x.reshape(n, D)

    w_in_p, wo_a, wq_p, wk_p, wv_p = _permute_weights(w_in[0], w_uq[0], w_ukv[0], w_o_swa[0])
    cos, sin = _rope_tables(positions, n)
    proj = _in_proj(x2, g_attn, w_in_p, n)
    qa, ka, qm, km, vm = _prep(proj, cos, sin, g_q, g_kv, wq_p, wk_p, wv_p, n)
    out_a = _swa(sinks[0], qa, ka, proj, b, s)
    out_b = _mla(qm, km, vm, b, s)

    mixed = _mix(out_a, out_b, proj, wo_a, w_o_mla[0].astype(BF16), b_gate, n)
    h, hnp, top_e, top_w, rank, cnt = _router(
        mixed, x2, w_out[0].astype(BF16), g_ffn, w_router[0].T, b_router[0][:, None], n)

    bm = 512
    n_blocks = -(-(n * TOPK) // bm) + E
    counts = cnt[:, 0].astype(jnp.int32)
    padded = (counts + bm - 1) // bm * bm
    pend = jnp.cumsum(padded)
    pstart = pend - padded
    n_used = pend[-1] // bm
    blocks = jnp.arange(n_blocks, dtype=jnp.int32)
    block_e = jnp.minimum(jnp.sum(pend[None, :] <= (blocks * bm)[:, None], axis=1), E - 1).astype(jnp.int32)
    block_e = jnp.where(blocks < n_used, block_e, block_e[jnp.maximum(n_used - 1, 0)])

    dest = _dest(pstart, top_e, rank, n)
    xg = _dispatch(counts, padded, pstart, dest, hnp, n, n_blocks * bm)

    wg, wu = _deinterleave(w_gate_up[0])
    bgu = b_gate_up[0]
    yp = _moe(block_e, n_used.reshape(1).astype(jnp.int32), xg, wg, wu, w_down[0].astype(BF16),
              bgu[:, None, 0::2], bgu[:, None, 1::2], b_down[0][:, None, :], n_blocks, bm)

    tw_rows = jnp.repeat(top_w.T, PACK_ROWS, axis=0)
    out = _combine(dest, yp, h, tw_rows, g_final[None, :], n)
    return out.reshape(b, s, D)
```

```python
import functools

import jax
import jax.numpy as jnp
from jax import lax
from jax.experimental import pallas as pl
from jax.experimental.pallas import tpu as pltpu

F32 = jnp.float32
BF16 = jnp.bfloat16
U32 = jnp.uint32

D = 2048
HQ, HKV, DH, WIN = 16, 2, 64, 128
G = HQ // HKV
MH, QR, KVR, DN, DR, DV = 8, 512, 256, 128, 64, 128
THETA = 10000.0
E, TOPK, DE = 32, 4, 2048
LIMIT, ALPHA = 7.0, 1.702
EPS = 1e-6
NEG = -1e30

LANES = 128
CHUNKS = D // LANES
PACK_ROWS = CHUNKS // 2
VMEM_LIMIT = 56 * 1024 * 1024

PROJ_COLS = 6400
COL_GATES, COL_Q, COL_DQ, COL_DKV, COL_K, COL_V, COL_KR = 0, 4096, 5120, 5632, 5888, 6016, 6144

NT = (((1,), (1,)), ((), ()))


def _params(sem, vmem=VMEM_LIMIT):
    return pltpu.CompilerParams(dimension_semantics=sem, vmem_limit_bytes=vmem)


def _rms(x, g):
    return x * lax.rsqrt(jnp.mean(x * x, axis=-1, keepdims=True) + EPS) * g


def _pack_pair(lo, hi):
    lo_bits = lax.bitcast_convert_type(lo.astype(BF16).astype(F32), U32)
    hi_bits = lax.bitcast_convert_type(hi.astype(BF16).astype(F32), U32)
    return (lo_bits >> 16) | hi_bits


def _unpack_pair(w):
    lo = lax.bitcast_convert_type(w << 16, F32)
    hi = lax.bitcast_convert_type(w & jnp.uint32(0xFFFF0000), F32)
    return lo, hi


def _rope128(v, cos, sin):
    lane = lax.broadcasted_iota(jnp.int32, v.shape, 1)
    rot = jnp.where((lane % DH) < (DH // 2), -pltpu.roll(v, LANES - DH // 2, 1), pltpu.roll(v, DH // 2, 1))
    return v * cos + rot * sin


def _rope_table_kernel(pos_ref, inv_ref, cos_ref, sin_ref):
    ang = pos_ref[...].astype(F32) * inv_ref[...]
    cos_ref[...] = jnp.cos(ang)
    sin_ref[...] = jnp.sin(ang)


def _rope_tables(positions, n):
    half = DH // 2
    per_row = LANES // half
    inv = jnp.power(THETA, -jnp.arange(half, dtype=F32) * 2.0 / DH)
    inv_row = jnp.tile(inv, per_row)[None, :]
    pos_rows = jnp.repeat(positions.reshape(n // per_row, per_row), half, axis=1)
    rows = n // per_row
    tr = min(rows, 1024)
    cos, sin = pl.pallas_call(
        _rope_table_kernel,
        grid=(rows // tr,),
        in_specs=[pl.BlockSpec((tr, LANES), lambda i: (i, 0)), pl.BlockSpec((1, LANES), lambda i: (0, 0))],
        out_specs=[pl.BlockSpec((tr, LANES), lambda i: (i, 0))] * 2,
        out_shape=[jax.ShapeDtypeStruct((rows, LANES), F32)] * 2,
        name="rope_tables",
    )(pos_rows, inv_row)
    cos = jnp.tile(cos.reshape(n, half), (1, per_row))
    sin = jnp.tile(sin.reshape(n, half), (1, per_row))
    return cos, sin


def _in_proj_kernel(x_ref, g_ref, w_ref, o_ref, xn_ref):
    @pl.when(pl.program_id(1) == 0)
    def _():
        xn_ref[...] = _rms(x_ref[...], g_ref[...]).astype(BF16)

    o_ref[...] = jnp.dot(xn_ref[...], w_ref[...], preferred_element_type=F32).astype(o_ref.dtype)


def _in_proj(x2, g_attn, w_in_p, n):
    tm = min(n, 1024)
    tn = 1280
    return pl.pallas_call(
        _in_proj_kernel,
        grid=(n // tm, PROJ_COLS // tn),
        in_specs=[
            pl.BlockSpec((tm, D), lambda i, j: (i, 0)),
            pl.BlockSpec((1, D), lambda i, j: (0, 0)),
            pl.BlockSpec((D, tn), lambda i, j: (0, j)),
        ],
        out_specs=pl.BlockSpec((tm, tn), lambda i, j: (i, j)),
        out_shape=jax.ShapeDtypeStruct((n, PROJ_COLS), BF16),
        scratch_shapes=[pltpu.VMEM((tm, D), BF16)],
        compiler_params=_params(("arbitrary", "arbitrary")),
        name="in_proj",
    )(x2, g_attn, w_in_p)


def _prep_kernel(q_ref, dq_ref, dkv_ref, k_ref, kr_ref, cos_ref, sin_ref, gq_ref, gkv_ref,
                 wq_ref, wk_ref, wv_ref, qa_ref, ka_ref, qm_ref, km_ref, vm_ref):
    cos = cos_ref[...]
    sin = sin_ref[...]
    swa_scale = DH ** -0.5
    for c in range(HQ * DH // LANES):
        sl = slice(c * LANES, (c + 1) * LANES)
        qa_ref[:, sl] = (_rope128(q_ref[:, sl].astype(F32), cos, sin) * swa_scale).astype(BF16)
    ka_ref[...] = _rope128(k_ref[...].astype(F32), cos, sin).astype(BF16)

    mla_scale = (DN + DR) ** -0.5
    cq = _rms(dq_ref[...].astype(F32), gq_ref[...]).astype(BF16)
    qb = jnp.dot(cq, wq_ref[...], preferred_element_type=F32)
    ckv = _rms(dkv_ref[...].astype(F32), gkv_ref[...]).astype(BF16)
    kn = jnp.dot(ckv, wk_ref[...], preferred_element_type=F32)
    vm_ref[...] = jnp.dot(ckv, wv_ref[...], preferred_element_type=F32).astype(BF16)
    kr = _rope128(kr_ref[...].astype(F32), cos, sin).astype(BF16)
    for h in range(MH):
        lo = h * 2 * LANES
        qm_ref[:, lo:lo + LANES] = (qb[:, lo:lo + LANES] * mla_scale).astype(BF16)
        qm_ref[:, lo + LANES:lo + 2 * LANES] = (
            _rope128(qb[:, lo + LANES:lo + 2 * LANES], cos, sin) * mla_scale).astype(BF16)
        km_ref[:, lo:lo + LANES] = kn[:, h * LANES:(h + 1) * LANES].astype(BF16)
        km_ref[:, lo + LANES:lo + 2 * LANES] = kr


def _prep(proj, cos, sin, g_q, g_kv, wq_p, wk_p, wv_p, n):
    tm = min(n, 512)
    row = lambda w, cb: pl.BlockSpec((tm, w), lambda i: (i, cb))
    full = lambda a: pl.BlockSpec(a.shape, lambda i: (0, 0))
    return pl.pallas_call(
        _prep_kernel,
        grid=(n // tm,),
        in_specs=[
            row(HQ * DH, COL_Q // (HQ * DH)), row(QR, COL_DQ // QR), row(KVR, COL_DKV // KVR),
            row(LANES, COL_K // LANES), row(LANES, COL_KR // LANES),
            row(LANES, 0), row(LANES, 0), full(g_q), full(g_kv), full(wq_p), full(wk_p), full(wv_p),
        ],
        out_specs=[row(HQ * DH, 0), row(LANES, 0), row(MH * 2 * LANES, 0), row(MH * 2 * LANES, 0),
                   row(MH * DV, 0)],
        out_shape=[
            jax.ShapeDtypeStruct((n, HQ * DH), BF16), jax.ShapeDtypeStruct((n, LANES), BF16),
            jax.ShapeDtypeStruct((n, MH * 2 * LANES), BF16), jax.ShapeDtypeStruct((n, MH * 2 * LANES), BF16),
            jax.ShapeDtypeStruct((n, MH * DV), BF16),
        ],
        compiler_params=_params(("arbitrary",)),
        name="rope_mla_prep",
    )(proj, proj, proj, proj, proj, cos, sin, g_q, g_kv, wq_p, wk_p, wv_p)


def _swa_kernel(sinks_ref, q_ref, kc_ref, kp_ref, vc_ref, vp_ref, o_ref):
    blk = pl.program_id(1)
    q2 = jnp.concatenate([q_ref[:, g * LANES:(g + 1) * LANES] for g in range(G)], axis=0)
    k2 = jnp.concatenate([kp_ref[...], kc_ref[...]], axis=0)
    v2t = jnp.concatenate([vp_ref[...], vc_ref[...]], axis=0).astype(F32).T
    lane = lax.broadcasted_iota(jnp.int32, (2 * WIN, LANES), 1)
    vrow = lax.broadcasted_iota(jnp.int32, (LANES, 2 * WIN), 0)
    kj = lax.broadcasted_iota(jnp.int32, (2 * WIN, WIN), 0)
    qi = lax.broadcasted_iota(jnp.int32, (2 * WIN, WIN), 1)
    valid = (kj > qi) & (kj <= qi + WIN) & ((kj >= WIN) | (blk > 0))
    acc = jnp.zeros((LANES, G * WIN), F32)
    for h in range(HKV):
        kz = jnp.where((lane >= h * DH) & (lane < (h + 1) * DH), k2, jnp.zeros_like(k2))
        vz = jnp.where((vrow >= h * DH) & (vrow < (h + 1) * DH), v2t, 0.0).astype(BF16)
        s = lax.dot_general(kz, q2, NT, preferred_element_type=F32)
        ps, invs = [], []
        for g in range(G):
            sg = jnp.where(valid, s[:, g * WIN:(g + 1) * WIN], NEG)
            sink = sinks_ref[h * G + g]
            m = jnp.maximum(jnp.max(sg, axis=0, keepdims=True), sink)
            p = jnp.exp(sg - m)
            l = jnp.sum(p, axis=0, keepdims=True) + jnp.exp(sink - m)
            ps.append(p.astype(BF16))
            invs.append(1.0 / l)
        o_h = jnp.dot(vz, jnp.concatenate(ps, axis=1), preferred_element_type=F32)
        acc = acc + o_h * jnp.concatenate(invs, axis=1)
    for g in range(G):
        o_ref[:, g * LANES:(g + 1) * LANES] = acc[:, g * WIN:(g + 1) * WIN].T.astype(BF16)


def _swa(sinks, qa, ka, proj, b, s):
    nb = s // WIN
    cur = lambda cb: pl.BlockSpec((WIN, LANES), lambda bi, n, sk: (bi * nb + n, cb))
    prev = lambda cb: pl.BlockSpec((WIN, LANES), lambda bi, n, sk: (bi * nb + jnp.maximum(n - 1, 0), cb))
    grid_spec = pltpu.PrefetchScalarGridSpec(
        num_scalar_prefetch=1,
        grid=(b, nb),
        in_specs=[
            pl.BlockSpec((WIN, HQ * DH), lambda bi, n, sk: (bi * nb + n, 0)),
            cur(0), prev(0), cur(COL_V // LANES), prev(COL_V // LANES),
        ],
        out_specs=pl.BlockSpec((WIN, HQ * DH), lambda bi, n, sk: (bi * nb + n, 0)),
    )
    return pl.pallas_call(
        _swa_kernel,
        grid_spec=grid_spec,
        out_shape=jax.ShapeDtypeStruct((b * s, HQ * DH), BF16),
        compiler_params=_params(("arbitrary", "arbitrary")),
        name="swa_attention",
    )(sinks, qa, ka, ka, proj, proj)


def _mla_kernel(q_ref, k_ref, v_ref, o_ref, *, seq, tq):
    for i in range(seq // tq):
        kv = (i + 1) * tq
        q = q_ref[i * tq:(i + 1) * tq, :]
        s = lax.dot_general(q, k_ref[0:kv, :], NT, preferred_element_type=F32)
        row = lax.broadcasted_iota(jnp.int32, (tq, kv), 0) + i * tq
        col = lax.broadcasted_iota(jnp.int32, (tq, kv), 1)
        s = jnp.where(col <= row, s, NEG)
        m = jnp.max(s, axis=-1, keepdims=True)
        p = jnp.exp(s - m)
        l = jnp.sum(p, axis=-1, keepdims=True)
        o = jnp.dot(p.astype(BF16), v_ref[0:kv, :], preferred_element_type=F32) * (1.0 / l)
        o_ref[i * tq:(i + 1) * tq, :] = o.astype(BF16)


def _mla(qm, km, vm, b, s):
    tq = min(s, 256)
    return pl.pallas_call(
        functools.partial(_mla_kernel, seq=s, tq=tq),
        grid=(b, MH),
        in_specs=[
            pl.BlockSpec((s, 2 * LANES), lambda bi, h: (bi, h)),
            pl.BlockSpec((s, 2 * LANES), lambda bi, h: (bi, h)),
            pl.BlockSpec((s, DV), lambda bi, h: (bi, h)),
        ],
        out_specs=pl.BlockSpec((s, DV), lambda bi, h: (bi, h)),
        out_shape=jax.ShapeDtypeStruct((b * s, MH * DV), BF16),
        compiler_params=_params(("arbitrary", "arbitrary")),
        name="mla_attention",
    )(qm, km, vm)


def _mix_kernel(oa_ref, ob_ref, ga_ref, gb_ref, wa_ref, wb_ref, ba_ref, bb_ref, o_ref):
    ya = jnp.dot(oa_ref[...], wa_ref[...], preferred_element_type=F32)
    yb = jnp.dot(ob_ref[...], wb_ref[...], preferred_element_type=F32)
    ga = jax.nn.sigmoid(ga_ref[...].astype(F32) + ba_ref[...])
    gb = jax.nn.sigmoid(gb_ref[...].astype(F32) + bb_ref[...])
    o_ref[...] = (ga * ya + gb * yb).astype(BF16)


def _mix(out_a, out_b, proj, wa, wb, b_gate, n):
    tm = min(n, 1024)
    tn = 1024
    nj = D // tn
    return pl.pallas_call(
        _mix_kernel,
        grid=(n // tm, nj),
        in_specs=[
            pl.BlockSpec((tm, HQ * DH), lambda i, j: (i, 0)),
            pl.BlockSpec((tm, MH * DV), lambda i, j: (i, 0)),
            pl.BlockSpec((tm, tn), lambda i, j: (i, j)),
            pl.BlockSpec((tm, tn), lambda i, j: (i, nj + j)),
            pl.BlockSpec((HQ * DH, tn), lambda i, j: (0, j)),
            pl.BlockSpec((MH * DV, tn), lambda i, j: (0, j)),
            pl.BlockSpec((1, tn), lambda i, j: (0, j)),
            pl.BlockSpec((1, tn), lambda i, j: (0, nj + j)),
        ],
        out_specs=pl.BlockSpec((tm, tn), lambda i, j: (i, j)),
        out_shape=jax.ShapeDtypeStruct((n, D), BF16),
        compiler_params=_params(("arbitrary", "arbitrary")),
        name="gated_mix",
    )(out_a, out_b, proj, proj, wa, wb, b_gate, b_gate)


def _router_kernel(mx_ref, x_ref, wo_ref, gf_ref, wr_ref, br_ref,
                   h_ref, hnp_ref, te_ref, tw_ref, rk_ref, cnt_ref, base_ref, *, tm):
    @pl.when(pl.program_id(0) == 0)
    def _():
        base_ref[...] = jnp.zeros_like(base_ref)

    h = x_ref[...] + jnp.dot(mx_ref[...], wo_ref[...], preferred_element_type=F32)
    h_ref[...] = h
    hn = _rms(h, gf_ref[...])
    for c in range(PACK_ROWS):
        hi = c + PACK_ROWS
        hnp_ref[pl.ds(c, tm, stride=PACK_ROWS), :] = _pack_pair(
            hn[:, c * LANES:(c + 1) * LANES], hn[:, hi * LANES:(hi + 1) * LANES])

    logits = lax.dot_general(wr_ref[...], hn, NT, preferred_element_type=F32,
                             precision=lax.Precision.HIGHEST) + br_ref[...]
    eidx = lax.broadcasted_iota(jnp.int32, (E, tm), 0)
    vals, idxs = [], []
    l = logits
    for _ in range(TOPK):
        m = jnp.max(l, axis=0, keepdims=True)
        idx = jnp.min(jnp.where(l == m, eidx, E), axis=0, keepdims=True)
        vals.append(m)
        idxs.append(idx)
        l = jnp.where(eidx == idx, -jnp.inf, l)
    ex = [jnp.exp(v - vals[0]) for v in vals]
    tot = ex[0] + ex[1] + ex[2] + ex[3]
    tw_ref[...] = jnp.concatenate([e / tot for e in ex], axis=0)
    te_ref[...] = jnp.concatenate(idxs, axis=0)

    onehot = jnp.zeros((E, tm), F32)
    for idx in idxs:
        onehot = onehot + jnp.where(eidx == idx, 1.0, 0.0)
    r = lax.broadcasted_iota(jnp.int32, (tm, tm), 0)
    c = lax.broadcasted_iota(jnp.int32, (tm, tm), 1)
    earlier = jnp.where(r < c, 1.0, 0.0).astype(BF16)
    before = jnp.dot(onehot.astype(BF16), earlier, preferred_element_type=F32) + base_ref[:, 0:1]
    rk_ref[...] = jnp.concatenate(
        [jnp.sum(jnp.where(eidx == idx, before, 0.0), axis=0, keepdims=True) for idx in idxs],
        axis=0).astype(jnp.int32)
    base_ref[...] = base_ref[...] + jnp.sum(onehot, axis=1, keepdims=True)
    cnt_ref[...] = base_ref[...]


def _router(mixed, x2, wo, g_ffn, wr_t, b_router, n):
    tm = min(n, 512)
    row = lambda w: pl.BlockSpec((tm, w), lambda i: (i, 0))
    full = lambda a: pl.BlockSpec(a.shape, lambda i: (0, 0), pipeline_mode=pl.Buffered(1))
    tok = pl.BlockSpec((TOPK, tm), lambda i: (0, i))
    return pl.pallas_call(
        functools.partial(_router_kernel, tm=tm),
        grid=(n // tm,),
        in_specs=[row(D), row(D), full(wo), full(g_ffn), full(wr_t), full(b_router)],
        out_specs=[
            row(D), pl.BlockSpec((tm * PACK_ROWS, LANES), lambda i: (i, 0)), tok, tok, tok,
            pl.BlockSpec((E, LANES), lambda i: (0, 0)),
        ],
        out_shape=[
            jax.ShapeDtypeStruct((n, D), F32), jax.ShapeDtypeStruct((n * PACK_ROWS, LANES), U32),
            jax.ShapeDtypeStruct((TOPK, n), jnp.int32), jax.ShapeDtypeStruct((TOPK, n), F32),
            jax.ShapeDtypeStruct((TOPK, n), jnp.int32), jax.ShapeDtypeStruct((E, LANES), F32),
        ],
        scratch_shapes=[pltpu.VMEM((E, LANES), F32)],
        compiler_params=_params(("arbitrary",)),
        name="outproj_router",
    )(mixed, x2, wo, g_ffn, wr_t, b_router)


def _dest_kernel(pstart_ref, te_ref, rk_ref, dest_ref):
    te = te_ref[...]
    d = rk_ref[...]
    for e in range(E):
        d = d + jnp.where(te == e, pstart_ref[e], 0)
    dest_ref[...] = d


def _dest(pstart, top_e, rank, n):
    tn = min(n, 8192)
    grid_spec = pltpu.PrefetchScalarGridSpec(
        num_scalar_prefetch=1, grid=(n // tn,),
        in_specs=[pl.BlockSpec((TOPK, tn), lambda i, ps: (0, i))] * 2,
        out_specs=pl.BlockSpec((TOPK, tn), lambda i, ps: (0, i)))
    return pl.pallas_call(
        _dest_kernel, grid_spec=grid_spec,
        out_shape=jax.ShapeDtypeStruct((TOPK, n), jnp.int32), name="dest_rows",
    )(pstart, top_e, rank)


def _dispatch_kernel(cnt_ref, pad_ref, pst_ref, dest_ref, hnp_ref, xg_ref, zrow_ref, sem, zsem, *, tt):
    i = pl.program_id(0)

    def row_copy(t, dst_row):
        return pltpu.make_async_copy(
            hnp_ref.at[pl.ds(pl.multiple_of(t * PACK_ROWS, PACK_ROWS), PACK_ROWS)],
            xg_ref.at[pl.ds(pl.multiple_of(dst_row * PACK_ROWS, PACK_ROWS), PACK_ROWS)], sem)

    def zero_copy(dst_row):
        return pltpu.make_async_copy(
            zrow_ref, xg_ref.at[pl.ds(pl.multiple_of(dst_row * PACK_ROWS, PACK_ROWS), PACK_ROWS)], zsem)

    @pl.when(i == 0)
    def _():
        zrow_ref[...] = jnp.zeros_like(zrow_ref)

        def per_expert(e, carry):
            base = pst_ref[e]

            def start(r, c):
                zero_copy(base + r).start()
                return c

            def wait(r, c):
                zero_copy(base + r).wait()
                return c

            lax.fori_loop(cnt_ref[e], pad_ref[e], start, 0)
            lax.fori_loop(cnt_ref[e], pad_ref[e], wait, 0)
            return carry

        lax.fori_loop(0, E, per_expert, 0)

    def start(t, c):
        for k in range(TOPK):
            row_copy(t, dest_ref[k, t]).start()
        return c

    def wait(t, c):
        for k in range(TOPK):
            row_copy(t, dest_ref[k, t]).wait()
        return c

    lax.fori_loop(0, tt, start, 0, unroll=8)
    lax.fori_loop(0, tt, wait, 0, unroll=8)


def _dispatch(counts, padded, pstart, dest, hnp, n, p_rows):
    tt = min(n, 512)
    grid_spec = pltpu.PrefetchScalarGridSpec(
        num_scalar_prefetch=3, grid=(n // tt,),
        in_specs=[
            pl.BlockSpec((TOPK, tt), lambda i, a, b, c: (0, i), memory_space=pltpu.SMEM),
            pl.BlockSpec((tt * PACK_ROWS, LANES), lambda i, a, b, c: (i, 0)),
        ],
        out_specs=pl.BlockSpec(memory_space=pl.ANY),
        scratch_shapes=[pltpu.VMEM((PACK_ROWS, LANES), U32), pltpu.SemaphoreType.DMA(()),
                        pltpu.SemaphoreType.DMA(())],
    )
    return pl.pallas_call(
        functools.partial(_dispatch_kernel, tt=tt), grid_spec=grid_spec,
        out_shape=jax.ShapeDtypeStruct((p_rows * PACK_ROWS, LANES), U32),
        compiler_params=pltpu.CompilerParams(dimension_semantics=("arbitrary",), has_side_effects=True),
        name="dispatch_rows",
    )(counts, padded, pstart, dest, hnp)


def _deinterleave_kernel(w_ref, g_ref, u_ref):
    chunk = 2 * LANES
    r = lax.broadcasted_iota(jnp.int32, (chunk, chunk), 0)
    c = lax.broadcasted_iota(jnp.int32, (chunk, chunk), 1)
    src = jnp.where(c < LANES, 2 * c, 2 * (c - LANES) + 1)
    perm = jnp.where(r == src, 1.0, 0.0).astype(BF16)
    for k in range(w_ref.shape[1] // chunk):
        w = w_ref[:, k * chunk:(k + 1) * chunk].astype(BF16)
        o = jnp.dot(w, perm, preferred_element_type=F32)
        g_ref[:, k * LANES:(k + 1) * LANES] = o[:, :LANES].astype(BF16)
        u_ref[:, k * LANES:(k + 1) * LANES] = o[:, LANES:].astype(BF16)


def _deinterleave(w_gate_up):
    tr = 512
    out = pl.BlockSpec((None, tr, DE), lambda e, i: (e, i, 0))
    return pl.pallas_call(
        _deinterleave_kernel,
        grid=(E, D // tr),
        in_specs=[pl.BlockSpec((None, tr, 2 * DE), lambda e, i: (e, i, 0))],
        out_specs=[out, out],
        out_shape=[jax.ShapeDtypeStruct((E, D, DE), BF16)] * 2,
        compiler_params=_params(("arbitrary", "arbitrary")),
        name="split_gate_up",
    )(w_gate_up)


def _moe_kernel(be_ref, nu_ref, bv_ref, xp_ref, wg_ref, wu_ref, wd_ref, bg_ref, bu_ref, bd_ref, yp_ref,
                x2_ref, acc_ref, *, bm, nj):
    blk = pl.program_id(0)
    j = pl.program_id(1)
    valid = bv_ref[blk]

    def body(rows):
        @pl.when(j == 0)
        def _():
            for c in range(PACK_ROWS):
                lo, hi = _unpack_pair(xp_ref[pl.ds(c, rows, stride=PACK_ROWS), :])
                x2_ref[0:rows, c * LANES:(c + 1) * LANES] = lo.astype(BF16)
                x2_ref[0:rows, (c + PACK_ROWS) * LANES:(c + PACK_ROWS + 1) * LANES] = hi.astype(BF16)

        x = x2_ref[0:rows, :]
        gate = jnp.dot(x, wg_ref[...], preferred_element_type=F32) + bg_ref[...]
        up = jnp.dot(x, wu_ref[...], preferred_element_type=F32) + bu_ref[...]
        gate = jnp.minimum(gate, LIMIT)
        up = jnp.clip(up, -LIMIT, LIMIT)
        act = (gate * jax.nn.sigmoid(ALPHA * gate) * (up + 1.0)).astype(BF16)
        part = jnp.dot(act, wd_ref[...], preferred_element_type=F32)

        @pl.when(j == 0)
        def _():
            acc_ref[0:rows, :] = part + bd_ref[...]

        @pl.when((j > 0) & (j < nj - 1))
        def _():
            acc_ref[0:rows, :] += part

        @pl.when(j == nj - 1)
        def _():
            for c in range(PACK_ROWS):
                hi = c + PACK_ROWS
                yp_ref[pl.ds(c, rows, stride=PACK_ROWS), :] = _pack_pair(
                    acc_ref[0:rows, c * LANES:(c + 1) * LANES] + part[:, c * LANES:(c + 1) * LANES],
                    acc_ref[0:rows, hi * LANES:(hi + 1) * LANES] + part[:, hi * LANES:(hi + 1) * LANES])
            if rows < bm:
                yp_ref[rows * PACK_ROWS:bm * PACK_ROWS, :] = jnp.zeros(((bm - rows) * PACK_ROWS, LANES), U32)

    @pl.when(valid > bm // 2)
    def _():
        body(bm)

    @pl.when((valid > 0) & (valid <= bm // 2))
    def _():
        body(bm // 2)


def _moe(block_e, n_used, block_valid, xg, wg, wu, wd, bg, bu, bd, n_blocks, bm):
    th = 1024
    nj = DE // th

    def wcol(blk, j, be, nu, bv):
        return (be[blk], 0, jnp.where(blk < nu[0], j, nj - 1))

    def wrow(blk, j, be, nu, bv):
        return (be[blk], jnp.where(blk < nu[0], j, nj - 1), 0)

    def xrow(blk, j, be, nu, bv):
        return (jnp.minimum(blk, nu[0] - 1), 0)

    grid_spec = pltpu.PrefetchScalarGridSpec(
        num_scalar_prefetch=3, grid=(n_blocks, nj),
        in_specs=[
            pl.BlockSpec((bm * PACK_ROWS, LANES), xrow),
            pl.BlockSpec((None, D, th), wcol), pl.BlockSpec((None, D, th), wcol),
            pl.BlockSpec((None, th, D), wrow),
            pl.BlockSpec((None, 1, th), wcol), pl.BlockSpec((None, 1, th), wcol),
            pl.BlockSpec((None, 1, D), lambda blk, j, be, nu, bv: (be[blk], 0, 0)),
        ],
        out_specs=pl.BlockSpec((bm * PACK_ROWS, LANES), xrow),
        scratch_shapes=[pltpu.VMEM((bm, D), BF16), pltpu.VMEM((bm, D), F32)],
    )
    return pl.pallas_call(
        functools.partial(_moe_kernel, bm=bm, nj=nj), grid_spec=grid_spec,
        out_shape=jax.ShapeDtypeStruct((n_blocks * bm * PACK_ROWS, LANES), U32),
        compiler_params=_params(("arbitrary", "arbitrary")),
        name="moe_experts",
    )(block_e, n_used, block_valid, xg, wg, wu, wd, bg, bu, bd)


def _combine_kernel(dcur_ref, dnxt_ref, yp_ref, h_ref, tw_ref, gf_ref, o_ref, buf_ref, sum_ref, sem, *, tt, nt):
    i = pl.program_id(0)
    slot = i % 2
    rows = tt * PACK_ROWS

    def row_copy(d_ref, k, t, s):
        return pltpu.make_async_copy(
            yp_ref.at[pl.ds(pl.multiple_of(d_ref[k, t] * PACK_ROWS, PACK_ROWS), PACK_ROWS)],
            buf_ref.at[s, pl.ds((k * tt + t) * PACK_ROWS, PACK_ROWS)], sem.at[s])

    def issue(d_ref, s):
        def body(t, c):
            for k in range(TOPK):
                row_copy(d_ref, k, t, s).start()
            return c
        lax.fori_loop(0, tt, body, 0, unroll=8)

    @pl.when(i == 0)
    def _():
        issue(dcur_ref, 0)

    @pl.when(i + 1 < nt)
    def _():
        issue(dnxt_ref, 1 - slot)

    def wait(t, c):
        for k in range(TOPK):
            row_copy(dcur_ref, k, t, slot).wait()
        return c

    lax.fori_loop(0, tt, wait, 0, unroll=8)

    tw = tw_ref[...]
    lo_sum = jnp.zeros((rows, LANES), F32)
    hi_sum = jnp.zeros((rows, LANES), F32)
    for k in range(TOPK):
        lo, hi = _unpack_pair(buf_ref[slot, k * rows:(k + 1) * rows, :])
        lo_sum = lo_sum + lo * tw[:, k:k + 1]
        hi_sum = hi_sum + hi * tw[:, k:k + 1]
    sum_ref[0] = lo_sum
    sum_ref[1] = hi_sum
    for c in range(PACK_ROWS):
        hi = c + PACK_ROWS
        o_ref[:, c * LANES:(c + 1) * LANES] = (
            h_ref[:, c * LANES:(c + 1) * LANES] + sum_ref[0, pl.ds(c, tt, stride=PACK_ROWS), :])
        o_ref[:, hi * LANES:(hi + 1) * LANES] = (
            h_ref[:, hi * LANES:(hi + 1) * LANES] + sum_ref[1, pl.ds(c, tt, stride=PACK_ROWS), :])
    o_ref[...] = _rms(o_ref[...], gf_ref[...])


def _combine(dest, yp, h, tw_rows, g_final, n):
    tt = min(n, 256)
    nt = n // tt
    grid_spec = pltpu.PrefetchScalarGridSpec(
        num_scalar_prefetch=0, grid=(nt,),
        in_specs=[
            pl.BlockSpec((TOPK, tt), lambda i: (0, i), memory_space=pltpu.SMEM),
            pl.BlockSpec((TOPK, tt), lambda i: (0, jnp.minimum(i + 1, nt - 1)), memory_space=pltpu.SMEM),
            pl.BlockSpec(memory_space=pl.ANY),
            pl.BlockSpec((tt, D), lambda i: (i, 0)),
            pl.BlockSpec((tt * PACK_ROWS, TOPK), lambda i: (i, 0)),
            pl.BlockSpec((1, D), lambda i: (0, 0)),
        ],
        out_specs=pl.BlockSpec((tt, D), lambda i: (i, 0)),
        scratch_shapes=[pltpu.VMEM((2, TOPK * tt * PACK_ROWS, LANES), U32),
                        pltpu.VMEM((2, tt * PACK_ROWS, LANES), F32), pltpu.SemaphoreType.DMA((2,))],
    )
    return pl.pallas_call(
        functools.partial(_combine_kernel, tt=tt, nt=nt), grid_spec=grid_spec,
        out_shape=jax.ShapeDtypeStruct((n, D), F32),
        compiler_params=_params(("arbitrary",)),
        name="combine_final_norm",
    )(dest, dest, yp, h, tw_rows, g_final)


def _permute_weights(w_in, w_uq, w_ukv, w_o_swa):
    widths = (HQ * DH, HKV * DH, HKV * DH, QR, KVR, DR, 2 * D)
    offs = [0]
    for w in widths:
        offs.append(offs[-1] + w)
    wq, wk, wv, wdq, wdkv, wkr, wgates = [w_in[:, offs[i]:offs[i + 1]] for i in range(7)]
    wq = wq.reshape(D, HKV, G, DH).transpose(0, 2, 1, 3).reshape(D, HQ * DH)
    pad = jnp.zeros((D, PROJ_COLS - COL_KR - DR), w_in.dtype)
    w_in_p = jnp.concatenate([wgates, wq, wdq, wdkv, wk, wv, wkr, pad], axis=1).astype(BF16)
    wo_a = w_o_swa.reshape(HKV, G, DH, D).transpose(1, 0, 2, 3).reshape(HQ * DH, D).astype(BF16)
    wuq = w_uq.reshape(QR, MH, DN + DR)
    wq_p = jnp.concatenate([wuq, jnp.zeros((QR, MH, 2 * LANES - DN - DR), w_uq.dtype)], axis=2)
    wq_p = wq_p.reshape(QR, MH * 2 * LANES).astype(BF16)
    wukv = w_ukv.reshape(KVR, MH, DN + DV)
    wk_p = wukv[:, :, :DN].reshape(KVR, MH * DN).astype(BF16)
    wv_p = wukv[:, :, DN:].reshape(KVR, MH * DV).astype(BF16)
    return w_in_p, wo_a, wq_p, wk_p, wv_p


def kernel(x, positions, g_attn, w_in, b_gate, sinks, g_q, w_uq, g_kv, w_ukv, w_o_swa, w_o_mla, w_out,
           g_ffn, w_router, b_router, w_gate_up, b_gate_up, w_down, b_down, g_final):
    b, s, _ = x.shape
    n = b * s
    assert w_in.shape[0] == 1, "single-layer block"
    x2 = x.reshape(n, D)

    w_in_p, wo_a, wq_p, wk_p, wv_p = _permute_weights(w_in[0], w_uq[0], w_ukv[0], w_o_swa[0])
    cos, sin = _rope_tables(positions, n)
    proj = _in_proj(x2, g_attn, w_in_p, n)
    qa, ka, qm, km, vm = _prep(proj, cos, sin, g_q, g_kv, wq_p, wk_p, wv_p, n)
    out_a = _swa(sinks[0], qa, ka, proj, b, s)
    out_b = _mla(qm, km, vm, b, s)

    mixed = _mix(out_a, out_b, proj, wo_a, w_o_mla[0].astype(BF16), b_gate, n)
    h, hnp, top_e, top_w, rank, cnt = _router(
        mixed, x2, w_out[0].astype(BF16), g_ffn, w_router[0].T, b_router[0][:, None], n)

    bm = 512
    n_blocks = -(-(n * TOPK) // bm) + E
    counts = cnt[:, 0].astype(jnp.int32)
    padded = (counts + bm - 1) // bm * bm
    pend = jnp.cumsum(padded)
    pstart = pend - padded
    n_used = pend[-1] // bm
    blocks = jnp.arange(n_blocks, dtype=jnp.int32)
    block_e = jnp.minimum(jnp.sum(pend[None, :] <= (blocks * bm)[:, None], axis=1), E - 1).astype(jnp.int32)
    block_valid = jnp.clip(counts[block_e] - (blocks * bm - pstart[block_e]), 0, bm).astype(jnp.int32)
    block_valid = jnp.where(blocks < n_used, block_valid, 0)
    block_e = jnp.where(blocks < n_used, block_e, block_e[jnp.maximum(n_used - 1, 0)])

    dest = _dest(pstart, top_e, rank, n)
    xg = _dispatch(counts, padded, pstart, dest, hnp, n, n_blocks * bm)

    wg, wu = _deinterleave(w_gate_up[0])
    bgu = b_gate_up[0]
    yp = _moe(block_e, n_used.reshape(1).astype(jnp.int32), block_valid, xg, wg, wu, w_down[0].astype(BF16),
              bgu[:, None, 0::2], bgu[:, None, 1::2], b_down[0][:, None, :], n_blocks, bm)

    tw_rows = jnp.repeat(top_w.T, PACK_ROWS, axis=0)
    out = _combine(dest, yp, h, tw_rows, g_final[None, :], n)
    return out.reshape(b, s, D)
```

```python
import functools

import jax
import jax.numpy as jnp
from jax import lax
from jax.experimental import pallas as pl
from jax.experimental.pallas import tpu as pltpu

F32 = jnp.float32
BF16 = jnp.bfloat16
U32 = jnp.uint32

D = 2048
HQ, HKV, DH, WIN = 16, 2, 64, 128
G = HQ // HKV
MH, QR, KVR, DN, DR, DV = 8, 512, 256, 128, 64, 128
THETA = 10000.0
E, TOPK, DE = 32, 4, 2048
LIMIT, ALPHA = 7.0, 1.702
EPS = 1e-6
NEG = -1e30

LANES = 128
CHUNKS = D // LANES
PACK_ROWS = CHUNKS // 2
VMEM_LIMIT = 56 * 1024 * 1024

PROJ_COLS = 6400
COL_GATES, COL_Q, COL_DQ, COL_DKV, COL_K, COL_V, COL_KR = 0, 4096, 5120, 5632, 5888, 6016, 6144

NT = (((1,), (1,)), ((), ()))


def _params(sem, vmem=VMEM_LIMIT):
    return pltpu.CompilerParams(dimension_semantics=sem, vmem_limit_bytes=vmem)


def _rms(x, g):
    return x * lax.rsqrt(jnp.mean(x * x, axis=-1, keepdims=True) + EPS) * g


def _pack_pair(lo, hi):
    lo_bits = lax.bitcast_convert_type(lo.astype(BF16).astype(F32), U32)
    hi_bits = lax.bitcast_convert_type(hi.astype(BF16).astype(F32), U32)
    return (lo_bits >> 16) | hi_bits


def _unpack_pair(w):
    lo = lax.bitcast_convert_type(w << 16, F32)
    hi = lax.bitcast_convert_type(w & jnp.uint32(0xFFFF0000), F32)
    return lo, hi


def _rope128(v, cos, sin):
    lane = lax.broadcasted_iota(jnp.int32, v.shape, 1)
    rot = jnp.where((lane % DH) < (DH // 2), -pltpu.roll(v, LANES - DH // 2, 1), pltpu.roll(v, DH // 2, 1))
    return v * cos + rot * sin


def _rope_table_kernel(pos_ref, inv_ref, cos_ref, sin_ref):
    ang = pos_ref[...].astype(F32) * inv_ref[...]
    cos_ref[...] = jnp.cos(ang)
    sin_ref[...] = jnp.sin(ang)


def _rope_tables(positions, n):
    half = DH // 2
    per_row = LANES // half
    inv = jnp.power(THETA, -jnp.arange(half, dtype=F32) * 2.0 / DH)
    inv_row = jnp.tile(inv, per_row)[None, :]
    pos_rows = jnp.repeat(positions.reshape(n // per_row, per_row), half, axis=1)
    rows = n // per_row
    tr = min(rows, 1024)
    cos, sin = pl.pallas_call(
        _rope_table_kernel,
        grid=(rows // tr,),
        in_specs=[pl.BlockSpec((tr, LANES), lambda i: (i, 0)), pl.BlockSpec((1, LANES), lambda i: (0, 0))],
        out_specs=[pl.BlockSpec((tr, LANES), lambda i: (i, 0))] * 2,
        out_shape=[jax.ShapeDtypeStruct((rows, LANES), F32)] * 2,
        name="rope_tables",
    )(pos_rows, inv_row)
    cos = jnp.tile(cos.reshape(n, half), (1, per_row))
    sin = jnp.tile(sin.reshape(n, half), (1, per_row))
    return cos, sin


def _in_proj_kernel(x_ref, g_ref, w_ref, o_ref, xn_ref):
    @pl.when(pl.program_id(1) == 0)
    def _():
        xn_ref[...] = _rms(x_ref[...], g_ref[...]).astype(BF16)

    o_ref[...] = jnp.dot(xn_ref[...], w_ref[...], preferred_element_type=F32).astype(o_ref.dtype)


def _in_proj(x2, g_attn, w_in_p, n):
    tm = min(n, 1024)
    tn = 1280
    return pl.pallas_call(
        _in_proj_kernel,
        grid=(n // tm, PROJ_COLS // tn),
        in_specs=[
            pl.BlockSpec((tm, D), lambda i, j: (i, 0)),
            pl.BlockSpec((1, D), lambda i, j: (0, 0)),
            pl.BlockSpec((D, tn), lambda i, j: (0, j)),
        ],
        out_specs=pl.BlockSpec((tm, tn), lambda i, j: (i, j)),
        out_shape=jax.ShapeDtypeStruct((n, PROJ_COLS), BF16),
        scratch_shapes=[pltpu.VMEM((tm, D), BF16)],
        compiler_params=_params(("arbitrary", "arbitrary")),
        name="in_proj",
    )(x2, g_attn, w_in_p)


def _prep_kernel(q_ref, dq_ref, dkv_ref, k_ref, kr_ref, cos_ref, sin_ref, gq_ref, gkv_ref,
                 wq_ref, wk_ref, wv_ref, qa_ref, ka_ref, qm_ref, km_ref, vm_ref):
    cos = cos_ref[...]
    sin = sin_ref[...]
    swa_scale = DH ** -0.5
    for c in range(HQ * DH // LANES):
        sl = slice(c * LANES, (c + 1) * LANES)
        qa_ref[:, sl] = (_rope128(q_ref[:, sl].astype(F32), cos, sin) * swa_scale).astype(BF16)
    ka_ref[...] = _rope128(k_ref[...].astype(F32), cos, sin).astype(BF16)

    mla_scale = (DN + DR) ** -0.5
    cq = _rms(dq_ref[...].astype(F32), gq_ref[...]).astype(BF16)
    qb = jnp.dot(cq, wq_ref[...], preferred_element_type=F32)
    ckv = _rms(dkv_ref[...].astype(F32), gkv_ref[...]).astype(BF16)
    kn = jnp.dot(ckv, wk_ref[...], preferred_element_type=F32)
    vm_ref[...] = jnp.dot(ckv, wv_ref[...], preferred_element_type=F32).astype(BF16)
    kr = _rope128(kr_ref[...].astype(F32), cos, sin).astype(BF16)
    for h in range(MH):
        lo = h * 2 * LANES
        qm_ref[:, lo:lo + LANES] = (qb[:, lo:lo + LANES] * mla_scale).astype(BF16)
        qm_ref[:, lo + LANES:lo + 2 * LANES] = (
            _rope128(qb[:, lo + LANES:lo + 2 * LANES], cos, sin) * mla_scale).astype(BF16)
        km_ref[:, lo:lo + LANES] = kn[:, h * LANES:(h + 1) * LANES].astype(BF16)
        km_ref[:, lo + LANES:lo + 2 * LANES] = kr


def _prep(proj, cos, sin, g_q, g_kv, wq_p, wk_p, wv_p, n):
    tm = min(n, 512)
    row = lambda w, cb: pl.BlockSpec((tm, w), lambda i: (i, cb))
    full = lambda a: pl.BlockSpec(a.shape, lambda i: (0, 0))
    return pl.pallas_call(
        _prep_kernel,
        grid=(n // tm,),
        in_specs=[
            row(HQ * DH, COL_Q // (HQ * DH)), row(QR, COL_DQ // QR), row(KVR, COL_DKV // KVR),
            row(LANES, COL_K // LANES), row(LANES, COL_KR // LANES),
            row(LANES, 0), row(LANES, 0), full(g_q), full(g_kv), full(wq_p), full(wk_p), full(wv_p),
        ],
        out_specs=[row(HQ * DH, 0), row(LANES, 0), row(MH * 2 * LANES, 0), row(MH * 2 * LANES, 0),
                   row(MH * DV, 0)],
        out_shape=[
            jax.ShapeDtypeStruct((n, HQ * DH), BF16), jax.ShapeDtypeStruct((n, LANES), BF16),
            jax.ShapeDtypeStruct((n, MH * 2 * LANES), BF16), jax.ShapeDtypeStruct((n, MH * 2 * LANES), BF16),
            jax.ShapeDtypeStruct((n, MH * DV), BF16),
        ],
        compiler_params=_params(("arbitrary",)),
        name="rope_mla_prep",
    )(proj, proj, proj, proj, proj, cos, sin, g_q, g_kv, wq_p, wk_p, wv_p)


def _swa_kernel(sinks_ref, q_ref, kc_ref, kp_ref, vc_ref, vp_ref, o_ref):
    blk = pl.program_id(1)
    q2 = jnp.concatenate([q_ref[:, g * LANES:(g + 1) * LANES] for g in range(G)], axis=0)
    k2 = jnp.concatenate([kp_ref[...], kc_ref[...]], axis=0)
    v2t = jnp.concatenate([vp_ref[...], vc_ref[...]], axis=0).astype(F32).T
    lane = lax.broadcasted_iota(jnp.int32, (2 * WIN, LANES), 1)
    vrow = lax.broadcasted_iota(jnp.int32, (LANES, 2 * WIN), 0)
    kj = lax.broadcasted_iota(jnp.int32, (2 * WIN, WIN), 0)
    qi = lax.broadcasted_iota(jnp.int32, (2 * WIN, WIN), 1)
    valid = (kj > qi) & (kj <= qi + WIN) & ((kj >= WIN) | (blk > 0))
    acc = jnp.zeros((LANES, G * WIN), F32)
    for h in range(HKV):
        kz = jnp.where((lane >= h * DH) & (lane < (h + 1) * DH), k2, jnp.zeros_like(k2))
        vz = jnp.where((vrow >= h * DH) & (vrow < (h + 1) * DH), v2t, 0.0).astype(BF16)
        s = lax.dot_general(kz, q2, NT, preferred_element_type=F32)
        ps, invs = [], []
        for g in range(G):
            sg = jnp.where(valid, s[:, g * WIN:(g + 1) * WIN], NEG)
            sink = sinks_ref[h * G + g]
            m = jnp.maximum(jnp.max(sg, axis=0, keepdims=True), sink)
            p = jnp.exp(sg - m)
            l = jnp.sum(p, axis=0, keepdims=True) + jnp.exp(sink - m)
            ps.append(p.astype(BF16))
            invs.append(1.0 / l)
        o_h = jnp.dot(vz, jnp.concatenate(ps, axis=1), preferred_element_type=F32)
        acc = acc + o_h * jnp.concatenate(invs, axis=1)
    for g in range(G):
        o_ref[:, g * LANES:(g + 1) * LANES] = acc[:, g * WIN:(g + 1) * WIN].T.astype(BF16)


def _swa(sinks, qa, ka, proj, b, s):
    nb = s // WIN
    cur = lambda cb: pl.BlockSpec((WIN, LANES), lambda bi, n, sk: (bi * nb + n, cb))
    prev = lambda cb: pl.BlockSpec((WIN, LANES), lambda bi, n, sk: (bi * nb + jnp.maximum(n - 1, 0), cb))
    grid_spec = pltpu.PrefetchScalarGridSpec(
        num_scalar_prefetch=1,
        grid=(b, nb),
        in_specs=[
            pl.BlockSpec((WIN, HQ * DH), lambda bi, n, sk: (bi * nb + n, 0)),
            cur(0), prev(0), cur(COL_V // LANES), prev(COL_V // LANES),
        ],
        out_specs=pl.BlockSpec((WIN, HQ * DH), lambda bi, n, sk: (bi * nb + n, 0)),
    )
    return pl.pallas_call(
        _swa_kernel,
        grid_spec=grid_spec,
        out_shape=jax.ShapeDtypeStruct((b * s, HQ * DH), BF16),
        compiler_params=_params(("arbitrary", "arbitrary")),
        name="swa_attention",
    )(sinks, qa, ka, ka, proj, proj)


def _mla_kernel(q_ref, k_ref, v_ref, o_ref, *, seq, tq):
    for i in range(seq // tq):
        kv = (i + 1) * tq
        q = q_ref[i * tq:(i + 1) * tq, :]
        s = lax.dot_general(q, k_ref[0:kv, :], NT, preferred_element_type=F32)
        row = lax.broadcasted_iota(jnp.int32, (tq, kv), 0) + i * tq
        col = lax.broadcasted_iota(jnp.int32, (tq, kv), 1)
        s = jnp.where(col <= row, s, NEG)
        m = jnp.max(s, axis=-1, keepdims=True)
        p = jnp.exp(s - m)
        l = jnp.sum(p, axis=-1, keepdims=True)
        o = jnp.dot(p.astype(BF16), v_ref[0:kv, :], preferred_element_type=F32) * (1.0 / l)
        o_ref[i * tq:(i + 1) * tq, :] = o.astype(BF16)


def _mla(qm, km, vm, b, s):
    tq = min(s, 256)
    return pl.pallas_call(
        functools.partial(_mla_kernel, seq=s, tq=tq),
        grid=(b, MH),
        in_specs=[
            pl.BlockSpec((s, 2 * LANES), lambda bi, h: (bi, h)),
            pl.BlockSpec((s, 2 * LANES), lambda bi, h: (bi, h)),
            pl.BlockSpec((s, DV), lambda bi, h: (bi, h)),
        ],
        out_specs=pl.BlockSpec((s, DV), lambda bi, h: (bi, h)),
        out_shape=jax.ShapeDtypeStruct((b * s, MH * DV), BF16),
        compiler_params=_params(("arbitrary", "arbitrary")),
        name="mla_attention",
    )(qm, km, vm)


def _mix_kernel(oa_ref, ob_ref, ga_ref, gb_ref, wa_ref, wb_ref, ba_ref, bb_ref, o_ref):
    ya = jnp.dot(oa_ref[...], wa_ref[...], preferred_element_type=F32)
    yb = jnp.dot(ob_ref[...], wb_ref[...], preferred_element_type=F32)
    ga = jax.nn.sigmoid(ga_ref[...].astype(F32) + ba_ref[...])
    gb = jax.nn.sigmoid(gb_ref[...].astype(F32) + bb_ref[...])
    o_ref[...] = (ga * ya + gb * yb).astype(BF16)


def _mix(out_a, out_b, proj, wa, wb, b_gate, n):
    tm = min(n, 1024)
    tn = 1024
    nj = D // tn
    return pl.pallas_call(
        _mix_kernel,
        grid=(n // tm, nj),
        in_specs=[
            pl.BlockSpec((tm, HQ * DH), lambda i, j: (i, 0)),
            pl.BlockSpec((tm, MH * DV), lambda i, j: (i, 0)),
            pl.BlockSpec((tm, tn), lambda i, j: (i, j)),
            pl.BlockSpec((tm, tn), lambda i, j: (i, nj + j)),
            pl.BlockSpec((HQ * DH, tn), lambda i, j: (0, j)),
            pl.BlockSpec((MH * DV, tn), lambda i, j: (0, j)),
            pl.BlockSpec((1, tn), lambda i, j: (0, j)),
            pl.BlockSpec((1, tn), lambda i, j: (0, nj + j)),
        ],
        out_specs=pl.BlockSpec((tm, tn), lambda i, j: (i, j)),
        out_shape=jax.ShapeDtypeStruct((n, D), BF16),
        compiler_params=_params(("arbitrary", "arbitrary")),
        name="gated_mix",
    )(out_a, out_b, proj, proj, wa, wb, b_gate, b_gate)


def _router_kernel(mx_ref, x_ref, wo_ref, gf_ref, wr_ref, br_ref,
                   h_ref, hnp_ref, te_ref, tw_ref, rk_ref, cnt_ref, base_ref, *, tm):
    @pl.when(pl.program_id(0) == 0)
    def _():
        base_ref[...] = jnp.zeros_like(base_ref)

    h = x_ref[...] + jnp.dot(mx_ref[...], wo_ref[...], preferred_element_type=F32)
    h_ref[...] = h
    hn = _rms(h, gf_ref[...])
    for c in range(PACK_ROWS):
        hi = c + PACK_ROWS
        hnp_ref[pl.ds(c, tm, stride=PACK_ROWS), :] = _pack_pair(
            hn[:, c * LANES:(c + 1) * LANES], hn[:, hi * LANES:(hi + 1) * LANES])

    logits = lax.dot_general(wr_ref[...], hn, NT, preferred_element_type=F32,
                             precision=lax.Precision.HIGHEST) + br_ref[...]
    eidx = lax.broadcasted_iota(jnp.int32, (E, tm), 0)
    vals, idxs = [], []
    l = logits
    for _ in range(TOPK):
        m = jnp.max(l, axis=0, keepdims=True)
        idx = jnp.min(jnp.where(l == m, eidx, E), axis=0, keepdims=True)
        vals.append(m)
        idxs.append(idx)
        l = jnp.where(eidx == idx, -jnp.inf, l)
    ex = [jnp.exp(v - vals[0]) for v in vals]
    tot = ex[0] + ex[1] + ex[2] + ex[3]
    tw_ref[...] = jnp.concatenate([e / tot for e in ex], axis=0)
    te_ref[...] = jnp.concatenate(idxs, axis=0)

    onehot = jnp.zeros((E, tm), F32)
    for idx in idxs:
        onehot = onehot + jnp.where(eidx == idx, 1.0, 0.0)
    r = lax.broadcasted_iota(jnp.int32, (tm, tm), 0)
    c = lax.broadcasted_iota(jnp.int32, (tm, tm), 1)
    earlier = jnp.where(r < c, 1.0, 0.0).astype(BF16)
    before = jnp.dot(onehot.astype(BF16), earlier, preferred_element_type=F32) + base_ref[:, 0:1]
    rk_ref[...] = jnp.concatenate(
        [jnp.sum(jnp.where(eidx == idx, before, 0.0), axis=0, keepdims=True) for idx in idxs],
        axis=0).astype(jnp.int32)
    base_ref[...] = base_ref[...] + jnp.sum(onehot, axis=1, keepdims=True)
    cnt_ref[...] = base_ref[...]


def _router(mixed, x2, wo, g_ffn, wr_t, b_router, n):
    tm = min(n, 512)
    row = lambda w: pl.BlockSpec((tm, w), lambda i: (i, 0))
    full = lambda a: pl.BlockSpec(a.shape, lambda i: (0, 0), pipeline_mode=pl.Buffered(1))
    tok = pl.BlockSpec((TOPK, tm), lambda i: (0, i))
    return pl.pallas_call(
        functools.partial(_router_kernel, tm=tm),
        grid=(n // tm,),
        in_specs=[row(D), row(D), full(wo), full(g_ffn), full(wr_t), full(b_router)],
        out_specs=[
            row(D), pl.BlockSpec((tm * PACK_ROWS, LANES), lambda i: (i, 0)), tok, tok, tok,
            pl.BlockSpec((E, LANES), lambda i: (0, 0)),
        ],
        out_shape=[
            jax.ShapeDtypeStruct((n, D), F32), jax.ShapeDtypeStruct((n * PACK_ROWS, LANES), U32),
            jax.ShapeDtypeStruct((TOPK, n), jnp.int32), jax.ShapeDtypeStruct((TOPK, n), F32),
            jax.ShapeDtypeStruct((TOPK, n), jnp.int32), jax.ShapeDtypeStruct((E, LANES), F32),
        ],
        scratch_shapes=[pltpu.VMEM((E, LANES), F32)],
        compiler_params=_params(("arbitrary",)),
        name="outproj_router",
    )(mixed, x2, wo, g_ffn, wr_t, b_router)


def _dest_kernel(pstart_ref, te_ref, rk_ref, dest_ref):
    te = te_ref[...]
    d = rk_ref[...]
    for e in range(E):
        d = d + jnp.where(te == e, pstart_ref[e], 0)
    dest_ref[...] = d


def _dest(pstart, top_e, rank, n):
    tn = min(n, 8192)
    grid_spec = pltpu.PrefetchScalarGridSpec(
        num_scalar_prefetch=1, grid=(n // tn,),
        in_specs=[pl.BlockSpec((TOPK, tn), lambda i, ps: (0, i))] * 2,
        out_specs=pl.BlockSpec((TOPK, tn), lambda i, ps: (0, i)))
    return pl.pallas_call(
        _dest_kernel, grid_spec=grid_spec,
        out_shape=jax.ShapeDtypeStruct((TOPK, n), jnp.int32), name="dest_rows",
    )(pstart, top_e, rank)


def _prep_dispatch_kernel(cnt_ref, pad_ref, pst_ref, dest_ref, hnp_ref, wgu_ref, wdn_ref,
                          g_ref, u_ref, d_ref, xg_ref, zrow_ref, sem, zsem, *, tt, n_disp):
    i = pl.program_id(0)

    def row_copy(t, dst_row):
        return pltpu.make_async_copy(
            hnp_ref.at[pl.ds(pl.multiple_of(t * PACK_ROWS, PACK_ROWS), PACK_ROWS)],
            xg_ref.at[pl.ds(pl.multiple_of(dst_row * PACK_ROWS, PACK_ROWS), PACK_ROWS)], sem)

    def zero_copy(dst_row):
        return pltpu.make_async_copy(
            zrow_ref, xg_ref.at[pl.ds(pl.multiple_of(dst_row * PACK_ROWS, PACK_ROWS), PACK_ROWS)], zsem)

    @pl.when(i == 0)
    def _():
        zrow_ref[...] = jnp.zeros_like(zrow_ref)

        def per_expert(e, carry):
            base = pst_ref[e]

            def start(r, c):
                zero_copy(base + r).start()
                return c

            def wait(r, c):
                zero_copy(base + r).wait()
                return c

            lax.fori_loop(cnt_ref[e], pad_ref[e], start, 0)
            lax.fori_loop(cnt_ref[e], pad_ref[e], wait, 0)
            return carry

        lax.fori_loop(0, E, per_expert, 0)

    @pl.when(i < n_disp)
    def _():
        def start(t, c):
            for k in range(TOPK):
                row_copy(t, dest_ref[k, t]).start()
            return c

        lax.fori_loop(0, tt, start, 0, unroll=8)

    chunk = 2 * LANES
    r = lax.broadcasted_iota(jnp.int32, (chunk, chunk), 0)
    c = lax.broadcasted_iota(jnp.int32, (chunk, chunk), 1)
    src = jnp.where(c < LANES, 2 * c, 2 * (c - LANES) + 1)
    perm = jnp.where(r == src, 1.0, 0.0).astype(BF16)
    for k in range(wgu_ref.shape[1] // chunk):
        w = wgu_ref[:, k * chunk:(k + 1) * chunk].astype(BF16)
        o = jnp.dot(w, perm, preferred_element_type=F32)
        g_ref[:, k * LANES:(k + 1) * LANES] = o[:, :LANES].astype(BF16)
        u_ref[:, k * LANES:(k + 1) * LANES] = o[:, LANES:].astype(BF16)
    d_ref[...] = wdn_ref[...].astype(BF16)

    @pl.when(i < n_disp)
    def _():
        def wait(t, c):
            for k in range(TOPK):
                row_copy(t, dest_ref[k, t]).wait()
            return c

        lax.fori_loop(0, tt, wait, 0, unroll=8)


def _prep_dispatch(counts, padded, pstart, dest, hnp, w_gate_up, w_down, n, p_rows):
    tr = 512
    steps = E * (D // tr)
    tt = min(n, 256)
    n_disp = n // tt
    assert n_disp <= steps, "more token tiles than weight tiles"
    tile = lambda i, a, b, c: (i // (D // tr), i % (D // tr), 0)
    tok = lambda i: jnp.minimum(i, n_disp - 1)
    wout = pl.BlockSpec((None, tr, DE), tile)
    grid_spec = pltpu.PrefetchScalarGridSpec(
        num_scalar_prefetch=3, grid=(steps,),
        in_specs=[
            pl.BlockSpec((TOPK, tt), lambda i, a, b, c: (0, tok(i)), memory_space=pltpu.SMEM),
            pl.BlockSpec((tt * PACK_ROWS, LANES), lambda i, a, b, c: (tok(i), 0)),
            pl.BlockSpec((None, tr, 2 * DE), tile),
            pl.BlockSpec((None, tr, D), tile),
        ],
        out_specs=[wout, wout, pl.BlockSpec((None, tr, D), tile), pl.BlockSpec(memory_space=pl.ANY)],
        scratch_shapes=[pltpu.VMEM((PACK_ROWS, LANES), U32), pltpu.SemaphoreType.DMA(()),
                        pltpu.SemaphoreType.DMA(())],
    )
    return pl.pallas_call(
        functools.partial(_prep_dispatch_kernel, tt=tt, n_disp=n_disp), grid_spec=grid_spec,
        out_shape=[jax.ShapeDtypeStruct((E, D, DE), BF16), jax.ShapeDtypeStruct((E, D, DE), BF16),
                   jax.ShapeDtypeStruct((E, DE, D), BF16),
                   jax.ShapeDtypeStruct((p_rows * PACK_ROWS, LANES), U32)],
        compiler_params=pltpu.CompilerParams(dimension_semantics=("arbitrary",), has_side_effects=True,
                                             vmem_limit_bytes=VMEM_LIMIT),
        name="weight_prep_dispatch",
    )(counts, padded, pstart, dest, hnp, w_gate_up, w_down)


def _moe_kernel(be_ref, nu_ref, bv_ref, xp_ref, wg_ref, wu_ref, wd_ref, bg_ref, bu_ref, bd_ref, yp_ref,
                x2_ref, acc_ref, *, bm, nj):
    blk = pl.program_id(0)
    j = pl.program_id(1)
    valid = bv_ref[blk]

    def body(rows):
        @pl.when(j == 0)
        def _():
            for c in range(PACK_ROWS):
                lo, hi = _unpack_pair(xp_ref[pl.ds(c, rows, stride=PACK_ROWS), :])
                x2_ref[0:rows, c * LANES:(c + 1) * LANES] = lo.astype(BF16)
                x2_ref[0:rows, (c + PACK_ROWS) * LANES:(c + PACK_ROWS + 1) * LANES] = hi.astype(BF16)

        x = x2_ref[0:rows, :]
        gate = jnp.dot(x, wg_ref[...], preferred_element_type=F32) + bg_ref[...]
        up = jnp.dot(x, wu_ref[...], preferred_element_type=F32) + bu_ref[...]
        gate = jnp.minimum(gate, LIMIT)
        up = jnp.clip(up, -LIMIT, LIMIT)
        act = (gate * jax.nn.sigmoid(ALPHA * gate) * (up + 1.0)).astype(BF16)
        part = jnp.dot(act, wd_ref[...], preferred_element_type=F32)

        @pl.when(j == 0)
        def _():
            acc_ref[0:rows, :] = part + bd_ref[...]

        @pl.when((j > 0) & (j < nj - 1))
        def _():
            acc_ref[0:rows, :] += part

        @pl.when(j == nj - 1)
        def _():
            for c in range(PACK_ROWS):
                hi = c + PACK_ROWS
                yp_ref[pl.ds(c, rows, stride=PACK_ROWS), :] = _pack_pair(
                    acc_ref[0:rows, c * LANES:(c + 1) * LANES] + part[:, c * LANES:(c + 1) * LANES],
                    acc_ref[0:rows, hi * LANES:(hi + 1) * LANES] + part[:, hi * LANES:(hi + 1) * LANES])
            if rows < bm:
                yp_ref[rows * PACK_ROWS:bm * PACK_ROWS, :] = jnp.zeros(((bm - rows) * PACK_ROWS, LANES), U32)

    @pl.when(valid > bm // 2)
    def _():
        body(bm)

    @pl.when((valid > 0) & (valid <= bm // 2))
    def _():
        body(bm // 2)


def _moe(block_e, n_used, block_valid, xg, wg, wu, wd, bg, bu, bd, n_blocks, bm):
    th = 1024
    nj = DE // th

    def wcol(blk, j, be, nu, bv):
        return (be[blk], 0, jnp.where(blk < nu[0], j, nj - 1))

    def wrow(blk, j, be, nu, bv):
        return (be[blk], jnp.where(blk < nu[0], j, nj - 1), 0)

    def xrow(blk, j, be, nu, bv):
        return (jnp.minimum(blk, nu[0] - 1), 0)

    grid_spec = pltpu.PrefetchScalarGridSpec(
        num_scalar_prefetch=3, grid=(n_blocks, nj),
        in_specs=[
            pl.BlockSpec((bm * PACK_ROWS, LANES), xrow),
            pl.BlockSpec((None, D, th), wcol), pl.BlockSpec((None, D, th), wcol),
            pl.BlockSpec((None, th, D), wrow),
            pl.BlockSpec((None, 1, th), wcol), pl.BlockSpec((None, 1, th), wcol),
            pl.BlockSpec((None, 1, D), lambda blk, j, be, nu, bv: (be[blk], 0, 0)),
        ],
        out_specs=pl.BlockSpec((bm * PACK_ROWS, LANES), xrow),
        scratch_shapes=[pltpu.VMEM((bm, D), BF16), pltpu.VMEM((bm, D), F32)],
    )
    return pl.pallas_call(
        functools.partial(_moe_kernel, bm=bm, nj=nj), grid_spec=grid_spec,
        out_shape=jax.ShapeDtypeStruct((n_blocks * bm * PACK_ROWS, LANES), U32),
        compiler_params=_params(("arbitrary", "arbitrary")),
        name="moe_experts",
    )(block_e, n_used, block_valid, xg, wg, wu, wd, bg, bu, bd)


def _combine_kernel(dcur_ref, dnxt_ref, yp_ref, h_ref, tw_ref, gf_ref, o_ref, buf_ref, sum_ref, sem, *, tt, nt):
    i = pl.program_id(0)
    slot = i % 2
    rows = tt * PACK_ROWS

    def row_copy(d_ref, k, t, s):
        return pltpu.make_async_copy(
            yp_ref.at[pl.ds(pl.multiple_of(d_ref[k, t] * PACK_ROWS, PACK_ROWS), PACK_ROWS)],
            buf_ref.at[s, pl.ds((k * tt + t) * PACK_ROWS, PACK_ROWS)], sem.at[s])

    def issue(d_ref, s):
        def body(t, c):
            for k in range(TOPK):
                row_copy(d_ref, k, t, s).start()
            return c
        lax.fori_loop(0, tt, body, 0, unroll=8)

    @pl.when(i == 0)
    def _():
        issue(dcur_ref, 0)

    @pl.when(i + 1 < nt)
    def _():
        issue(dnxt_ref, 1 - slot)

    def wait(t, c):
        for k in range(TOPK):
            row_copy(dcur_ref, k, t, slot).wait()
        return c

    lax.fori_loop(0, tt, wait, 0, unroll=8)

    tw = tw_ref[...]
    lo_sum = jnp.zeros((rows, LANES), F32)
    hi_sum = jnp.zeros((rows, LANES), F32)
    for k in range(TOPK):
        lo, hi = _unpack_pair(buf_ref[slot, k * rows:(k + 1) * rows, :])
        lo_sum = lo_sum + lo * tw[:, k:k + 1]
        hi_sum = hi_sum + hi * tw[:, k:k + 1]
    sum_ref[0] = lo_sum
    sum_ref[1] = hi_sum
    for c in range(PACK_ROWS):
        hi = c + PACK_ROWS
        o_ref[:, c * LANES:(c + 1) * LANES] = (
            h_ref[:, c * LANES:(c + 1) * LANES] + sum_ref[0, pl.ds(c, tt, stride=PACK_ROWS), :])
        o_ref[:, hi * LANES:(hi + 1) * LANES] = (
            h_ref[:, hi * LANES:(hi + 1) * LANES] + sum_ref[1, pl.ds(c, tt, stride=PACK_ROWS), :])
    o_ref[...] = _rms(o_ref[...], gf_ref[...])


def _combine(dest, yp, h, tw_rows, g_final, n):
    tt = min(n, 256)
    nt = n // tt
    grid_spec = pltpu.PrefetchScalarGridSpec(
        num_scalar_prefetch=0, grid=(nt,),
        in_specs=[
            pl.BlockSpec((TOPK, tt), lambda i: (0, i), memory_space=pltpu.SMEM),
            pl.BlockSpec((TOPK, tt), lambda i: (0, jnp.minimum(i + 1, nt - 1)), memory_space=pltpu.SMEM),
            pl.BlockSpec(memory_space=pl.ANY),
            pl.BlockSpec((tt, D), lambda i: (i, 0)),
            pl.BlockSpec((tt * PACK_ROWS, TOPK), lambda i: (i, 0)),
            pl.BlockSpec((1, D), lambda i: (0, 0)),
        ],
        out_specs=pl.BlockSpec((tt, D), lambda i: (i, 0)),
        scratch_shapes=[pltpu.VMEM((2, TOPK * tt * PACK_ROWS, LANES), U32),
                        pltpu.VMEM((2, tt * PACK_ROWS, LANES), F32), pltpu.SemaphoreType.DMA((2,))],
    )
    return pl.pallas_call(
        functools.partial(_combine_kernel, tt=tt, nt=nt), grid_spec=grid_spec,
        out_shape=jax.ShapeDtypeStruct((n, D), F32),
        compiler_params=_params(("arbitrary",)),
        name="combine_final_norm",
    )(dest, dest, yp, h, tw_rows, g_final)


def _permute_weights(w_in, w_uq, w_ukv, w_o_swa):
    widths = (HQ * DH, HKV * DH, HKV * DH, QR, KVR, DR, 2 * D)
    offs = [0]
    for w in widths:
        offs.append(offs[-1] + w)
    wq, wk, wv, wdq, wdkv, wkr, wgates = [w_in[:, offs[i]:offs[i + 1]] for i in range(7)]
    wq = wq.reshape(D, HKV, G, DH).transpose(0, 2, 1, 3).reshape(D, HQ * DH)
    pad = jnp.zeros((D, PROJ_COLS - COL_KR - DR), w_in.dtype)
    w_in_p = jnp.concatenate([wgates, wq, wdq, wdkv, wk, wv, wkr, pad], axis=1).astype(BF16)
    wo_a = w_o_swa.reshape(HKV, G, DH, D).transpose(1, 0, 2, 3).reshape(HQ * DH, D).astype(BF16)
    wuq = w_uq.reshape(QR, MH, DN + DR)
    wq_p = jnp.concatenate([wuq, jnp.zeros((QR, MH, 2 * LANES - DN - DR), w_uq.dtype)], axis=2)
    wq_p = wq_p.reshape(QR, MH * 2 * LANES).astype(BF16)
    wukv = w_ukv.reshape(KVR, MH, DN + DV)
    wk_p = wukv[:, :, :DN].reshape(KVR, MH * DN).astype(BF16)
    wv_p = wukv[:, :, DN:].reshape(KVR, MH * DV).astype(BF16)
    return w_in_p, wo_a, wq_p, wk_p, wv_p


def kernel(x, positions, g_attn, w_in, b_gate, sinks, g_q, w_uq, g_kv, w_ukv, w_o_swa, w_o_mla, w_out,
           g_ffn, w_router, b_router, w_gate_up, b_gate_up, w_down, b_down, g_final):
    b, s, _ = x.shape
    n = b * s
    assert w_in.shape[0] == 1, "single-layer block"
    x2 = x.reshape(n, D)

    w_in_p, wo_a, wq_p, wk_p, wv_p = _permute_weights(w_in[0], w_uq[0], w_ukv[0], w_o_swa[0])
    cos, sin = _rope_tables(positions, n)
    proj = _in_proj(x2, g_attn, w_in_p, n)
    qa, ka, qm, km, vm = _prep(proj, cos, sin, g_q, g_kv, wq_p, wk_p, wv_p, n)
    out_a = _swa(sinks[0], qa, ka, proj, b, s)
    out_b = _mla(qm, km, vm, b, s)

    mixed = _mix(out_a, out_b, proj, wo_a, w_o_mla[0].astype(BF16), b_gate, n)
    h, hnp, top_e, top_w, rank, cnt = _router(
        mixed, x2, w_out[0].astype(BF16), g_ffn, w_router[0].T, b_router[0][:, None], n)

    bm = 512
    n_blocks = -(-(n * TOPK) // bm) + E
    counts = cnt[:, 0].astype(jnp.int32)
    padded = (counts + bm - 1) // bm * bm
    pend = jnp.cumsum(padded)
    pstart = pend - padded
    n_used = pend[-1] // bm
    blocks = jnp.arange(n_blocks, dtype=jnp.int32)
    block_e = jnp.minimum(jnp.sum(pend[None, :] <= (blocks * bm)[:, None], axis=1), E - 1).astype(jnp.int32)
    block_valid = jnp.clip(counts[block_e] - (blocks * bm - pstart[block_e]), 0, bm).astype(jnp.int32)
    block_valid = jnp.where(blocks < n_used, block_valid, 0)
    block_e = jnp.where(blocks < n_used, block_e, block_e[jnp.maximum(n_used - 1, 0)])

    dest = _dest(pstart, top_e, rank, n)
    wg, wu, wd, xg = _prep_dispatch(counts, padded, pstart, dest, hnp, w_gate_up[0], w_down[0], n, n_blocks * bm)
    bgu = b_gate_up[0]
    yp = _moe(block_e, n_used.reshape(1).astype(jnp.int32), block_valid, xg, wg, wu, wd,
              bgu[:, None, 0::2], bgu[:, None, 1::2], b_down[0][:, None, :], n_blocks, bm)

    tw_rows = jnp.repeat(top_w.T, PACK_ROWS, axis=0)
    out = _combine(dest, yp, h, tw_rows, g_final[None, :], n)
    return out.reshape(b, s, D)
```

```python
import functools

import jax
import jax.numpy as jnp
from jax import lax
from jax.experimental import pallas as pl
from jax.experimental.pallas import tpu as pltpu

F32 = jnp.float32
BF16 = jnp.bfloat16
U32 = jnp.uint32

D = 2048
HQ, HKV, DH, WIN = 16, 2, 64, 128
G = HQ // HKV
MH, QR, KVR, DN, DR, DV = 8, 512, 256, 128, 64, 128
THETA = 10000.0
E, TOPK, DE = 32, 4, 2048
LIMIT, ALPHA = 7.0, 1.702
EPS = 1e-6
NEG = -1e30
LOG2E = 1.4426950408889634

LANES = 128
CHUNKS = D // LANES
PACK_ROWS = CHUNKS // 2
VMEM_LIMIT = 56 * 1024 * 1024

PROJ_COLS = 6400
COL_GATES, COL_Q, COL_DQ, COL_DKV, COL_K, COL_V, COL_KR = 0, 4096, 5120, 5632, 5888, 6016, 6144

NT = (((1,), (1,)), ((), ()))


def _params(sem, vmem=VMEM_LIMIT):
    return pltpu.CompilerParams(dimension_semantics=sem, vmem_limit_bytes=vmem)


def _rms(x, g):
    return x * lax.rsqrt(jnp.mean(x * x, axis=-1, keepdims=True) + EPS) * g


def _pack_pair(lo, hi):
    lo_bits = lax.bitcast_convert_type(lo.astype(BF16).astype(F32), U32)
    hi_bits = lax.bitcast_convert_type(hi.astype(BF16).astype(F32), U32)
    return (lo_bits >> 16) | hi_bits


def _unpack_pair(w):
    lo = lax.bitcast_convert_type(w << 16, F32)
    hi = lax.bitcast_convert_type(w & jnp.uint32(0xFFFF0000), F32)
    return lo, hi


def _rope128(v, cos, sin):
    lane = lax.broadcasted_iota(jnp.int32, v.shape, 1)
    rot = jnp.where((lane % DH) < (DH // 2), -pltpu.roll(v, LANES - DH // 2, 1), pltpu.roll(v, DH // 2, 1))
    return v * cos + rot * sin


def _rope_table_kernel(pos_ref, inv_ref, cos_ref, sin_ref):
    ang = pos_ref[...].astype(F32) * inv_ref[...]
    cos_ref[...] = jnp.cos(ang)
    sin_ref[...] = jnp.sin(ang)


def _rope_tables(positions, n):
    half = DH // 2
    per_row = LANES // half
    inv = jnp.power(THETA, -jnp.arange(half, dtype=F32) * 2.0 / DH)
    inv_row = jnp.tile(inv, per_row)[None, :]
    pos_rows = jnp.repeat(positions.reshape(n // per_row, per_row), half, axis=1)
    rows = n // per_row
    tr = min(rows, 1024)
    cos, sin = pl.pallas_call(
        _rope_table_kernel,
        grid=(rows // tr,),
        in_specs=[pl.BlockSpec((tr, LANES), lambda i: (i, 0)), pl.BlockSpec((1, LANES), lambda i: (0, 0))],
        out_specs=[pl.BlockSpec((tr, LANES), lambda i: (i, 0))] * 2,
        out_shape=[jax.ShapeDtypeStruct((rows, LANES), F32)] * 2,
        name="rope_tables",
    )(pos_rows, inv_row)
    cos = jnp.tile(cos.reshape(n, half), (1, per_row))
    sin = jnp.tile(sin.reshape(n, half), (1, per_row))
    return cos, sin


def _in_proj_kernel(x_ref, g_ref, w_ref, o_ref, xn_ref):
    @pl.when(pl.program_id(1) == 0)
    def _():
        xn_ref[...] = _rms(x_ref[...], g_ref[...]).astype(BF16)

    o_ref[...] = jnp.dot(xn_ref[...], w_ref[...], preferred_element_type=F32).astype(o_ref.dtype)


def _in_proj(x2, g_attn, w_in_p, n):
    tm = min(n, 1024)
    tn = 1280
    return pl.pallas_call(
        _in_proj_kernel,
        grid=(n // tm, PROJ_COLS // tn),
        in_specs=[
            pl.BlockSpec((tm, D), lambda i, j: (i, 0)),
            pl.BlockSpec((1, D), lambda i, j: (0, 0)),
            pl.BlockSpec((D, tn), lambda i, j: (0, j)),
        ],
        out_specs=pl.BlockSpec((tm, tn), lambda i, j: (i, j)),
        out_shape=jax.ShapeDtypeStruct((n, PROJ_COLS), BF16),
        scratch_shapes=[pltpu.VMEM((tm, D), BF16)],
        compiler_params=_params(("arbitrary", "arbitrary")),
        name="in_proj",
    )(x2, g_attn, w_in_p)


def _prep_kernel(q_ref, dq_ref, dkv_ref, k_ref, kr_ref, cos_ref, sin_ref, gq_ref, gkv_ref,
                 wq_ref, wk_ref, wv_ref, qa_ref, ka_ref, qm_ref, km_ref, vm_ref):
    cos = cos_ref[...]
    sin = sin_ref[...]
    swa_scale = DH ** -0.5 * LOG2E
    for c in range(HQ * DH // LANES):
        sl = slice(c * LANES, (c + 1) * LANES)
        qa_ref[:, sl] = (_rope128(q_ref[:, sl].astype(F32), cos, sin) * swa_scale).astype(BF16)
    ka_ref[...] = _rope128(k_ref[...].astype(F32), cos, sin).astype(BF16)

    mla_scale = (DN + DR) ** -0.5 * LOG2E
    cq = _rms(dq_ref[...].astype(F32), gq_ref[...]).astype(BF16)
    qb = jnp.dot(cq, wq_ref[...], preferred_element_type=F32)
    ckv = _rms(dkv_ref[...].astype(F32), gkv_ref[...]).astype(BF16)
    kn = jnp.dot(ckv, wk_ref[...], preferred_element_type=F32)
    vv = jnp.dot(ckv, wv_ref[...], preferred_element_type=F32)
    ones_col = jnp.where(lax.broadcasted_iota(jnp.int32, (vv.shape[0], LANES), 1) == 0, 1.0, 0.0).astype(BF16)
    kr = _rope128(kr_ref[...].astype(F32), cos, sin).astype(BF16)
    for h in range(MH):
        lo = h * 2 * LANES
        qm_ref[:, lo:lo + LANES] = (qb[:, lo:lo + LANES] * mla_scale).astype(BF16)
        qm_ref[:, lo + LANES:lo + 2 * LANES] = (
            _rope128(qb[:, lo + LANES:lo + 2 * LANES], cos, sin) * mla_scale).astype(BF16)
        km_ref[:, lo:lo + LANES] = kn[:, h * LANES:(h + 1) * LANES].astype(BF16)
        km_ref[:, lo + LANES:lo + 2 * LANES] = kr
        vm_ref[:, lo:lo + LANES] = vv[:, h * LANES:(h + 1) * LANES].astype(BF16)
        vm_ref[:, lo + LANES:lo + 2 * LANES] = ones_col


def _prep(proj, cos, sin, g_q, g_kv, wq_p, wk_p, wv_p, n):
    tm = min(n, 512)
    row = lambda w, cb: pl.BlockSpec((tm, w), lambda i: (i, cb))
    full = lambda a: pl.BlockSpec(a.shape, lambda i: (0, 0))
    return pl.pallas_call(
        _prep_kernel,
        grid=(n // tm,),
        in_specs=[
            row(HQ * DH, COL_Q // (HQ * DH)), row(QR, COL_DQ // QR), row(KVR, COL_DKV // KVR),
            row(LANES, COL_K // LANES), row(LANES, COL_KR // LANES),
            row(LANES, 0), row(LANES, 0), full(g_q), full(g_kv), full(wq_p), full(wk_p), full(wv_p),
        ],
        out_specs=[row(HQ * DH, 0), row(LANES, 0)] + [row(MH * 2 * LANES, 0)] * 3,
        out_shape=[jax.ShapeDtypeStruct((n, HQ * DH), BF16), jax.ShapeDtypeStruct((n, LANES), BF16)]
        + [jax.ShapeDtypeStruct((n, MH * 2 * LANES), BF16)] * 3,
        compiler_params=_params(("arbitrary",)),
        name="rope_mla_prep",
    )(proj, proj, proj, proj, proj, cos, sin, g_q, g_kv, wq_p, wk_p, wv_p)


def _swa_kernel(sinks_ref, q_ref, kc_ref, kp_ref, vc_ref, vp_ref, o_ref):
    blk = pl.program_id(1)
    q2 = jnp.concatenate([q_ref[:, g * LANES:(g + 1) * LANES] for g in range(G)], axis=0)
    k2 = jnp.concatenate([kp_ref[...], kc_ref[...]], axis=0)
    v2t = jnp.concatenate([vp_ref[...], vc_ref[...]], axis=0).astype(F32).T
    lane = lax.broadcasted_iota(jnp.int32, (2 * WIN, LANES), 1)
    vrow = lax.broadcasted_iota(jnp.int32, (LANES, 2 * WIN), 0)
    kj = lax.broadcasted_iota(jnp.int32, (2 * WIN, WIN), 0)
    qi = lax.broadcasted_iota(jnp.int32, (2 * WIN, WIN), 1)
    valid = (kj > qi) & (kj <= qi + WIN) & ((kj >= WIN) | (blk > 0))
    acc = jnp.zeros((LANES, G * WIN), F32)
    for h in range(HKV):
        kz = jnp.where((lane >= h * DH) & (lane < (h + 1) * DH), k2, jnp.zeros_like(k2))
        vz = jnp.where((vrow >= h * DH) & (vrow < (h + 1) * DH), v2t, 0.0).astype(BF16)
        s = lax.dot_general(kz, q2, NT, preferred_element_type=F32)
        ps, invs = [], []
        for g in range(G):
            sg = jnp.where(valid, s[:, g * WIN:(g + 1) * WIN], NEG)
            sink = sinks_ref[h * G + g] * LOG2E
            m = jnp.maximum(jnp.max(sg, axis=0, keepdims=True), sink)
            p = jnp.exp2(sg - m)
            l = jnp.sum(p, axis=0, keepdims=True) + jnp.exp2(sink - m)
            ps.append(p.astype(BF16))
            invs.append(1.0 / l)
        o_h = jnp.dot(vz, jnp.concatenate(ps, axis=1), preferred_element_type=F32)
        acc = acc + o_h * jnp.concatenate(invs, axis=1)
    for g in range(G):
        o_ref[:, g * LANES:(g + 1) * LANES] = acc[:, g * WIN:(g + 1) * WIN].T.astype(BF16)


def _swa(sinks, qa, ka, proj, b, s):
    nb = s // WIN
    cur = lambda cb: pl.BlockSpec((WIN, LANES), lambda bi, n, sk: (bi * nb + n, cb))
    prev = lambda cb: pl.BlockSpec((WIN, LANES), lambda bi, n, sk: (bi * nb + jnp.maximum(n - 1, 0), cb))
    grid_spec = pltpu.PrefetchScalarGridSpec(
        num_scalar_prefetch=1,
        grid=(b, nb),
        in_specs=[
            pl.BlockSpec((WIN, HQ * DH), lambda bi, n, sk: (bi * nb + n, 0)),
            cur(0), prev(0), cur(COL_V // LANES), prev(COL_V // LANES),
        ],
        out_specs=pl.BlockSpec((WIN, HQ * DH), lambda bi, n, sk: (bi * nb + n, 0)),
    )
    return pl.pallas_call(
        _swa_kernel,
        grid_spec=grid_spec,
        out_shape=jax.ShapeDtypeStruct((b * s, HQ * DH), BF16),
        compiler_params=_params(("arbitrary", "arbitrary")),
        name="swa_attention",
    )(sinks, qa, ka, ka, proj, proj)


def _mla_kernel(q_ref, k_ref, v_ref, o_ref, *, seq, tq):
    row = lax.broadcasted_iota(jnp.int32, (tq, tq), 0)
    col = lax.broadcasted_iota(jnp.int32, (tq, tq), 1)
    causal = col <= row
    for i in range(seq // tq):
        kv = (i + 1) * tq
        q = q_ref[i * tq:(i + 1) * tq, :]
        s = lax.dot_general(q, k_ref[0:kv, :], NT, preferred_element_type=F32)
        diag = jnp.where(causal, s[:, kv - tq:kv], NEG)
        s = diag if i == 0 else jnp.concatenate([s[:, 0:kv - tq], diag], axis=1)
        m = jnp.max(s, axis=-1, keepdims=True)
        p = jnp.exp2(s - m).astype(BF16)
        ov = jnp.dot(p, v_ref[0:kv, :], preferred_element_type=F32)
        o_ref[i * tq:(i + 1) * tq, :] = (ov[:, 0:DV] * (1.0 / ov[:, DV:DV + 1])).astype(BF16)


def _mla(qm, km, vm, b, s):
    tq = min(s, 512)
    return pl.pallas_call(
        functools.partial(_mla_kernel, seq=s, tq=tq),
        grid=(b, MH),
        in_specs=[
            pl.BlockSpec((s, 2 * LANES), lambda bi, h: (bi, h)),
            pl.BlockSpec((s, 2 * LANES), lambda bi, h: (bi, h)),
            pl.BlockSpec((s, 2 * LANES), lambda bi, h: (bi, h)),
        ],
        out_specs=pl.BlockSpec((s, DV), lambda bi, h: (bi, h)),
        out_shape=jax.ShapeDtypeStruct((b * s, MH * DV), BF16),
        compiler_params=_params(("arbitrary", "arbitrary")),
        name="mla_attention",
    )(qm, km, vm)


def _mix_kernel(oa_ref, ob_ref, ga_ref, gb_ref, wa_ref, wb_ref, ba_ref, bb_ref, o_ref):
    ya = jnp.dot(oa_ref[...], wa_ref[...], preferred_element_type=F32)
    yb = jnp.dot(ob_ref[...], wb_ref[...], preferred_element_type=F32)
    ga = jax.nn.sigmoid(ga_ref[...].astype(F32) + ba_ref[...])
    gb = jax.nn.sigmoid(gb_ref[...].astype(F32) + bb_ref[...])
    o_ref[...] = (ga * ya + gb * yb).astype(BF16)


def _mix(out_a, out_b, proj, wa, wb, b_gate, n):
    tm = min(n, 1024)
    tn = 1024
    nj = D // tn
    return pl.pallas_call(
        _mix_kernel,
        grid=(n // tm, nj),
        in_specs=[
            pl.BlockSpec((tm, HQ * DH), lambda i, j: (i, 0)),
            pl.BlockSpec((tm, MH * DV), lambda i, j: (i, 0)),
            pl.BlockSpec((tm, tn), lambda i, j: (i, j)),
            pl.BlockSpec((tm, tn), lambda i, j: (i, nj + j)),
            pl.BlockSpec((HQ * DH, tn), lambda i, j: (0, j)),
            pl.BlockSpec((MH * DV, tn), lambda i, j: (0, j)),
            pl.BlockSpec((1, tn), lambda i, j: (0, j)),
            pl.BlockSpec((1, tn), lambda i, j: (0, nj + j)),
        ],
        out_specs=pl.BlockSpec((tm, tn), lambda i, j: (i, j)),
        out_shape=jax.ShapeDtypeStruct((n, D), BF16),
        compiler_params=_params(("arbitrary", "arbitrary")),
        name="gated_mix",
    )(out_a, out_b, proj, proj, wa, wb, b_gate, b_gate)


def _router_kernel(mx_ref, x_ref, wo_ref, gf_ref, wr_ref, br_ref,
                   h_ref, hnp_ref, te_ref, tw_ref, rk_ref, cnt_ref, base_ref, *, tm):
    @pl.when(pl.program_id(0) == 0)
    def _():
        base_ref[...] = jnp.zeros_like(base_ref)

    h = x_ref[...] + jnp.dot(mx_ref[...], wo_ref[...], preferred_element_type=F32)
    h_ref[...] = h
    hn = _rms(h, gf_ref[...])
    for c in range(PACK_ROWS):
        hi = c + PACK_ROWS
        hnp_ref[pl.ds(c, tm, stride=PACK_ROWS), :] = _pack_pair(
            hn[:, c * LANES:(c + 1) * LANES], hn[:, hi * LANES:(hi + 1) * LANES])

    logits = lax.dot_general(wr_ref[...], hn, NT, preferred_element_type=F32,
                             precision=lax.Precision.HIGHEST) + br_ref[...]
    eidx = lax.broadcasted_iota(jnp.int32, (E, tm), 0)
    vals, idxs = [], []
    l = logits
    for _ in range(TOPK):
        m = jnp.max(l, axis=0, keepdims=True)
        idx = jnp.min(jnp.where(l == m, eidx, E), axis=0, keepdims=True)
        vals.append(m)
        idxs.append(idx)
        l = jnp.where(eidx == idx, -jnp.inf, l)
    ex = [jnp.exp(v - vals[0]) for v in vals]
    tot = ex[0] + ex[1] + ex[2] + ex[3]
    tw_ref[...] = jnp.concatenate([e / tot for e in ex], axis=0)
    te_ref[...] = jnp.concatenate(idxs, axis=0)

    onehot = jnp.zeros((E, tm), F32)
    for idx in idxs:
        onehot = onehot + jnp.where(eidx == idx, 1.0, 0.0)
    r = lax.broadcasted_iota(jnp.int32, (tm, tm), 0)
    c = lax.broadcasted_iota(jnp.int32, (tm, tm), 1)
    earlier = jnp.where(r < c, 1.0, 0.0).astype(BF16)
    before = jnp.dot(onehot.astype(BF16), earlier, preferred_element_type=F32) + base_ref[:, 0:1]
    rk_ref[...] = jnp.concatenate(
        [jnp.sum(jnp.where(eidx == idx, before, 0.0), axis=0, keepdims=True) for idx in idxs],
        axis=0).astype(jnp.int32)
    base_ref[...] = base_ref[...] + jnp.sum(onehot, axis=1, keepdims=True)
    cnt_ref[...] = base_ref[...]


def _router(mixed, x2, wo, g_ffn, wr_t, b_router, n):
    tm = min(n, 512)
    row = lambda w: pl.BlockSpec((tm, w), lambda i: (i, 0))
    full = lambda a: pl.BlockSpec(a.shape, lambda i: (0, 0), pipeline_mode=pl.Buffered(1))
    tok = pl.BlockSpec((TOPK, tm), lambda i: (0, i))
    return pl.pallas_call(
        functools.partial(_router_kernel, tm=tm),
        grid=(n // tm,),
        in_specs=[row(D), row(D), full(wo), full(g_ffn), full(wr_t), full(b_router)],
        out_specs=[
            row(D), pl.BlockSpec((tm * PACK_ROWS, LANES), lambda i: (i, 0)), tok, tok, tok,
            pl.BlockSpec((E, LANES), lambda i: (0, 0)),
        ],
        out_shape=[
            jax.ShapeDtypeStruct((n, D), F32), jax.ShapeDtypeStruct((n * PACK_ROWS, LANES), U32),
            jax.ShapeDtypeStruct((TOPK, n), jnp.int32), jax.ShapeDtypeStruct((TOPK, n), F32),
            jax.ShapeDtypeStruct((TOPK, n), jnp.int32), jax.ShapeDtypeStruct((E, LANES), F32),
        ],
        scratch_shapes=[pltpu.VMEM((E, LANES), F32)],
        compiler_params=_params(("arbitrary",)),
        name="outproj_router",
    )(mixed, x2, wo, g_ffn, wr_t, b_router)


def _dest_kernel(pstart_ref, te_ref, rk_ref, dest_ref):
    te = te_ref[...]
    d = rk_ref[...]
    for e in range(E):
        d = d + jnp.where(te == e, pstart_ref[e], 0)
    dest_ref[...] = d


def _dest(pstart, top_e, rank, n):
    tn = min(n, 8192)
    grid_spec = pltpu.PrefetchScalarGridSpec(
        num_scalar_prefetch=1, grid=(n // tn,),
        in_specs=[pl.BlockSpec((TOPK, tn), lambda i, ps: (0, i))] * 2,
        out_specs=pl.BlockSpec((TOPK, tn), lambda i, ps: (0, i)))
    return pl.pallas_call(
        _dest_kernel, grid_spec=grid_spec,
        out_shape=jax.ShapeDtypeStruct((TOPK, n), jnp.int32), name="dest_rows",
    )(pstart, top_e, rank)


def _prep_dispatch_kernel(cnt_ref, pad_ref, pst_ref, dest_ref, hnp_ref, wgu_ref, wdn_ref,
                          g_ref, u_ref, d_ref, xg_ref, zrow_ref, sem, zsem, *, tt, n_disp):
    i = pl.program_id(0)

    def row_copy(t, dst_row):
        return pltpu.make_async_copy(
            hnp_ref.at[pl.ds(pl.multiple_of(t * PACK_ROWS, PACK_ROWS), PACK_ROWS)],
            xg_ref.at[pl.ds(pl.multiple_of(dst_row * PACK_ROWS, PACK_ROWS), PACK_ROWS)], sem)

    def zero_copy(dst_row):
        return pltpu.make_async_copy(
            zrow_ref, xg_ref.at[pl.ds(pl.multiple_of(dst_row * PACK_ROWS, PACK_ROWS), PACK_ROWS)], zsem)

    @pl.when(i == 0)
    def _():
        zrow_ref[...] = jnp.zeros_like(zrow_ref)

        def per_expert(e, carry):
            base = pst_ref[e]

            def start(r, c):
                zero_copy(base + r).start()
                return c

            def wait(r, c):
                zero_copy(base + r).wait()
                return c

            lax.fori_loop(cnt_ref[e], pad_ref[e], start, 0)
            lax.fori_loop(cnt_ref[e], pad_ref[e], wait, 0)
            return carry

        lax.fori_loop(0, E, per_expert, 0)

    @pl.when(i < n_disp)
    def _():
        def start(t, c):
            for k in range(TOPK):
                row_copy(t, dest_ref[k, t]).start()
            return c

        lax.fori_loop(0, tt, start, 0, unroll=8)

    chunk = 2 * LANES
    r = lax.broadcasted_iota(jnp.int32, (chunk, chunk), 0)
    c = lax.broadcasted_iota(jnp.int32, (chunk, chunk), 1)
    src = jnp.where(c < LANES, 2 * c, 2 * (c - LANES) + 1)
    perm = jnp.where(r == src, 1.0, 0.0).astype(BF16)
    for k in range(wgu_ref.shape[1] // chunk):
        w = wgu_ref[:, k * chunk:(k + 1) * chunk].astype(BF16)
        o = jnp.dot(w, perm, preferred_element_type=F32)
        g_ref[:, k * LANES:(k + 1) * LANES] = o[:, :LANES].astype(BF16)
        u_ref[:, k * LANES:(k + 1) * LANES] = o[:, LANES:].astype(BF16)
    d_ref[...] = wdn_ref[...].astype(BF16)

    @pl.when(i < n_disp)
    def _():
        for _ in range(TOPK):
            pltpu.make_async_copy(hnp_ref, xg_ref.at[pl.ds(0, tt * PACK_ROWS)], sem).wait()


def _prep_dispatch(counts, padded, pstart, dest, hnp, w_gate_up, w_down, n, p_rows):
    tr = 512
    steps = E * (D // tr)
    tt = min(n, 256)
    n_disp = n // tt
    assert n_disp <= steps, "more token tiles than weight tiles"
    tile = lambda i, a, b, c: (i // (D // tr), i % (D // tr), 0)
    tok = lambda i: jnp.minimum(i, n_disp - 1)
    wout = pl.BlockSpec((None, tr, DE), tile)
    grid_spec = pltpu.PrefetchScalarGridSpec(
        num_scalar_prefetch=3, grid=(steps,),
        in_specs=[
            pl.BlockSpec((TOPK, tt), lambda i, a, b, c: (0, tok(i)), memory_space=pltpu.SMEM),
            pl.BlockSpec((tt * PACK_ROWS, LANES), lambda i, a, b, c: (tok(i), 0)),
            pl.BlockSpec((None, tr, 2 * DE), tile),
            pl.BlockSpec((None, tr, D), tile),
        ],
        out_specs=[wout, wout, pl.BlockSpec((None, tr, D), tile), pl.BlockSpec(memory_space=pl.ANY)],
        scratch_shapes=[pltpu.VMEM((PACK_ROWS, LANES), U32), pltpu.SemaphoreType.DMA(()),
                        pltpu.SemaphoreType.DMA(())],
    )
    return pl.pallas_call(
        functools.partial(_prep_dispatch_kernel, tt=tt, n_disp=n_disp), grid_spec=grid_spec,
        out_shape=[jax.ShapeDtypeStruct((E, D, DE), BF16), jax.ShapeDtypeStruct((E, D, DE), BF16),
                   jax.ShapeDtypeStruct((E, DE, D), BF16),
                   jax.ShapeDtypeStruct((p_rows * PACK_ROWS, LANES), U32)],
        compiler_params=pltpu.CompilerParams(dimension_semantics=("arbitrary",), has_side_effects=True,
                                             vmem_limit_bytes=VMEM_LIMIT),
        name="weight_prep_dispatch",
    )(counts, padded, pstart, dest, hnp, w_gate_up, w_down)


def _moe_kernel(be_ref, nu_ref, bv_ref, xp_ref, wg_ref, wu_ref, wd_ref, bg_ref, bu_ref, bd_ref, yp_ref,
                x2_ref, acc_ref, *, bm, nj):
    blk = pl.program_id(0)
    j = pl.program_id(1)
    valid = bv_ref[blk]

    def body(rows):
        @pl.when(j == 0)
        def _():
            for c in range(PACK_ROWS):
                lo, hi = _unpack_pair(xp_ref[pl.ds(c, rows, stride=PACK_ROWS), :])
                x2_ref[0:rows, c * LANES:(c + 1) * LANES] = lo.astype(BF16)
                x2_ref[0:rows, (c + PACK_ROWS) * LANES:(c + PACK_ROWS + 1) * LANES] = hi.astype(BF16)

        x = x2_ref[0:rows, :]
        gate = jnp.dot(x, wg_ref[...], preferred_element_type=F32) + bg_ref[...]
        up = jnp.dot(x, wu_ref[...], preferred_element_type=F32) + bu_ref[...]
        gate = jnp.minimum(gate, LIMIT)
        up = jnp.clip(up, -LIMIT, LIMIT)
        act = (gate * jax.nn.sigmoid(ALPHA * gate) * (up + 1.0)).astype(BF16)
        part = jnp.dot(act, wd_ref[...], preferred_element_type=F32)

        @pl.when(j == 0)
        def _():
            acc_ref[0:rows, :] = part + bd_ref[...]

        @pl.when((j > 0) & (j < nj - 1))
        def _():
            acc_ref[0:rows, :] += part

        @pl.when(j == nj - 1)
        def _():
            for c in range(PACK_ROWS):
                hi = c + PACK_ROWS
                yp_ref[pl.ds(c, rows, stride=PACK_ROWS), :] = _pack_pair(
                    acc_ref[0:rows, c * LANES:(c + 1) * LANES] + part[:, c * LANES:(c + 1) * LANES],
                    acc_ref[0:rows, hi * LANES:(hi + 1) * LANES] + part[:, hi * LANES:(hi + 1) * LANES])
            if rows < bm:
                yp_ref[rows * PACK_ROWS:bm * PACK_ROWS, :] = jnp.zeros(((bm - rows) * PACK_ROWS, LANES), U32)

    @pl.when(valid > bm // 2)
    def _():
        body(bm)

    @pl.when((valid > 0) & (valid <= bm // 2))
    def _():
        body(bm // 2)


def _moe(block_e, n_used, block_valid, xg, wg, wu, wd, bg, bu, bd, n_blocks, bm):
    th = 1024
    nj = DE // th

    def wcol(blk, j, be, nu, bv):
        return (be[blk], 0, jnp.where(blk < nu[0], j, nj - 1))

    def wrow(blk, j, be, nu, bv):
        return (be[blk], jnp.where(blk < nu[0], j, nj - 1), 0)

    def xrow(blk, j, be, nu, bv):
        return (jnp.minimum(blk, nu[0] - 1), 0)

    grid_spec = pltpu.PrefetchScalarGridSpec(
        num_scalar_prefetch=3, grid=(n_blocks, nj),
        in_specs=[
            pl.BlockSpec((bm * PACK_ROWS, LANES), xrow),
            pl.BlockSpec((None, D, th), wcol), pl.BlockSpec((None, D, th), wcol),
            pl.BlockSpec((None, th, D), wrow),
            pl.BlockSpec((None, 1, th), wcol), pl.BlockSpec((None, 1, th), wcol),
            pl.BlockSpec((None, 1, D), lambda blk, j, be, nu, bv: (be[blk], 0, 0)),
        ],
        out_specs=pl.BlockSpec((bm * PACK_ROWS, LANES), xrow),
        scratch_shapes=[pltpu.VMEM((bm, D), BF16), pltpu.VMEM((bm, D), F32)],
    )
    return pl.pallas_call(
        functools.partial(_moe_kernel, bm=bm, nj=nj), grid_spec=grid_spec,
        out_shape=jax.ShapeDtypeStruct((n_blocks * bm * PACK_ROWS, LANES), U32),
        compiler_params=_params(("arbitrary", "arbitrary")),
        name="moe_experts",
    )(block_e, n_used, block_valid, xg, wg, wu, wd, bg, bu, bd)


def _combine_kernel(dcur_ref, dnxt_ref, yp_ref, h_ref, tw_ref, gf_ref, o_ref, buf_ref, sum_ref, sem, *, tt, nt):
    i = pl.program_id(0)
    slot = i % 2
    rows = tt * PACK_ROWS

    def row_copy(d_ref, k, t, s):
        return pltpu.make_async_copy(
            yp_ref.at[pl.ds(pl.multiple_of(d_ref[k, t] * PACK_ROWS, PACK_ROWS), PACK_ROWS)],
            buf_ref.at[s, pl.ds((k * tt + t) * PACK_ROWS, PACK_ROWS)], sem.at[s])

    def issue(d_ref, s):
        def body(t, c):
            for k in range(TOPK):
                row_copy(d_ref, k, t, s).start()
            return c
        lax.fori_loop(0, tt, body, 0, unroll=8)

    @pl.when(i == 0)
    def _():
        issue(dcur_ref, 0)

    @pl.when(i + 1 < nt)
    def _():
        issue(dnxt_ref, 1 - slot)

    pltpu.make_async_copy(yp_ref.at[pl.ds(0, TOPK * rows)], buf_ref.at[slot], sem.at[slot]).wait()

    tw = tw_ref[...]
    lo_sum = jnp.zeros((rows, LANES), F32)
    hi_sum = jnp.zeros((rows, LANES), F32)
    for k in range(TOPK):
        lo, hi = _unpack_pair(buf_ref[slot, k * rows:(k + 1) * rows, :])
        lo_sum = lo_sum + lo * tw[:, k:k + 1]
        hi_sum = hi_sum + hi * tw[:, k:k + 1]
    sum_ref[0] = lo_sum
    sum_ref[1] = hi_sum
    for c in range(PACK_ROWS):
        hi = c + PACK_ROWS
        o_ref[:, c * LANES:(c + 1) * LANES] = (
            h_ref[:, c * LANES:(c + 1) * LANES] + sum_ref[0, pl.ds(c, tt, stride=PACK_ROWS), :])
        o_ref[:, hi * LANES:(hi + 1) * LANES] = (
            h_ref[:, hi * LANES:(hi + 1) * LANES] + sum_ref[1, pl.ds(c, tt, stride=PACK_ROWS), :])
    o_ref[...] = _rms(o_ref[...], gf_ref[...])


def _combine(dest, yp, h, tw_rows, g_final, n):
    tt = min(n, 256)
    nt = n // tt
    grid_spec = pltpu.PrefetchScalarGridSpec(
        num_scalar_prefetch=0, grid=(nt,),
        in_specs=[
            pl.BlockSpec((TOPK, tt), lambda i: (0, i), memory_space=pltpu.SMEM),
            pl.BlockSpec((TOPK, tt), lambda i: (0, jnp.minimum(i + 1, nt - 1)), memory_space=pltpu.SMEM),
            pl.BlockSpec(memory_space=pl.ANY),
            pl.BlockSpec((tt, D), lambda i: (i, 0)),
            pl.BlockSpec((tt * PACK_ROWS, TOPK), lambda i: (i, 0)),
            pl.BlockSpec((1, D), lambda i: (0, 0)),
        ],
        out_specs=pl.BlockSpec((tt, D), lambda i: (i, 0)),
        scratch_shapes=[pltpu.VMEM((2, TOPK * tt * PACK_ROWS, LANES), U32),
                        pltpu.VMEM((2, tt * PACK_ROWS, LANES), F32), pltpu.SemaphoreType.DMA((2,))],
    )
    return pl.pallas_call(
        functools.partial(_combine_kernel, tt=tt, nt=nt), grid_spec=grid_spec,
        out_shape=jax.ShapeDtypeStruct((n, D), F32),
        compiler_params=_params(("arbitrary",)),
        name="combine_final_norm",
    )(dest, dest, yp, h, tw_rows, g_final)


def _permute_weights(w_in, w_uq, w_ukv, w_o_swa):
    widths = (HQ * DH, HKV * DH, HKV * DH, QR, KVR, DR, 2 * D)
    offs = [0]
    for w in widths:
        offs.append(offs[-1] + w)
    wq, wk, wv, wdq, wdkv, wkr, wgates = [w_in[:, offs[i]:offs[i + 1]] for i in range(7)]
    wq = wq.reshape(D, HKV, G, DH).transpose(0, 2, 1, 3).reshape(D, HQ * DH)
    pad = jnp.zeros((D, PROJ_COLS - COL_KR - DR), w_in.dtype)
    w_in_p = jnp.concatenate([wgates, wq, wdq, wdkv, wk, wv, wkr, pad], axis=1).astype(BF16)
    wo_a = w_o_swa.reshape(HKV, G, DH, D).transpose(1, 0, 2, 3).reshape(HQ * DH, D).astype(BF16)
    wuq = w_uq.reshape(QR, MH, DN + DR)
    wq_p = jnp.concatenate([wuq, jnp.zeros((QR, MH, 2 * LANES - DN - DR), w_uq.dtype)], axis=2)
    wq_p = wq_p.reshape(QR, MH * 2 * LANES).astype(BF16)
    wukv = w_ukv.reshape(KVR, MH, DN + DV)
    wk_p = wukv[:, :, :DN].reshape(KVR, MH * DN).astype(BF16)
    wv_p = wukv[:, :, DN:].reshape(KVR, MH * DV).astype(BF16)
    return w_in_p, wo_a, wq_p, wk_p, wv_p


def kernel(x, positions, g_attn, w_in, b_gate, sinks, g_q, w_uq, g_kv, w_ukv, w_o_swa, w_o_mla, w_out,
           g_ffn, w_router, b_router, w_gate_up, b_gate_up, w_down, b_down, g_final):
    b, s, _ = x.shape
    n = b * s
    assert w_in.shape[0] == 1, "single-layer block"
    x2 = x.reshape(n, D)

    w_in_p, wo_a, wq_p, wk_p, wv_p = _permute_weights(w_in[0], w_uq[0], w_ukv[0], w_o_swa[0])
    cos, sin = _rope_tables(positions, n)
    proj = _in_proj(x2, g_attn, w_in_p, n)
    qa, ka, qm, km, vm = _prep(proj, cos, sin, g_q, g_kv, wq_p, wk_p, wv_p, n)
    out_a = _swa(sinks[0], qa, ka, proj, b, s)
    out_b = _mla(qm, km, vm, b, s)

    mixed = _mix(out_a, out_b, proj, wo_a, w_o_mla[0].astype(BF16), b_gate, n)
    h, hnp, top_e, top_w, rank, cnt = _router(
        mixed, x2, w_out[0].astype(BF16), g_ffn, w_router[0].T, b_router[0][:, None], n)

    bm = 512
    n_blocks = -(-(n * TOPK) // bm) + E
    counts = cnt[:, 0].astype(jnp.int32)
    padded = (counts + bm - 1) // bm * bm
    pend = jnp.cumsum(padded)
    pstart = pend - padded
    n_used = pend[-1] // bm
    blocks = jnp.arange(n_blocks, dtype=jnp.int32)
    block_e = jnp.minimum(jnp.sum(pend[None, :] <= (blocks * bm)[:, None], axis=1), E - 1).astype(jnp.int32)
    block_valid = jnp.clip(counts[block_e] - (blocks * bm - pstart[block_e]), 0, bm).astype(jnp.int32)
    block_valid = jnp.where(blocks < n_used, block_valid, 0)
    block_e = jnp.where(blocks < n_used, block_e, block_e[jnp.maximum(n_used - 1, 0)])

    dest = _dest(pstart, top_e, rank, n)
    wg, wu, wd, xg = _prep_dispatch(counts, padded, pstart, dest, hnp, w_gate_up[0], w_down[0], n, n_blocks * bm)
    bgu = b_gate_up[0]
    yp = _moe(block_e, n_used.reshape(1).astype(jnp.int32), block_valid, xg, wg, wu, wd,
              bgu[:, None, 0::2], bgu[:, None, 1::2], b_down[0][:, None, :], n_blocks, bm)

    tw_rows = jnp.repeat(top_w.T, PACK_ROWS, axis=0)
    out = _combine(dest, yp, h, tw_rows, g_final[None, :], n)
    return out.reshape(b, s, D)
```

```python
import functools

import jax
import jax.numpy as jnp
from jax import lax
from jax.experimental import pallas as pl
from jax.experimental.pallas import tpu as pltpu

F32 = jnp.float32
BF16 = jnp.bfloat16
U32 = jnp.uint32

D = 2048
HQ, HKV, DH, WIN = 16, 2, 64, 128
G = HQ // HKV
MH, QR, KVR, DN, DR, DV = 8, 512, 256, 128, 64, 128
THETA = 10000.0
E, TOPK, DE = 32, 4, 2048
LIMIT, ALPHA = 7.0, 1.702
EPS = 1e-6
NEG = -1e30
LOG2E = 1.4426950408889634

LANES = 128
CHUNKS = D // LANES
PACK_ROWS = CHUNKS // 2
PIECE = 16
VMEM_LIMIT = 56 * 1024 * 1024

PROJ_COLS = 6400
COL_GATES, COL_Q, COL_DQ, COL_DKV, COL_K, COL_V, COL_KR = 0, 4096, 5120, 5632, 5888, 6016, 6144

NT = (((1,), (1,)), ((), ()))


def _params(sem, vmem=VMEM_LIMIT):
    return pltpu.CompilerParams(dimension_semantics=sem, vmem_limit_bytes=vmem)


def _rms(x, g):
    return x * lax.rsqrt(jnp.mean(x * x, axis=-1, keepdims=True) + EPS) * g


def _pack_pair(lo, hi):
    lo_bits = lax.bitcast_convert_type(lo.astype(BF16).astype(F32), U32)
    hi_bits = lax.bitcast_convert_type(hi.astype(BF16).astype(F32), U32)
    return (lo_bits >> 16) | hi_bits


def _unpack_pair(w):
    lo = lax.bitcast_convert_type(w << 16, F32)
    hi = lax.bitcast_convert_type(w & jnp.uint32(0xFFFF0000), F32)
    return lo, hi


def _rope128(v, cos, sin):
    lane = lax.broadcasted_iota(jnp.int32, v.shape, 1)
    rot = jnp.where((lane % DH) < (DH // 2), -pltpu.roll(v, LANES - DH // 2, 1), pltpu.roll(v, DH // 2, 1))
    return v * cos + rot * sin


def _rope_table_kernel(pos_ref, inv_ref, cos_ref, sin_ref):
    ang = pos_ref[...].astype(F32) * inv_ref[...]
    cos_ref[...] = jnp.cos(ang)
    sin_ref[...] = jnp.sin(ang)


def _rope_tables(positions, n):
    half = DH // 2
    per_row = LANES // half
    inv = jnp.power(THETA, -jnp.arange(half, dtype=F32) * 2.0 / DH)
    inv_row = jnp.tile(inv, per_row)[None, :]
    pos_rows = jnp.repeat(positions.reshape(n // per_row, per_row), half, axis=1)
    rows = n // per_row
    tr = min(rows, 1024)
    cos, sin = pl.pallas_call(
        _rope_table_kernel,
        grid=(rows // tr,),
        in_specs=[pl.BlockSpec((tr, LANES), lambda i: (i, 0)), pl.BlockSpec((1, LANES), lambda i: (0, 0))],
        out_specs=[pl.BlockSpec((tr, LANES), lambda i: (i, 0))] * 2,
        out_shape=[jax.ShapeDtypeStruct((rows, LANES), F32)] * 2,
        name="rope_tables",
    )(pos_rows, inv_row)
    cos = jnp.tile(cos.reshape(n, half), (1, per_row))
    sin = jnp.tile(sin.reshape(n, half), (1, per_row))
    return cos, sin


def _in_proj_kernel(x_ref, g_ref, w_ref, o_ref, xn_ref):
    @pl.when(pl.program_id(1) == 0)
    def _():
        xn_ref[...] = _rms(x_ref[...], g_ref[...]).astype(BF16)

    o_ref[...] = jnp.dot(xn_ref[...], w_ref[...], preferred_element_type=F32).astype(o_ref.dtype)


def _in_proj(x2, g_attn, w_in_p, n):
    tm = min(n, 1024)
    tn = 1280
    return pl.pallas_call(
        _in_proj_kernel,
        grid=(n // tm, PROJ_COLS // tn),
        in_specs=[
            pl.BlockSpec((tm, D), lambda i, j: (i, 0)),
            pl.BlockSpec((1, D), lambda i, j: (0, 0)),
            pl.BlockSpec((D, tn), lambda i, j: (0, j)),
        ],
        out_specs=pl.BlockSpec((tm, tn), lambda i, j: (i, j)),
        out_shape=jax.ShapeDtypeStruct((n, PROJ_COLS), BF16),
        scratch_shapes=[pltpu.VMEM((tm, D), BF16)],
        compiler_params=_params(("arbitrary", "arbitrary")),
        name="in_proj",
    )(x2, g_attn, w_in_p)


def _prep_kernel(q_ref, dq_ref, dkv_ref, k_ref, kr_ref, cos_ref, sin_ref, gq_ref, gkv_ref,
                 wq_ref, wk_ref, wv_ref, qa_ref, ka_ref, qm_ref, km_ref, vm_ref):
    cos = cos_ref[...]
    sin = sin_ref[...]
    swa_scale = DH ** -0.5 * LOG2E
    for c in range(HQ * DH // LANES):
        sl = slice(c * LANES, (c + 1) * LANES)
        qa_ref[:, sl] = (_rope128(q_ref[:, sl].astype(F32), cos, sin) * swa_scale).astype(BF16)
    ka_ref[...] = _rope128(k_ref[...].astype(F32), cos, sin).astype(BF16)

    mla_scale = (DN + DR) ** -0.5 * LOG2E
    cq = _rms(dq_ref[...].astype(F32), gq_ref[...]).astype(BF16)
    qb = jnp.dot(cq, wq_ref[...], preferred_element_type=F32)
    ckv = _rms(dkv_ref[...].astype(F32), gkv_ref[...]).astype(BF16)
    kn = jnp.dot(ckv, wk_ref[...], preferred_element_type=F32)
    vv = jnp.dot(ckv, wv_ref[...], preferred_element_type=F32)
    ones_col = jnp.where(lax.broadcasted_iota(jnp.int32, (vv.shape[0], LANES), 1) == 0, 1.0, 0.0).astype(BF16)
    kr = _rope128(kr_ref[...].astype(F32), cos, sin).astype(BF16)
    for h in range(MH):
        lo = h * 2 * LANES
        qm_ref[:, lo:lo + LANES] = (qb[:, lo:lo + LANES] * mla_scale).astype(BF16)
        qm_ref[:, lo + LANES:lo + 2 * LANES] = (
            _rope128(qb[:, lo + LANES:lo + 2 * LANES], cos, sin) * mla_scale).astype(BF16)
        km_ref[:, lo:lo + LANES] = kn[:, h * LANES:(h + 1) * LANES].astype(BF16)
        km_ref[:, lo + LANES:lo + 2 * LANES] = kr
        vm_ref[:, lo:lo + LANES] = vv[:, h * LANES:(h + 1) * LANES].astype(BF16)
        vm_ref[:, lo + LANES:lo + 2 * LANES] = ones_col


def _prep(proj, cos, sin, g_q, g_kv, wq_p, wk_p, wv_p, n):
    tm = min(n, 512)
    row = lambda w, cb: pl.BlockSpec((tm, w), lambda i: (i, cb))
    full = lambda a: pl.BlockSpec(a.shape, lambda i: (0, 0))
    return pl.pallas_call(
        _prep_kernel,
        grid=(n // tm,),
        in_specs=[
            row(HQ * DH, COL_Q // (HQ * DH)), row(QR, COL_DQ // QR), row(KVR, COL_DKV // KVR),
            row(LANES, COL_K // LANES), row(LANES, COL_KR // LANES),
            row(LANES, 0), row(LANES, 0), full(g_q), full(g_kv), full(wq_p), full(wk_p), full(wv_p),
        ],
        out_specs=[row(HQ * DH, 0), row(LANES, 0)] + [row(MH * 2 * LANES, 0)] * 3,
        out_shape=[jax.ShapeDtypeStruct((n, HQ * DH), BF16), jax.ShapeDtypeStruct((n, LANES), BF16)]
        + [jax.ShapeDtypeStruct((n, MH * 2 * LANES), BF16)] * 3,
        compiler_params=_params(("arbitrary",)),
        name="rope_mla_prep",
    )(proj, proj, proj, proj, proj, cos, sin, g_q, g_kv, wq_p, wk_p, wv_p)


def _swa_kernel(sinks_ref, q_ref, kc_ref, kp_ref, vc_ref, vp_ref, o_ref):
    blk = pl.program_id(1)
    q2 = jnp.concatenate([q_ref[:, g * LANES:(g + 1) * LANES] for g in range(G)], axis=0)
    k2 = jnp.concatenate([kp_ref[...], kc_ref[...]], axis=0)
    v2t = jnp.concatenate([vp_ref[...], vc_ref[...]], axis=0).astype(F32).T
    lane = lax.broadcasted_iota(jnp.int32, (2 * WIN, LANES), 1)
    vrow = lax.broadcasted_iota(jnp.int32, (LANES, 2 * WIN), 0)
    kj = lax.broadcasted_iota(jnp.int32, (2 * WIN, WIN), 0)
    qi = lax.broadcasted_iota(jnp.int32, (2 * WIN, WIN), 1)
    valid = (kj > qi) & (kj <= qi + WIN) & ((kj >= WIN) | (blk > 0))
    acc = jnp.zeros((LANES, G * WIN), F32)
    for h in range(HKV):
        kz = jnp.where((lane >= h * DH) & (lane < (h + 1) * DH), k2, jnp.zeros_like(k2))
        vz = jnp.where((vrow >= h * DH) & (vrow < (h + 1) * DH), v2t, 0.0).astype(BF16)
        s = lax.dot_general(kz, q2, NT, preferred_element_type=F32)
        ps, invs = [], []
        for g in range(G):
            sg = jnp.where(valid, s[:, g * WIN:(g + 1) * WIN], NEG)
            sink = sinks_ref[h * G + g] * LOG2E
            m = jnp.maximum(jnp.max(sg, axis=0, keepdims=True), sink)
            p = jnp.exp2(sg - m)
            l = jnp.sum(p, axis=0, keepdims=True) + jnp.exp2(sink - m)
            ps.append(p.astype(BF16))
            invs.append(1.0 / l)
        o_h = jnp.dot(vz, jnp.concatenate(ps, axis=1), preferred_element_type=F32)
        acc = acc + o_h * jnp.concatenate(invs, axis=1)
    for g in range(G):
        o_ref[:, g * LANES:(g + 1) * LANES] = acc[:, g * WIN:(g + 1) * WIN].T.astype(BF16)


def _swa(sinks, qa, ka, proj, b, s):
    nb = s // WIN
    cur = lambda cb: pl.BlockSpec((WIN, LANES), lambda bi, n, sk: (bi * nb + n, cb))
    prev = lambda cb: pl.BlockSpec((WIN, LANES), lambda bi, n, sk: (bi * nb + jnp.maximum(n - 1, 0), cb))
    grid_spec = pltpu.PrefetchScalarGridSpec(
        num_scalar_prefetch=1,
        grid=(b, nb),
        in_specs=[
            pl.BlockSpec((WIN, HQ * DH), lambda bi, n, sk: (bi * nb + n, 0)),
            cur(0), prev(0), cur(COL_V // LANES), prev(COL_V // LANES),
        ],
        out_specs=pl.BlockSpec((WIN, HQ * DH), lambda bi, n, sk: (bi * nb + n, 0)),
    )
    return pl.pallas_call(
        _swa_kernel,
        grid_spec=grid_spec,
        out_shape=jax.ShapeDtypeStruct((b * s, HQ * DH), BF16),
        compiler_params=_params(("arbitrary", "arbitrary")),
        name="swa_attention",
    )(sinks, qa, ka, ka, proj, proj)


def _mla_kernel(q_ref, k_ref, v_ref, o_ref, *, seq, tq):
    row = lax.broadcasted_iota(jnp.int32, (tq, tq), 0)
    col = lax.broadcasted_iota(jnp.int32, (tq, tq), 1)
    causal = col <= row
    for i in range(seq // tq):
        kv = (i + 1) * tq
        q = q_ref[i * tq:(i + 1) * tq, :]
        s = lax.dot_general(q, k_ref[0:kv, :], NT, preferred_element_type=F32)
        diag = jnp.where(causal, s[:, kv - tq:kv], NEG)
        s = diag if i == 0 else jnp.concatenate([s[:, 0:kv - tq], diag], axis=1)
        m = jnp.max(s, axis=-1, keepdims=True)
        p = jnp.exp2(s - m).astype(BF16)
        ov = jnp.dot(p, v_ref[0:kv, :], preferred_element_type=F32)
        o_ref[i * tq:(i + 1) * tq, :] = (ov[:, 0:DV] * (1.0 / ov[:, DV:DV + 1])).astype(BF16)


def _mla(qm, km, vm, b, s):
    tq = min(s, 512)
    return pl.pallas_call(
        functools.partial(_mla_kernel, seq=s, tq=tq),
        grid=(b, MH),
        in_specs=[
            pl.BlockSpec((s, 2 * LANES), lambda bi, h: (bi, h)),
            pl.BlockSpec((s, 2 * LANES), lambda bi, h: (bi, h)),
            pl.BlockSpec((s, 2 * LANES), lambda bi, h: (bi, h)),
        ],
        out_specs=pl.BlockSpec((s, DV), lambda bi, h: (bi, h)),
        out_shape=jax.ShapeDtypeStruct((b * s, MH * DV), BF16),
        compiler_params=_params(("arbitrary", "arbitrary")),
        name="mla_attention",
    )(qm, km, vm)


def _mix_kernel(oa_ref, ob_ref, ga_ref, gb_ref, wa_ref, wb_ref, ba_ref, bb_ref, o_ref):
    ya = jnp.dot(oa_ref[...], wa_ref[...], preferred_element_type=F32)
    yb = jnp.dot(ob_ref[...], wb_ref[...], preferred_element_type=F32)
    ga = jax.nn.sigmoid(ga_ref[...].astype(F32) + ba_ref[...])
    gb = jax.nn.sigmoid(gb_ref[...].astype(F32) + bb_ref[...])
    o_ref[...] = (ga * ya + gb * yb).astype(BF16)


def _mix(out_a, out_b, proj, wa, wb, b_gate, n):
    tm = min(n, 1024)
    tn = 1024
    nj = D // tn
    return pl.pallas_call(
        _mix_kernel,
        grid=(n // tm, nj),
        in_specs=[
            pl.BlockSpec((tm, HQ * DH), lambda i, j: (i, 0)),
            pl.BlockSpec((tm, MH * DV), lambda i, j: (i, 0)),
            pl.BlockSpec((tm, tn), lambda i, j: (i, j)),
            pl.BlockSpec((tm, tn), lambda i, j: (i, nj + j)),
            pl.BlockSpec((HQ * DH, tn), lambda i, j: (0, j)),
            pl.BlockSpec((MH * DV, tn), lambda i, j: (0, j)),
            pl.BlockSpec((1, tn), lambda i, j: (0, j)),
            pl.BlockSpec((1, tn), lambda i, j: (0, nj + j)),
        ],
        out_specs=pl.BlockSpec((tm, tn), lambda i, j: (i, j)),
        out_shape=jax.ShapeDtypeStruct((n, D), BF16),
        compiler_params=_params(("arbitrary", "arbitrary")),
        name="gated_mix",
    )(out_a, out_b, proj, proj, wa, wb, b_gate, b_gate)


def _router_kernel(mx_ref, x_ref, wo_ref, gf_ref, wr_ref, br_ref,
                   h_ref, hnp_ref, te_ref, tw_ref, rk_ref, cnt_ref, tb_ref, base_ref, *, tm, sub):
    @pl.when(pl.program_id(0) == 0)
    def _():
        base_ref[...] = jnp.zeros_like(base_ref)

    h = x_ref[...] + jnp.dot(mx_ref[...], wo_ref[...], preferred_element_type=F32)
    h_ref[...] = h
    hn = _rms(h, gf_ref[...])
    for c in range(PACK_ROWS):
        hi = c + PACK_ROWS
        hnp_ref[pl.ds(c, tm, stride=PACK_ROWS), :] = _pack_pair(
            hn[:, c * LANES:(c + 1) * LANES], hn[:, hi * LANES:(hi + 1) * LANES])

    logits = lax.dot_general(wr_ref[...], hn, NT, preferred_element_type=F32,
                             precision=lax.Precision.HIGHEST) + br_ref[...]
    eidx = lax.broadcasted_iota(jnp.int32, (E, tm), 0)
    vals, idxs = [], []
    l = logits
    for _ in range(TOPK):
        m = jnp.max(l, axis=0, keepdims=True)
        idx = jnp.min(jnp.where(l == m, eidx, E), axis=0, keepdims=True)
        vals.append(m)
        idxs.append(idx)
        l = jnp.where(eidx == idx, -jnp.inf, l)
    ex = [jnp.exp(v - vals[0]) for v in vals]
    tot = ex[0] + ex[1] + ex[2] + ex[3]
    tw_ref[...] = jnp.concatenate([e / tot for e in ex], axis=0)
    te_ref[...] = jnp.concatenate(idxs, axis=0)

    onehot = jnp.zeros((E, tm), F32)
    for idx in idxs:
        onehot = onehot + jnp.where(eidx == idx, 1.0, 0.0)
    r = lax.broadcasted_iota(jnp.int32, (tm, tm), 0)
    c = lax.broadcasted_iota(jnp.int32, (tm, tm), 1)
    earlier = jnp.where(r < c, 1.0, 0.0).astype(BF16)
    before = jnp.dot(onehot.astype(BF16), earlier, preferred_element_type=F32) + base_ref[:, 0:1]
    rk_ref[...] = jnp.concatenate(
        [jnp.sum(jnp.where(eidx == idx, before, 0.0), axis=0, keepdims=True) for idx in idxs],
        axis=0).astype(jnp.int32)
    lane = lax.broadcasted_iota(jnp.int32, (E, LANES), 1)
    tb = jnp.zeros((E, LANES), F32)
    for s in range(tm // sub):
        tb = jnp.where(lane == s, before[:, s * sub:s * sub + 1], tb)
    tb_ref[...] = tb
    base_ref[...] = base_ref[...] + jnp.sum(onehot, axis=1, keepdims=True)
    cnt_ref[...] = base_ref[...]


def _router(mixed, x2, wo, g_ffn, wr_t, b_router, n, tm, sub):
    row = lambda w: pl.BlockSpec((tm, w), lambda i: (i, 0))
    full = lambda a: pl.BlockSpec(a.shape, lambda i: (0, 0), pipeline_mode=pl.Buffered(1))
    tok = pl.BlockSpec((TOPK, tm), lambda i: (0, i))
    return pl.pallas_call(
        functools.partial(_router_kernel, tm=tm, sub=sub),
        grid=(n // tm,),
        in_specs=[row(D), row(D), full(wo), full(g_ffn), full(wr_t), full(b_router)],
        out_specs=[
            row(D), pl.BlockSpec((tm * PACK_ROWS, LANES), lambda i: (i, 0)), tok, tok, tok,
            pl.BlockSpec((E, LANES), lambda i: (0, 0)), pl.BlockSpec((E, LANES), lambda i: (i, 0)),
        ],
        out_shape=[
            jax.ShapeDtypeStruct((n, D), F32), jax.ShapeDtypeStruct((n * PACK_ROWS, LANES), U32),
            jax.ShapeDtypeStruct((TOPK, n), jnp.int32), jax.ShapeDtypeStruct((TOPK, n), F32),
            jax.ShapeDtypeStruct((TOPK, n), jnp.int32), jax.ShapeDtypeStruct((E, LANES), F32),
            jax.ShapeDtypeStruct((n // tm * E, LANES), F32),
        ],
        scratch_shapes=[pltpu.VMEM((E, LANES), F32)],
        compiler_params=_params(("arbitrary",)),
        name="outproj_router",
    )(mixed, x2, wo, g_ffn, wr_t, b_router)


def _dest_kernel(pstart_ref, te_ref, rk_ref, dest_ref):
    te = te_ref[...]
    d = rk_ref[...]
    for e in range(E):
        d = d + jnp.where(te == e, pstart_ref[e], 0)
    dest_ref[...] = d


def _dest(pstart, top_e, rank, n):
    tn = min(n, 8192)
    grid_spec = pltpu.PrefetchScalarGridSpec(
        num_scalar_prefetch=1, grid=(n // tn,),
        in_specs=[pl.BlockSpec((TOPK, tn), lambda i, ps: (0, i))] * 2,
        out_specs=pl.BlockSpec((TOPK, tn), lambda i, ps: (0, i)))
    return pl.pallas_call(
        _dest_kernel, grid_spec=grid_spec,
        out_shape=jax.ShapeDtypeStruct((TOPK, n), jnp.int32), name="dest_rows",
    )(pstart, top_e, rank)


def _prep_dispatch_kernel(cnt_ref, pad_ref, pst_ref, dest_ref, hnp_ref, wgu_ref, wdn_ref,
                          g_ref, u_ref, d_ref, xg_ref, zrow_ref, sem, zsem, *, tt, n_disp):
    i = pl.program_id(0)

    def row_copy(t, dst_row):
        return pltpu.make_async_copy(
            hnp_ref.at[pl.ds(pl.multiple_of(t * PACK_ROWS, PACK_ROWS), PACK_ROWS)],
            xg_ref.at[pl.ds(pl.multiple_of(dst_row * PACK_ROWS, PACK_ROWS), PACK_ROWS)], sem)

    def zero_copy(dst_row):
        return pltpu.make_async_copy(
            zrow_ref, xg_ref.at[pl.ds(pl.multiple_of(dst_row * PACK_ROWS, PACK_ROWS), PACK_ROWS)], zsem)

    @pl.when(i == 0)
    def _():
        zrow_ref[...] = jnp.zeros_like(zrow_ref)

        def per_expert(e, carry):
            base = pst_ref[e]

            def start(r, c):
                zero_copy(base + r).start()
                return c

            def wait(r, c):
                zero_copy(base + r).wait()
                return c

            lax.fori_loop(cnt_ref[e], pad_ref[e], start, 0)
            lax.fori_loop(cnt_ref[e], pad_ref[e], wait, 0)
            return carry

        lax.fori_loop(0, E, per_expert, 0)

    @pl.when(i < n_disp)
    def _():
        def start(t, c):
            for k in range(TOPK):
                row_copy(t, dest_ref[k, t]).start()
            return c

        lax.fori_loop(0, tt, start, 0, unroll=8)

    chunk = 2 * LANES
    r = lax.broadcasted_iota(jnp.int32, (chunk, chunk), 0)
    c = lax.broadcasted_iota(jnp.int32, (chunk, chunk), 1)
    src = jnp.where(c < LANES, 2 * c, 2 * (c - LANES) + 1)
    perm = jnp.where(r == src, 1.0, 0.0).astype(BF16)
    for k in range(wgu_ref.shape[1] // chunk):
        w = wgu_ref[:, k * chunk:(k + 1) * chunk].astype(BF16)
        o = jnp.dot(w, perm, preferred_element_type=F32)
        g_ref[:, k * LANES:(k + 1) * LANES] = o[:, :LANES].astype(BF16)
        u_ref[:, k * LANES:(k + 1) * LANES] = o[:, LANES:].astype(BF16)
    d_ref[...] = wdn_ref[...].astype(BF16)

    @pl.when(i < n_disp)
    def _():
        for _ in range(TOPK):
            pltpu.make_async_copy(hnp_ref, xg_ref.at[pl.ds(0, tt * PACK_ROWS)], sem).wait()


def _prep_dispatch(counts, padded, pstart, dest, hnp, w_gate_up, w_down, n, p_rows):
    tr = 512
    steps = E * (D // tr)
    tt = min(n, 256)
    n_disp = n // tt
    assert n_disp <= steps, "more token tiles than weight tiles"
    tile = lambda i, a, b, c: (i // (D // tr), i % (D // tr), 0)
    tok = lambda i: jnp.minimum(i, n_disp - 1)
    wout = pl.BlockSpec((None, tr, DE), tile)
    grid_spec = pltpu.PrefetchScalarGridSpec(
        num_scalar_prefetch=3, grid=(steps,),
        in_specs=[
            pl.BlockSpec((TOPK, tt), lambda i, a, b, c: (0, tok(i)), memory_space=pltpu.SMEM),
            pl.BlockSpec((tt * PACK_ROWS, LANES), lambda i, a, b, c: (tok(i), 0)),
            pl.BlockSpec((None, tr, 2 * DE), tile),
            pl.BlockSpec((None, tr, D), tile),
        ],
        out_specs=[wout, wout, pl.BlockSpec((None, tr, D), tile), pl.BlockSpec(memory_space=pl.ANY)],
        scratch_shapes=[pltpu.VMEM((PACK_ROWS, LANES), U32), pltpu.SemaphoreType.DMA(()),
                        pltpu.SemaphoreType.DMA(())],
    )
    return pl.pallas_call(
        functools.partial(_prep_dispatch_kernel, tt=tt, n_disp=n_disp), grid_spec=grid_spec,
        out_shape=[jax.ShapeDtypeStruct((E, D, DE), BF16), jax.ShapeDtypeStruct((E, D, DE), BF16),
                   jax.ShapeDtypeStruct((E, DE, D), BF16),
                   jax.ShapeDtypeStruct((p_rows * PACK_ROWS, LANES), U32)],
        compiler_params=pltpu.CompilerParams(dimension_semantics=("arbitrary",), has_side_effects=True,
                                             vmem_limit_bytes=VMEM_LIMIT),
        name="weight_prep_dispatch",
    )(counts, padded, pstart, dest, hnp, w_gate_up, w_down)


def _moe_kernel(be_ref, nu_ref, bv_ref, xp_ref, wg_ref, wu_ref, wd_ref, bg_ref, bu_ref, bd_ref, y_ref,
                x2_ref, acc_ref, *, bm, nj):
    blk = pl.program_id(0)
    j = pl.program_id(1)
    valid = bv_ref[blk]

    def body(rows):
        @pl.when(j == 0)
        def _():
            for c in range(PACK_ROWS):
                lo, hi = _unpack_pair(xp_ref[pl.ds(c, rows, stride=PACK_ROWS), :])
                x2_ref[0:rows, c * LANES:(c + 1) * LANES] = lo.astype(BF16)
                x2_ref[0:rows, (c + PACK_ROWS) * LANES:(c + PACK_ROWS + 1) * LANES] = hi.astype(BF16)

        x = x2_ref[0:rows, :]
        gate = jnp.dot(x, wg_ref[...], preferred_element_type=F32) + bg_ref[...]
        up = jnp.dot(x, wu_ref[...], preferred_element_type=F32) + bu_ref[...]
        gate = jnp.minimum(gate, LIMIT)
        up = jnp.clip(up, -LIMIT, LIMIT)
        act = (gate * jax.nn.sigmoid(ALPHA * gate) * (up + 1.0)).astype(BF16)
        part = jnp.dot(act, wd_ref[...], preferred_element_type=F32)

        @pl.when(j == 0)
        def _():
            acc_ref[0:rows, :] = part + bd_ref[...]

        @pl.when((j > 0) & (j < nj - 1))
        def _():
            acc_ref[0:rows, :] += part

        @pl.when(j == nj - 1)
        def _():
            y_ref[0:rows, :] = (acc_ref[0:rows, :] + part).astype(BF16)
            if rows < bm:
                y_ref[rows:bm, :] = jnp.zeros((bm - rows, D), BF16)

    @pl.when(valid > bm // 2)
    def _():
        body(bm)

    @pl.when((valid > 0) & (valid <= bm // 2))
    def _():
        body(bm // 2)


def _moe(block_e, n_used, block_valid, xg, wg, wu, wd, bg, bu, bd, n_blocks, bm):
    th = 1024
    nj = DE // th

    def wcol(blk, j, be, nu, bv):
        return (be[blk], 0, jnp.where(blk < nu[0], j, nj - 1))

    def wrow(blk, j, be, nu, bv):
        return (be[blk], jnp.where(blk < nu[0], j, nj - 1), 0)

    def xrow(blk, j, be, nu, bv):
        return (jnp.minimum(blk, nu[0] - 1), 0)

    grid_spec = pltpu.PrefetchScalarGridSpec(
        num_scalar_prefetch=3, grid=(n_blocks, nj),
        in_specs=[
            pl.BlockSpec((bm * PACK_ROWS, LANES), xrow),
            pl.BlockSpec((None, D, th), wcol), pl.BlockSpec((None, D, th), wcol),
            pl.BlockSpec((None, th, D), wrow),
            pl.BlockSpec((None, 1, th), wcol), pl.BlockSpec((None, 1, th), wcol),
            pl.BlockSpec((None, 1, D), lambda blk, j, be, nu, bv: (be[blk], 0, 0)),
        ],
        out_specs=pl.BlockSpec((bm, D), xrow),
        scratch_shapes=[pltpu.VMEM((bm, D), BF16), pltpu.VMEM((bm, D), F32)],
    )
    return pl.pallas_call(
        functools.partial(_moe_kernel, bm=bm, nj=nj), grid_spec=grid_spec,
        out_shape=jax.ShapeDtypeStruct((n_blocks * bm, D), BF16),
        compiler_params=_params(("arbitrary", "arbitrary")),
        name="moe_experts",
    )(block_e, n_used, block_valid, xg, wg, wu, wd, bg, bu, bd)


def _combine_kernel(ast_ref, np_ref, off_ref, y_ref, h_ref, slot_ref, tw_ref, gf_ref, o_ref, ycat_ref, sem,
                    *, tt, nt, kcat):
    i = pl.program_id(0)
    cur = i % 2

    def for_pieces(tile, s, fn):
        def per_expert(e, carry):
            src0 = ast_ref[tile, e]
            dst0 = off_ref[tile, e]

            def body(p, c):
                fn(pltpu.make_async_copy(
                    y_ref.at[pl.ds(pl.multiple_of(src0 + p * PIECE, PIECE), PIECE), :],
                    ycat_ref.at[s, pl.ds(pl.multiple_of(dst0 + p * PIECE, PIECE), PIECE), :], sem.at[s]))
                return c

            lax.fori_loop(0, np_ref[tile, e], body, 0)
            return carry

        lax.fori_loop(0, E, per_expert, 0)

    @pl.when(i == 0)
    def _():
        ycat_ref[...] = jnp.zeros_like(ycat_ref)
        for_pieces(0, 0, lambda cp: cp.start())

    @pl.when(i + 1 < nt)
    def _():
        for_pieces(i + 1, 1 - cur, lambda cp: cp.start())

    for_pieces(i, cur, lambda cp: cp.wait())

    slot = slot_ref[...]
    tw = tw_ref[...]
    col = lax.broadcasted_iota(jnp.int32, (tt, kcat), 1)
    sel = jnp.zeros((tt, kcat), F32)
    for k in range(TOPK):
        sel = jnp.where(col == slot[:, k:k + 1], tw[:, k:k + 1], sel)
    moe = jnp.dot(sel.astype(BF16), ycat_ref[cur], preferred_element_type=F32)
    o_ref[...] = _rms(h_ref[...] + moe, gf_ref[...])


def _combine(astart, npieces, off, y, h, slot, tw, g_final, n, tt, kcat):
    nt = n // tt
    grid_spec = pltpu.PrefetchScalarGridSpec(
        num_scalar_prefetch=3, grid=(nt,),
        in_specs=[
            pl.BlockSpec(memory_space=pl.ANY),
            pl.BlockSpec((tt, D), lambda i, a, b, c: (i, 0)),
            pl.BlockSpec((tt, TOPK), lambda i, a, b, c: (i, 0)),
            pl.BlockSpec((tt, TOPK), lambda i, a, b, c: (i, 0)),
            pl.BlockSpec((1, D), lambda i, a, b, c: (0, 0)),
        ],
        out_specs=pl.BlockSpec((tt, D), lambda i, a, b, c: (i, 0)),
        scratch_shapes=[pltpu.VMEM((2, kcat, D), BF16), pltpu.SemaphoreType.DMA((2,))],
    )
    return pl.pallas_call(
        functools.partial(_combine_kernel, tt=tt, nt=nt, kcat=kcat), grid_spec=grid_spec,
        out_shape=jax.ShapeDtypeStruct((n, D), F32),
        compiler_params=_params(("arbitrary",)),
        name="combine_final_norm",
    )(astart, npieces, off, y, h, slot, tw, g_final)


def _permute_weights(w_in, w_uq, w_ukv, w_o_swa):
    widths = (HQ * DH, HKV * DH, HKV * DH, QR, KVR, DR, 2 * D)
    offs = [0]
    for w in widths:
        offs.append(offs[-1] + w)
    wq, wk, wv, wdq, wdkv, wkr, wgates = [w_in[:, offs[i]:offs[i + 1]] for i in range(7)]
    wq = wq.reshape(D, HKV, G, DH).transpose(0, 2, 1, 3).reshape(D, HQ * DH)
    pad = jnp.zeros((D, PROJ_COLS - COL_KR - DR), w_in.dtype)
    w_in_p = jnp.concatenate([wgates, wq, wdq, wdkv, wk, wv, wkr, pad], axis=1).astype(BF16)
    wo_a = w_o_swa.reshape(HKV, G, DH, D).transpose(1, 0, 2, 3).reshape(HQ * DH, D).astype(BF16)
    wuq = w_uq.reshape(QR, MH, DN + DR)
    wq_p = jnp.concatenate([wuq, jnp.zeros((QR, MH, 2 * LANES - DN - DR), w_uq.dtype)], axis=2)
    wq_p = wq_p.reshape(QR, MH * 2 * LANES).astype(BF16)
    wukv = w_ukv.reshape(KVR, MH, DN + DV)
    wk_p = wukv[:, :, :DN].reshape(KVR, MH * DN).astype(BF16)
    wv_p = wukv[:, :, DN:].reshape(KVR, MH * DV).astype(BF16)
    return w_in_p, wo_a, wq_p, wk_p, wv_p


def kernel(x, positions, g_attn, w_in, b_gate, sinks, g_q, w_uq, g_kv, w_ukv, w_o_swa, w_o_mla, w_out,
           g_ffn, w_router, b_router, w_gate_up, b_gate_up, w_down, b_down, g_final):
    b, s, _ = x.shape
    n = b * s
    assert w_in.shape[0] == 1, "single-layer block"
    x2 = x.reshape(n, D)

    w_in_p, wo_a, wq_p, wk_p, wv_p = _permute_weights(w_in[0], w_uq[0], w_ukv[0], w_o_swa[0])
    cos, sin = _rope_tables(positions, n)
    proj = _in_proj(x2, g_attn, w_in_p, n)
    qa, ka, qm, km, vm = _prep(proj, cos, sin, g_q, g_kv, wq_p, wk_p, wv_p, n)
    out_a = _swa(sinks[0], qa, ka, proj, b, s)
    out_b = _mla(qm, km, vm, b, s)

    mixed = _mix(out_a, out_b, proj, wo_a, w_o_mla[0].astype(BF16), b_gate, n)
    tm = min(n, 512)
    tt = min(n, 256)
    h, hnp, top_e, top_w, rank, cnt, tile_base = _router(
        mixed, x2, w_out[0].astype(BF16), g_ffn, w_router[0].T, b_router[0][:, None], n, tm, tt)

    bm = 512
    n_blocks = -(-(n * TOPK) // bm) + E
    counts = cnt[:, 0].astype(jnp.int32)
    padded = (counts + bm - 1) // bm * bm
    pend = jnp.cumsum(padded)
    pstart = pend - padded
    n_used = pend[-1] // bm
    blocks = jnp.arange(n_blocks, dtype=jnp.int32)
    block_e = jnp.minimum(jnp.sum(pend[None, :] <= (blocks * bm)[:, None], axis=1), E - 1).astype(jnp.int32)
    block_valid = jnp.clip(counts[block_e] - (blocks * bm - pstart[block_e]), 0, bm).astype(jnp.int32)
    block_valid = jnp.where(blocks < n_used, block_valid, 0)
    block_e = jnp.where(blocks < n_used, block_e, block_e[jnp.maximum(n_used - 1, 0)])

    dest = _dest(pstart, top_e, rank, n)
    wg, wu, wd, xg = _prep_dispatch(counts, padded, pstart, dest, hnp, w_gate_up[0], w_down[0], n, n_blocks * bm)
    bgu = b_gate_up[0]
    y = _moe(block_e, n_used.reshape(1).astype(jnp.int32), block_valid, xg, wg, wu, wd,
             bgu[:, None, 0::2], bgu[:, None, 1::2], b_down[0][:, None, :], n_blocks, bm)

    nsub = tm // tt
    before = tile_base.reshape(n // tm, E, LANES)[:, :, :nsub].transpose(0, 2, 1).reshape(n // tt, E)
    before = before.astype(jnp.int32)
    after = jnp.concatenate([before[1:], counts[None, :]], axis=0)
    first = pstart[None, :] + before
    astart = first // PIECE * PIECE
    npieces = jnp.where(after > before, (first + (after - before) - astart + PIECE - 1) // PIECE, 0)
    off = (jnp.cumsum(npieces, axis=1) - npieces) * PIECE
    kcat = -(-(tt * TOPK + E * 2 * (PIECE - 1)) // (2 * LANES)) * (2 * LANES)
    shift = astart - off
    tile_of = jnp.arange(n, dtype=jnp.int32) // tt
    slot = dest - shift[tile_of[None, :], top_e]
    out = _combine(astart.astype(jnp.int32), npieces.astype(jnp.int32), off.astype(jnp.int32), y, h,
                   slot.T, top_w.T, g_final[None, :], n, tt, kcat)
    return out.reshape(b, s, D)
```

```python
import functools

import jax
import jax.numpy as jnp
from jax import lax
from jax.experimental import pallas as pl
from jax.experimental.pallas import tpu as pltpu

F32 = jnp.float32
BF16 = jnp.bfloat16
U32 = jnp.uint32

D = 2048
HQ, HKV, DH, WIN = 16, 2, 64, 128
G = HQ // HKV
MH, QR, KVR, DN, DR, DV = 8, 512, 256, 128, 64, 128
THETA = 10000.0
E, TOPK, DE = 32, 4, 2048
LIMIT, ALPHA = 7.0, 1.702
EPS = 1e-6
NEG = -1e30
LOG2E = 1.4426950408889634

LANES = 128
CHUNKS = D // LANES
PACK_ROWS = CHUNKS // 2
PIECE = 16
VMEM_LIMIT = 56 * 1024 * 1024

PROJ_COLS = 6400
COL_GATES, COL_Q, COL_DQ, COL_DKV, COL_K, COL_V, COL_KR = 0, 4096, 5120, 5632, 5888, 6016, 6144

NT = (((1,), (1,)), ((), ()))


def _params(sem, vmem=VMEM_LIMIT):
    return pltpu.CompilerParams(dimension_semantics=sem, vmem_limit_bytes=vmem)


def _rms(x, g):
    return x * lax.rsqrt(jnp.mean(x * x, axis=-1, keepdims=True) + EPS) * g


def _pack_pair(lo, hi):
    lo_bits = lax.bitcast_convert_type(lo.astype(BF16).astype(F32), U32)
    hi_bits = lax.bitcast_convert_type(hi.astype(BF16).astype(F32), U32)
    return (lo_bits >> 16) | hi_bits


def _unpack_pair(w):
    lo = lax.bitcast_convert_type(w << 16, F32)
    hi = lax.bitcast_convert_type(w & jnp.uint32(0xFFFF0000), F32)
    return lo, hi


def _rope128(v, cos, sin):
    lane = lax.broadcasted_iota(jnp.int32, v.shape, 1)
    rot = jnp.where((lane % DH) < (DH // 2), -pltpu.roll(v, LANES - DH // 2, 1), pltpu.roll(v, DH // 2, 1))
    return v * cos + rot * sin


def _rope_table_kernel(pos_ref, inv_ref, cos_ref, sin_ref):
    ang = pos_ref[...].astype(F32) * inv_ref[...]
    cos_ref[...] = jnp.cos(ang)
    sin_ref[...] = jnp.sin(ang)


def _rope_tables(positions, n):
    half = DH // 2
    per_row = LANES // half
    inv = jnp.power(THETA, -jnp.arange(half, dtype=F32) * 2.0 / DH)
    inv_row = jnp.tile(inv, per_row)[None, :]
    pos_rows = jnp.repeat(positions.reshape(n // per_row, per_row), half, axis=1)
    rows = n // per_row
    tr = min(rows, 1024)
    cos, sin = pl.pallas_call(
        _rope_table_kernel,
        grid=(rows // tr,),
        in_specs=[pl.BlockSpec((tr, LANES), lambda i: (i, 0)), pl.BlockSpec((1, LANES), lambda i: (0, 0))],
        out_specs=[pl.BlockSpec((tr, LANES), lambda i: (i, 0))] * 2,
        out_shape=[jax.ShapeDtypeStruct((rows, LANES), F32)] * 2,
        name="rope_tables",
    )(pos_rows, inv_row)
    cos = jnp.tile(cos.reshape(n, half), (1, per_row))
    sin = jnp.tile(sin.reshape(n, half), (1, per_row))
    return cos, sin


def _in_proj_kernel(x_ref, g_ref, w_ref, o_ref, xn_ref):
    @pl.when(pl.program_id(1) == 0)
    def _():
        xn_ref[...] = _rms(x_ref[...], g_ref[...]).astype(BF16)

    o_ref[...] = jnp.dot(xn_ref[...], w_ref[...], preferred_element_type=F32).astype(o_ref.dtype)


def _in_proj(x2, g_attn, w_in_p, n):
    tm = min(n, 1024)
    tn = 1280
    return pl.pallas_call(
        _in_proj_kernel,
        grid=(n // tm, PROJ_COLS // tn),
        in_specs=[
            pl.BlockSpec((tm, D), lambda i, j: (i, 0)),
            pl.BlockSpec((1, D), lambda i, j: (0, 0)),
            pl.BlockSpec((D, tn), lambda i, j: (0, j)),
        ],
        out_specs=pl.BlockSpec((tm, tn), lambda i, j: (i, j)),
        out_shape=jax.ShapeDtypeStruct((n, PROJ_COLS), BF16),
        scratch_shapes=[pltpu.VMEM((tm, D), BF16)],
        compiler_params=_params(("arbitrary", "arbitrary")),
        name="in_proj",
    )(x2, g_attn, w_in_p)


def _prep_kernel(q_ref, dq_ref, dkv_ref, k_ref, kr_ref, cos_ref, sin_ref, gq_ref, gkv_ref,
                 wq_ref, wk_ref, wv_ref, qa_ref, ka_ref, qm_ref, km_ref, vm_ref):
    cos = cos_ref[...]
    sin = sin_ref[...]
    swa_scale = DH ** -0.5 * LOG2E
    for c in range(HQ * DH // LANES):
        sl = slice(c * LANES, (c + 1) * LANES)
        qa_ref[:, sl] = (_rope128(q_ref[:, sl].astype(F32), cos, sin) * swa_scale).astype(BF16)
    ka_ref[...] = _rope128(k_ref[...].astype(F32), cos, sin).astype(BF16)

    mla_scale = (DN + DR) ** -0.5 * LOG2E
    cq = _rms(dq_ref[...].astype(F32), gq_ref[...]).astype(BF16)
    qb = jnp.dot(cq, wq_ref[...], preferred_element_type=F32)
    ckv = _rms(dkv_ref[...].astype(F32), gkv_ref[...]).astype(BF16)
    kn = jnp.dot(ckv, wk_ref[...], preferred_element_type=F32)
    vv = jnp.dot(ckv, wv_ref[...], preferred_element_type=F32)
    ones_col = jnp.where(lax.broadcasted_iota(jnp.int32, (vv.shape[0], LANES), 1) == 0, 1.0, 0.0).astype(BF16)
    kr = _rope128(kr_ref[...].astype(F32), cos, sin).astype(BF16)
    for h in range(MH):
        lo = h * 2 * LANES
        qm_ref[:, lo:lo + LANES] = (qb[:, lo:lo + LANES] * mla_scale).astype(BF16)
        qm_ref[:, lo + LANES:lo + 2 * LANES] = (
            _rope128(qb[:, lo + LANES:lo + 2 * LANES], cos, sin) * mla_scale).astype(BF16)
        km_ref[:, lo:lo + LANES] = kn[:, h * LANES:(h + 1) * LANES].astype(BF16)
        km_ref[:, lo + LANES:lo + 2 * LANES] = kr
        vm_ref[:, lo:lo + LANES] = vv[:, h * LANES:(h + 1) * LANES].astype(BF16)
        vm_ref[:, lo + LANES:lo + 2 * LANES] = ones_col


def _prep(proj, cos, sin, g_q, g_kv, wq_p, wk_p, wv_p, n):
    tm = min(n, 512)
    row = lambda w, cb: pl.BlockSpec((tm, w), lambda i: (i, cb))
    full = lambda a: pl.BlockSpec(a.shape, lambda i: (0, 0))
    return pl.pallas_call(
        _prep_kernel,
        grid=(n // tm,),
        in_specs=[
            row(HQ * DH, COL_Q // (HQ * DH)), row(QR, COL_DQ // QR), row(KVR, COL_DKV // KVR),
            row(LANES, COL_K // LANES), row(LANES, COL_KR // LANES),
            row(LANES, 0), row(LANES, 0), full(g_q), full(g_kv), full(wq_p), full(wk_p), full(wv_p),
        ],
        out_specs=[row(HQ * DH, 0), row(LANES, 0)] + [row(MH * 2 * LANES, 0)] * 3,
        out_shape=[jax.ShapeDtypeStruct((n, HQ * DH), BF16), jax.ShapeDtypeStruct((n, LANES), BF16)]
        + [jax.ShapeDtypeStruct((n, MH * 2 * LANES), BF16)] * 3,
        compiler_params=_params(("arbitrary",)),
        name="rope_mla_prep",
    )(proj, proj, proj, proj, proj, cos, sin, g_q, g_kv, wq_p, wk_p, wv_p)


def _swa_kernel(sinks_ref, q_ref, kc_ref, kp_ref, vc_ref, vp_ref, o_ref):
    blk = pl.program_id(1)
    q2 = jnp.concatenate([q_ref[:, g * LANES:(g + 1) * LANES] for g in range(G)], axis=0)
    k2 = jnp.concatenate([kp_ref[...], kc_ref[...]], axis=0)
    v2t = jnp.concatenate([vp_ref[...], vc_ref[...]], axis=0).astype(F32).T
    lane = lax.broadcasted_iota(jnp.int32, (2 * WIN, LANES), 1)
    vrow = lax.broadcasted_iota(jnp.int32, (LANES, 2 * WIN), 0)
    kj = lax.broadcasted_iota(jnp.int32, (2 * WIN, WIN), 0)
    qi = lax.broadcasted_iota(jnp.int32, (2 * WIN, WIN), 1)
    valid = (kj > qi) & (kj <= qi + WIN) & ((kj >= WIN) | (blk > 0))
    acc = jnp.zeros((LANES, G * WIN), F32)
    for h in range(HKV):
        kz = jnp.where((lane >= h * DH) & (lane < (h + 1) * DH), k2, jnp.zeros_like(k2))
        vz = jnp.where((vrow >= h * DH) & (vrow < (h + 1) * DH), v2t, 0.0).astype(BF16)
        s = lax.dot_general(kz, q2, NT, preferred_element_type=F32)
        ps, invs = [], []
        for g in range(G):
            sg = jnp.where(valid, s[:, g * WIN:(g + 1) * WIN], NEG)
            sink = sinks_ref[h * G + g] * LOG2E
            m = jnp.maximum(jnp.max(sg, axis=0, keepdims=True), sink)
            p = jnp.exp2(sg - m)
            l = jnp.sum(p, axis=0, keepdims=True) + jnp.exp2(sink - m)
            ps.append(p.astype(BF16))
            invs.append(1.0 / l)
        o_h = jnp.dot(vz, jnp.concatenate(ps, axis=1), preferred_element_type=F32)
        acc = acc + o_h * jnp.concatenate(invs, axis=1)
    for g in range(G):
        o_ref[:, g * LANES:(g + 1) * LANES] = acc[:, g * WIN:(g + 1) * WIN].T.astype(BF16)


def _swa(sinks, qa, ka, proj, b, s):
    nb = s // WIN
    cur = lambda cb: pl.BlockSpec((WIN, LANES), lambda bi, n, sk: (bi * nb + n, cb))
    prev = lambda cb: pl.BlockSpec((WIN, LANES), lambda bi, n, sk: (bi * nb + jnp.maximum(n - 1, 0), cb))
    grid_spec = pltpu.PrefetchScalarGridSpec(
        num_scalar_prefetch=1,
        grid=(b, nb),
        in_specs=[
            pl.BlockSpec((WIN, HQ * DH), lambda bi, n, sk: (bi * nb + n, 0)),
            cur(0), prev(0), cur(COL_V // LANES), prev(COL_V // LANES),
        ],
        out_specs=pl.BlockSpec((WIN, HQ * DH), lambda bi, n, sk: (bi * nb + n, 0)),
    )
    return pl.pallas_call(
        _swa_kernel,
        grid_spec=grid_spec,
        out_shape=jax.ShapeDtypeStruct((b * s, HQ * DH), BF16),
        compiler_params=_params(("arbitrary", "arbitrary")),
        name="swa_attention",
    )(sinks, qa, ka, ka, proj, proj)


def _mla_kernel(q_ref, k_ref, v_ref, o_ref, *, seq, tq):
    row = lax.broadcasted_iota(jnp.int32, (tq, tq), 0)
    col = lax.broadcasted_iota(jnp.int32, (tq, tq), 1)
    causal = col <= row
    for i in range(seq // tq):
        kv = (i + 1) * tq
        q = q_ref[i * tq:(i + 1) * tq, :]
        s = lax.dot_general(q, k_ref[0:kv, :], NT, preferred_element_type=F32)
        diag = jnp.where(causal, s[:, kv - tq:kv], NEG)
        s = diag if i == 0 else jnp.concatenate([s[:, 0:kv - tq], diag], axis=1)
        m = jnp.max(s, axis=-1, keepdims=True)
        p = jnp.exp2(s - m).astype(BF16)
        ov = jnp.dot(p, v_ref[0:kv, :], preferred_element_type=F32)
        o_ref[i * tq:(i + 1) * tq, :] = (ov[:, 0:DV] * (1.0 / ov[:, DV:DV + 1])).astype(BF16)


def _mla(qm, km, vm, b, s):
    tq = min(s, 512)
    return pl.pallas_call(
        functools.partial(_mla_kernel, seq=s, tq=tq),
        grid=(b, MH),
        in_specs=[
            pl.BlockSpec((s, 2 * LANES), lambda bi, h: (bi, h)),
            pl.BlockSpec((s, 2 * LANES), lambda bi, h: (bi, h)),
            pl.BlockSpec((s, 2 * LANES), lambda bi, h: (bi, h)),
        ],
        out_specs=pl.BlockSpec((s, DV), lambda bi, h: (bi, h)),
        out_shape=jax.ShapeDtypeStruct((b * s, MH * DV), BF16),
        compiler_params=_params(("arbitrary", "arbitrary")),
        name="mla_attention",
    )(qm, km, vm)


def _mix_kernel(oa_ref, ob_ref, ga_ref, gb_ref, wa_ref, wb_ref, ba_ref, bb_ref, o_ref):
    ya = jnp.dot(oa_ref[...], wa_ref[...], preferred_element_type=F32)
    yb = jnp.dot(ob_ref[...], wb_ref[...], preferred_element_type=F32)
    ga = jax.nn.sigmoid(ga_ref[...].astype(F32) + ba_ref[...])
    gb = jax.nn.sigmoid(gb_ref[...].astype(F32) + bb_ref[...])
    o_ref[...] = (ga * ya + gb * yb).astype(BF16)


def _mix(out_a, out_b, proj, wa, wb, b_gate, n):
    tm = min(n, 1024)
    tn = 1024
    nj = D // tn
    return pl.pallas_call(
        _mix_kernel,
        grid=(n // tm, nj),
        in_specs=[
            pl.BlockSpec((tm, HQ * DH), lambda i, j: (i, 0)),
            pl.BlockSpec((tm, MH * DV), lambda i, j: (i, 0)),
            pl.BlockSpec((tm, tn), lambda i, j: (i, j)),
            pl.BlockSpec((tm, tn), lambda i, j: (i, nj + j)),
            pl.BlockSpec((HQ * DH, tn), lambda i, j: (0, j)),
            pl.BlockSpec((MH * DV, tn), lambda i, j: (0, j)),
            pl.BlockSpec((1, tn), lambda i, j: (0, j)),
            pl.BlockSpec((1, tn), lambda i, j: (0, nj + j)),
        ],
        out_specs=pl.BlockSpec((tm, tn), lambda i, j: (i, j)),
        out_shape=jax.ShapeDtypeStruct((n, D), BF16),
        compiler_params=_params(("arbitrary", "arbitrary")),
        name="gated_mix",
    )(out_a, out_b, proj, proj, wa, wb, b_gate, b_gate)


def _router_kernel(mx_ref, x_ref, wo_ref, gf_ref, wr_ref, br_ref,
                   h_ref, hnp_ref, te_ref, tw_ref, rk_ref, cnt_ref, tb_ref, base_ref, *, tm, sub):
    @pl.when(pl.program_id(0) == 0)
    def _():
        base_ref[...] = jnp.zeros_like(base_ref)

    h = x_ref[...] + jnp.dot(mx_ref[...], wo_ref[...], preferred_element_type=F32)
    h_ref[...] = h
    hn = _rms(h, gf_ref[...])
    for c in range(PACK_ROWS):
        hi = c + PACK_ROWS
        hnp_ref[pl.ds(c, tm, stride=PACK_ROWS), :] = _pack_pair(
            hn[:, c * LANES:(c + 1) * LANES], hn[:, hi * LANES:(hi + 1) * LANES])

    logits = lax.dot_general(wr_ref[...], hn, NT, preferred_element_type=F32,
                             precision=lax.Precision.HIGHEST) + br_ref[...]
    eidx = lax.broadcasted_iota(jnp.int32, (E, tm), 0)
    vals, idxs = [], []
    l = logits
    for _ in range(TOPK):
        m = jnp.max(l, axis=0, keepdims=True)
        idx = jnp.min(jnp.where(l == m, eidx, E), axis=0, keepdims=True)
        vals.append(m)
        idxs.append(idx)
        l = jnp.where(eidx == idx, -jnp.inf, l)
    ex = [jnp.exp(v - vals[0]) for v in vals]
    tot = ex[0] + ex[1] + ex[2] + ex[3]
    tw_ref[...] = jnp.concatenate([e / tot for e in ex], axis=0)
    te_ref[...] = jnp.concatenate(idxs, axis=0)

    onehot = jnp.zeros((E, tm), F32)
    for idx in idxs:
        onehot = onehot + jnp.where(eidx == idx, 1.0, 0.0)
    r = lax.broadcasted_iota(jnp.int32, (tm, tm), 0)
    c = lax.broadcasted_iota(jnp.int32, (tm, tm), 1)
    earlier = jnp.where(r < c, 1.0, 0.0).astype(BF16)
    before = jnp.dot(onehot.astype(BF16), earlier, preferred_element_type=F32) + base_ref[:, 0:1]
    rk_ref[...] = jnp.concatenate(
        [jnp.sum(jnp.where(eidx == idx, before, 0.0), axis=0, keepdims=True) for idx in idxs],
        axis=0).astype(jnp.int32)
    lane = lax.broadcasted_iota(jnp.int32, (E, LANES), 1)
    tb = jnp.zeros((E, LANES), F32)
    for s in range(tm // sub):
        tb = jnp.where(lane == s, before[:, s * sub:s * sub + 1], tb)
    tb_ref[...] = tb
    base_ref[...] = base_ref[...] + jnp.sum(onehot, axis=1, keepdims=True)
    cnt_ref[...] = base_ref[...]


def _router(mixed, x2, wo, g_ffn, wr_t, b_router, n, tm, sub):
    row = lambda w: pl.BlockSpec((tm, w), lambda i: (i, 0))
    full = lambda a: pl.BlockSpec(a.shape, lambda i: (0, 0), pipeline_mode=pl.Buffered(1))
    tok = pl.BlockSpec((TOPK, tm), lambda i: (0, i))
    return pl.pallas_call(
        functools.partial(_router_kernel, tm=tm, sub=sub),
        grid=(n // tm,),
        in_specs=[row(D), row(D), full(wo), full(g_ffn), full(wr_t), full(b_router)],
        out_specs=[
            row(D), pl.BlockSpec((tm * PACK_ROWS, LANES), lambda i: (i, 0)), tok, tok, tok,
            pl.BlockSpec((E, LANES), lambda i: (0, 0)), pl.BlockSpec((E, LANES), lambda i: (i, 0)),
        ],
        out_shape=[
            jax.ShapeDtypeStruct((n, D), F32), jax.ShapeDtypeStruct((n * PACK_ROWS, LANES), U32),
            jax.ShapeDtypeStruct((TOPK, n), jnp.int32), jax.ShapeDtypeStruct((TOPK, n), F32),
            jax.ShapeDtypeStruct((TOPK, n), jnp.int32), jax.ShapeDtypeStruct((E, LANES), F32),
            jax.ShapeDtypeStruct((n // tm * E, LANES), F32),
        ],
        scratch_shapes=[pltpu.VMEM((E, LANES), F32)],
        compiler_params=_params(("arbitrary",)),
        name="outproj_router",
    )(mixed, x2, wo, g_ffn, wr_t, b_router)


def _dest_kernel(pstart_ref, shift_ref, te_ref, rk_ref, dest_ref, slot_ref):
    tile = pl.program_id(0)
    te = te_ref[...]
    d = rk_ref[...]
    s = rk_ref[...]
    for e in range(E):
        hit = te == e
        d = d + jnp.where(hit, pstart_ref[e], 0)
        s = s + jnp.where(hit, pstart_ref[e] - shift_ref[tile, e], 0)
    dest_ref[...] = d
    slot_ref[...] = s


def _dest(pstart, shift, top_e, rank, n, tt):
    tok = pl.BlockSpec((TOPK, tt), lambda i, ps, sh: (0, i))
    grid_spec = pltpu.PrefetchScalarGridSpec(
        num_scalar_prefetch=2, grid=(n // tt,), in_specs=[tok, tok], out_specs=[tok, tok])
    return pl.pallas_call(
        _dest_kernel, grid_spec=grid_spec,
        out_shape=[jax.ShapeDtypeStruct((TOPK, n), jnp.int32)] * 2, name="dest_rows",
    )(pstart, shift, top_e, rank)


def _prep_dispatch_kernel(cnt_ref, pad_ref, pst_ref, dest_ref, hnp_ref, wgu_ref, wdn_ref,
                          g_ref, u_ref, d_ref, xg_ref, zrow_ref, sem, zsem, *, tt, n_disp):
    i = pl.program_id(0)

    def row_copy(t, dst_row):
        return pltpu.make_async_copy(
            hnp_ref.at[pl.ds(pl.multiple_of(t * PACK_ROWS, PACK_ROWS), PACK_ROWS)],
            xg_ref.at[pl.ds(pl.multiple_of(dst_row * PACK_ROWS, PACK_ROWS), PACK_ROWS)], sem)

    def zero_copy(dst_row):
        return pltpu.make_async_copy(
            zrow_ref, xg_ref.at[pl.ds(pl.multiple_of(dst_row * PACK_ROWS, PACK_ROWS), PACK_ROWS)], zsem)

    @pl.when(i == 0)
    def _():
        zrow_ref[...] = jnp.zeros_like(zrow_ref)

        def per_expert(e, carry):
            base = pst_ref[e]

            def start(r, c):
                zero_copy(base + r).start()
                return c

            def wait(r, c):
                zero_copy(base + r).wait()
                return c

            lax.fori_loop(cnt_ref[e], pad_ref[e], start, 0)
            lax.fori_loop(cnt_ref[e], pad_ref[e], wait, 0)
            return carry

        lax.fori_loop(0, E, per_expert, 0)

    @pl.when(i < n_disp)
    def _():
        def start(t, c):
            for k in range(TOPK):
                row_copy(t, dest_ref[k, t]).start()
            return c

        lax.fori_loop(0, tt, start, 0, unroll=8)

    chunk = 2 * LANES
    r = lax.broadcasted_iota(jnp.int32, (chunk, chunk), 0)
    c = lax.broadcasted_iota(jnp.int32, (chunk, chunk), 1)
    src = jnp.where(c < LANES, 2 * c, 2 * (c - LANES) + 1)
    perm = jnp.where(r == src, 1.0, 0.0).astype(BF16)
    for k in range(wgu_ref.shape[1] // chunk):
        w = wgu_ref[:, k * chunk:(k + 1) * chunk].astype(BF16)
        o = jnp.dot(w, perm, preferred_element_type=F32)
        g_ref[:, k * LANES:(k + 1) * LANES] = o[:, :LANES].astype(BF16)
        u_ref[:, k * LANES:(k + 1) * LANES] = o[:, LANES:].astype(BF16)
    d_ref[...] = wdn_ref[...].astype(BF16)

    @pl.when(i < n_disp)
    def _():
        for _ in range(TOPK):
            pltpu.make_async_copy(hnp_ref, xg_ref.at[pl.ds(0, tt * PACK_ROWS)], sem).wait()


def _prep_dispatch(counts, padded, pstart, dest, hnp, w_gate_up, w_down, n, p_rows):
    tr = 512
    steps = E * (D // tr)
    tt = min(n, 256)
    n_disp = n // tt
    assert n_disp <= steps, "more token tiles than weight tiles"
    tile = lambda i, a, b, c: (i // (D // tr), i % (D // tr), 0)
    tok = lambda i: jnp.minimum(i, n_disp - 1)
    wout = pl.BlockSpec((None, tr, DE), tile)
    grid_spec = pltpu.PrefetchScalarGridSpec(
        num_scalar_prefetch=3, grid=(steps,),
        in_specs=[
            pl.BlockSpec((TOPK, tt), lambda i, a, b, c: (0, tok(i)), memory_space=pltpu.SMEM),
            pl.BlockSpec((tt * PACK_ROWS, LANES), lambda i, a, b, c: (tok(i), 0)),
            pl.BlockSpec((None, tr, 2 * DE), tile),
            pl.BlockSpec((None, tr, D), tile),
        ],
        out_specs=[wout, wout, pl.BlockSpec((None, tr, D), tile), pl.BlockSpec(memory_space=pl.ANY)],
        scratch_shapes=[pltpu.VMEM((PACK_ROWS, LANES), U32), pltpu.SemaphoreType.DMA(()),
                        pltpu.SemaphoreType.DMA(())],
    )
    return pl.pallas_call(
        functools.partial(_prep_dispatch_kernel, tt=tt, n_disp=n_disp), grid_spec=grid_spec,
        out_shape=[jax.ShapeDtypeStruct((E, D, DE), BF16), jax.ShapeDtypeStruct((E, D, DE), BF16),
                   jax.ShapeDtypeStruct((E, DE, D), BF16),
                   jax.ShapeDtypeStruct((p_rows * PACK_ROWS, LANES), U32)],
        compiler_params=pltpu.CompilerParams(dimension_semantics=("arbitrary",), has_side_effects=True,
                                             vmem_limit_bytes=VMEM_LIMIT),
        name="weight_prep_dispatch",
    )(counts, padded, pstart, dest, hnp, w_gate_up, w_down)


def _moe_kernel(be_ref, nu_ref, bv_ref, xp_ref, wg_ref, wu_ref, wd_ref, bg_ref, bu_ref, bd_ref, y_ref,
                x2_ref, acc_ref, *, bm, nj):
    blk = pl.program_id(0)
    j = pl.program_id(1)
    valid = bv_ref[blk]

    def body(rows):
        @pl.when(j == 0)
        def _():
            for c in range(PACK_ROWS):
                lo, hi = _unpack_pair(xp_ref[pl.ds(c, rows, stride=PACK_ROWS), :])
                x2_ref[0:rows, c * LANES:(c + 1) * LANES] = lo.astype(BF16)
                x2_ref[0:rows, (c + PACK_ROWS) * LANES:(c + PACK_ROWS + 1) * LANES] = hi.astype(BF16)

        x = x2_ref[0:rows, :]
        gate = jnp.dot(x, wg_ref[...], preferred_element_type=F32) + bg_ref[...]
        up = jnp.dot(x, wu_ref[...], preferred_element_type=F32) + bu_ref[...]
        gate = jnp.minimum(gate, LIMIT)
        up = jnp.clip(up, -LIMIT, LIMIT)
        act = (gate * jax.nn.sigmoid(ALPHA * gate) * (up + 1.0)).astype(BF16)
        part = jnp.dot(act, wd_ref[...], preferred_element_type=F32)

        @pl.when(j == 0)
        def _():
            acc_ref[0:rows, :] = part + bd_ref[...]

        @pl.when((j > 0) & (j < nj - 1))
        def _():
            acc_ref[0:rows, :] += part

        @pl.when(j == nj - 1)
        def _():
            y_ref[0:rows, :] = (acc_ref[0:rows, :] + part).astype(BF16)
            if rows < bm:
                y_ref[rows:bm, :] = jnp.zeros((bm - rows, D), BF16)

    @pl.when(valid > bm // 2)
    def _():
        body(bm)

    @pl.when((valid > 0) & (valid <= bm // 2))
    def _():
        body(bm // 2)


def _moe(block_e, n_used, block_valid, xg, wg, wu, wd, bg, bu, bd, n_blocks, bm):
    th = 1024
    nj = DE // th

    def wcol(blk, j, be, nu, bv):
        return (be[blk], 0, jnp.where(blk < nu[0], j, nj - 1))

    def wrow(blk, j, be, nu, bv):
        return (be[blk], jnp.where(blk < nu[0], j, nj - 1), 0)

    def xrow(blk, j, be, nu, bv):
        return (jnp.minimum(blk, nu[0] - 1), 0)

    grid_spec = pltpu.PrefetchScalarGridSpec(
        num_scalar_prefetch=3, grid=(n_blocks, nj),
        in_specs=[
            pl.BlockSpec((bm * PACK_ROWS, LANES), xrow),
            pl.BlockSpec((None, D, th), wcol), pl.BlockSpec((None, D, th), wcol),
            pl.BlockSpec((None, th, D), wrow),
            pl.BlockSpec((None, 1, th), wcol), pl.BlockSpec((None, 1, th), wcol),
            pl.BlockSpec((None, 1, D), lambda blk, j, be, nu, bv: (be[blk], 0, 0)),
        ],
        out_specs=pl.BlockSpec((bm, D), xrow),
        scratch_shapes=[pltpu.VMEM((bm, D), BF16), pltpu.VMEM((bm, D), F32)],
    )
    return pl.pallas_call(
        functools.partial(_moe_kernel, bm=bm, nj=nj), grid_spec=grid_spec,
        out_shape=jax.ShapeDtypeStruct((n_blocks * bm, D), BF16),
        compiler_params=_params(("arbitrary", "arbitrary")),
        name="moe_experts",
    )(block_e, n_used, block_valid, xg, wg, wu, wd, bg, bu, bd)


def _combine_kernel(src_ref, tot_ref, y_ref, h_ref, slot_ref, tw_ref, gf_ref, o_ref, ycat_ref, sem,
                    *, tt, nt, kcat):
    i = pl.program_id(0)
    cur = i % 2

    def for_pieces(tile, s, fn):
        def body(q, c):
            fn(pltpu.make_async_copy(
                y_ref.at[pl.ds(pl.multiple_of(src_ref[tile, q], PIECE), PIECE), :],
                ycat_ref.at[s, pl.ds(pl.multiple_of(q * PIECE, PIECE), PIECE), :], sem.at[s]))
            return c

        lax.fori_loop(0, tot_ref[tile], body, 0)

    @pl.when(i == 0)
    def _():
        ycat_ref[...] = jnp.zeros_like(ycat_ref)
        for_pieces(0, 0, lambda cp: cp.start())

    @pl.when(i + 1 < nt)
    def _():
        for_pieces(i + 1, 1 - cur, lambda cp: cp.start())

    for_pieces(i, cur, lambda cp: cp.wait())

    def combine(k_rows):
        slot = slot_ref[...]
        tw = tw_ref[...]
        col = lax.broadcasted_iota(jnp.int32, (tt, k_rows), 1)
        sel = jnp.zeros((tt, k_rows), F32)
        for k in range(TOPK):
            sel = jnp.where(col == slot[:, k:k + 1], tw[:, k:k + 1], sel)
        moe = jnp.dot(sel.astype(BF16), ycat_ref[cur, 0:k_rows, :], preferred_element_type=F32)
        o_ref[...] = _rms(h_ref[...] + moe, gf_ref[...])

    k_short = kcat * 3 // 4
    used = tot_ref[i] * PIECE

    @pl.when(used <= k_short)
    def _():
        combine(k_short)

    @pl.when(used > k_short)
    def _():
        combine(kcat)


def _combine(src, total, y, h, slot, tw, g_final, n, tt, kcat):
    nt = n // tt
    grid_spec = pltpu.PrefetchScalarGridSpec(
        num_scalar_prefetch=2, grid=(nt,),
        in_specs=[
            pl.BlockSpec(memory_space=pl.ANY),
            pl.BlockSpec((tt, D), lambda i, a, b: (i, 0)),
            pl.BlockSpec((tt, TOPK), lambda i, a, b: (i, 0)),
            pl.BlockSpec((tt, TOPK), lambda i, a, b: (i, 0)),
            pl.BlockSpec((1, D), lambda i, a, b: (0, 0)),
        ],
        out_specs=pl.BlockSpec((tt, D), lambda i, a, b: (i, 0)),
        scratch_shapes=[pltpu.VMEM((2, kcat, D), BF16), pltpu.SemaphoreType.DMA((2,))],
    )
    return pl.pallas_call(
        functools.partial(_combine_kernel, tt=tt, nt=nt, kcat=kcat), grid_spec=grid_spec,
        out_shape=jax.ShapeDtypeStruct((n, D), F32),
        compiler_params=_params(("arbitrary",)),
        name="combine_final_norm",
    )(src, total, y, h, slot, tw, g_final)


def _permute_weights(w_in, w_uq, w_ukv, w_o_swa):
    widths = (HQ * DH, HKV * DH, HKV * DH, QR, KVR, DR, 2 * D)
    offs = [0]
    for w in widths:
        offs.append(offs[-1] + w)
    wq, wk, wv, wdq, wdkv, wkr, wgates = [w_in[:, offs[i]:offs[i + 1]] for i in range(7)]
    wq = wq.reshape(D, HKV, G, DH).transpose(0, 2, 1, 3).reshape(D, HQ * DH)
    pad = jnp.zeros((D, PROJ_COLS - COL_KR - DR), w_in.dtype)
    w_in_p = jnp.concatenate([wgates, wq, wdq, wdkv, wk, wv, wkr, pad], axis=1).astype(BF16)
    wo_a = w_o_swa.reshape(HKV, G, DH, D).transpose(1, 0, 2, 3).reshape(HQ * DH, D).astype(BF16)
    wuq = w_uq.reshape(QR, MH, DN + DR)
    wq_p = jnp.concatenate([wuq, jnp.zeros((QR, MH, 2 * LANES - DN - DR), w_uq.dtype)], axis=2)
    wq_p = wq_p.reshape(QR, MH * 2 * LANES).astype(BF16)
    wukv = w_ukv.reshape(KVR, MH, DN + DV)
    wk_p = wukv[:, :, :DN].reshape(KVR, MH * DN).astype(BF16)
    wv_p = wukv[:, :, DN:].reshape(KVR, MH * DV).astype(BF16)
    return w_in_p, wo_a, wq_p, wk_p, wv_p


def kernel(x, positions, g_attn, w_in, b_gate, sinks, g_q, w_uq, g_kv, w_ukv, w_o_swa, w_o_mla, w_out,
           g_ffn, w_router, b_router, w_gate_up, b_gate_up, w_down, b_down, g_final):
    b, s, _ = x.shape
    n = b * s
    assert w_in.shape[0] == 1, "single-layer block"
    x2 = x.reshape(n, D)

    w_in_p, wo_a, wq_p, wk_p, wv_p = _permute_weights(w_in[0], w_uq[0], w_ukv[0], w_o_swa[0])
    cos, sin = _rope_tables(positions, n)
    proj = _in_proj(x2, g_attn, w_in_p, n)
    qa, ka, qm, km, vm = _prep(proj, cos, sin, g_q, g_kv, wq_p, wk_p, wv_p, n)
    out_a = _swa(sinks[0], qa, ka, proj, b, s)
    out_b = _mla(qm, km, vm, b, s)

    mixed = _mix(out_a, out_b, proj, wo_a, w_o_mla[0].astype(BF16), b_gate, n)
    tm = min(n, 512)
    tt = min(n, 256)
    h, hnp, top_e, top_w, rank, cnt, tile_base = _router(
        mixed, x2, w_out[0].astype(BF16), g_ffn, w_router[0].T, b_router[0][:, None], n, tm, tt)

    bm = 512
    n_blocks = -(-(n * TOPK) // bm) + E
    counts = cnt[:, 0].astype(jnp.int32)
    padded = (counts + bm - 1) // bm * bm
    pend = jnp.cumsum(padded)
    pstart = pend - padded
    n_used = pend[-1] // bm
    blocks = jnp.arange(n_blocks, dtype=jnp.int32)
    block_e = jnp.minimum(jnp.sum(pend[None, :] <= (blocks * bm)[:, None], axis=1), E - 1).astype(jnp.int32)
    block_valid = jnp.clip(counts[block_e] - (blocks * bm - pstart[block_e]), 0, bm).astype(jnp.int32)
    block_valid = jnp.where(blocks < n_used, block_valid, 0)
    block_e = jnp.where(blocks < n_used, block_e, block_e[jnp.maximum(n_used - 1, 0)])

    nsub = tm // tt
    before = tile_base.reshape(n // tm, E, LANES)[:, :, :nsub].transpose(0, 2, 1).reshape(n // tt, E)
    before = before.astype(jnp.int32)
    after = jnp.concatenate([before[1:], counts[None, :]], axis=0)
    first = pstart[None, :] + before
    astart = first // PIECE * PIECE
    npieces = jnp.where(after > before, (first + (after - before) - astart + PIECE - 1) // PIECE, 0)
    cum = jnp.cumsum(npieces, axis=1)
    pbase = cum - npieces
    kcat = -(-(tt * TOPK + E * 2 * (PIECE - 1)) // (2 * LANES)) * (2 * LANES)
    shift = (astart - pbase * PIECE).astype(jnp.int32)
    q = jnp.arange(kcat // PIECE, dtype=jnp.int32)
    owner = jnp.sum(cum[:, None, :] <= q[None, :, None], axis=2)
    own = owner[:, :, None] == jnp.arange(E, dtype=jnp.int32)[None, None, :]
    piece_src = jnp.sum(jnp.where(own, shift[:, None, :], 0), axis=2) + q[None, :] * PIECE
    piece_src = jnp.where(owner < E, piece_src, 0).astype(jnp.int32)
    total = cum[:, -1].astype(jnp.int32)

    dest, slot = _dest(pstart, shift, top_e, rank, n, tt)
    wg, wu, wd, xg = _prep_dispatch(counts, padded, pstart, dest, hnp, w_gate_up[0], w_down[0], n, n_blocks * bm)
    bgu = b_gate_up[0]
    y = _moe(block_e, n_used.reshape(1).astype(jnp.int32), block_valid, xg, wg, wu, wd,
             bgu[:, None, 0::2], bgu[:, None, 1::2], b_down[0][:, None, :], n_blocks, bm)
    out = _combine(piece_src, total, y, h, slot.T, top_w.T, g_final[None, :], n, tt, kcat)
    return out.reshape(b, s, D)
```

```python
import functools

import jax
import jax.numpy as jnp
from jax import lax
from jax.experimental import pallas as pl
from jax.experimental.pallas import tpu as pltpu

F32 = jnp.float32
BF16 = jnp.bfloat16
U32 = jnp.uint32

D = 2048
HQ, HKV, DH, WIN = 16, 2, 64, 128
G = HQ // HKV
MH, QR, KVR, DN, DR, DV = 8, 512, 256, 128, 64, 128
THETA = 10000.0
E, TOPK, DE = 32, 4, 2048
LIMIT, ALPHA = 7.0, 1.702
EPS = 1e-6
NEG = -1e30
LOG2E = 1.4426950408889634

LANES = 128
CHUNKS = D // LANES
PACK_ROWS = CHUNKS // 2
PIECE = 16
VMEM_LIMIT = 56 * 1024 * 1024

PROJ_COLS = 6400
COL_GATES, COL_Q, COL_DQ, COL_DKV, COL_K, COL_V, COL_KR = 0, 4096, 5120, 5632, 5888, 6016, 6144

NT = (((1,), (1,)), ((), ()))


def _params(sem, vmem=VMEM_LIMIT):
    return pltpu.CompilerParams(dimension_semantics=sem, vmem_limit_bytes=vmem)


def _rms(x, g):
    return x * lax.rsqrt(jnp.mean(x * x, axis=-1, keepdims=True) + EPS) * g


def _pack_pair(lo, hi):
    lo_bits = lax.bitcast_convert_type(lo.astype(BF16).astype(F32), U32)
    hi_bits = lax.bitcast_convert_type(hi.astype(BF16).astype(F32), U32)
    return (lo_bits >> 16) | hi_bits


def _unpack_pair(w):
    lo = lax.bitcast_convert_type(w << 16, F32)
    hi = lax.bitcast_convert_type(w & jnp.uint32(0xFFFF0000), F32)
    return lo, hi


def _rope128(v, cos, sin):
    lane = lax.broadcasted_iota(jnp.int32, v.shape, 1)
    rot = jnp.where((lane % DH) < (DH // 2), -pltpu.roll(v, LANES - DH // 2, 1), pltpu.roll(v, DH // 2, 1))
    return v * cos + rot * sin


def _rope_table_kernel(pos_ref, inv_ref, cos_ref, sin_ref):
    ang = pos_ref[...].astype(F32) * inv_ref[...]
    cos_ref[...] = jnp.cos(ang)
    sin_ref[...] = jnp.sin(ang)


def _rope_tables(positions, n):
    half = DH // 2
    per_row = LANES // half
    inv = jnp.power(THETA, -jnp.arange(half, dtype=F32) * 2.0 / DH)
    inv_row = jnp.tile(inv, per_row)[None, :]
    pos_rows = jnp.repeat(positions.reshape(n // per_row, per_row), half, axis=1)
    rows = n // per_row
    tr = min(rows, 1024)
    cos, sin = pl.pallas_call(
        _rope_table_kernel,
        grid=(rows // tr,),
        in_specs=[pl.BlockSpec((tr, LANES), lambda i: (i, 0)), pl.BlockSpec((1, LANES), lambda i: (0, 0))],
        out_specs=[pl.BlockSpec((tr, LANES), lambda i: (i, 0))] * 2,
        out_shape=[jax.ShapeDtypeStruct((rows, LANES), F32)] * 2,
        name="rope_tables",
    )(pos_rows, inv_row)
    cos = jnp.tile(cos.reshape(n, half), (1, per_row))
    sin = jnp.tile(sin.reshape(n, half), (1, per_row))
    return cos, sin


def _in_proj_kernel(x_ref, g_ref, w_ref, o_ref, xn_ref):
    @pl.when(pl.program_id(1) == 0)
    def _():
        xn_ref[...] = _rms(x_ref[...], g_ref[...]).astype(BF16)

    o_ref[...] = jnp.dot(xn_ref[...], w_ref[...], preferred_element_type=F32).astype(o_ref.dtype)


def _in_proj(x2, g_attn, w_in_p, n):
    tm = min(n, 1024)
    tn = 1280
    return pl.pallas_call(
        _in_proj_kernel,
        grid=(n // tm, PROJ_COLS // tn),
        in_specs=[
            pl.BlockSpec((tm, D), lambda i, j: (i, 0)),
            pl.BlockSpec((1, D), lambda i, j: (0, 0)),
            pl.BlockSpec((D, tn), lambda i, j: (0, j)),
        ],
        out_specs=pl.BlockSpec((tm, tn), lambda i, j: (i, j)),
        out_shape=jax.ShapeDtypeStruct((n, PROJ_COLS), BF16),
        scratch_shapes=[pltpu.VMEM((tm, D), BF16)],
        compiler_params=_params(("arbitrary", "arbitrary")),
        name="in_proj",
    )(x2, g_attn, w_in_p)


def _prep_kernel(q_ref, dq_ref, dkv_ref, k_ref, kr_ref, cos_ref, sin_ref, gq_ref, gkv_ref,
                 wq_ref, wk_ref, wv_ref, qa_ref, ka_ref, qm_ref, km_ref, vm_ref):
    cos = cos_ref[...]
    sin = sin_ref[...]
    swa_scale = DH ** -0.5 * LOG2E
    for c in range(HQ * DH // LANES):
        sl = slice(c * LANES, (c + 1) * LANES)
        qa_ref[:, sl] = (_rope128(q_ref[:, sl].astype(F32), cos, sin) * swa_scale).astype(BF16)
    ka_ref[...] = _rope128(k_ref[...].astype(F32), cos, sin).astype(BF16)

    mla_scale = (DN + DR) ** -0.5 * LOG2E
    cq = _rms(dq_ref[...].astype(F32), gq_ref[...]).astype(BF16)
    qb = jnp.dot(cq, wq_ref[...], preferred_element_type=F32)
    ckv = _rms(dkv_ref[...].astype(F32), gkv_ref[...]).astype(BF16)
    kn = jnp.dot(ckv, wk_ref[...], preferred_element_type=F32)
    vv = jnp.dot(ckv, wv_ref[...], preferred_element_type=F32)
    ones_col = jnp.where(lax.broadcasted_iota(jnp.int32, (vv.shape[0], LANES), 1) == 0, 1.0, 0.0).astype(BF16)
    kr = _rope128(kr_ref[...].astype(F32), cos, sin).astype(BF16)
    for h in range(MH):
        lo = h * 2 * LANES
        qm_ref[:, lo:lo + LANES] = (qb[:, lo:lo + LANES] * mla_scale).astype(BF16)
        qm_ref[:, lo + LANES:lo + 2 * LANES] = (
            _rope128(qb[:, lo + LANES:lo + 2 * LANES], cos, sin) * mla_scale).astype(BF16)
        km_ref[:, lo:lo + LANES] = kn[:, h * LANES:(h + 1) * LANES].astype(BF16)
        km_ref[:, lo + LANES:lo + 2 * LANES] = kr
        vm_ref[:, lo:lo + LANES] = vv[:, h * LANES:(h + 1) * LANES].astype(BF16)
        vm_ref[:, lo + LANES:lo + 2 * LANES] = ones_col


def _prep(proj, cos, sin, g_q, g_kv, wq_p, wk_p, wv_p, n):
    tm = min(n, 512)
    row = lambda w, cb: pl.BlockSpec((tm, w), lambda i: (i, cb))
    full = lambda a: pl.BlockSpec(a.shape, lambda i: (0, 0))
    return pl.pallas_call(
        _prep_kernel,
        grid=(n // tm,),
        in_specs=[
            row(HQ * DH, COL_Q // (HQ * DH)), row(QR, COL_DQ // QR), row(KVR, COL_DKV // KVR),
            row(LANES, COL_K // LANES), row(LANES, COL_KR // LANES),
            row(LANES, 0), row(LANES, 0), full(g_q), full(g_kv), full(wq_p), full(wk_p), full(wv_p),
        ],
        out_specs=[row(HQ * DH, 0), row(LANES, 0)] + [row(MH * 2 * LANES, 0)] * 3,
        out_shape=[jax.ShapeDtypeStruct((n, HQ * DH), BF16), jax.ShapeDtypeStruct((n, LANES), BF16)]
        + [jax.ShapeDtypeStruct((n, MH * 2 * LANES), BF16)] * 3,
        compiler_params=_params(("arbitrary",)),
        name="rope_mla_prep",
    )(proj, proj, proj, proj, proj, cos, sin, g_q, g_kv, wq_p, wk_p, wv_p)


def _swa_kernel(sinks_ref, q_ref, kc_ref, kp_ref, vc_ref, vp_ref, o_ref):
    blk = pl.program_id(1)
    q2 = jnp.concatenate([q_ref[:, g * LANES:(g + 1) * LANES] for g in range(G)], axis=0)
    k2 = jnp.concatenate([kp_ref[...], kc_ref[...]], axis=0)
    v2t = jnp.concatenate([vp_ref[...], vc_ref[...]], axis=0).astype(F32).T
    lane = lax.broadcasted_iota(jnp.int32, (2 * WIN, LANES), 1)
    vrow = lax.broadcasted_iota(jnp.int32, (LANES, 2 * WIN), 0)
    kj = lax.broadcasted_iota(jnp.int32, (2 * WIN, WIN), 0)
    qi = lax.broadcasted_iota(jnp.int32, (2 * WIN, WIN), 1)
    valid = (kj > qi) & (kj <= qi + WIN) & ((kj >= WIN) | (blk > 0))
    acc = jnp.zeros((LANES, G * WIN), F32)
    for h in range(HKV):
        kz = jnp.where((lane >= h * DH) & (lane < (h + 1) * DH), k2, jnp.zeros_like(k2))
        vz = jnp.where((vrow >= h * DH) & (vrow < (h + 1) * DH), v2t, 0.0).astype(BF16)
        s = lax.dot_general(kz, q2, NT, preferred_element_type=F32)
        ps, invs = [], []
        for g in range(G):
            sg = jnp.where(valid, s[:, g * WIN:(g + 1) * WIN], NEG)
            sink = sinks_ref[h * G + g] * LOG2E
            m = jnp.maximum(jnp.max(sg, axis=0, keepdims=True), sink)
            p = jnp.exp2(sg - m)
            l = jnp.sum(p, axis=0, keepdims=True) + jnp.exp2(sink - m)
            ps.append(p.astype(BF16))
            invs.append(1.0 / l)
        o_h = jnp.dot(vz, jnp.concatenate(ps, axis=1), preferred_element_type=F32)
        acc = acc + o_h * jnp.concatenate(invs, axis=1)
    for g in range(G):
        o_ref[:, g * LANES:(g + 1) * LANES] = acc[:, g * WIN:(g + 1) * WIN].T.astype(BF16)


def _swa(sinks, qa, ka, proj, b, s):
    nb = s // WIN
    cur = lambda cb: pl.BlockSpec((WIN, LANES), lambda bi, n, sk: (bi * nb + n, cb))
    prev = lambda cb: pl.BlockSpec((WIN, LANES), lambda bi, n, sk: (bi * nb + jnp.maximum(n - 1, 0), cb))
    grid_spec = pltpu.PrefetchScalarGridSpec(
        num_scalar_prefetch=1,
        grid=(b, nb),
        in_specs=[
            pl.BlockSpec((WIN, HQ * DH), lambda bi, n, sk: (bi * nb + n, 0)),
            cur(0), prev(0), cur(COL_V // LANES), prev(COL_V // LANES),
        ],
        out_specs=pl.BlockSpec((WIN, HQ * DH), lambda bi, n, sk: (bi * nb + n, 0)),
    )
    return pl.pallas_call(
        _swa_kernel,
        grid_spec=grid_spec,
        out_shape=jax.ShapeDtypeStruct((b * s, HQ * DH), BF16),
        compiler_params=_params(("arbitrary", "arbitrary")),
        name="swa_attention",
    )(sinks, qa, ka, ka, proj, proj)


def _mla_kernel(q_ref, k_ref, v_ref, o_ref, *, seq, tq):
    row = lax.broadcasted_iota(jnp.int32, (tq, tq), 0)
    col = lax.broadcasted_iota(jnp.int32, (tq, tq), 1)
    causal = col <= row
    for i in range(seq // tq):
        kv = (i + 1) * tq
        q = q_ref[i * tq:(i + 1) * tq, :]
        s = lax.dot_general(q, k_ref[0:kv, :], NT, preferred_element_type=F32)
        diag = jnp.where(causal, s[:, kv - tq:kv], NEG)
        s = diag if i == 0 else jnp.concatenate([s[:, 0:kv - tq], diag], axis=1)
        m = jnp.max(s, axis=-1, keepdims=True)
        p = jnp.exp2(s - m).astype(BF16)
        ov = jnp.dot(p, v_ref[0:kv, :], preferred_element_type=F32)
        o_ref[i * tq:(i + 1) * tq, :] = (ov[:, 0:DV] * (1.0 / ov[:, DV:DV + 1])).astype(BF16)


def _mla(qm, km, vm, b, s):
    tq = min(s, 512)
    return pl.pallas_call(
        functools.partial(_mla_kernel, seq=s, tq=tq),
        grid=(b, MH),
        in_specs=[
            pl.BlockSpec((s, 2 * LANES), lambda bi, h: (bi, h)),
            pl.BlockSpec((s, 2 * LANES), lambda bi, h: (bi, h)),
            pl.BlockSpec((s, 2 * LANES), lambda bi, h: (bi, h)),
        ],
        out_specs=pl.BlockSpec((s, DV), lambda bi, h: (bi, h)),
        out_shape=jax.ShapeDtypeStruct((b * s, MH * DV), BF16),
        compiler_params=_params(("arbitrary", "arbitrary")),
        name="mla_attention",
    )(qm, km, vm)


def _mix_kernel(oa_ref, ob_ref, ga_ref, gb_ref, wa_ref, wb_ref, ba_ref, bb_ref, o_ref):
    ya = jnp.dot(oa_ref[...], wa_ref[...], preferred_element_type=F32)
    yb = jnp.dot(ob_ref[...], wb_ref[...], preferred_element_type=F32)
    ga = jax.nn.sigmoid(ga_ref[...].astype(F32) + ba_ref[...])
    gb = jax.nn.sigmoid(gb_ref[...].astype(F32) + bb_ref[...])
    o_ref[...] = (ga * ya + gb * yb).astype(BF16)


def _mix(out_a, out_b, proj, wa, wb, b_gate, n):
    tm = min(n, 1024)
    tn = 1024
    nj = D // tn
    return pl.pallas_call(
        _mix_kernel,
        grid=(n // tm, nj),
        in_specs=[
            pl.BlockSpec((tm, HQ * DH), lambda i, j: (i, 0)),
            pl.BlockSpec((tm, MH * DV), lambda i, j: (i, 0)),
            pl.BlockSpec((tm, tn), lambda i, j: (i, j)),
            pl.BlockSpec((tm, tn), lambda i, j: (i, nj + j)),
            pl.BlockSpec((HQ * DH, tn), lambda i, j: (0, j)),
            pl.BlockSpec((MH * DV, tn), lambda i, j: (0, j)),
            pl.BlockSpec((1, tn), lambda i, j: (0, j)),
            pl.BlockSpec((1, tn), lambda i, j: (0, nj + j)),
        ],
        out_specs=pl.BlockSpec((tm, tn), lambda i, j: (i, j)),
        out_shape=jax.ShapeDtypeStruct((n, D), BF16),
        compiler_params=_params(("arbitrary", "arbitrary")),
        name="gated_mix",
    )(out_a, out_b, proj, proj, wa, wb, b_gate, b_gate)


def _router_kernel(mx_ref, x_ref, wo_ref, gf_ref, wr_ref, br_ref,
                   h_ref, hn_ref, te_ref, tw_ref, rk_ref, cnt_ref, tb_ref, base_ref, *, tm, sub):
    @pl.when(pl.program_id(0) == 0)
    def _():
        base_ref[...] = jnp.zeros_like(base_ref)

    h = x_ref[...] + jnp.dot(mx_ref[...], wo_ref[...], preferred_element_type=F32)
    h_ref[...] = h
    hn = _rms(h, gf_ref[...])
    hn_ref[...] = hn.astype(BF16)

    logits = lax.dot_general(wr_ref[...], hn, NT, preferred_element_type=F32,
                             precision=lax.Precision.HIGHEST) + br_ref[...]
    eidx = lax.broadcasted_iota(jnp.int32, (E, tm), 0)
    vals, idxs = [], []
    l = logits
    for _ in range(TOPK):
        m = jnp.max(l, axis=0, keepdims=True)
        idx = jnp.min(jnp.where(l == m, eidx, E), axis=0, keepdims=True)
        vals.append(m)
        idxs.append(idx)
        l = jnp.where(eidx == idx, -jnp.inf, l)
    ex = [jnp.exp(v - vals[0]) for v in vals]
    tot = ex[0] + ex[1] + ex[2] + ex[3]
    tw_ref[...] = jnp.concatenate([e / tot for e in ex], axis=0)
    te_ref[...] = jnp.concatenate(idxs, axis=0)

    onehot = jnp.zeros((E, tm), F32)
    for idx in idxs:
        onehot = onehot + jnp.where(eidx == idx, 1.0, 0.0)
    r = lax.broadcasted_iota(jnp.int32, (tm, tm), 0)
    c = lax.broadcasted_iota(jnp.int32, (tm, tm), 1)
    earlier = jnp.where(r < c, 1.0, 0.0).astype(BF16)
    before = jnp.dot(onehot.astype(BF16), earlier, preferred_element_type=F32) + base_ref[:, 0:1]
    rk_ref[...] = jnp.concatenate(
        [jnp.sum(jnp.where(eidx == idx, before, 0.0), axis=0, keepdims=True) for idx in idxs],
        axis=0).astype(jnp.int32)
    lane = lax.broadcasted_iota(jnp.int32, (E, LANES), 1)
    tb = jnp.zeros((E, LANES), F32)
    for s in range(tm // sub):
        tb = jnp.where(lane == s, before[:, s * sub:s * sub + 1], tb)
    tb_ref[...] = tb
    base_ref[...] = base_ref[...] + jnp.sum(onehot, axis=1, keepdims=True)
    cnt_ref[...] = base_ref[...]


def _router(mixed, x2, wo, g_ffn, wr_t, b_router, n, tm, sub):
    row = lambda w: pl.BlockSpec((tm, w), lambda i: (i, 0))
    full = lambda a: pl.BlockSpec(a.shape, lambda i: (0, 0), pipeline_mode=pl.Buffered(1))
    tok = pl.BlockSpec((TOPK, tm), lambda i: (0, i))
    return pl.pallas_call(
        functools.partial(_router_kernel, tm=tm, sub=sub),
        grid=(n // tm,),
        in_specs=[row(D), row(D), full(wo), full(g_ffn), full(wr_t), full(b_router)],
        out_specs=[
            row(D), row(D), tok, tok, tok,
            pl.BlockSpec((E, LANES), lambda i: (0, 0)), pl.BlockSpec((E, LANES), lambda i: (i, 0)),
        ],
        out_shape=[
            jax.ShapeDtypeStruct((n, D), F32), jax.ShapeDtypeStruct((n, D), BF16),
            jax.ShapeDtypeStruct((TOPK, n), jnp.int32), jax.ShapeDtypeStruct((TOPK, n), F32),
            jax.ShapeDtypeStruct((TOPK, n), jnp.int32), jax.ShapeDtypeStruct((E, LANES), F32),
            jax.ShapeDtypeStruct((n // tm * E, LANES), F32),
        ],
        scratch_shapes=[pltpu.VMEM((E, LANES), F32)],
        compiler_params=_params(("arbitrary",)),
        name="outproj_router",
    )(mixed, x2, wo, g_ffn, wr_t, b_router)


def _dest_kernel(pstart_ref, shift_ref, te_ref, rk_ref, dest_ref, slot_ref):
    tile = pl.program_id(0)
    te = te_ref[...]
    d = rk_ref[...]
    s = rk_ref[...]
    for e in range(E):
        hit = te == e
        d = d + jnp.where(hit, pstart_ref[e], 0)
        s = s + jnp.where(hit, pstart_ref[e] - shift_ref[tile, e], 0)
    dest_ref[...] = d
    slot_ref[...] = s


def _dest(pstart, shift, top_e, rank, n, tt):
    tok = pl.BlockSpec((TOPK, tt), lambda i, ps, sh: (0, i))
    grid_spec = pltpu.PrefetchScalarGridSpec(
        num_scalar_prefetch=2, grid=(n // tt,), in_specs=[tok, tok], out_specs=[tok, tok])
    return pl.pallas_call(
        _dest_kernel, grid_spec=grid_spec,
        out_shape=[jax.ShapeDtypeStruct((TOPK, n), jnp.int32)] * 2, name="dest_rows",
    )(pstart, shift, top_e, rank)


def _prep_dispatch_kernel(pdst_ref, ntot_ref, nwr_ref, pb_ref, np_ref, ff_ref, lf_ref, cnt_ref, pad_ref, pst_ref,
                          slot_ref, hn_ref, wgu_ref, wdn_ref, g_ref, u_ref, d_ref, xg_ref,
                          stage_ref, carry_ref, sem, fsem, *, tt, n_disp, kcat):
    i = pl.program_id(0)

    def stage_piece(row):
        return stage_ref.at[pl.ds(pl.multiple_of(row, PIECE), PIECE), :]

    def out_piece(row):
        return xg_ref.at[pl.ds(pl.multiple_of(row, PIECE), PIECE), :]

    @pl.when(i == 0)
    def _():
        carry_ref[...] = jnp.zeros_like(carry_ref)

    @pl.when(i < n_disp)
    def _():
        slot = slot_ref[...]
        chunk = 512
        for r0 in range(0, kcat, chunk):
            row = lax.broadcasted_iota(jnp.int32, (chunk, tt), 0) + r0
            hit = row == slot[0:1, :]
            for k in range(1, TOPK):
                hit = hit | (row == slot[k:k + 1, :])
            sel = jnp.where(hit, 1.0, 0.0).astype(BF16)
            stage_ref[r0:r0 + chunk, :] = jnp.dot(sel, hn_ref[...], preferred_element_type=F32).astype(BF16)

        def per_expert(e, c):
            pieces = np_ref[i, e]

            @pl.when(pieces > 0)
            def _():
                first = pb_ref[i, e]
                last = first + (pieces - 1) * PIECE

                @pl.when(ff_ref[i, e] == 1)
                def _():
                    stage_piece(first)[...] = stage_piece(first)[...] + carry_ref[e]

                @pl.when(lf_ref[i, e] == 1)
                def _():
                    carry_ref[e] = stage_piece(last)[...]

            return c

        lax.fori_loop(0, E, per_expert, 0)

        def start(q, c):
            dst = pdst_ref[i, q]

            @pl.when(dst >= 0)
            def _():
                pltpu.make_async_copy(stage_piece(q * PIECE), out_piece(dst), sem).start()

            return c

        lax.fori_loop(0, ntot_ref[i], start, 0)

    chunk = 2 * LANES
    r = lax.broadcasted_iota(jnp.int32, (chunk, chunk), 0)
    c = lax.broadcasted_iota(jnp.int32, (chunk, chunk), 1)
    src = jnp.where(c < LANES, 2 * c, 2 * (c - LANES) + 1)
    perm = jnp.where(r == src, 1.0, 0.0).astype(BF16)
    for k in range(wgu_ref.shape[1] // chunk):
        w = wgu_ref[:, k * chunk:(k + 1) * chunk].astype(BF16)
        o = jnp.dot(w, perm, preferred_element_type=F32)
        g_ref[:, k * LANES:(k + 1) * LANES] = o[:, :LANES].astype(BF16)
        u_ref[:, k * LANES:(k + 1) * LANES] = o[:, LANES:].astype(BF16)
    d_ref[...] = wdn_ref[...].astype(BF16)

    @pl.when(i < n_disp)
    def _():
        def wait(q, c):
            pltpu.make_async_copy(stage_piece(0), out_piece(0), sem).wait()
            return c

        lax.fori_loop(0, nwr_ref[i], wait, 0)

    @pl.when(i == n_disp - 1)
    def _():
        stage_ref[0:PIECE, :] = jnp.zeros((PIECE, D), BF16)

        def per_expert(e, c):
            real = cnt_ref[e]
            tail = real % PIECE
            base = pst_ref[e]

            @pl.when(tail != 0)
            def _():
                cp = pltpu.make_async_copy(carry_ref.at[e], out_piece(base + real - tail), fsem)
                cp.start()
                cp.wait()

            zero_from = (real + PIECE - 1) // PIECE
            zero_to = pad_ref[e] // PIECE

            def zstart(z, c2):
                pltpu.make_async_copy(stage_piece(0), out_piece(base + z * PIECE), fsem).start()
                return c2

            def zwait(z, c2):
                pltpu.make_async_copy(stage_piece(0), out_piece(base + z * PIECE), fsem).wait()
                return c2

            lax.fori_loop(zero_from, zero_to, zstart, 0)
            lax.fori_loop(zero_from, zero_to, zwait, 0)
            return c

        lax.fori_loop(0, E, per_expert, 0)


def _prep_dispatch(tables, counts, padded, pstart, slot, hn, w_gate_up, w_down, n, p_rows, tt, kcat):
    tr = 512
    steps = E * (D // tr)
    n_disp = n // tt
    assert n_disp <= steps, "more token tiles than weight tiles"
    nsp = len(tables) + 3
    tile = lambda i, *_: (i // (D // tr), i % (D // tr), 0)
    tok = lambda i: jnp.minimum(i, n_disp - 1)
    wout = pl.BlockSpec((None, tr, DE), tile)
    grid_spec = pltpu.PrefetchScalarGridSpec(
        num_scalar_prefetch=nsp, grid=(steps,),
        in_specs=[
            pl.BlockSpec((TOPK, tt), lambda i, *_: (0, tok(i))),
            pl.BlockSpec((tt, D), lambda i, *_: (tok(i), 0)),
            pl.BlockSpec((None, tr, 2 * DE), tile),
            pl.BlockSpec((None, tr, D), tile),
        ],
        out_specs=[wout, wout, pl.BlockSpec((None, tr, D), tile), pl.BlockSpec(memory_space=pl.ANY)],
        scratch_shapes=[pltpu.VMEM((kcat, D), BF16), pltpu.VMEM((E, PIECE, D), BF16),
                        pltpu.SemaphoreType.DMA(()), pltpu.SemaphoreType.DMA(())],
    )
    return pl.pallas_call(
        functools.partial(_prep_dispatch_kernel, tt=tt, n_disp=n_disp, kcat=kcat), grid_spec=grid_spec,
        out_shape=[jax.ShapeDtypeStruct((E, D, DE), BF16), jax.ShapeDtypeStruct((E, D, DE), BF16),
                   jax.ShapeDtypeStruct((E, DE, D), BF16), jax.ShapeDtypeStruct((p_rows, D), BF16)],
        compiler_params=pltpu.CompilerParams(dimension_semantics=("arbitrary",), has_side_effects=True,
                                             vmem_limit_bytes=VMEM_LIMIT),
        name="weight_prep_dispatch",
    )(*tables, counts, padded, pstart, slot, hn, w_gate_up, w_down)


def _moe_kernel(be_ref, nu_ref, bv_ref, x_ref, wg_ref, wu_ref, wd_ref, bg_ref, bu_ref, bd_ref, y_ref,
                acc_ref, *, bm, nj):
    blk = pl.program_id(0)
    j = pl.program_id(1)
    valid = bv_ref[blk]

    def body(rows):
        x = x_ref[0:rows, :]
        gate = jnp.dot(x, wg_ref[...], preferred_element_type=F32) + bg_ref[...]
        up = jnp.dot(x, wu_ref[...], preferred_element_type=F32) + bu_ref[...]
        gate = jnp.minimum(gate, LIMIT)
        up = jnp.clip(up, -LIMIT, LIMIT)
        act = (gate * jax.nn.sigmoid(ALPHA * gate) * (up + 1.0)).astype(BF16)
        part = jnp.dot(act, wd_ref[...], preferred_element_type=F32)

        @pl.when(j == 0)
        def _():
            acc_ref[0:rows, :] = part + bd_ref[...]

        @pl.when((j > 0) & (j < nj - 1))
        def _():
            acc_ref[0:rows, :] += part

        @pl.when(j == nj - 1)
        def _():
            y_ref[0:rows, :] = (acc_ref[0:rows, :] + part).astype(BF16)
            if rows < bm:
                y_ref[rows:bm, :] = jnp.zeros((bm - rows, D), BF16)

    @pl.when(valid > bm // 2)
    def _():
        body(bm)

    @pl.when((valid > 0) & (valid <= bm // 2))
    def _():
        body(bm // 2)


def _moe(block_e, n_used, block_valid, xg, wg, wu, wd, bg, bu, bd, n_blocks, bm):
    th = 1024
    nj = DE // th

    def wcol(blk, j, be, nu, bv):
        return (be[blk], 0, jnp.where(blk < nu[0], j, nj - 1))

    def wrow(blk, j, be, nu, bv):
        return (be[blk], jnp.where(blk < nu[0], j, nj - 1), 0)

    def xrow(blk, j, be, nu, bv):
        return (jnp.minimum(blk, nu[0] - 1), 0)

    grid_spec = pltpu.PrefetchScalarGridSpec(
        num_scalar_prefetch=3, grid=(n_blocks, nj),
        in_specs=[
            pl.BlockSpec((bm, D), xrow),
            pl.BlockSpec((None, D, th), wcol), pl.BlockSpec((None, D, th), wcol),
            pl.BlockSpec((None, th, D), wrow),
            pl.BlockSpec((None, 1, th), wcol), pl.BlockSpec((None, 1, th), wcol),
            pl.BlockSpec((None, 1, D), lambda blk, j, be, nu, bv: (be[blk], 0, 0)),
        ],
        out_specs=pl.BlockSpec((bm, D), xrow),
        scratch_shapes=[pltpu.VMEM((bm, D), F32)],
    )
    return pl.pallas_call(
        functools.partial(_moe_kernel, bm=bm, nj=nj), grid_spec=grid_spec,
        out_shape=jax.ShapeDtypeStruct((n_blocks * bm, D), BF16),
        compiler_params=_params(("arbitrary", "arbitrary")),
        name="moe_experts",
    )(block_e, n_used, block_valid, xg, wg, wu, wd, bg, bu, bd)


def _combine_kernel(src_ref, tot_ref, y_ref, h_ref, slot_ref, tw_ref, gf_ref, o_ref, ycat_ref, sem,
                    *, tt, nt, kcat):
    i = pl.program_id(0)
    cur = i % 2

    def for_pieces(tile, s, fn):
        def body(q, c):
            fn(pltpu.make_async_copy(
                y_ref.at[pl.ds(pl.multiple_of(src_ref[tile, q], PIECE), PIECE), :],
                ycat_ref.at[s, pl.ds(pl.multiple_of(q * PIECE, PIECE), PIECE), :], sem.at[s]))
            return c

        lax.fori_loop(0, tot_ref[tile], body, 0)

    @pl.when(i == 0)
    def _():
        ycat_ref[...] = jnp.zeros_like(ycat_ref)
        for_pieces(0, 0, lambda cp: cp.start())

    @pl.when(i + 1 < nt)
    def _():
        for_pieces(i + 1, 1 - cur, lambda cp: cp.start())

    for_pieces(i, cur, lambda cp: cp.wait())

    def combine(k_rows):
        slot = slot_ref[...]
        tw = tw_ref[...]
        col = lax.broadcasted_iota(jnp.int32, (tt, k_rows), 1)
        sel = jnp.zeros((tt, k_rows), F32)
        for k in range(TOPK):
            sel = jnp.where(col == slot[:, k:k + 1], tw[:, k:k + 1], sel)
        moe = jnp.dot(sel.astype(BF16), ycat_ref[cur, 0:k_rows, :], preferred_element_type=F32)
        o_ref[...] = _rms(h_ref[...] + moe, gf_ref[...])

    k_short = kcat * 3 // 4
    used = tot_ref[i] * PIECE

    @pl.when(used <= k_short)
    def _():
        combine(k_short)

    @pl.when(used > k_short)
    def _():
        combine(kcat)


def _combine(src, total, y, h, slot, tw, g_final, n, tt, kcat):
    nt = n // tt
    grid_spec = pltpu.PrefetchScalarGridSpec(
        num_scalar_prefetch=2, grid=(nt,),
        in_specs=[
            pl.BlockSpec(memory_space=pl.ANY),
            pl.BlockSpec((tt, D), lambda i, a, b: (i, 0)),
            pl.BlockSpec((tt, TOPK), lambda i, a, b: (i, 0)),
            pl.BlockSpec((tt, TOPK), lambda i, a, b: (i, 0)),
            pl.BlockSpec((1, D), lambda i, a, b: (0, 0)),
        ],
        out_specs=pl.BlockSpec((tt, D), lambda i, a, b: (i, 0)),
        scratch_shapes=[pltpu.VMEM((2, kcat, D), BF16), pltpu.SemaphoreType.DMA((2,))],
    )
    return pl.pallas_call(
        functools.partial(_combine_kernel, tt=tt, nt=nt, kcat=kcat), grid_spec=grid_spec,
        out_shape=jax.ShapeDtypeStruct((n, D), F32),
        compiler_params=_params(("arbitrary",)),
        name="combine_final_norm",
    )(src, total, y, h, slot, tw, g_final)


def _permute_weights(w_in, w_uq, w_ukv, w_o_swa):
    widths = (HQ * DH, HKV * DH, HKV * DH, QR, KVR, DR, 2 * D)
    offs = [0]
    for w in widths:
        offs.append(offs[-1] + w)
    wq, wk, wv, wdq, wdkv, wkr, wgates = [w_in[:, offs[i]:offs[i + 1]] for i in range(7)]
    wq = wq.reshape(D, HKV, G, DH).transpose(0, 2, 1, 3).reshape(D, HQ * DH)
    pad = jnp.zeros((D, PROJ_COLS - COL_KR - DR), w_in.dtype)
    w_in_p = jnp.concatenate([wgates, wq, wdq, wdkv, wk, wv, wkr, pad], axis=1).astype(BF16)
    wo_a = w_o_swa.reshape(HKV, G, DH, D).transpose(1, 0, 2, 3).reshape(HQ * DH, D).astype(BF16)
    wuq = w_uq.reshape(QR, MH, DN + DR)
    wq_p = jnp.concatenate([wuq, jnp.zeros((QR, MH, 2 * LANES - DN - DR), w_uq.dtype)], axis=2)
    wq_p = wq_p.reshape(QR, MH * 2 * LANES).astype(BF16)
    wukv = w_ukv.reshape(KVR, MH, DN + DV)
    wk_p = wukv[:, :, :DN].reshape(KVR, MH * DN).astype(BF16)
    wv_p = wukv[:, :, DN:].reshape(KVR, MH * DV).astype(BF16)
    return w_in_p, wo_a, wq_p, wk_p, wv_p


def kernel(x, positions, g_attn, w_in, b_gate, sinks, g_q, w_uq, g_kv, w_ukv, w_o_swa, w_o_mla, w_out,
           g_ffn, w_router, b_router, w_gate_up, b_gate_up, w_down, b_down, g_final):
    b, s, _ = x.shape
    n = b * s
    assert w_in.shape[0] == 1, "single-layer block"
    x2 = x.reshape(n, D)

    w_in_p, wo_a, wq_p, wk_p, wv_p = _permute_weights(w_in[0], w_uq[0], w_ukv[0], w_o_swa[0])
    cos, sin = _rope_tables(positions, n)
    proj = _in_proj(x2, g_attn, w_in_p, n)
    qa, ka, qm, km, vm = _prep(proj, cos, sin, g_q, g_kv, wq_p, wk_p, wv_p, n)
    out_a = _swa(sinks[0], qa, ka, proj, b, s)
    out_b = _mla(qm, km, vm, b, s)

    mixed = _mix(out_a, out_b, proj, wo_a, w_o_mla[0].astype(BF16), b_gate, n)
    tm = min(n, 512)
    tt = min(n, 256)
    h, hn, top_e, top_w, rank, cnt, tile_base = _router(
        mixed, x2, w_out[0].astype(BF16), g_ffn, w_router[0].T, b_router[0][:, None], n, tm, tt)

    bm = 512
    n_blocks = -(-(n * TOPK) // bm) + E
    counts = cnt[:, 0].astype(jnp.int32)
    padded = (counts + bm - 1) // bm * bm
    pend = jnp.cumsum(padded)
    pstart = pend - padded
    n_used = pend[-1] // bm
    blocks = jnp.arange(n_blocks, dtype=jnp.int32)
    block_e = jnp.minimum(jnp.sum(pend[None, :] <= (blocks * bm)[:, None], axis=1), E - 1).astype(jnp.int32)
    block_valid = jnp.clip(counts[block_e] - (blocks * bm - pstart[block_e]), 0, bm).astype(jnp.int32)
    block_valid = jnp.where(blocks < n_used, block_valid, 0)
    block_e = jnp.where(blocks < n_used, block_e, block_e[jnp.maximum(n_used - 1, 0)])

    nsub = tm // tt
    before = tile_base.reshape(n // tm, E, LANES)[:, :, :nsub].transpose(0, 2, 1).reshape(n // tt, E)
    before = before.astype(jnp.int32)
    after = jnp.concatenate([before[1:], counts[None, :]], axis=0)
    first = pstart[None, :] + before
    astart = first // PIECE * PIECE
    npieces = jnp.where(after > before, (first + (after - before) - astart + PIECE - 1) // PIECE, 0)
    cum = jnp.cumsum(npieces, axis=1)
    pbase = cum - npieces
    kcat = -(-(tt * TOPK + E * 2 * (PIECE - 1)) // (2 * LANES)) * (2 * LANES)
    shift = (astart - pbase * PIECE).astype(jnp.int32)
    q = jnp.arange(kcat // PIECE, dtype=jnp.int32)
    owner = jnp.sum(cum[:, None, :] <= q[None, :, None], axis=2)
    own = owner[:, :, None] == jnp.arange(E, dtype=jnp.int32)[None, None, :]
    piece_src = jnp.sum(jnp.where(own, shift[:, None, :], 0), axis=2) + q[None, :] * PIECE
    piece_src = jnp.where(owner < E, piece_src, 0).astype(jnp.int32)
    total = cum[:, -1].astype(jnp.int32)
    last = first + (after - before)
    starts_inside = ((npieces > 0) & (first % PIECE != 0)).astype(jnp.int32)
    ends_inside = ((npieces > 0) & (last % PIECE != 0)).astype(jnp.int32)
    held = jnp.sum(jnp.where(own & (q[None, :, None] == cum[:, None, :] - 1), ends_inside[:, None, :], 0), axis=2)
    piece_dst = jnp.where((owner < E) & (held == 0), piece_src, -1).astype(jnp.int32)
    written = (total - jnp.sum(ends_inside, axis=1)).astype(jnp.int32)
    tables = (piece_dst, total, written, (pbase * PIECE).astype(jnp.int32), npieces.astype(jnp.int32),
              starts_inside, ends_inside)

    _, slot = _dest(pstart, shift, top_e, rank, n, tt)
    wg, wu, wd, xg = _prep_dispatch(tables, counts, padded, pstart, slot, hn, w_gate_up[0], w_down[0], n,
                                    n_blocks * bm, tt, kcat)
    bgu = b_gate_up[0]
    y = _moe(block_e, n_used.reshape(1).astype(jnp.int32), block_valid, xg, wg, wu, wd,
             bgu[:, None, 0::2], bgu[:, None, 1::2], b_down[0][:, None, :], n_blocks, bm)
    out = _combine(piece_src, total, y, h, slot.T, top_w.T, g_final[None, :], n, tt, kcat)
    return out.reshape(b, s, D)
```

```python
import functools

import jax
import jax.numpy as jnp
from jax import lax
from jax.experimental import pallas as pl
from jax.experimental.pallas import tpu as pltpu

F32 = jnp.float32
BF16 = jnp.bfloat16
U32 = jnp.uint32

D = 2048
HQ, HKV, DH, WIN = 16, 2, 64, 128
G = HQ // HKV
MH, QR, KVR, DN, DR, DV = 8, 512, 256, 128, 64, 128
THETA = 10000.0
E, TOPK, DE = 32, 4, 2048
LIMIT, ALPHA = 7.0, 1.702
EPS = 1e-6
NEG = -1e30
LOG2E = 1.4426950408889634

LANES = 128
CHUNKS = D // LANES
PACK_ROWS = CHUNKS // 2
PIECE = 16
VMEM_LIMIT = 56 * 1024 * 1024

PROJ_COLS = 6400
COL_GATES, COL_Q, COL_DQ, COL_DKV, COL_K, COL_V, COL_KR = 0, 4096, 5120, 5632, 5888, 6016, 6144

NT = (((1,), (1,)), ((), ()))


def _params(sem, vmem=VMEM_LIMIT):
    return pltpu.CompilerParams(dimension_semantics=sem, vmem_limit_bytes=vmem)


def _rms(x, g):
    return x * lax.rsqrt(jnp.mean(x * x, axis=-1, keepdims=True) + EPS) * g


def _pack_pair(lo, hi):
    lo_bits = lax.bitcast_convert_type(lo.astype(BF16).astype(F32), U32)
    hi_bits = lax.bitcast_convert_type(hi.astype(BF16).astype(F32), U32)
    return (lo_bits >> 16) | hi_bits


def _unpack_pair(w):
    lo = lax.bitcast_convert_type(w << 16, F32)
    hi = lax.bitcast_convert_type(w & jnp.uint32(0xFFFF0000), F32)
    return lo, hi


def _pipelined(n, produce, consume):
    nxt = produce(0)
    for i in range(n):
        cur = nxt
        if i + 1 < n:
            nxt = produce(i + 1)
        consume(i, cur)


def _rope128(v, cos, sin):
    lane = lax.broadcasted_iota(jnp.int32, v.shape, 1)
    rot = jnp.where((lane % DH) < (DH // 2), -pltpu.roll(v, LANES - DH // 2, 1), pltpu.roll(v, DH // 2, 1))
    return v * cos + rot * sin


def _rope_table_kernel(pos_ref, inv_ref, cos_ref, sin_ref):
    ang = pos_ref[...].astype(F32) * inv_ref[...]
    cos_ref[...] = jnp.cos(ang)
    sin_ref[...] = jnp.sin(ang)


def _rope_tables(positions, n):
    half = DH // 2
    per_row = LANES // half
    inv = jnp.power(THETA, -jnp.arange(half, dtype=F32) * 2.0 / DH)
    inv_row = jnp.tile(inv, per_row)[None, :]
    pos_rows = jnp.repeat(positions.reshape(n // per_row, per_row), half, axis=1)
    rows = n // per_row
    tr = min(rows, 1024)
    cos, sin = pl.pallas_call(
        _rope_table_kernel,
        grid=(rows // tr,),
        in_specs=[pl.BlockSpec((tr, LANES), lambda i: (i, 0)), pl.BlockSpec((1, LANES), lambda i: (0, 0))],
        out_specs=[pl.BlockSpec((tr, LANES), lambda i: (i, 0))] * 2,
        out_shape=[jax.ShapeDtypeStruct((rows, LANES), F32)] * 2,
        name="rope_tables",
    )(pos_rows, inv_row)
    cos = jnp.tile(cos.reshape(n, half), (1, per_row))
    sin = jnp.tile(sin.reshape(n, half), (1, per_row))
    return cos, sin


def _in_proj_kernel(x_ref, g_ref, w_ref, o_ref, xn_ref):
    @pl.when(pl.program_id(1) == 0)
    def _():
        xn_ref[...] = _rms(x_ref[...], g_ref[...]).astype(BF16)

    o_ref[...] = jnp.dot(xn_ref[...], w_ref[...], preferred_element_type=F32).astype(o_ref.dtype)


def _in_proj(x2, g_attn, w_in_p, n):
    tm = min(n, 1024)
    tn = 1280
    return pl.pallas_call(
        _in_proj_kernel,
        grid=(n // tm, PROJ_COLS // tn),
        in_specs=[
            pl.BlockSpec((tm, D), lambda i, j: (i, 0)),
            pl.BlockSpec((1, D), lambda i, j: (0, 0)),
            pl.BlockSpec((D, tn), lambda i, j: (0, j)),
        ],
        out_specs=pl.BlockSpec((tm, tn), lambda i, j: (i, j)),
        out_shape=jax.ShapeDtypeStruct((n, PROJ_COLS), BF16),
        scratch_shapes=[pltpu.VMEM((tm, D), BF16)],
        compiler_params=_params(("arbitrary", "arbitrary")),
        name="in_proj",
    )(x2, g_attn, w_in_p)


def _prep_kernel(q_ref, dq_ref, dkv_ref, k_ref, kr_ref, cos_ref, sin_ref, gq_ref, gkv_ref,
                 wq_ref, wk_ref, wv_ref, qa_ref, ka_ref, qm_ref, km_ref, vm_ref):
    cos = cos_ref[...]
    sin = sin_ref[...]
    swa_scale = DH ** -0.5 * LOG2E
    for c in range(HQ * DH // LANES):
        sl = slice(c * LANES, (c + 1) * LANES)
        qa_ref[:, sl] = (_rope128(q_ref[:, sl].astype(F32), cos, sin) * swa_scale).astype(BF16)
    ka_ref[...] = _rope128(k_ref[...].astype(F32), cos, sin).astype(BF16)

    mla_scale = (DN + DR) ** -0.5 * LOG2E
    cq = _rms(dq_ref[...].astype(F32), gq_ref[...]).astype(BF16)
    qb = jnp.dot(cq, wq_ref[...], preferred_element_type=F32)
    ckv = _rms(dkv_ref[...].astype(F32), gkv_ref[...]).astype(BF16)
    kn = jnp.dot(ckv, wk_ref[...], preferred_element_type=F32)
    vv = jnp.dot(ckv, wv_ref[...], preferred_element_type=F32)
    ones_col = jnp.where(lax.broadcasted_iota(jnp.int32, (vv.shape[0], LANES), 1) == 0, 1.0, 0.0).astype(BF16)
    kr = _rope128(kr_ref[...].astype(F32), cos, sin).astype(BF16)
    for h in range(MH):
        lo = h * 2 * LANES
        qm_ref[:, lo:lo + LANES] = (qb[:, lo:lo + LANES] * mla_scale).astype(BF16)
        qm_ref[:, lo + LANES:lo + 2 * LANES] = (
            _rope128(qb[:, lo + LANES:lo + 2 * LANES], cos, sin) * mla_scale).astype(BF16)
        km_ref[:, lo:lo + LANES] = kn[:, h * LANES:(h + 1) * LANES].astype(BF16)
        km_ref[:, lo + LANES:lo + 2 * LANES] = kr
        vm_ref[:, lo:lo + LANES] = vv[:, h * LANES:(h + 1) * LANES].astype(BF16)
        vm_ref[:, lo + LANES:lo + 2 * LANES] = ones_col


def _prep(proj, cos, sin, g_q, g_kv, wq_p, wk_p, wv_p, n):
    tm = min(n, 512)
    row = lambda w, cb: pl.BlockSpec((tm, w), lambda i: (i, cb))
    full = lambda a: pl.BlockSpec(a.shape, lambda i: (0, 0))
    return pl.pallas_call(
        _prep_kernel,
        grid=(n // tm,),
        in_specs=[
            row(HQ * DH, COL_Q // (HQ * DH)), row(QR, COL_DQ // QR), row(KVR, COL_DKV // KVR),
            row(LANES, COL_K // LANES), row(LANES, COL_KR // LANES),
            row(LANES, 0), row(LANES, 0), full(g_q), full(g_kv), full(wq_p), full(wk_p), full(wv_p),
        ],
        out_specs=[row(HQ * DH, 0), row(LANES, 0)] + [row(MH * 2 * LANES, 0)] * 3,
        out_shape=[jax.ShapeDtypeStruct((n, HQ * DH), BF16), jax.ShapeDtypeStruct((n, LANES), BF16)]
        + [jax.ShapeDtypeStruct((n, MH * 2 * LANES), BF16)] * 3,
        compiler_params=_params(("arbitrary",)),
        name="rope_mla_prep",
    )(proj, proj, proj, proj, proj, cos, sin, g_q, g_kv, wq_p, wk_p, wv_p)


def _swa_kernel(sinks_ref, q_ref, kc_ref, kp_ref, vc_ref, vp_ref, o_ref):
    blk = pl.program_id(1)
    q2 = jnp.concatenate([q_ref[:, g * LANES:(g + 1) * LANES] for g in range(G)], axis=0)
    k2 = jnp.concatenate([kp_ref[...], kc_ref[...]], axis=0)
    v2t = jnp.concatenate([vp_ref[...], vc_ref[...]], axis=0).astype(F32).T
    lane = lax.broadcasted_iota(jnp.int32, (2 * WIN, LANES), 1)
    vrow = lax.broadcasted_iota(jnp.int32, (LANES, 2 * WIN), 0)
    kj = lax.broadcasted_iota(jnp.int32, (2 * WIN, WIN), 0)
    qi = lax.broadcasted_iota(jnp.int32, (2 * WIN, WIN), 1)
    valid = (kj > qi) & (kj <= qi + WIN) & ((kj >= WIN) | (blk > 0))
    outs = []

    def scores(h):
        kz = jnp.where((lane >= h * DH) & (lane < (h + 1) * DH), k2, jnp.zeros_like(k2))
        return lax.dot_general(kz, q2, NT, preferred_element_type=F32)

    def attend(h, s):
        vz = jnp.where((vrow >= h * DH) & (vrow < (h + 1) * DH), v2t, 0.0).astype(BF16)
        ps, invs = [], []
        for g in range(G):
            sg = jnp.where(valid, s[:, g * WIN:(g + 1) * WIN], NEG)
            sink = sinks_ref[h * G + g] * LOG2E
            m = jnp.maximum(jnp.max(sg, axis=0, keepdims=True), sink)
            p = jnp.exp2(sg - m)
            l = jnp.sum(p, axis=0, keepdims=True) + jnp.exp2(sink - m)
            ps.append(p.astype(BF16))
            invs.append(1.0 / l)
        o_h = jnp.dot(vz, jnp.concatenate(ps, axis=1), preferred_element_type=F32)
        outs.append(o_h * jnp.concatenate(invs, axis=1))

    _pipelined(HKV, scores, attend)
    acc = outs[0] + outs[1]
    for g in range(G):
        o_ref[:, g * LANES:(g + 1) * LANES] = acc[:, g * WIN:(g + 1) * WIN].T.astype(BF16)


def _swa(sinks, qa, ka, proj, b, s):
    nb = s // WIN
    cur = lambda cb: pl.BlockSpec((WIN, LANES), lambda bi, n, sk: (bi * nb + n, cb))
    prev = lambda cb: pl.BlockSpec((WIN, LANES), lambda bi, n, sk: (bi * nb + jnp.maximum(n - 1, 0), cb))
    grid_spec = pltpu.PrefetchScalarGridSpec(
        num_scalar_prefetch=1,
        grid=(b, nb),
        in_specs=[
            pl.BlockSpec((WIN, HQ * DH), lambda bi, n, sk: (bi * nb + n, 0)),
            cur(0), prev(0), cur(COL_V // LANES), prev(COL_V // LANES),
        ],
        out_specs=pl.BlockSpec((WIN, HQ * DH), lambda bi, n, sk: (bi * nb + n, 0)),
    )
    return pl.pallas_call(
        _swa_kernel,
        grid_spec=grid_spec,
        out_shape=jax.ShapeDtypeStruct((b * s, HQ * DH), BF16),
        compiler_params=_params(("arbitrary", "arbitrary")),
        name="swa_attention",
    )(sinks, qa, ka, ka, proj, proj)


def _mla_kernel(q_ref, k_ref, v_ref, o_ref, *, seq, tq):
    row = lax.broadcasted_iota(jnp.int32, (tq, tq), 0)
    col = lax.broadcasted_iota(jnp.int32, (tq, tq), 1)
    causal = col <= row
    nq = seq // tq

    def scores(i):
        kv = (i + 1) * tq
        q = q_ref[i * tq:(i + 1) * tq, :]
        s = lax.dot_general(q, k_ref[0:kv, :], NT, preferred_element_type=F32)
        diag = jnp.where(causal, s[:, kv - tq:kv], NEG)
        return diag if i == 0 else jnp.concatenate([s[:, 0:kv - tq], diag], axis=1)

    def finish(i, s):
        kv = (i + 1) * tq
        m = jnp.max(s, axis=-1, keepdims=True)
        p = jnp.exp2(s - m).astype(BF16)
        ov = jnp.dot(p, v_ref[0:kv, :], preferred_element_type=F32)
        o_ref[i * tq:(i + 1) * tq, :] = (ov[:, 0:DV] * (1.0 / ov[:, DV:DV + 1])).astype(BF16)

    _pipelined(nq, scores, finish)


def _mla(qm, km, vm, b, s):
    tq = min(s, 512)
    return pl.pallas_call(
        functools.partial(_mla_kernel, seq=s, tq=tq),
        grid=(b, MH),
        in_specs=[
            pl.BlockSpec((s, 2 * LANES), lambda bi, h: (bi, h)),
            pl.BlockSpec((s, 2 * LANES), lambda bi, h: (bi, h)),
            pl.BlockSpec((s, 2 * LANES), lambda bi, h: (bi, h)),
        ],
        out_specs=pl.BlockSpec((s, DV), lambda bi, h: (bi, h)),
        out_shape=jax.ShapeDtypeStruct((b * s, MH * DV), BF16),
        compiler_params=_params(("arbitrary", "arbitrary")),
        name="mla_attention",
    )(qm, km, vm)


def _mix_kernel(oa_ref, ob_ref, ga_ref, gb_ref, wa_ref, wb_ref, ba_ref, bb_ref, o_ref):
    ya = jnp.dot(oa_ref[...], wa_ref[...], preferred_element_type=F32)
    yb = jnp.dot(ob_ref[...], wb_ref[...], preferred_element_type=F32)
    ga = jax.nn.sigmoid(ga_ref[...].astype(F32) + ba_ref[...])
    gb = jax.nn.sigmoid(gb_ref[...].astype(F32) + bb_ref[...])
    o_ref[...] = (ga * ya + gb * yb).astype(BF16)


def _mix(out_a, out_b, proj, wa, wb, b_gate, n):
    tm = min(n, 1024)
    tn = 1024
    nj = D // tn
    return pl.pallas_call(
        _mix_kernel,
        grid=(n // tm, nj),
        in_specs=[
            pl.BlockSpec((tm, HQ * DH), lambda i, j: (i, 0)),
            pl.BlockSpec((tm, MH * DV), lambda i, j: (i, 0)),
            pl.BlockSpec((tm, tn), lambda i, j: (i, j)),
            pl.BlockSpec((tm, tn), lambda i, j: (i, nj + j)),
            pl.BlockSpec((HQ * DH, tn), lambda i, j: (0, j)),
            pl.BlockSpec((MH * DV, tn), lambda i, j: (0, j)),
            pl.BlockSpec((1, tn), lambda i, j: (0, j)),
            pl.BlockSpec((1, tn), lambda i, j: (0, nj + j)),
        ],
        out_specs=pl.BlockSpec((tm, tn), lambda i, j: (i, j)),
        out_shape=jax.ShapeDtypeStruct((n, D), BF16),
        compiler_params=_params(("arbitrary", "arbitrary")),
        name="gated_mix",
    )(out_a, out_b, proj, proj, wa, wb, b_gate, b_gate)


def _router_kernel(mx_ref, x_ref, wo_ref, gf_ref, wr_ref, br_ref,
                   h_ref, hn_ref, te_ref, tw_ref, rk_ref, cnt_ref, tb_ref, base_ref, *, tm, sub):
    @pl.when(pl.program_id(0) == 0)
    def _():
        base_ref[...] = jnp.zeros_like(base_ref)

    h = x_ref[...] + jnp.dot(mx_ref[...], wo_ref[...], preferred_element_type=F32)
    h_ref[...] = h
    hn = _rms(h, gf_ref[...])
    hn_ref[...] = hn.astype(BF16)

    logits = lax.dot_general(wr_ref[...], hn, NT, preferred_element_type=F32,
                             precision=lax.Precision.HIGHEST) + br_ref[...]
    eidx = lax.broadcasted_iota(jnp.int32, (E, tm), 0)
    vals, idxs = [], []
    l = logits
    for _ in range(TOPK):
        m = jnp.max(l, axis=0, keepdims=True)
        idx = jnp.min(jnp.where(l == m, eidx, E), axis=0, keepdims=True)
        vals.append(m)
        idxs.append(idx)
        l = jnp.where(eidx == idx, -jnp.inf, l)
    ex = [jnp.exp(v - vals[0]) for v in vals]
    tot = ex[0] + ex[1] + ex[2] + ex[3]
    tw_ref[...] = jnp.concatenate([e / tot for e in ex], axis=0)
    te_ref[...] = jnp.concatenate(idxs, axis=0)

    onehot = jnp.zeros((E, tm), F32)
    for idx in idxs:
        onehot = onehot + jnp.where(eidx == idx, 1.0, 0.0)
    r = lax.broadcasted_iota(jnp.int32, (tm, tm), 0)
    c = lax.broadcasted_iota(jnp.int32, (tm, tm), 1)
    earlier = jnp.where(r < c, 1.0, 0.0).astype(BF16)
    before = jnp.dot(onehot.astype(BF16), earlier, preferred_element_type=F32) + base_ref[:, 0:1]
    rk_ref[...] = jnp.concatenate(
        [jnp.sum(jnp.where(eidx == idx, before, 0.0), axis=0, keepdims=True) for idx in idxs],
        axis=0).astype(jnp.int32)
    lane = lax.broadcasted_iota(jnp.int32, (E, LANES), 1)
    tb = jnp.zeros((E, LANES), F32)
    for s in range(tm // sub):
        tb = jnp.where(lane == s, before[:, s * sub:s * sub + 1], tb)
    tb_ref[...] = tb
    base_ref[...] = base_ref[...] + jnp.sum(onehot, axis=1, keepdims=True)
    cnt_ref[...] = base_ref[...]


def _router(mixed, x2, wo, g_ffn, wr_t, b_router, n, tm, sub):
    row = lambda w: pl.BlockSpec((tm, w), lambda i: (i, 0))
    full = lambda a: pl.BlockSpec(a.shape, lambda i: (0, 0), pipeline_mode=pl.Buffered(1))
    tok = pl.BlockSpec((TOPK, tm), lambda i: (0, i))
    return pl.pallas_call(
        functools.partial(_router_kernel, tm=tm, sub=sub),
        grid=(n // tm,),
        in_specs=[row(D), row(D), full(wo), full(g_ffn), full(wr_t), full(b_router)],
        out_specs=[
            row(D), row(D), tok, tok, tok,
            pl.BlockSpec((E, LANES), lambda i: (0, 0)), pl.BlockSpec((E, LANES), lambda i: (i, 0)),
        ],
        out_shape=[
            jax.ShapeDtypeStruct((n, D), F32), jax.ShapeDtypeStruct((n, D), BF16),
            jax.ShapeDtypeStruct((TOPK, n), jnp.int32), jax.ShapeDtypeStruct((TOPK, n), F32),
            jax.ShapeDtypeStruct((TOPK, n), jnp.int32), jax.ShapeDtypeStruct((E, LANES), F32),
            jax.ShapeDtypeStruct((n // tm * E, LANES), F32),
        ],
        scratch_shapes=[pltpu.VMEM((E, LANES), F32)],
        compiler_params=_params(("arbitrary",)),
        name="outproj_router",
    )(mixed, x2, wo, g_ffn, wr_t, b_router)


def _dest_kernel(pstart_ref, shift_ref, te_ref, rk_ref, dest_ref, slot_ref):
    tile = pl.program_id(0)
    te = te_ref[...]
    d = rk_ref[...]
    s = rk_ref[...]
    for e in range(E):
        hit = te == e
        d = d + jnp.where(hit, pstart_ref[e], 0)
        s = s + jnp.where(hit, pstart_ref[e] - shift_ref[tile, e], 0)
    dest_ref[...] = d
    slot_ref[...] = s


def _dest(pstart, shift, top_e, rank, n, tt):
    tok = pl.BlockSpec((TOPK, tt), lambda i, ps, sh: (0, i))
    grid_spec = pltpu.PrefetchScalarGridSpec(
        num_scalar_prefetch=2, grid=(n // tt,), in_specs=[tok, tok], out_specs=[tok, tok])
    return pl.pallas_call(
        _dest_kernel, grid_spec=grid_spec,
        out_shape=[jax.ShapeDtypeStruct((TOPK, n), jnp.int32)] * 2, name="dest_rows",
    )(pstart, shift, top_e, rank)


def _prep_dispatch_kernel(pdst_ref, ntot_ref, nwr_ref, pb_ref, np_ref, ff_ref, lf_ref, cnt_ref, pad_ref, pst_ref,
                          slot_ref, hn_ref, wgu_ref, wdn_ref, g_ref, u_ref, d_ref, xg_ref,
                          stage_ref, carry_ref, sem, fsem, *, tt, n_disp, kcat):
    i = pl.program_id(0)

    def stage_piece(row):
        return stage_ref.at[pl.ds(pl.multiple_of(row, PIECE), PIECE), :]

    def out_piece(row):
        return xg_ref.at[pl.ds(pl.multiple_of(row, PIECE), PIECE), :]

    @pl.when(i == 0)
    def _():
        carry_ref[...] = jnp.zeros_like(carry_ref)

    @pl.when(i < n_disp)
    def _():
        slot = slot_ref[...]
        chunk = 512
        for r0 in range(0, kcat, chunk):
            row = lax.broadcasted_iota(jnp.int32, (chunk, tt), 0) + r0
            hit = row == slot[0:1, :]
            for k in range(1, TOPK):
                hit = hit | (row == slot[k:k + 1, :])
            sel = jnp.where(hit, 1.0, 0.0).astype(BF16)
            stage_ref[r0:r0 + chunk, :] = jnp.dot(sel, hn_ref[...], preferred_element_type=F32).astype(BF16)

        def per_expert(e, c):
            pieces = np_ref[i, e]

            @pl.when(pieces > 0)
            def _():
                first = pb_ref[i, e]
                last = first + (pieces - 1) * PIECE

                @pl.when(ff_ref[i, e] == 1)
                def _():
                    stage_piece(first)[...] = stage_piece(first)[...] + carry_ref[e]

                @pl.when(lf_ref[i, e] == 1)
                def _():
                    carry_ref[e] = stage_piece(last)[...]

            return c

        lax.fori_loop(0, E, per_expert, 0)

        def start(q, c):
            dst = pdst_ref[i, q]

            @pl.when(dst >= 0)
            def _():
                pltpu.make_async_copy(stage_piece(q * PIECE), out_piece(dst), sem).start()

            return c

        lax.fori_loop(0, ntot_ref[i], start, 0)

    chunk = 2 * LANES
    r = lax.broadcasted_iota(jnp.int32, (chunk, chunk), 0)
    c = lax.broadcasted_iota(jnp.int32, (chunk, chunk), 1)
    src = jnp.where(c < LANES, 2 * c, 2 * (c - LANES) + 1)
    perm = jnp.where(r == src, 1.0, 0.0).astype(BF16)
    for k in range(wgu_ref.shape[1] // chunk):
        w = wgu_ref[:, k * chunk:(k + 1) * chunk].astype(BF16)
        o = jnp.dot(w, perm, preferred_element_type=F32)
        g_ref[:, k * LANES:(k + 1) * LANES] = o[:, :LANES].astype(BF16)
        u_ref[:, k * LANES:(k + 1) * LANES] = o[:, LANES:].astype(BF16)
    d_ref[...] = wdn_ref[...].astype(BF16)

    @pl.when(i < n_disp)
    def _():
        def wait(q, c):
            pltpu.make_async_copy(stage_piece(0), out_piece(0), sem).wait()
            return c

        lax.fori_loop(0, nwr_ref[i], wait, 0)

    @pl.when(i == n_disp - 1)
    def _():
        stage_ref[0:PIECE, :] = jnp.zeros((PIECE, D), BF16)

        def per_expert(e, c):
            real = cnt_ref[e]
            tail = real % PIECE
            base = pst_ref[e]

            @pl.when(tail != 0)
            def _():
                cp = pltpu.make_async_copy(carry_ref.at[e], out_piece(base + real - tail), fsem)
                cp.start()
                cp.wait()

            zero_from = (real + PIECE - 1) // PIECE
            zero_to = pad_ref[e] // PIECE

            def zstart(z, c2):
                pltpu.make_async_copy(stage_piece(0), out_piece(base + z * PIECE), fsem).start()
                return c2

            def zwait(z, c2):
                pltpu.make_async_copy(stage_piece(0), out_piece(base + z * PIECE), fsem).wait()
                return c2

            lax.fori_loop(zero_from, zero_to, zstart, 0)
            lax.fori_loop(zero_from, zero_to, zwait, 0)
            return c

        lax.fori_loop(0, E, per_expert, 0)


def _prep_dispatch(tables, counts, padded, pstart, slot, hn, w_gate_up, w_down, n, p_rows, tt, kcat):
    tr = 512
    steps = E * (D // tr)
    n_disp = n // tt
    assert n_disp <= steps, "more token tiles than weight tiles"
    nsp = len(tables) + 3
    tile = lambda i, *_: (i // (D // tr), i % (D // tr), 0)
    tok = lambda i: jnp.minimum(i, n_disp - 1)
    wout = pl.BlockSpec((None, tr, DE), tile)
    grid_spec = pltpu.PrefetchScalarGridSpec(
        num_scalar_prefetch=nsp, grid=(steps,),
        in_specs=[
            pl.BlockSpec((TOPK, tt), lambda i, *_: (0, tok(i))),
            pl.BlockSpec((tt, D), lambda i, *_: (tok(i), 0)),
            pl.BlockSpec((None, tr, 2 * DE), tile),
            pl.BlockSpec((None, tr, D), tile),
        ],
        out_specs=[wout, wout, pl.BlockSpec((None, tr, D), tile), pl.BlockSpec(memory_space=pl.ANY)],
        scratch_shapes=[pltpu.VMEM((kcat, D), BF16), pltpu.VMEM((E, PIECE, D), BF16),
                        pltpu.SemaphoreType.DMA(()), pltpu.SemaphoreType.DMA(())],
    )
    return pl.pallas_call(
        functools.partial(_prep_dispatch_kernel, tt=tt, n_disp=n_disp, kcat=kcat), grid_spec=grid_spec,
        out_shape=[jax.ShapeDtypeStruct((E, D, DE), BF16), jax.ShapeDtypeStruct((E, D, DE), BF16),
                   jax.ShapeDtypeStruct((E, DE, D), BF16), jax.ShapeDtypeStruct((p_rows, D), BF16)],
        compiler_params=pltpu.CompilerParams(dimension_semantics=("arbitrary",), has_side_effects=True,
                                             vmem_limit_bytes=VMEM_LIMIT),
        name="weight_prep_dispatch",
    )(*tables, counts, padded, pstart, slot, hn, w_gate_up, w_down)


def _moe_kernel(be_ref, nu_ref, bv_ref, x_ref, wg_ref, wu_ref, wd_ref, bg_ref, bu_ref, bd_ref, y_ref,
                acc_ref, *, bm, nj, sub, dsub):
    blk = pl.program_id(0)
    j = pl.program_id(1)
    valid = bv_ref[blk]

    th = wg_ref.shape[1]

    def body(rows, last):
        x = x_ref[0:rows, :]
        acts = []

        def gate_up(s):
            sl = slice(s * sub, (s + 1) * sub)
            return (jnp.dot(x, wg_ref[:, sl], preferred_element_type=F32) + bg_ref[:, sl],
                    jnp.dot(x, wu_ref[:, sl], preferred_element_type=F32) + bu_ref[:, sl])

        def activate(s, gu):
            gate = jnp.minimum(gu[0], LIMIT)
            up = jnp.clip(gu[1], -LIMIT, LIMIT)
            acts.append((gate * jax.nn.sigmoid(ALPHA * gate) * (up + 1.0)).astype(BF16))

        _pipelined(th // sub, gate_up, activate)
        act = jnp.concatenate(acts, axis=1)

        def down(c):
            return jnp.dot(act, wd_ref[:, c * dsub:(c + 1) * dsub], preferred_element_type=F32)

        def emit(c, part):
            sl = slice(c * dsub, (c + 1) * dsub)
            if last:
                y_ref[0:rows, sl] = (acc_ref[0:rows, sl] + part).astype(BF16)
            else:
                acc_ref[0:rows, sl] = part + bd_ref[:, sl]

        _pipelined(D // dsub, down, emit)
        if last and rows < bm:
            y_ref[rows:bm, :] = jnp.zeros((bm - rows, D), BF16)

    assert nj == 2, "one accumulate step then one output step"
    for rows, cond in ((bm, valid > bm // 2), (bm // 2, (valid > 0) & (valid <= bm // 2))):
        for last in (False, True):
            @pl.when(cond & (j == int(last)))
            def _(rows=rows, last=last):
                body(rows, last)


def _moe(block_e, n_used, block_valid, xg, wg, wu, wd, bg, bu, bd, n_blocks, bm):
    th = 1024
    nj = DE // th

    def wcol(blk, j, be, nu, bv):
        return (be[blk], 0, jnp.where(blk < nu[0], j, nj - 1))

    def wrow(blk, j, be, nu, bv):
        return (be[blk], jnp.where(blk < nu[0], j, nj - 1), 0)

    def xrow(blk, j, be, nu, bv):
        return (jnp.minimum(blk, nu[0] - 1), 0)

    grid_spec = pltpu.PrefetchScalarGridSpec(
        num_scalar_prefetch=3, grid=(n_blocks, nj),
        in_specs=[
            pl.BlockSpec((bm, D), xrow),
            pl.BlockSpec((None, D, th), wcol), pl.BlockSpec((None, D, th), wcol),
            pl.BlockSpec((None, th, D), wrow),
            pl.BlockSpec((None, 1, th), wcol), pl.BlockSpec((None, 1, th), wcol),
            pl.BlockSpec((None, 1, D), lambda blk, j, be, nu, bv: (be[blk], 0, 0)),
        ],
        out_specs=pl.BlockSpec((bm, D), xrow),
        scratch_shapes=[pltpu.VMEM((bm, D), F32)],
    )
    return pl.pallas_call(
        functools.partial(_moe_kernel, bm=bm, nj=nj, sub=256, dsub=512), grid_spec=grid_spec,
        out_shape=jax.ShapeDtypeStruct((n_blocks * bm, D), BF16),
        compiler_params=_params(("arbitrary", "arbitrary")),
        name="moe_experts",
    )(block_e, n_used, block_valid, xg, wg, wu, wd, bg, bu, bd)


def _combine_kernel(src_ref, tot_ref, y_ref, h_ref, slot_ref, tw_ref, gf_ref, o_ref, ycat_ref, sem,
                    *, tt, nt, kcat):
    i = pl.program_id(0)
    cur = i % 2

    def for_pieces(tile, s, fn):
        def body(q, c):
            fn(pltpu.make_async_copy(
                y_ref.at[pl.ds(pl.multiple_of(src_ref[tile, q], PIECE), PIECE), :],
                ycat_ref.at[s, pl.ds(pl.multiple_of(q * PIECE, PIECE), PIECE), :], sem.at[s]))
            return c

        lax.fori_loop(0, tot_ref[tile], body, 0)

    @pl.when(i == 0)
    def _():
        ycat_ref[...] = jnp.zeros_like(ycat_ref)
        for_pieces(0, 0, lambda cp: cp.start())

    @pl.when(i + 1 < nt)
    def _():
        for_pieces(i + 1, 1 - cur, lambda cp: cp.start())

    for_pieces(i, cur, lambda cp: cp.wait())

    def combine(k_rows):
        slot = slot_ref[...]
        tw = tw_ref[...]
        col = lax.broadcasted_iota(jnp.int32, (tt, k_rows), 1)
        sel = jnp.zeros((tt, k_rows), F32)
        for k in range(TOPK):
            sel = jnp.where(col == slot[:, k:k + 1], tw[:, k:k + 1], sel)
        moe = jnp.dot(sel.astype(BF16), ycat_ref[cur, 0:k_rows, :], preferred_element_type=F32)
        o_ref[...] = _rms(h_ref[...] + moe, gf_ref[...])

    k_short = kcat * 3 // 4
    used = tot_ref[i] * PIECE

    @pl.when(used <= k_short)
    def _():
        combine(k_short)

    @pl.when(used > k_short)
    def _():
        combine(kcat)


def _combine(src, total, y, h, slot, tw, g_final, n, tt, kcat):
    nt = n // tt
    grid_spec = pltpu.PrefetchScalarGridSpec(
        num_scalar_prefetch=2, grid=(nt,),
        in_specs=[
            pl.BlockSpec(memory_space=pl.ANY),
            pl.BlockSpec((tt, D), lambda i, a, b: (i, 0)),
            pl.BlockSpec((tt, TOPK), lambda i, a, b: (i, 0)),
            pl.BlockSpec((tt, TOPK), lambda i, a, b: (i, 0)),
            pl.BlockSpec((1, D), lambda i, a, b: (0, 0)),
        ],
        out_specs=pl.BlockSpec((tt, D), lambda i, a, b: (i, 0)),
        scratch_shapes=[pltpu.VMEM((2, kcat, D), BF16), pltpu.SemaphoreType.DMA((2,))],
    )
    return pl.pallas_call(
        functools.partial(_combine_kernel, tt=tt, nt=nt, kcat=kcat), grid_spec=grid_spec,
        out_shape=jax.ShapeDtypeStruct((n, D), F32),
        compiler_params=_params(("arbitrary",)),
        name="combine_final_norm",
    )(src, total, y, h, slot, tw, g_final)


def _permute_weights(w_in, w_uq, w_ukv, w_o_swa):
    widths = (HQ * DH, HKV * DH, HKV * DH, QR, KVR, DR, 2 * D)
    offs = [0]
    for w in widths:
        offs.append(offs[-1] + w)
    wq, wk, wv, wdq, wdkv, wkr, wgates = [w_in[:, offs[i]:offs[i + 1]] for i in range(7)]
    wq = wq.reshape(D, HKV, G, DH).transpose(0, 2, 1, 3).reshape(D, HQ * DH)
    pad = jnp.zeros((D, PROJ_COLS - COL_KR - DR), w_in.dtype)
    w_in_p = jnp.concatenate([wgates, wq, wdq, wdkv, wk, wv, wkr, pad], axis=1).astype(BF16)
    wo_a = w_o_swa.reshape(HKV, G, DH, D).transpose(1, 0, 2, 3).reshape(HQ * DH, D).astype(BF16)
    wuq = w_uq.reshape(QR, MH, DN + DR)
    wq_p = jnp.concatenate([wuq, jnp.zeros((QR, MH, 2 * LANES - DN - DR), w_uq.dtype)], axis=2)
    wq_p = wq_p.reshape(QR, MH * 2 * LANES).astype(BF16)
    wukv = w_ukv.reshape(KVR, MH, DN + DV)
    wk_p = wukv[:, :, :DN].reshape(KVR, MH * DN).astype(BF16)
    wv_p = wukv[:, :, DN:].reshape(KVR, MH * DV).astype(BF16)
    return w_in_p, wo_a, wq_p, wk_p, wv_p


def kernel(x, positions, g_attn, w_in, b_gate, sinks, g_q, w_uq, g_kv, w_ukv, w_o_swa, w_o_mla, w_out,
           g_ffn, w_router, b_router, w_gate_up, b_gate_up, w_down, b_down, g_final):
    b, s, _ = x.shape
    n = b * s
    assert w_in.shape[0] == 1, "single-layer block"
    x2 = x.reshape(n, D)

    w_in_p, wo_a, wq_p, wk_p, wv_p = _permute_weights(w_in[0], w_uq[0], w_ukv[0], w_o_swa[0])
    cos, sin = _rope_tables(positions, n)
    proj = _in_proj(x2, g_attn, w_in_p, n)
    qa, ka, qm, km, vm = _prep(proj, cos, sin, g_q, g_kv, wq_p, wk_p, wv_p, n)
    out_a = _swa(sinks[0], qa, ka, proj, b, s)
    out_b = _mla(qm, km, vm, b, s)

    mixed = _mix(out_a, out_b, proj, wo_a, w_o_mla[0].astype(BF16), b_gate, n)
    tm = min(n, 512)
    tt = min(n, 256)
    h, hn, top_e, top_w, rank, cnt, tile_base = _router(
        mixed, x2, w_out[0].astype(BF16), g_ffn, w_router[0].T, b_router[0][:, None], n, tm, tt)

    bm = 512
    n_blocks = -(-(n * TOPK) // bm) + E
    counts = cnt[:, 0].astype(jnp.int32)
    padded = (counts + bm - 1) // bm * bm
    pend = jnp.cumsum(padded)
    pstart = pend - padded
    n_used = pend[-1] // bm
    blocks = jnp.arange(n_blocks, dtype=jnp.int32)
    block_e = jnp.minimum(jnp.sum(pend[None, :] <= (blocks * bm)[:, None], axis=1), E - 1).astype(jnp.int32)
    block_valid = jnp.clip(counts[block_e] - (blocks * bm - pstart[block_e]), 0, bm).astype(jnp.int32)
    block_valid = jnp.where(blocks < n_used, block_valid, 0)
    block_e = jnp.where(blocks < n_used, block_e, block_e[jnp.maximum(n_used - 1, 0)])

    nsub = tm // tt
    before = tile_base.reshape(n // tm, E, LANES)[:, :, :nsub].transpose(0, 2, 1).reshape(n // tt, E)
    before = before.astype(jnp.int32)
    after = jnp.concatenate([before[1:], counts[None, :]], axis=0)
    first = pstart[None, :] + before
    astart = first // PIECE * PIECE
    npieces = jnp.where(after > before, (first + (after - before) - astart + PIECE - 1) // PIECE, 0)
    cum = jnp.cumsum(npieces, axis=1)
    pbase = cum - npieces
    kcat = -(-(tt * TOPK + E * 2 * (PIECE - 1)) // (2 * LANES)) * (2 * LANES)
    shift = (astart - pbase * PIECE).astype(jnp.int32)
    q = jnp.arange(kcat // PIECE, dtype=jnp.int32)
    owner = jnp.sum(cum[:, None, :] <= q[None, :, None], axis=2)
    own = owner[:, :, None] == jnp.arange(E, dtype=jnp.int32)[None, None, :]
    piece_src = jnp.sum(jnp.where(own, shift[:, None, :], 0), axis=2) + q[None, :] * PIECE
    piece_src = jnp.where(owner < E, piece_src, 0).astype(jnp.int32)
    total = cum[:, -1].astype(jnp.int32)
    last = first + (after - before)
    starts_inside = ((npieces > 0) & (first % PIECE != 0)).astype(jnp.int32)
    ends_inside = ((npieces > 0) & (last % PIECE != 0)).astype(jnp.int32)
    held = jnp.sum(jnp.where(own & (q[None, :, None] == cum[:, None, :] - 1), ends_inside[:, None, :], 0), axis=2)
    piece_dst = jnp.where((owner < E) & (held == 0), piece_src, -1).astype(jnp.int32)
    written = (total - jnp.sum(ends_inside, axis=1)).astype(jnp.int32)
    tables = (piece_dst, total, written, (pbase * PIECE).astype(jnp.int32), npieces.astype(jnp.int32),
              starts_inside, ends_inside)

    _, slot = _dest(pstart, shift, top_e, rank, n, tt)
    wg, wu, wd, xg = _prep_dispatch(tables, counts, padded, pstart, slot, hn, w_gate_up[0], w_down[0], n,
                                    n_blocks * bm, tt, kcat)
    bgu = b_gate_up[0]
    y = _moe(block_e, n_used.reshape(1).astype(jnp.int32), block_valid, xg, wg, wu, wd,
             bgu[:, None, 0::2], bgu[:, None, 1::2], b_down[0][:, None, :], n_blocks, bm)
    out = _combine(piece_src, total, y, h, slot.T, top_w.T, g_final[None, :], n, tt, kcat)
    return out.reshape(b, s, D)
```

```python
import functools

import jax
import jax.numpy as jnp
from jax import lax
from jax.experimental import pallas as pl
from jax.experimental.pallas import tpu as pltpu

F32 = jnp.float32
BF16 = jnp.bfloat16
U32 = jnp.uint32

D = 2048
HQ, HKV, DH, WIN = 16, 2, 64, 128
G = HQ // HKV
MH, QR, KVR, DN, DR, DV = 8, 512, 256, 128, 64, 128
THETA = 10000.0
E, TOPK, DE = 32, 4, 2048
LIMIT, ALPHA = 7.0, 1.702
EPS = 1e-6
NEG = -1e30
LOG2E = 1.4426950408889634

LANES = 128
CHUNKS = D // LANES
PACK_ROWS = CHUNKS // 2
PIECE = 16
VMEM_LIMIT = 56 * 1024 * 1024

PROJ_COLS = 6400
COL_GATES, COL_Q, COL_DQ, COL_DKV, COL_K, COL_V, COL_KR = 0, 4096, 5120, 5632, 5888, 6016, 6144

NT = (((1,), (1,)), ((), ()))


def _params(sem, vmem=VMEM_LIMIT):
    return pltpu.CompilerParams(dimension_semantics=sem, vmem_limit_bytes=vmem)


def _rms(x, g):
    return x * lax.rsqrt(jnp.mean(x * x, axis=-1, keepdims=True) + EPS) * g


def _pack_pair(lo, hi):
    lo_bits = lax.bitcast_convert_type(lo.astype(BF16).astype(F32), U32)
    hi_bits = lax.bitcast_convert_type(hi.astype(BF16).astype(F32), U32)
    return (lo_bits >> 16) | hi_bits


def _unpack_pair(w):
    lo = lax.bitcast_convert_type(w << 16, F32)
    hi = lax.bitcast_convert_type(w & jnp.uint32(0xFFFF0000), F32)
    return lo, hi


def _pipelined(n, produce, consume):
    nxt = produce(0)
    for i in range(n):
        cur = nxt
        if i + 1 < n:
            nxt = produce(i + 1)
        consume(i, cur)


def _rope128(v, cos, sin):
    lane = lax.broadcasted_iota(jnp.int32, v.shape, 1)
    rot = jnp.where((lane % DH) < (DH // 2), -pltpu.roll(v, LANES - DH // 2, 1), pltpu.roll(v, DH // 2, 1))
    return v * cos + rot * sin


def _rope_table_kernel(pos_ref, inv_ref, cos_ref, sin_ref):
    ang = pos_ref[...].astype(F32) * inv_ref[...]
    cos_ref[...] = jnp.cos(ang)
    sin_ref[...] = jnp.sin(ang)


def _rope_tables(positions, n):
    half = DH // 2
    per_row = LANES // half
    inv = jnp.power(THETA, -jnp.arange(half, dtype=F32) * 2.0 / DH)
    inv_row = jnp.tile(inv, per_row)[None, :]
    pos_rows = jnp.repeat(positions.reshape(n // per_row, per_row), half, axis=1)
    rows = n // per_row
    tr = min(rows, 1024)
    cos, sin = pl.pallas_call(
        _rope_table_kernel,
        grid=(rows // tr,),
        in_specs=[pl.BlockSpec((tr, LANES), lambda i: (i, 0)), pl.BlockSpec((1, LANES), lambda i: (0, 0))],
        out_specs=[pl.BlockSpec((tr, LANES), lambda i: (i, 0))] * 2,
        out_shape=[jax.ShapeDtypeStruct((rows, LANES), F32)] * 2,
        name="rope_tables",
    )(pos_rows, inv_row)
    cos = jnp.tile(cos.reshape(n, half), (1, per_row))
    sin = jnp.tile(sin.reshape(n, half), (1, per_row))
    return cos, sin


def _in_proj_kernel(x_ref, g_ref, w_ref, o_ref, xn_ref):
    @pl.when(pl.program_id(1) == 0)
    def _():
        xn_ref[...] = _rms(x_ref[...], g_ref[...]).astype(BF16)

    o_ref[...] = jnp.dot(xn_ref[...], w_ref[...], preferred_element_type=F32).astype(o_ref.dtype)


def _in_proj(x2, g_attn, w_in_p, n):
    tm = min(n, 1024)
    tn = 1280
    return pl.pallas_call(
        _in_proj_kernel,
        grid=(n // tm, PROJ_COLS // tn),
        in_specs=[
            pl.BlockSpec((tm, D), lambda i, j: (i, 0)),
            pl.BlockSpec((1, D), lambda i, j: (0, 0)),
            pl.BlockSpec((D, tn), lambda i, j: (0, j)),
        ],
        out_specs=pl.BlockSpec((tm, tn), lambda i, j: (i, j)),
        out_shape=jax.ShapeDtypeStruct((n, PROJ_COLS), BF16),
        scratch_shapes=[pltpu.VMEM((tm, D), BF16)],
        compiler_params=_params(("arbitrary", "arbitrary")),
        name="in_proj",
    )(x2, g_attn, w_in_p)


def _prep_kernel(q_ref, dq_ref, dkv_ref, k_ref, kr_ref, cos_ref, sin_ref, gq_ref, gkv_ref,
                 wq_ref, wk_ref, wv_ref, qa_ref, ka_ref, qm_ref, km_ref, vm_ref):
    cos = cos_ref[...]
    sin = sin_ref[...]
    swa_scale = DH ** -0.5 * LOG2E
    for c in range(HQ * DH // LANES):
        sl = slice(c * LANES, (c + 1) * LANES)
        qa_ref[:, sl] = (_rope128(q_ref[:, sl].astype(F32), cos, sin) * swa_scale).astype(BF16)
    ka_ref[...] = _rope128(k_ref[...].astype(F32), cos, sin).astype(BF16)

    mla_scale = (DN + DR) ** -0.5 * LOG2E
    cq = _rms(dq_ref[...].astype(F32), gq_ref[...]).astype(BF16)
    qb = jnp.dot(cq, wq_ref[...], preferred_element_type=F32)
    ckv = _rms(dkv_ref[...].astype(F32), gkv_ref[...]).astype(BF16)
    kn = jnp.dot(ckv, wk_ref[...], preferred_element_type=F32)
    vv = jnp.dot(ckv, wv_ref[...], preferred_element_type=F32)
    ones_col = jnp.where(lax.broadcasted_iota(jnp.int32, (vv.shape[0], LANES), 1) == 0, 1.0, 0.0).astype(BF16)
    kr = _rope128(kr_ref[...].astype(F32), cos, sin).astype(BF16)
    for h in range(MH):
        lo = h * 2 * LANES
        qm_ref[:, lo:lo + LANES] = (qb[:, lo:lo + LANES] * mla_scale).astype(BF16)
        qm_ref[:, lo + LANES:lo + 2 * LANES] = (
            _rope128(qb[:, lo + LANES:lo + 2 * LANES], cos, sin) * mla_scale).astype(BF16)
        km_ref[:, lo:lo + LANES] = kn[:, h * LANES:(h + 1) * LANES].astype(BF16)
        km_ref[:, lo + LANES:lo + 2 * LANES] = kr
        vm_ref[:, lo:lo + LANES] = vv[:, h * LANES:(h + 1) * LANES].astype(BF16)
        vm_ref[:, lo + LANES:lo + 2 * LANES] = ones_col


def _prep(proj, cos, sin, g_q, g_kv, wq_p, wk_p, wv_p, n):
    tm = min(n, 512)
    row = lambda w, cb: pl.BlockSpec((tm, w), lambda i: (i, cb))
    full = lambda a: pl.BlockSpec(a.shape, lambda i: (0, 0))
    return pl.pallas_call(
        _prep_kernel,
        grid=(n // tm,),
        in_specs=[
            row(HQ * DH, COL_Q // (HQ * DH)), row(QR, COL_DQ // QR), row(KVR, COL_DKV // KVR),
            row(LANES, COL_K // LANES), row(LANES, COL_KR // LANES),
            row(LANES, 0), row(LANES, 0), full(g_q), full(g_kv), full(wq_p), full(wk_p), full(wv_p),
        ],
        out_specs=[row(HQ * DH, 0), row(LANES, 0)] + [row(MH * 2 * LANES, 0)] * 3,
        out_shape=[jax.ShapeDtypeStruct((n, HQ * DH), BF16), jax.ShapeDtypeStruct((n, LANES), BF16)]
        + [jax.ShapeDtypeStruct((n, MH * 2 * LANES), BF16)] * 3,
        compiler_params=_params(("arbitrary",)),
        name="rope_mla_prep",
    )(proj, proj, proj, proj, proj, cos, sin, g_q, g_kv, wq_p, wk_p, wv_p)


def _swa_kernel(sinks_ref, q_ref, kc_ref, kp_ref, vc_ref, vp_ref, o_ref):
    blk = pl.program_id(1)
    q2 = jnp.concatenate([q_ref[:, g * LANES:(g + 1) * LANES] for g in range(G)], axis=0)
    k2 = jnp.concatenate([kp_ref[...], kc_ref[...]], axis=0)
    v2t = jnp.concatenate([vp_ref[...], vc_ref[...]], axis=0).astype(F32).T
    lane = lax.broadcasted_iota(jnp.int32, (2 * WIN, LANES), 1)
    vrow = lax.broadcasted_iota(jnp.int32, (LANES, 2 * WIN), 0)
    kj = lax.broadcasted_iota(jnp.int32, (2 * WIN, WIN), 0)
    qi = lax.broadcasted_iota(jnp.int32, (2 * WIN, WIN), 1)
    valid = (kj > qi) & (kj <= qi + WIN) & ((kj >= WIN) | (blk > 0))
    outs = []

    def scores(h):
        kz = jnp.where((lane >= h * DH) & (lane < (h + 1) * DH), k2, jnp.zeros_like(k2))
        return lax.dot_general(kz, q2, NT, preferred_element_type=F32)

    def attend(h, s):
        vz = jnp.where((vrow >= h * DH) & (vrow < (h + 1) * DH), v2t, 0.0).astype(BF16)
        ps, invs = [], []
        for g in range(G):
            sg = jnp.where(valid, s[:, g * WIN:(g + 1) * WIN], NEG)
            sink = sinks_ref[h * G + g] * LOG2E
            m = jnp.maximum(jnp.max(sg, axis=0, keepdims=True), sink)
            p = jnp.exp2(sg - m)
            l = jnp.sum(p, axis=0, keepdims=True) + jnp.exp2(sink - m)
            ps.append(p.astype(BF16))
            invs.append(1.0 / l)
        o_h = jnp.dot(vz, jnp.concatenate(ps, axis=1), preferred_element_type=F32)
        outs.append(o_h * jnp.concatenate(invs, axis=1))

    _pipelined(HKV, scores, attend)
    acc = outs[0] + outs[1]
    for g in range(G):
        o_ref[:, g * LANES:(g + 1) * LANES] = acc[:, g * WIN:(g + 1) * WIN].T.astype(BF16)


def _swa(sinks, qa, ka, proj, b, s):
    nb = s // WIN
    cur = lambda cb: pl.BlockSpec((WIN, LANES), lambda bi, n, sk: (bi * nb + n, cb))
    prev = lambda cb: pl.BlockSpec((WIN, LANES), lambda bi, n, sk: (bi * nb + jnp.maximum(n - 1, 0), cb))
    grid_spec = pltpu.PrefetchScalarGridSpec(
        num_scalar_prefetch=1,
        grid=(b, nb),
        in_specs=[
            pl.BlockSpec((WIN, HQ * DH), lambda bi, n, sk: (bi * nb + n, 0)),
            cur(0), prev(0), cur(COL_V // LANES), prev(COL_V // LANES),
        ],
        out_specs=pl.BlockSpec((WIN, HQ * DH), lambda bi, n, sk: (bi * nb + n, 0)),
    )
    return pl.pallas_call(
        _swa_kernel,
        grid_spec=grid_spec,
        out_shape=jax.ShapeDtypeStruct((b * s, HQ * DH), BF16),
        compiler_params=_params(("arbitrary", "arbitrary")),
        name="swa_attention",
    )(sinks, qa, ka, ka, proj, proj)


def _mla_kernel(q_ref, k_ref, v_ref, o_ref, *, seq, tq):
    row = lax.broadcasted_iota(jnp.int32, (tq, tq), 0)
    col = lax.broadcasted_iota(jnp.int32, (tq, tq), 1)
    causal = col <= row
    nq = seq // tq

    def scores(i):
        kv = (i + 1) * tq
        q = q_ref[i * tq:(i + 1) * tq, :]
        s = lax.dot_general(q, k_ref[0:kv, :], NT, preferred_element_type=F32)
        diag = jnp.where(causal, s[:, kv - tq:kv], NEG)
        return diag if i == 0 else jnp.concatenate([s[:, 0:kv - tq], diag], axis=1)

    def finish(i, s):
        kv = (i + 1) * tq
        m = jnp.max(s, axis=-1, keepdims=True)
        p = jnp.exp2(s - m).astype(BF16)
        ov = jnp.dot(p, v_ref[0:kv, :], preferred_element_type=F32)
        o_ref[i * tq:(i + 1) * tq, :] = (ov[:, 0:DV] * (1.0 / ov[:, DV:DV + 1])).astype(BF16)

    _pipelined(nq, scores, finish)


def _mla(qm, km, vm, b, s):
    tq = min(s, 512)
    return pl.pallas_call(
        functools.partial(_mla_kernel, seq=s, tq=tq),
        grid=(b, MH),
        in_specs=[
            pl.BlockSpec((s, 2 * LANES), lambda bi, h: (bi, h)),
            pl.BlockSpec((s, 2 * LANES), lambda bi, h: (bi, h)),
            pl.BlockSpec((s, 2 * LANES), lambda bi, h: (bi, h)),
        ],
        out_specs=pl.BlockSpec((s, DV), lambda bi, h: (bi, h)),
        out_shape=jax.ShapeDtypeStruct((b * s, MH * DV), BF16),
        compiler_params=_params(("arbitrary", "arbitrary")),
        name="mla_attention",
    )(qm, km, vm)


def _mix_kernel(oa_ref, ob_ref, ga_ref, gb_ref, wa_ref, wb_ref, ba_ref, bb_ref, o_ref):
    ya = jnp.dot(oa_ref[...], wa_ref[...], preferred_element_type=F32)
    yb = jnp.dot(ob_ref[...], wb_ref[...], preferred_element_type=F32)
    ga = jax.nn.sigmoid(ga_ref[...].astype(F32) + ba_ref[...])
    gb = jax.nn.sigmoid(gb_ref[...].astype(F32) + bb_ref[...])
    o_ref[...] = (ga * ya + gb * yb).astype(BF16)


def _mix(out_a, out_b, proj, wa, wb, b_gate, n):
    tm = min(n, 1024)
    tn = 1024
    nj = D // tn
    return pl.pallas_call(
        _mix_kernel,
        grid=(n // tm, nj),
        in_specs=[
            pl.BlockSpec((tm, HQ * DH), lambda i, j: (i, 0)),
            pl.BlockSpec((tm, MH * DV), lambda i, j: (i, 0)),
            pl.BlockSpec((tm, tn), lambda i, j: (i, j)),
            pl.BlockSpec((tm, tn), lambda i, j: (i, nj + j)),
            pl.BlockSpec((HQ * DH, tn), lambda i, j: (0, j)),
            pl.BlockSpec((MH * DV, tn), lambda i, j: (0, j)),
            pl.BlockSpec((1, tn), lambda i, j: (0, j)),
            pl.BlockSpec((1, tn), lambda i, j: (0, nj + j)),
        ],
        out_specs=pl.BlockSpec((tm, tn), lambda i, j: (i, j)),
        out_shape=jax.ShapeDtypeStruct((n, D), BF16),
        compiler_params=_params(("arbitrary", "arbitrary")),
        name="gated_mix",
    )(out_a, out_b, proj, proj, wa, wb, b_gate, b_gate)


def _router_kernel(mx_ref, x_ref, wo_ref, gf_ref, wr_ref, br_ref,
                   h_ref, hn_ref, te_ref, tw_ref, rk_ref, cnt_ref, tb_ref, base_ref, *, tm, sub):
    @pl.when(pl.program_id(0) == 0)
    def _():
        base_ref[...] = jnp.zeros_like(base_ref)

    h = x_ref[...] + jnp.dot(mx_ref[...], wo_ref[...], preferred_element_type=F32)
    h_ref[...] = h
    hn = _rms(h, gf_ref[...])
    hn_ref[...] = hn.astype(BF16)

    logits = lax.dot_general(wr_ref[...], hn, NT, preferred_element_type=F32,
                             precision=lax.Precision.HIGHEST) + br_ref[...]
    eidx = lax.broadcasted_iota(jnp.int32, (E, tm), 0)
    vals, idxs = [], []
    l = logits
    for _ in range(TOPK):
        m = jnp.max(l, axis=0, keepdims=True)
        idx = jnp.min(jnp.where(l == m, eidx, E), axis=0, keepdims=True)
        vals.append(m)
        idxs.append(idx)
        l = jnp.where(eidx == idx, -jnp.inf, l)
    ex = [jnp.exp(v - vals[0]) for v in vals]
    tot = ex[0] + ex[1] + ex[2] + ex[3]
    tw_ref[...] = jnp.concatenate([e / tot for e in ex], axis=0)
    te_ref[...] = jnp.concatenate(idxs, axis=0)

    onehot = jnp.zeros((E, tm), F32)
    for idx in idxs:
        onehot = onehot + jnp.where(eidx == idx, 1.0, 0.0)
    r = lax.broadcasted_iota(jnp.int32, (tm, tm), 0)
    c = lax.broadcasted_iota(jnp.int32, (tm, tm), 1)
    earlier = jnp.where(r < c, 1.0, 0.0).astype(BF16)
    before = jnp.dot(onehot.astype(BF16), earlier, preferred_element_type=F32) + base_ref[:, 0:1]
    rk_ref[...] = jnp.concatenate(
        [jnp.sum(jnp.where(eidx == idx, before, 0.0), axis=0, keepdims=True) for idx in idxs],
        axis=0).astype(jnp.int32)
    lane = lax.broadcasted_iota(jnp.int32, (E, LANES), 1)
    tb = jnp.zeros((E, LANES), F32)
    for s in range(tm // sub):
        tb = jnp.where(lane == s, before[:, s * sub:s * sub + 1], tb)
    tb_ref[...] = tb
    base_ref[...] = base_ref[...] + jnp.sum(onehot, axis=1, keepdims=True)
    cnt_ref[...] = base_ref[...]


def _router(mixed, x2, wo, g_ffn, wr_t, b_router, n, tm, sub):
    row = lambda w: pl.BlockSpec((tm, w), lambda i: (i, 0))
    full = lambda a: pl.BlockSpec(a.shape, lambda i: (0, 0), pipeline_mode=pl.Buffered(1))
    tok = pl.BlockSpec((TOPK, tm), lambda i: (0, i))
    return pl.pallas_call(
        functools.partial(_router_kernel, tm=tm, sub=sub),
        grid=(n // tm,),
        in_specs=[row(D), row(D), full(wo), full(g_ffn), full(wr_t), full(b_router)],
        out_specs=[
            row(D), row(D), tok, tok, tok,
            pl.BlockSpec((E, LANES), lambda i: (0, 0)), pl.BlockSpec((E, LANES), lambda i: (i, 0)),
        ],
        out_shape=[
            jax.ShapeDtypeStruct((n, D), F32), jax.ShapeDtypeStruct((n, D), BF16),
            jax.ShapeDtypeStruct((TOPK, n), jnp.int32), jax.ShapeDtypeStruct((TOPK, n), F32),
            jax.ShapeDtypeStruct((TOPK, n), jnp.int32), jax.ShapeDtypeStruct((E, LANES), F32),
            jax.ShapeDtypeStruct((n // tm * E, LANES), F32),
        ],
        scratch_shapes=[pltpu.VMEM((E, LANES), F32)],
        compiler_params=_params(("arbitrary",)),
        name="outproj_router",
    )(mixed, x2, wo, g_ffn, wr_t, b_router)


def _dest_kernel(pstart_ref, shift_ref, te_ref, rk_ref, dest_ref, slot_ref):
    tile = pl.program_id(0)
    te = te_ref[...]
    d = rk_ref[...]
    s = rk_ref[...]
    for e in range(E):
        hit = te == e
        d = d + jnp.where(hit, pstart_ref[e], 0)
        s = s + jnp.where(hit, pstart_ref[e] - shift_ref[tile, e], 0)
    dest_ref[...] = d
    slot_ref[...] = s


def _dest(pstart, shift, top_e, rank, n, tt):
    tok = pl.BlockSpec((TOPK, tt), lambda i, ps, sh: (0, i))
    grid_spec = pltpu.PrefetchScalarGridSpec(
        num_scalar_prefetch=2, grid=(n // tt,), in_specs=[tok, tok], out_specs=[tok, tok])
    return pl.pallas_call(
        _dest_kernel, grid_spec=grid_spec,
        out_shape=[jax.ShapeDtypeStruct((TOPK, n), jnp.int32)] * 2, name="dest_rows",
    )(pstart, shift, top_e, rank)


def _prep_dispatch_kernel(pdst_ref, ntot_ref, nwr_ref, pb_ref, np_ref, ff_ref, lf_ref, cnt_ref, pad_ref, pst_ref,
                          slot_ref, hn_ref, wgu_ref, wdn_ref, g_ref, u_ref, d_ref, xg_ref,
                          stage_ref, carry_ref, sem, fsem, *, tt, n_disp, kcat):
    i = pl.program_id(0)

    def stage_piece(row):
        return stage_ref.at[pl.ds(pl.multiple_of(row, PIECE), PIECE), :]

    def out_piece(row):
        return xg_ref.at[pl.ds(pl.multiple_of(row, PIECE), PIECE), :]

    @pl.when(i == 0)
    def _():
        carry_ref[...] = jnp.zeros_like(carry_ref)

    @pl.when(i < n_disp)
    def _():
        slot = slot_ref[...]
        chunk = 512
        for r0 in range(0, kcat, chunk):
            row = lax.broadcasted_iota(jnp.int32, (chunk, tt), 0) + r0
            hit = row == slot[0:1, :]
            for k in range(1, TOPK):
                hit = hit | (row == slot[k:k + 1, :])
            sel = jnp.where(hit, 1.0, 0.0).astype(BF16)
            stage_ref[r0:r0 + chunk, :] = jnp.dot(sel, hn_ref[...], preferred_element_type=F32).astype(BF16)

        def per_expert(e, c):
            pieces = np_ref[i, e]

            @pl.when(pieces > 0)
            def _():
                first = pb_ref[i, e]
                last = first + (pieces - 1) * PIECE

                @pl.when(ff_ref[i, e] == 1)
                def _():
                    stage_piece(first)[...] = stage_piece(first)[...] + carry_ref[e]

                @pl.when(lf_ref[i, e] == 1)
                def _():
                    carry_ref[e] = stage_piece(last)[...]

            return c

        lax.fori_loop(0, E, per_expert, 0)

        def start(q, c):
            dst = pdst_ref[i, q]

            @pl.when(dst >= 0)
            def _():
                pltpu.make_async_copy(stage_piece(q * PIECE), out_piece(dst), sem).start()

            return c

        lax.fori_loop(0, ntot_ref[i], start, 0)

    chunk = 2 * LANES
    r = lax.broadcasted_iota(jnp.int32, (chunk, chunk), 0)
    c = lax.broadcasted_iota(jnp.int32, (chunk, chunk), 1)
    src = jnp.where(c < LANES, 2 * c, 2 * (c - LANES) + 1)
    perm = jnp.where(r == src, 1.0, 0.0).astype(BF16)
    for k in range(wgu_ref.shape[1] // chunk):
        w = wgu_ref[:, k * chunk:(k + 1) * chunk].astype(BF16)
        o = jnp.dot(w, perm, preferred_element_type=F32)
        g_ref[:, k * LANES:(k + 1) * LANES] = o[:, :LANES].astype(BF16)
        u_ref[:, k * LANES:(k + 1) * LANES] = o[:, LANES:].astype(BF16)
    d_ref[...] = wdn_ref[...].astype(BF16)

    @pl.when(i < n_disp)
    def _():
        def wait(q, c):
            pltpu.make_async_copy(stage_piece(0), out_piece(0), sem).wait()
            return c

        lax.fori_loop(0, nwr_ref[i], wait, 0)

    @pl.when(i == n_disp - 1)
    def _():
        stage_ref[0:PIECE, :] = jnp.zeros((PIECE, D), BF16)

        def per_expert(e, c):
            real = cnt_ref[e]
            tail = real % PIECE
            base = pst_ref[e]

            @pl.when(tail != 0)
            def _():
                cp = pltpu.make_async_copy(carry_ref.at[e], out_piece(base + real - tail), fsem)
                cp.start()
                cp.wait()

            zero_from = (real + PIECE - 1) // PIECE
            zero_to = pad_ref[e] // PIECE

            def zstart(z, c2):
                pltpu.make_async_copy(stage_piece(0), out_piece(base + z * PIECE), fsem).start()
                return c2

            def zwait(z, c2):
                pltpu.make_async_copy(stage_piece(0), out_piece(base + z * PIECE), fsem).wait()
                return c2

            lax.fori_loop(zero_from, zero_to, zstart, 0)
            lax.fori_loop(zero_from, zero_to, zwait, 0)
            return c

        lax.fori_loop(0, E, per_expert, 0)


def _prep_dispatch(tables, counts, padded, pstart, slot, hn, w_gate_up, w_down, n, p_rows, tt, kcat):
    tr = 512
    steps = E * (D // tr)
    n_disp = n // tt
    assert n_disp <= steps, "more token tiles than weight tiles"
    nsp = len(tables) + 3
    tile = lambda i, *_: (i // (D // tr), i % (D // tr), 0)
    tok = lambda i: jnp.minimum(i, n_disp - 1)
    wout = pl.BlockSpec((None, tr, DE), tile)
    grid_spec = pltpu.PrefetchScalarGridSpec(
        num_scalar_prefetch=nsp, grid=(steps,),
        in_specs=[
            pl.BlockSpec((TOPK, tt), lambda i, *_: (0, tok(i))),
            pl.BlockSpec((tt, D), lambda i, *_: (tok(i), 0)),
            pl.BlockSpec((None, tr, 2 * DE), tile),
            pl.BlockSpec((None, tr, D), tile),
        ],
        out_specs=[wout, wout, pl.BlockSpec((None, tr, D), tile), pl.BlockSpec(memory_space=pl.ANY)],
        scratch_shapes=[pltpu.VMEM((kcat, D), BF16), pltpu.VMEM((E, PIECE, D), BF16),
                        pltpu.SemaphoreType.DMA(()), pltpu.SemaphoreType.DMA(())],
    )
    return pl.pallas_call(
        functools.partial(_prep_dispatch_kernel, tt=tt, n_disp=n_disp, kcat=kcat), grid_spec=grid_spec,
        out_shape=[jax.ShapeDtypeStruct((E, D, DE), BF16), jax.ShapeDtypeStruct((E, D, DE), BF16),
                   jax.ShapeDtypeStruct((E, DE, D), BF16), jax.ShapeDtypeStruct((p_rows, D), BF16)],
        compiler_params=pltpu.CompilerParams(dimension_semantics=("arbitrary",), has_side_effects=True,
                                             vmem_limit_bytes=VMEM_LIMIT),
        name="weight_prep_dispatch",
    )(*tables, counts, padded, pstart, slot, hn, w_gate_up, w_down)


def _moe_kernel(be_ref, nu_ref, bv_ref, x_ref, wg_ref, wu_ref, wd_ref, bg_ref, bu_ref, bd_ref, y_ref,
                *, bm, sub, dsub):
    valid = bv_ref[pl.program_id(0)]

    def body(rows):
        x = x_ref[0:rows, :]
        acts = []

        def gate_up(s):
            sl = slice(s * sub, (s + 1) * sub)
            return (jnp.dot(x, wg_ref[:, sl], preferred_element_type=F32) + bg_ref[:, sl],
                    jnp.dot(x, wu_ref[:, sl], preferred_element_type=F32) + bu_ref[:, sl])

        def activate(s, gu):
            gate = jnp.minimum(gu[0], LIMIT)
            up = jnp.clip(gu[1], -LIMIT, LIMIT)
            acts.append((gate * jax.nn.sigmoid(ALPHA * gate) * (up + 1.0)).astype(BF16))

        _pipelined(DE // sub, gate_up, activate)
        act = jnp.concatenate(acts, axis=1)

        def down(c):
            return jnp.dot(act, wd_ref[:, c * dsub:(c + 1) * dsub], preferred_element_type=F32)

        def emit(c, part):
            sl = slice(c * dsub, (c + 1) * dsub)
            y_ref[0:rows, sl] = (part + bd_ref[:, sl]).astype(BF16)

        _pipelined(D // dsub, down, emit)
        if rows < bm:
            y_ref[rows:bm, :] = jnp.zeros((bm - rows, D), BF16)

    @pl.when(valid > bm // 2)
    def _():
        body(bm)

    @pl.when((valid > 0) & (valid <= bm // 2))
    def _():
        body(bm // 2)


def _moe(block_e, n_used, block_valid, xg, wg, wu, wd, bg, bu, bd, n_blocks, bm):
    def expert(blk, be, nu, bv):
        return (be[blk], 0, 0)

    def rows(blk, be, nu, bv):
        return (jnp.minimum(blk, nu[0] - 1), 0)

    single = lambda shape: pl.BlockSpec(shape, expert, pipeline_mode=pl.Buffered(1))
    double = lambda shape: pl.BlockSpec(shape, expert)
    grid_spec = pltpu.PrefetchScalarGridSpec(
        num_scalar_prefetch=3, grid=(n_blocks,),
        in_specs=[
            pl.BlockSpec((bm, D), rows),
            double((None, D, DE)), single((None, D, DE)), double((None, DE, D)),
            double((None, 1, DE)), double((None, 1, DE)), double((None, 1, D)),
        ],
        out_specs=pl.BlockSpec((bm, D), rows),
    )
    return pl.pallas_call(
        functools.partial(_moe_kernel, bm=bm, sub=256, dsub=512), grid_spec=grid_spec,
        out_shape=jax.ShapeDtypeStruct((n_blocks * bm, D), BF16),
        compiler_params=_params(("arbitrary",)),
        name="moe_experts",
    )(block_e, n_used, block_valid, xg, wg, wu, wd, bg, bu, bd)


def _combine_kernel(src_ref, tot_ref, y_ref, h_ref, slot_ref, tw_ref, gf_ref, o_ref, ycat_ref, sem,
                    *, tt, nt, kcat):
    i = pl.program_id(0)
    cur = i % 2

    def for_pieces(tile, s, fn):
        def body(q, c):
            fn(pltpu.make_async_copy(
                y_ref.at[pl.ds(pl.multiple_of(src_ref[tile, q], PIECE), PIECE), :],
                ycat_ref.at[s, pl.ds(pl.multiple_of(q * PIECE, PIECE), PIECE), :], sem.at[s]))
            return c

        lax.fori_loop(0, tot_ref[tile], body, 0)

    @pl.when(i == 0)
    def _():
        ycat_ref[...] = jnp.zeros_like(ycat_ref)
        for_pieces(0, 0, lambda cp: cp.start())

    @pl.when(i + 1 < nt)
    def _():
        for_pieces(i + 1, 1 - cur, lambda cp: cp.start())

    for_pieces(i, cur, lambda cp: cp.wait())

    def combine(k_rows):
        slot = slot_ref[...]
        tw = tw_ref[...]
        col = lax.broadcasted_iota(jnp.int32, (tt, k_rows), 1)
        sel = jnp.zeros((tt, k_rows), F32)
        for k in range(TOPK):
            sel = jnp.where(col == slot[:, k:k + 1], tw[:, k:k + 1], sel)
        moe = jnp.dot(sel.astype(BF16), ycat_ref[cur, 0:k_rows, :], preferred_element_type=F32)
        o_ref[...] = _rms(h_ref[...] + moe, gf_ref[...])

    k_short = kcat * 3 // 4
    used = tot_ref[i] * PIECE

    @pl.when(used <= k_short)
    def _():
        combine(k_short)

    @pl.when(used > k_short)
    def _():
        combine(kcat)


def _combine(src, total, y, h, slot, tw, g_final, n, tt, kcat):
    nt = n // tt
    grid_spec = pltpu.PrefetchScalarGridSpec(
        num_scalar_prefetch=2, grid=(nt,),
        in_specs=[
            pl.BlockSpec(memory_space=pl.ANY),
            pl.BlockSpec((tt, D), lambda i, a, b: (i, 0)),
            pl.BlockSpec((tt, TOPK), lambda i, a, b: (i, 0)),
            pl.BlockSpec((tt, TOPK), lambda i, a, b: (i, 0)),
            pl.BlockSpec((1, D), lambda i, a, b: (0, 0)),
        ],
        out_specs=pl.BlockSpec((tt, D), lambda i, a, b: (i, 0)),
        scratch_shapes=[pltpu.VMEM((2, kcat, D), BF16), pltpu.SemaphoreType.DMA((2,))],
    )
    return pl.pallas_call(
        functools.partial(_combine_kernel, tt=tt, nt=nt, kcat=kcat), grid_spec=grid_spec,
        out_shape=jax.ShapeDtypeStruct((n, D), F32),
        compiler_params=_params(("arbitrary",)),
        name="combine_final_norm",
    )(src, total, y, h, slot, tw, g_final)


def _permute_weights(w_in, w_uq, w_ukv, w_o_swa):
    widths = (HQ * DH, HKV * DH, HKV * DH, QR, KVR, DR, 2 * D)
    offs = [0]
    for w in widths:
        offs.append(offs[-1] + w)
    wq, wk, wv, wdq, wdkv, wkr, wgates = [w_in[:, offs[i]:offs[i + 1]] for i in range(7)]
    wq = wq.reshape(D, HKV, G, DH).transpose(0, 2, 1, 3).reshape(D, HQ * DH)
    pad = jnp.zeros((D, PROJ_COLS - COL_KR - DR), w_in.dtype)
    w_in_p = jnp.concatenate([wgates, wq, wdq, wdkv, wk, wv, wkr, pad], axis=1).astype(BF16)
    wo_a = w_o_swa.reshape(HKV, G, DH, D).transpose(1, 0, 2, 3).reshape(HQ * DH, D).astype(BF16)
    wuq = w_uq.reshape(QR, MH, DN + DR)
    wq_p = jnp.concatenate([wuq, jnp.zeros((QR, MH, 2 * LANES - DN - DR), w_uq.dtype)], axis=2)
    wq_p = wq_p.reshape(QR, MH * 2 * LANES).astype(BF16)
    wukv = w_ukv.reshape(KVR, MH, DN + DV)
    wk_p = wukv[:, :, :DN].reshape(KVR, MH * DN).astype(BF16)
    wv_p = wukv[:, :, DN:].reshape(KVR, MH * DV).astype(BF16)
    return w_in_p, wo_a, wq_p, wk_p, wv_p


def kernel(x, positions, g_attn, w_in, b_gate, sinks, g_q, w_uq, g_kv, w_ukv, w_o_swa, w_o_mla, w_out,
           g_ffn, w_router, b_router, w_gate_up, b_gate_up, w_down, b_down, g_final):
    b, s, _ = x.shape
    n = b * s
    assert w_in.shape[0] == 1, "single-layer block"
    x2 = x.reshape(n, D)

    w_in_p, wo_a, wq_p, wk_p, wv_p = _permute_weights(w_in[0], w_uq[0], w_ukv[0], w_o_swa[0])
    cos, sin = _rope_tables(positions, n)
    proj = _in_proj(x2, g_attn, w_in_p, n)
    qa, ka, qm, km, vm = _prep(proj, cos, sin, g_q, g_kv, wq_p, wk_p, wv_p, n)
    out_a = _swa(sinks[0], qa, ka, proj, b, s)
    out_b = _mla(qm, km, vm, b, s)

    mixed = _mix(out_a, out_b, proj, wo_a, w_o_mla[0].astype(BF16), b_gate, n)
    tm = min(n, 512)
    tt = min(n, 256)
    h, hn, top_e, top_w, rank, cnt, tile_base = _router(
        mixed, x2, w_out[0].astype(BF16), g_ffn, w_router[0].T, b_router[0][:, None], n, tm, tt)

    bm = 512
    n_blocks = -(-(n * TOPK) // bm) + E
    counts = cnt[:, 0].astype(jnp.int32)
    padded = (counts + bm - 1) // bm * bm
    pend = jnp.cumsum(padded)
    pstart = pend - padded
    n_used = pend[-1] // bm
    blocks = jnp.arange(n_blocks, dtype=jnp.int32)
    block_e = jnp.minimum(jnp.sum(pend[None, :] <= (blocks * bm)[:, None], axis=1), E - 1).astype(jnp.int32)
    block_valid = jnp.clip(counts[block_e] - (blocks * bm - pstart[block_e]), 0, bm).astype(jnp.int32)
    block_valid = jnp.where(blocks < n_used, block_valid, 0)
    block_e = jnp.where(blocks < n_used, block_e, block_e[jnp.maximum(n_used - 1, 0)])

    nsub = tm // tt
    before = tile_base.reshape(n // tm, E, LANES)[:, :, :nsub].transpose(0, 2, 1).reshape(n // tt, E)
    before = before.astype(jnp.int32)
    after = jnp.concatenate([before[1:], counts[None, :]], axis=0)
    first = pstart[None, :] + before
    astart = first // PIECE * PIECE
    npieces = jnp.where(after > before, (first + (after - before) - astart + PIECE - 1) // PIECE, 0)
    cum = jnp.cumsum(npieces, axis=1)
    pbase = cum - npieces
    kcat = -(-(tt * TOPK + E * 2 * (PIECE - 1)) // (2 * LANES)) * (2 * LANES)
    shift = (astart - pbase * PIECE).astype(jnp.int32)
    q = jnp.arange(kcat // PIECE, dtype=jnp.int32)
    owner = jnp.sum(cum[:, None, :] <= q[None, :, None], axis=2)
    own = owner[:, :, None] == jnp.arange(E, dtype=jnp.int32)[None, None, :]
    piece_src = jnp.sum(jnp.where(own, shift[:, None, :], 0), axis=2) + q[None, :] * PIECE
    piece_src = jnp.where(owner < E, piece_src, 0).astype(jnp.int32)
    total = cum[:, -1].astype(jnp.int32)
    last = first + (after - before)
    starts_inside = ((npieces > 0) & (first % PIECE != 0)).astype(jnp.int32)
    ends_inside = ((npieces > 0) & (last % PIECE != 0)).astype(jnp.int32)
    held = jnp.sum(jnp.where(own & (q[None, :, None] == cum[:, None, :] - 1), ends_inside[:, None, :], 0), axis=2)
    piece_dst = jnp.where((owner < E) & (held == 0), piece_src, -1).astype(jnp.int32)
    written = (total - jnp.sum(ends_inside, axis=1)).astype(jnp.int32)
    tables = (piece_dst, total, written, (pbase * PIECE).astype(jnp.int32), npieces.astype(jnp.int32),
              starts_inside, ends_inside)

    _, slot = _dest(pstart, shift, top_e, rank, n, tt)
    wg, wu, wd, xg = _prep_dispatch(tables, counts, padded, pstart, slot, hn, w_gate_up[0], w_down[0], n,
                                    n_blocks * bm, tt, kcat)
    bgu = b_gate_up[0]
    y = _moe(block_e, n_used.reshape(1).astype(jnp.int32), block_valid, xg, wg, wu, wd,
             bgu[:, None, 0::2], bgu[:, None, 1::2], b_down[0][:, None, :], n_blocks, bm)
    out = _combine(piece_src, total, y, h, slot.T, top_w.T, g_final[None, :], n, tt, kcat)
    return out.reshape(b, s, D)
```

```python
import functools

import jax
import jax.numpy as jnp
from jax import lax
from jax.experimental import pallas as pl
from jax.experimental.pallas import tpu as pltpu

F32 = jnp.float32
BF16 = jnp.bfloat16
U32 = jnp.uint32

D = 2048
HQ, HKV, DH, WIN = 16, 2, 64, 128
G = HQ // HKV
MH, QR, KVR, DN, DR, DV = 8, 512, 256, 128, 64, 128
THETA = 10000.0
E, TOPK, DE = 32, 4, 2048
LIMIT, ALPHA = 7.0, 1.702
EPS = 1e-6
NEG = -1e30
LOG2E = 1.4426950408889634

LANES = 128
PIECE = 16
VMEM_LIMIT = 56 * 1024 * 1024

PROJ_COLS = 6400
COL_GATES, COL_Q, COL_DQ, COL_DKV, COL_K, COL_V, COL_KR = 0, 4096, 5120, 5632, 5888, 6016, 6144

NT = (((1,), (1,)), ((), ()))


def _params(sem, vmem=VMEM_LIMIT):
    return pltpu.CompilerParams(dimension_semantics=sem, vmem_limit_bytes=vmem)


def _rms(x, g):
    return x * lax.rsqrt(jnp.mean(x * x, axis=-1, keepdims=True) + EPS) * g


def _bf16_part(x):
    bits = lax.bitcast_convert_type(x, U32) & jnp.uint32(0xFFFF0000)
    return lax.bitcast_convert_type(bits, F32)


def _pipelined(n, produce, consume):
    nxt = produce(0)
    for i in range(n):
        cur = nxt
        if i + 1 < n:
            nxt = produce(i + 1)
        consume(i, cur)


def _rope128(v, cos, sin):
    lane = lax.broadcasted_iota(jnp.int32, v.shape, 1)
    rot = jnp.where((lane % DH) < (DH // 2), -pltpu.roll(v, LANES - DH // 2, 1), pltpu.roll(v, DH // 2, 1))
    return v * cos + rot * sin


def _rope_table_kernel(pos_ref, inv_ref, cos_ref, sin_ref):
    ang = pos_ref[...].astype(F32) * inv_ref[...]
    cos_ref[...] = jnp.cos(ang)
    sin_ref[...] = jnp.sin(ang)


def _rope_tables(positions, n):
    half = DH // 2
    per_row = LANES // half
    inv = jnp.power(THETA, -jnp.arange(half, dtype=F32) * 2.0 / DH)
    inv_row = jnp.tile(inv, per_row)[None, :]
    pos_rows = jnp.repeat(positions.reshape(n // per_row, per_row), half, axis=1)
    rows = n // per_row
    tr = min(rows, 1024)
    cos, sin = pl.pallas_call(
        _rope_table_kernel,
        grid=(rows // tr,),
        in_specs=[pl.BlockSpec((tr, LANES), lambda i: (i, 0)), pl.BlockSpec((1, LANES), lambda i: (0, 0))],
        out_specs=[pl.BlockSpec((tr, LANES), lambda i: (i, 0))] * 2,
        out_shape=[jax.ShapeDtypeStruct((rows, LANES), F32)] * 2,
        name="rope_tables",
    )(pos_rows, inv_row)
    cos = jnp.tile(cos.reshape(n, half), (1, per_row))
    sin = jnp.tile(sin.reshape(n, half), (1, per_row))
    return cos, sin


def _in_proj_kernel(x_ref, g_ref, w_ref, o_ref, xn_ref):
    @pl.when(pl.program_id(1) == 0)
    def _():
        xn_ref[...] = _rms(x_ref[...], g_ref[...]).astype(BF16)

    o_ref[...] = jnp.dot(xn_ref[...], w_ref[...], preferred_element_type=F32).astype(o_ref.dtype)


def _in_proj(x2, g_attn, w_in_p, n):
    tm = min(n, 1024)
    tn = 1280
    return pl.pallas_call(
        _in_proj_kernel,
        grid=(n // tm, PROJ_COLS // tn),
        in_specs=[
            pl.BlockSpec((tm, D), lambda i, j: (i, 0)),
            pl.BlockSpec((1, D), lambda i, j: (0, 0)),
            pl.BlockSpec((D, tn), lambda i, j: (0, j)),
        ],
        out_specs=pl.BlockSpec((tm, tn), lambda i, j: (i, j)),
        out_shape=jax.ShapeDtypeStruct((n, PROJ_COLS), BF16),
        scratch_shapes=[pltpu.VMEM((tm, D), BF16)],
        compiler_params=_params(("arbitrary", "arbitrary")),
        name="in_proj",
    )(x2, g_attn, w_in_p)


def _prep_kernel(q_ref, dq_ref, dkv_ref, k_ref, kr_ref, cos_ref, sin_ref, gq_ref, gkv_ref,
                 wq_ref, wk_ref, wv_ref, qa_ref, ka_ref, qm_ref, km_ref, vm_ref):
    cos = cos_ref[...]
    sin = sin_ref[...]
    swa_scale = DH ** -0.5 * LOG2E
    for c in range(HQ * DH // LANES):
        sl = slice(c * LANES, (c + 1) * LANES)
        qa_ref[:, sl] = (_rope128(q_ref[:, sl].astype(F32), cos, sin) * swa_scale).astype(BF16)
    ka_ref[...] = _rope128(k_ref[...].astype(F32), cos, sin).astype(BF16)

    mla_scale = (DN + DR) ** -0.5 * LOG2E
    cq = _rms(dq_ref[...].astype(F32), gq_ref[...]).astype(BF16)
    qb = jnp.dot(cq, wq_ref[...], preferred_element_type=F32)
    ckv = _rms(dkv_ref[...].astype(F32), gkv_ref[...]).astype(BF16)
    kn = jnp.dot(ckv, wk_ref[...], preferred_element_type=F32)
    vv = jnp.dot(ckv, wv_ref[...], preferred_element_type=F32)
    ones_col = jnp.where(lax.broadcasted_iota(jnp.int32, (vv.shape[0], LANES), 1) == 0, 1.0, 0.0).astype(BF16)
    kr = _rope128(kr_ref[...].astype(F32), cos, sin).astype(BF16)
    for h in range(MH):
        lo = h * 2 * LANES
        qm_ref[:, lo:lo + LANES] = (qb[:, lo:lo + LANES] * mla_scale).astype(BF16)
        qm_ref[:, lo + LANES:lo + 2 * LANES] = (
            _rope128(qb[:, lo + LANES:lo + 2 * LANES], cos, sin) * mla_scale).astype(BF16)
        km_ref[:, lo:lo + LANES] = kn[:, h * LANES:(h + 1) * LANES].astype(BF16)
        km_ref[:, lo + LANES:lo + 2 * LANES] = kr
        vm_ref[:, lo:lo + LANES] = vv[:, h * LANES:(h + 1) * LANES].astype(BF16)
        vm_ref[:, lo + LANES:lo + 2 * LANES] = ones_col


def _prep(proj, cos, sin, g_q, g_kv, wq_p, wk_p, wv_p, n):
    tm = min(n, 512)
    row = lambda w, cb: pl.BlockSpec((tm, w), lambda i: (i, cb))
    full = lambda a: pl.BlockSpec(a.shape, lambda i: (0, 0))
    return pl.pallas_call(
        _prep_kernel,
        grid=(n // tm,),
        in_specs=[
            row(HQ * DH, COL_Q // (HQ * DH)), row(QR, COL_DQ // QR), row(KVR, COL_DKV // KVR),
            row(LANES, COL_K // LANES), row(LANES, COL_KR // LANES),
            row(LANES, 0), row(LANES, 0), full(g_q), full(g_kv), full(wq_p), full(wk_p), full(wv_p),
        ],
        out_specs=[row(HQ * DH, 0), row(LANES, 0)] + [row(MH * 2 * LANES, 0)] * 3,
        out_shape=[jax.ShapeDtypeStruct((n, HQ * DH), BF16), jax.ShapeDtypeStruct((n, LANES), BF16)]
        + [jax.ShapeDtypeStruct((n, MH * 2 * LANES), BF16)] * 3,
        compiler_params=_params(("arbitrary",)),
        name="rope_mla_prep",
    )(proj, proj, proj, proj, proj, cos, sin, g_q, g_kv, wq_p, wk_p, wv_p)


def _swa_kernel(sinks_ref, q_ref, kc_ref, kp_ref, vc_ref, vp_ref, o_ref):
    blk = pl.program_id(1)
    q2 = jnp.concatenate([q_ref[:, g * LANES:(g + 1) * LANES] for g in range(G)], axis=0)
    k2 = jnp.concatenate([kp_ref[...], kc_ref[...]], axis=0)
    v2t = jnp.concatenate([vp_ref[...], vc_ref[...]], axis=0).astype(F32).T
    lane = lax.broadcasted_iota(jnp.int32, (2 * WIN, LANES), 1)
    vrow = lax.broadcasted_iota(jnp.int32, (LANES, 2 * WIN), 0)
    kj = lax.broadcasted_iota(jnp.int32, (2 * WIN, WIN), 0)
    qi = lax.broadcasted_iota(jnp.int32, (2 * WIN, WIN), 1)
    valid = (kj > qi) & (kj <= qi + WIN) & ((kj >= WIN) | (blk > 0))
    outs = []

    def scores(h):
        kz = jnp.where((lane >= h * DH) & (lane < (h + 1) * DH), k2, jnp.zeros_like(k2))
        return lax.dot_general(kz, q2, NT, preferred_element_type=F32)

    def attend(h, s):
        vz = jnp.where((vrow >= h * DH) & (vrow < (h + 1) * DH), v2t, 0.0).astype(BF16)
        ps, invs = [], []
        for g in range(G):
            sg = jnp.where(valid, s[:, g * WIN:(g + 1) * WIN], NEG)
            sink = sinks_ref[h * G + g] * LOG2E
            m = jnp.maximum(jnp.max(sg, axis=0, keepdims=True), sink)
            p = jnp.exp2(sg - m)
            l = jnp.sum(p, axis=0, keepdims=True) + jnp.exp2(sink - m)
            ps.append(p.astype(BF16))
            invs.append(1.0 / l)
        o_h = jnp.dot(vz, jnp.concatenate(ps, axis=1), preferred_element_type=F32)
        outs.append(o_h * jnp.concatenate(invs, axis=1))

    _pipelined(HKV, scores, attend)
    acc = outs[0] + outs[1]
    for g in range(G):
        o_ref[:, g * LANES:(g + 1) * LANES] = acc[:, g * WIN:(g + 1) * WIN].T.astype(BF16)


def _swa(sinks, qa, ka, proj, b, s):
    nb = s // WIN
    cur = lambda cb: pl.BlockSpec((WIN, LANES), lambda bi, n, sk: (bi * nb + n, cb))
    prev = lambda cb: pl.BlockSpec((WIN, LANES), lambda bi, n, sk: (bi * nb + jnp.maximum(n - 1, 0), cb))
    grid_spec = pltpu.PrefetchScalarGridSpec(
        num_scalar_prefetch=1,
        grid=(b, nb),
        in_specs=[
            pl.BlockSpec((WIN, HQ * DH), lambda bi, n, sk: (bi * nb + n, 0)),
            cur(0), prev(0), cur(COL_V // LANES), prev(COL_V // LANES),
        ],
        out_specs=pl.BlockSpec((WIN, HQ * DH), lambda bi, n, sk: (bi * nb + n, 0)),
    )
    return pl.pallas_call(
        _swa_kernel,
        grid_spec=grid_spec,
        out_shape=jax.ShapeDtypeStruct((b * s, HQ * DH), BF16),
        compiler_params=_params(("arbitrary", "arbitrary")),
        name="swa_attention",
    )(sinks, qa, ka, ka, proj, proj)


def _mla_kernel(q_ref, k_ref, v_ref, o_ref, *, seq, tq):
    row = lax.broadcasted_iota(jnp.int32, (tq, tq), 0)
    col = lax.broadcasted_iota(jnp.int32, (tq, tq), 1)
    causal = col <= row
    nq = seq // tq

    def scores(i):
        kv = (i + 1) * tq
        q = q_ref[i * tq:(i + 1) * tq, :]
        s = lax.dot_general(q, k_ref[0:kv, :], NT, preferred_element_type=F32)
        diag = jnp.where(causal, s[:, kv - tq:kv], NEG)
        return diag if i == 0 else jnp.concatenate([s[:, 0:kv - tq], diag], axis=1)

    def finish(i, s):
        kv = (i + 1) * tq
        m = jnp.max(s, axis=-1, keepdims=True)
        p = jnp.exp2(s - m).astype(BF16)
        ov = jnp.dot(p, v_ref[0:kv, :], preferred_element_type=F32)
        o_ref[i * tq:(i + 1) * tq, :] = (ov[:, 0:DV] * (1.0 / ov[:, DV:DV + 1])).astype(BF16)

    _pipelined(nq, scores, finish)


def _mla(qm, km, vm, b, s):
    tq = min(s, 512)
    return pl.pallas_call(
        functools.partial(_mla_kernel, seq=s, tq=tq),
        grid=(b, MH),
        in_specs=[
            pl.BlockSpec((s, 2 * LANES), lambda bi, h: (bi, h)),
            pl.BlockSpec((s, 2 * LANES), lambda bi, h: (bi, h)),
            pl.BlockSpec((s, 2 * LANES), lambda bi, h: (bi, h)),
        ],
        out_specs=pl.BlockSpec((s, DV), lambda bi, h: (bi, h)),
        out_shape=jax.ShapeDtypeStruct((b * s, MH * DV), BF16),
        compiler_params=_params(("arbitrary", "arbitrary")),
        name="mla_attention",
    )(qm, km, vm)


def _mix_kernel(oa_ref, ob_ref, ga_ref, gb_ref, wa_ref, wb_ref, ba_ref, bb_ref, o_ref):
    ya = jnp.dot(oa_ref[...], wa_ref[...], preferred_element_type=F32)
    yb = jnp.dot(ob_ref[...], wb_ref[...], preferred_element_type=F32)
    ga = jax.nn.sigmoid(ga_ref[...].astype(F32) + ba_ref[...])
    gb = jax.nn.sigmoid(gb_ref[...].astype(F32) + bb_ref[...])
    o_ref[...] = (ga * ya + gb * yb).astype(BF16)


def _mix(out_a, out_b, proj, wa, wb, b_gate, n):
    tm = min(n, 1024)
    tn = 1024
    nj = D // tn
    return pl.pallas_call(
        _mix_kernel,
        grid=(n // tm, nj),
        in_specs=[
            pl.BlockSpec((tm, HQ * DH), lambda i, j: (i, 0)),
            pl.BlockSpec((tm, MH * DV), lambda i, j: (i, 0)),
            pl.BlockSpec((tm, tn), lambda i, j: (i, j)),
            pl.BlockSpec((tm, tn), lambda i, j: (i, nj + j)),
            pl.BlockSpec((HQ * DH, tn), lambda i, j: (0, j)),
            pl.BlockSpec((MH * DV, tn), lambda i, j: (0, j)),
            pl.BlockSpec((1, tn), lambda i, j: (0, j)),
            pl.BlockSpec((1, tn), lambda i, j: (0, nj + j)),
        ],
        out_specs=pl.BlockSpec((tm, tn), lambda i, j: (i, j)),
        out_shape=jax.ShapeDtypeStruct((n, D), BF16),
        compiler_params=_params(("arbitrary", "arbitrary")),
        name="gated_mix",
    )(out_a, out_b, proj, proj, wa, wb, b_gate, b_gate)


def _router_kernel(mx_ref, x_ref, wo_ref, gf_ref, wr_ref, br_ref,
                   h_ref, hn_ref, te_ref, tw_ref, rk_ref, cnt_ref, tb_ref, base_ref, *, tm, sub):
    @pl.when(pl.program_id(0) == 0)
    def _():
        base_ref[...] = jnp.zeros_like(base_ref)

    h = x_ref[...] + jnp.dot(mx_ref[...], wo_ref[...], preferred_element_type=F32)
    h_ref[...] = h
    hn = _rms(h, gf_ref[...])
    hn_ref[...] = hn.astype(BF16)

    hn_top = _bf16_part(hn)
    hn_hi = hn_top.astype(BF16)
    hn_lo = (hn - hn_top).astype(BF16)
    logits = (lax.dot_general(wr_ref[0], hn_hi, NT, preferred_element_type=F32)
              + lax.dot_general(wr_ref[0], hn_lo, NT, preferred_element_type=F32)
              + lax.dot_general(wr_ref[1], hn_hi, NT, preferred_element_type=F32)) + br_ref[...]
    eidx = lax.broadcasted_iota(jnp.int32, (E, tm), 0)
    vals, idxs = [], []
    l = logits
    for _ in range(TOPK):
        m = jnp.max(l, axis=0, keepdims=True)
        idx = jnp.min(jnp.where(l == m, eidx, E), axis=0, keepdims=True)
        vals.append(m)
        idxs.append(idx)
        l = jnp.where(eidx == idx, -jnp.inf, l)
    ex = [jnp.exp(v - vals[0]) for v in vals]
    tot = ex[0] + ex[1] + ex[2] + ex[3]
    tw_ref[...] = jnp.concatenate([e / tot for e in ex], axis=0)
    te_ref[...] = jnp.concatenate(idxs, axis=0)

    onehot = jnp.zeros((E, tm), F32)
    for idx in idxs:
        onehot = onehot + jnp.where(eidx == idx, 1.0, 0.0)
    r = lax.broadcasted_iota(jnp.int32, (tm, tm), 0)
    c = lax.broadcasted_iota(jnp.int32, (tm, tm), 1)
    earlier = jnp.where(r < c, 1.0, 0.0).astype(BF16)
    before = jnp.dot(onehot.astype(BF16), earlier, preferred_element_type=F32) + base_ref[:, 0:1]
    rk_ref[...] = jnp.concatenate(
        [jnp.sum(jnp.where(eidx == idx, before, 0.0), axis=0, keepdims=True) for idx in idxs],
        axis=0).astype(jnp.int32)
    lane = lax.broadcasted_iota(jnp.int32, (E, LANES), 1)
    tb = jnp.zeros((E, LANES), F32)
    for s in range(tm // sub):
        tb = jnp.where(lane == s, before[:, s * sub:s * sub + 1], tb)
    tb_ref[...] = tb
    base_ref[...] = base_ref[...] + jnp.sum(onehot, axis=1, keepdims=True)
    cnt_ref[...] = base_ref[...]


def _router(mixed, x2, wo, g_ffn, wr_t, b_router, n, tm, sub):
    row = lambda w: pl.BlockSpec((tm, w), lambda i: (i, 0))
    full = lambda a: pl.BlockSpec(a.shape, lambda i: (0,) * a.ndim, pipeline_mode=pl.Buffered(1))
    tok = pl.BlockSpec((TOPK, tm), lambda i: (0, i))
    return pl.pallas_call(
        functools.partial(_router_kernel, tm=tm, sub=sub),
        grid=(n // tm,),
        in_specs=[row(D), row(D), full(wo), full(g_ffn), full(wr_t), full(b_router)],
        out_specs=[
            row(D), row(D), tok, tok, tok,
            pl.BlockSpec((E, LANES), lambda i: (0, 0)), pl.BlockSpec((E, LANES), lambda i: (i, 0)),
        ],
        out_shape=[
            jax.ShapeDtypeStruct((n, D), F32), jax.ShapeDtypeStruct((n, D), BF16),
            jax.ShapeDtypeStruct((TOPK, n), jnp.int32), jax.ShapeDtypeStruct((TOPK, n), F32),
            jax.ShapeDtypeStruct((TOPK, n), jnp.int32), jax.ShapeDtypeStruct((E, LANES), F32),
            jax.ShapeDtypeStruct((n // tm * E, LANES), F32),
        ],
        scratch_shapes=[pltpu.VMEM((E, LANES), F32)],
        compiler_params=_params(("arbitrary",)),
        name="outproj_router",
    )(mixed, x2, wo, g_ffn, wr_t, b_router)


def _slot_kernel(pstart_ref, shift_ref, te_ref, rk_ref, slot_ref, *, tt, group):
    for g in range(group):
        tile = pl.program_id(0) * group + g
        te = te_ref[:, g * tt:(g + 1) * tt]
        s = rk_ref[:, g * tt:(g + 1) * tt]
        for e in range(E):
            s = s + jnp.where(te == e, pstart_ref[e] - shift_ref[tile, e], 0)
        slot_ref[:, g * tt:(g + 1) * tt] = s


def _slots(pstart, shift, top_e, rank, n, tt):
    group = min(8, n // tt)
    tok = pl.BlockSpec((TOPK, group * tt), lambda i, ps, sh: (0, i))
    grid_spec = pltpu.PrefetchScalarGridSpec(
        num_scalar_prefetch=2, grid=(n // (group * tt),), in_specs=[tok, tok], out_specs=tok)
    return pl.pallas_call(
        functools.partial(_slot_kernel, tt=tt, group=group), grid_spec=grid_spec,
        out_shape=jax.ShapeDtypeStruct((TOPK, n), jnp.int32), name="staging_slots",
    )(pstart, shift, top_e, rank)


def _prep_dispatch_kernel(pdst_ref, ntot_ref, nwr_ref, pb_ref, np_ref, ff_ref, lf_ref, cnt_ref, pad_ref, pst_ref,
                          slot_ref, hn_ref, wgu_ref, wdn_ref, g_ref, u_ref, d_ref, xg_ref,
                          stage_ref, carry_ref, sem, fsem, *, tt, n_disp, kcat):
    i = pl.program_id(0)

    def stage_piece(row):
        return stage_ref.at[pl.ds(pl.multiple_of(row, PIECE), PIECE), :]

    def out_piece(row):
        return xg_ref.at[pl.ds(pl.multiple_of(row, PIECE), PIECE), :]

    @pl.when(i == 0)
    def _():
        carry_ref[...] = jnp.zeros_like(carry_ref)

    @pl.when(i < n_disp)
    def _():
        slot = slot_ref[...]
        chunk = 512
        for r0 in range(0, kcat, chunk):
            row = lax.broadcasted_iota(jnp.int32, (chunk, tt), 0) + r0
            hit = row == slot[0:1, :]
            for k in range(1, TOPK):
                hit = hit | (row == slot[k:k + 1, :])
            sel = jnp.where(hit, 1.0, 0.0).astype(BF16)
            stage_ref[r0:r0 + chunk, :] = jnp.dot(sel, hn_ref[...], preferred_element_type=F32).astype(BF16)

        def per_expert(e, c):
            pieces = np_ref[i, e]

            @pl.when(pieces > 0)
            def _():
                first = pb_ref[i, e]
                last = first + (pieces - 1) * PIECE

                @pl.when(ff_ref[i, e] == 1)
                def _():
                    stage_piece(first)[...] = stage_piece(first)[...] + carry_ref[e]

                @pl.when(lf_ref[i, e] == 1)
                def _():
                    carry_ref[e] = stage_piece(last)[...]

            return c

        lax.fori_loop(0, E, per_expert, 0)

        def start(q, c):
            dst = pdst_ref[i, q]

            @pl.when(dst >= 0)
            def _():
                pltpu.make_async_copy(stage_piece(q * PIECE), out_piece(dst), sem).start()

            return c

        lax.fori_loop(0, ntot_ref[i], start, 0)

    chunk = 2 * LANES
    r = lax.broadcasted_iota(jnp.int32, (chunk, chunk), 0)
    c = lax.broadcasted_iota(jnp.int32, (chunk, chunk), 1)
    src = jnp.where(c < LANES, 2 * c, 2 * (c - LANES) + 1)
    perm = jnp.where(r == src, 1.0, 0.0).astype(BF16)
    for k in range(wgu_ref.shape[1] // chunk):
        w = wgu_ref[:, k * chunk:(k + 1) * chunk].astype(BF16)
        o = jnp.dot(w, perm, preferred_element_type=F32)
        g_ref[:, k * LANES:(k + 1) * LANES] = o[:, :LANES].astype(BF16)
        u_ref[:, k * LANES:(k + 1) * LANES] = o[:, LANES:].astype(BF16)
    d_ref[...] = wdn_ref[...].astype(BF16)

    @pl.when(i < n_disp)
    def _():
        def wait(q, c):
            pltpu.make_async_copy(stage_piece(0), out_piece(0), sem).wait()
            return c

        lax.fori_loop(0, nwr_ref[i], wait, 0)

    @pl.when(i == n_disp - 1)
    def _():
        stage_ref[0:PIECE, :] = jnp.zeros((PIECE, D), BF16)

        def per_expert(e, c):
            real = cnt_ref[e]
            tail = real % PIECE
            base = pst_ref[e]

            @pl.when(tail != 0)
            def _():
                cp = pltpu.make_async_copy(carry_ref.at[e], out_piece(base + real - tail), fsem)
                cp.start()
                cp.wait()

            zero_from = (real + PIECE - 1) // PIECE
            zero_to = pad_ref[e] // PIECE

            def zstart(z, c2):
                pltpu.make_async_copy(stage_piece(0), out_piece(base + z * PIECE), fsem).start()
                return c2

            def zwait(z, c2):
                pltpu.make_async_copy(stage_piece(0), out_piece(base + z * PIECE), fsem).wait()
                return c2

            lax.fori_loop(zero_from, zero_to, zstart, 0)
            lax.fori_loop(zero_from, zero_to, zwait, 0)
            return c

        lax.fori_loop(0, E, per_expert, 0)


def _prep_dispatch(tables, counts, padded, pstart, slot, hn, w_gate_up, w_down, n, p_rows, tt, kcat):
    tr = 512
    steps = E * (D // tr)
    n_disp = n // tt
    assert n_disp <= steps, "more token tiles than weight tiles"
    nsp = len(tables) + 3
    tile = lambda i, *_: (i // (D // tr), i % (D // tr), 0)
    tok = lambda i: jnp.minimum(i, n_disp - 1)
    wout = pl.BlockSpec((None, tr, DE), tile)
    grid_spec = pltpu.PrefetchScalarGridSpec(
        num_scalar_prefetch=nsp, grid=(steps,),
        in_specs=[
            pl.BlockSpec((TOPK, tt), lambda i, *_: (0, tok(i))),
            pl.BlockSpec((tt, D), lambda i, *_: (tok(i), 0)),
            pl.BlockSpec((None, tr, 2 * DE), tile),
            pl.BlockSpec((None, tr, D), tile),
        ],
        out_specs=[wout, wout, pl.BlockSpec((None, tr, D), tile), pl.BlockSpec(memory_space=pl.ANY)],
        scratch_shapes=[pltpu.VMEM((kcat, D), BF16), pltpu.VMEM((E, PIECE, D), BF16),
                        pltpu.SemaphoreType.DMA(()), pltpu.SemaphoreType.DMA(())],
    )
    return pl.pallas_call(
        functools.partial(_prep_dispatch_kernel, tt=tt, n_disp=n_disp, kcat=kcat), grid_spec=grid_spec,
        out_shape=[jax.ShapeDtypeStruct((E, D, DE), BF16), jax.ShapeDtypeStruct((E, D, DE), BF16),
                   jax.ShapeDtypeStruct((E, DE, D), BF16), jax.ShapeDtypeStruct((p_rows, D), BF16)],
        compiler_params=pltpu.CompilerParams(dimension_semantics=("arbitrary",), has_side_effects=True,
                                             vmem_limit_bytes=VMEM_LIMIT),
        name="weight_prep_dispatch",
    )(*tables, counts, padded, pstart, slot, hn, w_gate_up, w_down)


def _moe_kernel(be_ref, nu_ref, bv_ref, x_ref, wg_ref, wu_ref, wd_ref, bg_ref, bu_ref, bd_ref, y_ref,
                *, bm, sub, dsub):
    valid = bv_ref[pl.program_id(0)]

    def body(rows):
        x = x_ref[0:rows, :]
        acts = []

        def gate_up(s):
            sl = slice(s * sub, (s + 1) * sub)
            return (jnp.dot(x, wg_ref[:, sl], preferred_element_type=F32) + bg_ref[:, sl],
                    jnp.dot(x, wu_ref[:, sl], preferred_element_type=F32) + bu_ref[:, sl])

        def activate(s, gu):
            gate = jnp.minimum(gu[0], LIMIT)
            up = jnp.clip(gu[1], -LIMIT, LIMIT)
            acts.append((gate * jax.nn.sigmoid(ALPHA * gate) * (up + 1.0)).astype(BF16))

        _pipelined(DE // sub, gate_up, activate)
        act = jnp.concatenate(acts, axis=1)

        def down(c):
            return jnp.dot(act, wd_ref[:, c * dsub:(c + 1) * dsub], preferred_element_type=F32)

        def emit(c, part):
            sl = slice(c * dsub, (c + 1) * dsub)
            y_ref[0:rows, sl] = (part + bd_ref[:, sl]).astype(BF16)

        _pipelined(D // dsub, down, emit)
        if rows < bm:
            y_ref[rows:bm, :] = jnp.zeros((bm - rows, D), BF16)

    @pl.when(valid > bm // 2)
    def _():
        body(bm)

    @pl.when((valid > 0) & (valid <= bm // 2))
    def _():
        body(bm // 2)


def _moe(block_e, n_used, block_valid, xg, wg, wu, wd, bg, bu, bd, n_blocks, bm):
    def expert(blk, be, nu, bv):
        return (be[blk], 0, 0)

    def rows(blk, be, nu, bv):
        return (jnp.minimum(blk, nu[0] - 1), 0)

    single = lambda shape: pl.BlockSpec(shape, expert, pipeline_mode=pl.Buffered(1))
    double = lambda shape: pl.BlockSpec(shape, expert)
    grid_spec = pltpu.PrefetchScalarGridSpec(
        num_scalar_prefetch=3, grid=(n_blocks,),
        in_specs=[
            pl.BlockSpec((bm, D), rows),
            double((None, D, DE)), single((None, D, DE)), double((None, DE, D)),
            double((None, 1, DE)), double((None, 1, DE)), double((None, 1, D)),
        ],
        out_specs=pl.BlockSpec((bm, D), rows),
    )
    return pl.pallas_call(
        functools.partial(_moe_kernel, bm=bm, sub=256, dsub=512), grid_spec=grid_spec,
        out_shape=jax.ShapeDtypeStruct((n_blocks * bm, D), BF16),
        compiler_params=_params(("arbitrary",)),
        name="moe_experts",
    )(block_e, n_used, block_valid, xg, wg, wu, wd, bg, bu, bd)


def _combine_kernel(src_ref, tot_ref, y_ref, h_ref, slot_ref, tw_ref, gf_ref, o_ref, ycat_ref, sem,
                    *, tt, nt, kcat):
    i = pl.program_id(0)
    cur = i % 2

    def for_pieces(tile, s, fn):
        def body(q, c):
            fn(pltpu.make_async_copy(
                y_ref.at[pl.ds(pl.multiple_of(src_ref[tile, q], PIECE), PIECE), :],
                ycat_ref.at[s, pl.ds(pl.multiple_of(q * PIECE, PIECE), PIECE), :], sem.at[s]))
            return c

        lax.fori_loop(0, tot_ref[tile], body, 0)

    @pl.when(i == 0)
    def _():
        ycat_ref[...] = jnp.zeros_like(ycat_ref)
        for_pieces(0, 0, lambda cp: cp.start())

    @pl.when(i + 1 < nt)
    def _():
        for_pieces(i + 1, 1 - cur, lambda cp: cp.start())

    for_pieces(i, cur, lambda cp: cp.wait())

    def combine(k_rows):
        slot = slot_ref[...]
        tw = tw_ref[...]
        col = lax.broadcasted_iota(jnp.int32, (tt, k_rows), 1)
        sel = jnp.zeros((tt, k_rows), F32)
        for k in range(TOPK):
            sel = jnp.where(col == slot[:, k:k + 1], tw[:, k:k + 1], sel)
        moe = jnp.dot(sel.astype(BF16), ycat_ref[cur, 0:k_rows, :], preferred_element_type=F32)
        o_ref[...] = _rms(h_ref[...] + moe, gf_ref[...])

    k_short = kcat * 3 // 4
    used = tot_ref[i] * PIECE

    @pl.when(used <= k_short)
    def _():
        combine(k_short)

    @pl.when(used > k_short)
    def _():
        combine(kcat)


def _combine(src, total, y, h, slot, tw, g_final, n, tt, kcat):
    nt = n // tt
    grid_spec = pltpu.PrefetchScalarGridSpec(
        num_scalar_prefetch=2, grid=(nt,),
        in_specs=[
            pl.BlockSpec(memory_space=pl.ANY),
            pl.BlockSpec((tt, D), lambda i, a, b: (i, 0)),
            pl.BlockSpec((tt, TOPK), lambda i, a, b: (i, 0)),
            pl.BlockSpec((tt, TOPK), lambda i, a, b: (i, 0)),
            pl.BlockSpec((1, D), lambda i, a, b: (0, 0)),
        ],
        out_specs=pl.BlockSpec((tt, D), lambda i, a, b: (i, 0)),
        scratch_shapes=[pltpu.VMEM((2, kcat, D), BF16), pltpu.SemaphoreType.DMA((2,))],
    )
    return pl.pallas_call(
        functools.partial(_combine_kernel, tt=tt, nt=nt, kcat=kcat), grid_spec=grid_spec,
        out_shape=jax.ShapeDtypeStruct((n, D), F32),
        compiler_params=_params(("arbitrary",)),
        name="combine_final_norm",
    )(src, total, y, h, slot, tw, g_final)


def _permute_weights(w_in, w_uq, w_ukv, w_o_swa):
    widths = (HQ * DH, HKV * DH, HKV * DH, QR, KVR, DR, 2 * D)
    offs = [0]
    for w in widths:
        offs.append(offs[-1] + w)
    wq, wk, wv, wdq, wdkv, wkr, wgates = [w_in[:, offs[i]:offs[i + 1]] for i in range(7)]
    wq = wq.reshape(D, HKV, G, DH).transpose(0, 2, 1, 3).reshape(D, HQ * DH)
    pad = jnp.zeros((D, PROJ_COLS - COL_KR - DR), w_in.dtype)
    w_in_p = jnp.concatenate([wgates, wq, wdq, wdkv, wk, wv, wkr, pad], axis=1).astype(BF16)
    wo_a = w_o_swa.reshape(HKV, G, DH, D).transpose(1, 0, 2, 3).reshape(HQ * DH, D).astype(BF16)
    wuq = w_uq.reshape(QR, MH, DN + DR)
    wq_p = jnp.concatenate([wuq, jnp.zeros((QR, MH, 2 * LANES - DN - DR), w_uq.dtype)], axis=2)
    wq_p = wq_p.reshape(QR, MH * 2 * LANES).astype(BF16)
    wukv = w_ukv.reshape(KVR, MH, DN + DV)
    wk_p = wukv[:, :, :DN].reshape(KVR, MH * DN).astype(BF16)
    wv_p = wukv[:, :, DN:].reshape(KVR, MH * DV).astype(BF16)
    return w_in_p, wo_a, wq_p, wk_p, wv_p


def kernel(x, positions, g_attn, w_in, b_gate, sinks, g_q, w_uq, g_kv, w_ukv, w_o_swa, w_o_mla, w_out,
           g_ffn, w_router, b_router, w_gate_up, b_gate_up, w_down, b_down, g_final):
    b, s, _ = x.shape
    n = b * s
    assert w_in.shape[0] == 1, "single-layer block"
    x2 = x.reshape(n, D)

    w_in_p, wo_a, wq_p, wk_p, wv_p = _permute_weights(w_in[0], w_uq[0], w_ukv[0], w_o_swa[0])
    cos, sin = _rope_tables(positions, n)
    proj = _in_proj(x2, g_attn, w_in_p, n)
    qa, ka, qm, km, vm = _prep(proj, cos, sin, g_q, g_kv, wq_p, wk_p, wv_p, n)
    out_a = _swa(sinks[0], qa, ka, proj, b, s)
    out_b = _mla(qm, km, vm, b, s)

    mixed = _mix(out_a, out_b, proj, wo_a, w_o_mla[0].astype(BF16), b_gate, n)
    wr_t = w_router[0].T
    wr_top = _bf16_part(wr_t)
    wr_split = jnp.stack([wr_top.astype(BF16), (wr_t - wr_top).astype(BF16)])
    tm = min(n, 512)
    tt = min(n, 256)
    h, hn, top_e, top_w, rank, cnt, tile_base = _router(
        mixed, x2, w_out[0].astype(BF16), g_ffn, wr_split, b_router[0][:, None], n, tm, tt)

    bm = 512
    n_blocks = -(-(n * TOPK) // bm) + E
    counts = cnt[:, 0].astype(jnp.int32)
    padded = (counts + bm - 1) // bm * bm
    pend = jnp.cumsum(padded)
    pstart = pend - padded
    n_used = pend[-1] // bm
    blocks = jnp.arange(n_blocks, dtype=jnp.int32)
    block_e = jnp.minimum(jnp.sum(pend[None, :] <= (blocks * bm)[:, None], axis=1), E - 1).astype(jnp.int32)
    block_valid = jnp.clip(counts[block_e] - (blocks * bm - pstart[block_e]), 0, bm).astype(jnp.int32)
    block_valid = jnp.where(blocks < n_used, block_valid, 0)
    block_e = jnp.where(blocks < n_used, block_e, block_e[jnp.maximum(n_used - 1, 0)])

    nsub = tm // tt
    before = tile_base.reshape(n // tm, E, LANES)[:, :, :nsub].transpose(0, 2, 1).reshape(n // tt, E)
    before = before.astype(jnp.int32)
    after = jnp.concatenate([before[1:], counts[None, :]], axis=0)
    first = pstart[None, :] + before
    astart = first // PIECE * PIECE
    npieces = jnp.where(after > before, (first + (after - before) - astart + PIECE - 1) // PIECE, 0)
    cum = jnp.cumsum(npieces, axis=1)
    pbase = cum - npieces
    kcat = -(-(tt * TOPK + E * 2 * (PIECE - 1)) // (2 * LANES)) * (2 * LANES)
    shift = (astart - pbase * PIECE).astype(jnp.int32)
    q = jnp.arange(kcat // PIECE, dtype=jnp.int32)
    owner = jnp.sum(cum[:, None, :] <= q[None, :, None], axis=2)
    own = owner[:, :, None] == jnp.arange(E, dtype=jnp.int32)[None, None, :]
    piece_src = jnp.sum(jnp.where(own, shift[:, None, :], 0), axis=2) + q[None, :] * PIECE
    piece_src = jnp.where(owner < E, piece_src, 0).astype(jnp.int32)
    total = cum[:, -1].astype(jnp.int32)
    last = first + (after - before)
    starts_inside = ((npieces > 0) & (first % PIECE != 0)).astype(jnp.int32)
    ends_inside = ((npieces > 0) & (last % PIECE != 0)).astype(jnp.int32)
    held = jnp.sum(jnp.where(own & (q[None, :, None] == cum[:, None, :] - 1), ends_inside[:, None, :], 0), axis=2)
    piece_dst = jnp.where((owner < E) & (held == 0), piece_src, -1).astype(jnp.int32)
    written = (total - jnp.sum(ends_inside, axis=1)).astype(jnp.int32)
    tables = (piece_dst, total, written, (pbase * PIECE).astype(jnp.int32), npieces.astype(jnp.int32),
              starts_inside, ends_inside)

    slot = _slots(pstart, shift, top_e, rank, n, tt)
    wg, wu, wd, xg = _prep_dispatch(tables, counts, padded, pstart, slot, hn, w_gate_up[0], w_down[0], n,
                                    n_blocks * bm, tt, kcat)
    bgu = b_gate_up[0]
    y = _moe(block_e, n_used.reshape(1).astype(jnp.int32), block_valid, xg, wg, wu, wd,
             bgu[:, None, 0::2], bgu[:, None, 1::2], b_down[0][:, None, :], n_blocks, bm)
    out = _combine(piece_src, total, y, h, slot.T, top_w.T, g_final[None, :], n, tt, kcat)
    return out.reshape(b, s, D)
```

```python
import functools

import jax
import jax.numpy as jnp
from jax import lax
from jax.experimental import pallas as pl
from jax.experimental.pallas import tpu as pltpu

F32 = jnp.float32
BF16 = jnp.bfloat16
U32 = jnp.uint32

D = 2048
HQ, HKV, DH, WIN = 16, 2, 64, 128
G = HQ // HKV
MH, QR, KVR, DN, DR, DV = 8, 512, 256, 128, 64, 128
THETA = 10000.0
E, TOPK, DE = 32, 4, 2048
LIMIT, ALPHA = 7.0, 1.702
EPS = 1e-6
NEG = -1e30
LOG2E = 1.4426950408889634

LANES = 128
PIECE = 16
VMEM_LIMIT = 56 * 1024 * 1024

PROJ_COLS = 6400
COL_GATES, COL_Q, COL_DQ, COL_DKV, COL_K, COL_V, COL_KR = 0, 4096, 5120, 5632, 5888, 6016, 6144

NT = (((1,), (1,)), ((), ()))


def _params(sem, vmem=VMEM_LIMIT):
    return pltpu.CompilerParams(dimension_semantics=sem, vmem_limit_bytes=vmem)


def _rms(x, g):
    return x * lax.rsqrt(jnp.mean(x * x, axis=-1, keepdims=True) + EPS) * g


def _bf16_part(x):
    bits = lax.bitcast_convert_type(x, U32) & jnp.uint32(0xFFFF0000)
    return lax.bitcast_convert_type(bits, F32)


def _pipelined(n, produce, consume):
    nxt = produce(0)
    for i in range(n):
        cur = nxt
        if i + 1 < n:
            nxt = produce(i + 1)
        consume(i, cur)


def _rope128(v, cos, sin):
    lane = lax.broadcasted_iota(jnp.int32, v.shape, 1)
    rot = jnp.where((lane % DH) < (DH // 2), -pltpu.roll(v, LANES - DH // 2, 1), pltpu.roll(v, DH // 2, 1))
    return v * cos + rot * sin


def _rope_table_kernel(pos_ref, inv_ref, cos_ref, sin_ref):
    ang = pos_ref[...].astype(F32) * inv_ref[...]
    cos_ref[...] = jnp.cos(ang)
    sin_ref[...] = jnp.sin(ang)


def _rope_tables(positions, n):
    half = DH // 2
    per_row = LANES // half
    inv = jnp.power(THETA, -jnp.arange(half, dtype=F32) * 2.0 / DH)
    inv_row = jnp.tile(inv, per_row)[None, :]
    pos_rows = jnp.repeat(positions.reshape(n // per_row, per_row), half, axis=1)
    rows = n // per_row
    tr = min(rows, 1024)
    cos, sin = pl.pallas_call(
        _rope_table_kernel,
        grid=(rows // tr,),
        in_specs=[pl.BlockSpec((tr, LANES), lambda i: (i, 0)), pl.BlockSpec((1, LANES), lambda i: (0, 0))],
        out_specs=[pl.BlockSpec((tr, LANES), lambda i: (i, 0))] * 2,
        out_shape=[jax.ShapeDtypeStruct((rows, LANES), F32)] * 2,
        name="rope_tables",
    )(pos_rows, inv_row)
    cos = jnp.tile(cos.reshape(n, half), (1, per_row))
    sin = jnp.tile(sin.reshape(n, half), (1, per_row))
    return cos, sin


def _in_proj_kernel(x_ref, g_ref, w_ref, o_ref, xn_ref):
    @pl.when(pl.program_id(1) == 0)
    def _():
        xn_ref[...] = _rms(x_ref[...], g_ref[...]).astype(BF16)

    o_ref[...] = jnp.dot(xn_ref[...], w_ref[...], preferred_element_type=F32).astype(o_ref.dtype)


def _in_proj(x2, g_attn, w_in_p, n):
    tm = min(n, 1024)
    tn = 1280
    return pl.pallas_call(
        _in_proj_kernel,
        grid=(n // tm, PROJ_COLS // tn),
        in_specs=[
            pl.BlockSpec((tm, D), lambda i, j: (i, 0)),
            pl.BlockSpec((1, D), lambda i, j: (0, 0)),
            pl.BlockSpec((D, tn), lambda i, j: (0, j)),
        ],
        out_specs=pl.BlockSpec((tm, tn), lambda i, j: (i, j)),
        out_shape=jax.ShapeDtypeStruct((n, PROJ_COLS), BF16),
        scratch_shapes=[pltpu.VMEM((tm, D), BF16)],
        compiler_params=_params(("arbitrary", "arbitrary")),
        name="in_proj",
    )(x2, g_attn, w_in_p)


def _prep_kernel(q_ref, dq_ref, dkv_ref, k_ref, kr_ref, cos_ref, sin_ref, gq_ref, gkv_ref,
                 wq_ref, wk_ref, wv_ref, qa_ref, ka_ref, qm_ref, km_ref, vm_ref):
    cos = cos_ref[...]
    sin = sin_ref[...]
    swa_scale = DH ** -0.5 * LOG2E
    for c in range(HQ * DH // LANES):
        sl = slice(c * LANES, (c + 1) * LANES)
        qa_ref[:, sl] = (_rope128(q_ref[:, sl].astype(F32), cos, sin) * swa_scale).astype(BF16)
    ka_ref[...] = _rope128(k_ref[...].astype(F32), cos, sin).astype(BF16)

    mla_scale = (DN + DR) ** -0.5 * LOG2E
    cq = _rms(dq_ref[...].astype(F32), gq_ref[...]).astype(BF16)
    qb = jnp.dot(cq, wq_ref[...], preferred_element_type=F32)
    ckv = _rms(dkv_ref[...].astype(F32), gkv_ref[...]).astype(BF16)
    kn = jnp.dot(ckv, wk_ref[...], preferred_element_type=F32)
    vv = jnp.dot(ckv, wv_ref[...], preferred_element_type=F32)
    ones_col = jnp.where(lax.broadcasted_iota(jnp.int32, (vv.shape[0], LANES), 1) == 0, 1.0, 0.0).astype(BF16)
    kr = _rope128(kr_ref[...].astype(F32), cos, sin).astype(BF16)
    for h in range(MH):
        lo = h * 2 * LANES
        qm_ref[:, lo:lo + LANES] = (qb[:, lo:lo + LANES] * mla_scale).astype(BF16)
        qm_ref[:, lo + LANES:lo + 2 * LANES] = (
            _rope128(qb[:, lo + LANES:lo + 2 * LANES], cos, sin) * mla_scale).astype(BF16)
        km_ref[:, lo:lo + LANES] = kn[:, h * LANES:(h + 1) * LANES].astype(BF16)
        km_ref[:, lo + LANES:lo + 2 * LANES] = kr
        vm_ref[:, lo:lo + LANES] = vv[:, h * LANES:(h + 1) * LANES].astype(BF16)
        vm_ref[:, lo + LANES:lo + 2 * LANES] = ones_col


def _prep(proj, cos, sin, g_q, g_kv, wq_p, wk_p, wv_p, n):
    tm = min(n, 512)
    row = lambda w, cb: pl.BlockSpec((tm, w), lambda i: (i, cb))
    full = lambda a: pl.BlockSpec(a.shape, lambda i: (0, 0))
    return pl.pallas_call(
        _prep_kernel,
        grid=(n // tm,),
        in_specs=[
            row(HQ * DH, COL_Q // (HQ * DH)), row(QR, COL_DQ // QR), row(KVR, COL_DKV // KVR),
            row(LANES, COL_K // LANES), row(LANES, COL_KR // LANES),
            row(LANES, 0), row(LANES, 0), full(g_q), full(g_kv), full(wq_p), full(wk_p), full(wv_p),
        ],
        out_specs=[row(HQ * DH, 0), row(LANES, 0)] + [row(MH * 2 * LANES, 0)] * 3,
        out_shape=[jax.ShapeDtypeStruct((n, HQ * DH), BF16), jax.ShapeDtypeStruct((n, LANES), BF16)]
        + [jax.ShapeDtypeStruct((n, MH * 2 * LANES), BF16)] * 3,
        compiler_params=_params(("arbitrary",)),
        name="rope_mla_prep",
    )(proj, proj, proj, proj, proj, cos, sin, g_q, g_kv, wq_p, wk_p, wv_p)


def _swa_kernel(sinks_ref, q_ref, kc_ref, kp_ref, vc_ref, vp_ref, o_ref):
    blk = pl.program_id(1)
    q2 = jnp.concatenate([q_ref[:, g * LANES:(g + 1) * LANES] for g in range(G)], axis=0)
    k2 = jnp.concatenate([kp_ref[...], kc_ref[...]], axis=0)
    v2t = jnp.concatenate([vp_ref[...], vc_ref[...]], axis=0).astype(F32).T
    lane = lax.broadcasted_iota(jnp.int32, (2 * WIN, LANES), 1)
    vrow = lax.broadcasted_iota(jnp.int32, (LANES, 2 * WIN), 0)
    kj = lax.broadcasted_iota(jnp.int32, (2 * WIN, WIN), 0)
    qi = lax.broadcasted_iota(jnp.int32, (2 * WIN, WIN), 1)
    valid = (kj > qi) & (kj <= qi + WIN) & ((kj >= WIN) | (blk > 0))
    outs = []

    def scores(h):
        kz = jnp.where((lane >= h * DH) & (lane < (h + 1) * DH), k2, jnp.zeros_like(k2))
        return lax.dot_general(kz, q2, NT, preferred_element_type=F32)

    def attend(h, s):
        vz = jnp.where((vrow >= h * DH) & (vrow < (h + 1) * DH), v2t, 0.0).astype(BF16)
        ps, invs = [], []
        for g in range(G):
            sg = jnp.where(valid, s[:, g * WIN:(g + 1) * WIN], NEG)
            sink = sinks_ref[h * G + g] * LOG2E
            m = jnp.maximum(jnp.max(sg, axis=0, keepdims=True), sink)
            p = jnp.exp2(sg - m)
            l = jnp.sum(p, axis=0, keepdims=True) + jnp.exp2(sink - m)
            ps.append(p.astype(BF16))
            invs.append(1.0 / l)
        o_h = jnp.dot(vz, jnp.concatenate(ps, axis=1), preferred_element_type=F32)
        outs.append(o_h * jnp.concatenate(invs, axis=1))

    _pipelined(HKV, scores, attend)
    acc = outs[0] + outs[1]
    for g in range(G):
        o_ref[:, g * LANES:(g + 1) * LANES] = acc[:, g * WIN:(g + 1) * WIN].T.astype(BF16)


def _swa(sinks, qa, ka, proj, b, s):
    nb = s // WIN
    cur = lambda cb: pl.BlockSpec((WIN, LANES), lambda bi, n, sk: (bi * nb + n, cb))
    prev = lambda cb: pl.BlockSpec((WIN, LANES), lambda bi, n, sk: (bi * nb + jnp.maximum(n - 1, 0), cb))
    grid_spec = pltpu.PrefetchScalarGridSpec(
        num_scalar_prefetch=1,
        grid=(b, nb),
        in_specs=[
            pl.BlockSpec((WIN, HQ * DH), lambda bi, n, sk: (bi * nb + n, 0)),
            cur(0), prev(0), cur(COL_V // LANES), prev(COL_V // LANES),
        ],
        out_specs=pl.BlockSpec((WIN, HQ * DH), lambda bi, n, sk: (bi * nb + n, 0)),
    )
    return pl.pallas_call(
        _swa_kernel,
        grid_spec=grid_spec,
        out_shape=jax.ShapeDtypeStruct((b * s, HQ * DH), BF16),
        compiler_params=_params(("arbitrary", "arbitrary")),
        name="swa_attention",
    )(sinks, qa, ka, ka, proj, proj)


def _mla_kernel(q_ref, k_ref, v_ref, o_ref, *, seq, tq):
    row = lax.broadcasted_iota(jnp.int32, (tq, tq), 0)
    col = lax.broadcasted_iota(jnp.int32, (tq, tq), 1)
    causal = col <= row
    nq = seq // tq

    def scores(i):
        kv = (i + 1) * tq
        q = q_ref[i * tq:(i + 1) * tq, :]
        s = lax.dot_general(q, k_ref[0:kv, :], NT, preferred_element_type=F32)
        diag = jnp.where(causal, s[:, kv - tq:kv], NEG)
        return diag if i == 0 else jnp.concatenate([s[:, 0:kv - tq], diag], axis=1)

    def finish(i, s):
        kv = (i + 1) * tq
        m = jnp.max(s, axis=-1, keepdims=True)
        p = jnp.exp2(s - m).astype(BF16)
        ov = jnp.dot(p, v_ref[0:kv, :], preferred_element_type=F32)
        o_ref[i * tq:(i + 1) * tq, :] = (ov[:, 0:DV] * (1.0 / ov[:, DV:DV + 1])).astype(BF16)

    _pipelined(nq, scores, finish)


def _mla(qm, km, vm, b, s):
    tq = min(s, 512)
    return pl.pallas_call(
        functools.partial(_mla_kernel, seq=s, tq=tq),
        grid=(b, MH),
        in_specs=[
            pl.BlockSpec((s, 2 * LANES), lambda bi, h: (bi, h)),
            pl.BlockSpec((s, 2 * LANES), lambda bi, h: (bi, h)),
            pl.BlockSpec((s, 2 * LANES), lambda bi, h: (bi, h)),
        ],
        out_specs=pl.BlockSpec((s, DV), lambda bi, h: (bi, h)),
        out_shape=jax.ShapeDtypeStruct((b * s, MH * DV), BF16),
        compiler_params=_params(("arbitrary", "arbitrary")),
        name="mla_attention",
    )(qm, km, vm)


def _mix_kernel(oa_ref, ob_ref, ga_ref, gb_ref, wa_ref, wb_ref, ba_ref, bb_ref, o_ref):
    ya = jnp.dot(oa_ref[...], wa_ref[...], preferred_element_type=F32)
    yb = jnp.dot(ob_ref[...], wb_ref[...], preferred_element_type=F32)
    ga = jax.nn.sigmoid(ga_ref[...].astype(F32) + ba_ref[...])
    gb = jax.nn.sigmoid(gb_ref[...].astype(F32) + bb_ref[...])
    o_ref[...] = (ga * ya + gb * yb).astype(BF16)


def _mix(out_a, out_b, proj, wa, wb, b_gate, n):
    tm = min(n, 1024)
    tn = 1024
    nj = D // tn
    return pl.pallas_call(
        _mix_kernel,
        grid=(n // tm, nj),
        in_specs=[
            pl.BlockSpec((tm, HQ * DH), lambda i, j: (i, 0)),
            pl.BlockSpec((tm, MH * DV), lambda i, j: (i, 0)),
            pl.BlockSpec((tm, tn), lambda i, j: (i, j)),
            pl.BlockSpec((tm, tn), lambda i, j: (i, nj + j)),
            pl.BlockSpec((HQ * DH, tn), lambda i, j: (0, j)),
            pl.BlockSpec((MH * DV, tn), lambda i, j: (0, j)),
            pl.BlockSpec((1, tn), lambda i, j: (0, j)),
            pl.BlockSpec((1, tn), lambda i, j: (0, nj + j)),
        ],
        out_specs=pl.BlockSpec((tm, tn), lambda i, j: (i, j)),
        out_shape=jax.ShapeDtypeStruct((n, D), BF16),
        compiler_params=_params(("arbitrary", "arbitrary")),
        name="gated_mix",
    )(out_a, out_b, proj, proj, wa, wb, b_gate, b_gate)


def _router_kernel(mx_ref, x_ref, wo_ref, gf_ref, wr_ref, br_ref,
                   h_ref, hn_ref, te_ref, tw_ref, rk_ref, cnt_ref, tb_ref, base_ref, *, tm, sub):
    @pl.when(pl.program_id(0) == 0)
    def _():
        base_ref[...] = jnp.zeros_like(base_ref)

    h = x_ref[...] + jnp.dot(mx_ref[...], wo_ref[...], preferred_element_type=F32)
    h_ref[...] = h
    hn = _rms(h, gf_ref[...])
    hn_ref[...] = hn.astype(BF16)

    hn_top = _bf16_part(hn)
    hn_hi = hn_top.astype(BF16)
    hn_lo = (hn - hn_top).astype(BF16)
    logits = (lax.dot_general(wr_ref[0], hn_hi, NT, preferred_element_type=F32)
              + lax.dot_general(wr_ref[0], hn_lo, NT, preferred_element_type=F32)
              + lax.dot_general(wr_ref[1], hn_hi, NT, preferred_element_type=F32)) + br_ref[...]
    eidx = lax.broadcasted_iota(jnp.int32, (E, tm), 0)
    vals, idxs = [], []
    l = logits
    for _ in range(TOPK):
        m = jnp.max(l, axis=0, keepdims=True)
        idx = jnp.min(jnp.where(l == m, eidx, E), axis=0, keepdims=True)
        vals.append(m)
        idxs.append(idx)
        l = jnp.where(eidx == idx, -jnp.inf, l)
    ex = [jnp.exp(v - vals[0]) for v in vals]
    tot = ex[0] + ex[1] + ex[2] + ex[3]
    tw_ref[...] = jnp.concatenate([e / tot for e in ex], axis=0)
    te_ref[...] = jnp.concatenate(idxs, axis=0)

    onehot = jnp.zeros((E, tm), F32)
    for idx in idxs:
        onehot = onehot + jnp.where(eidx == idx, 1.0, 0.0)
    r = lax.broadcasted_iota(jnp.int32, (tm, tm), 0)
    c = lax.broadcasted_iota(jnp.int32, (tm, tm), 1)
    earlier = jnp.where(r < c, 1.0, 0.0).astype(BF16)
    before = jnp.dot(onehot.astype(BF16), earlier, preferred_element_type=F32) + base_ref[:, 0:1]
    rk_ref[...] = jnp.concatenate(
        [jnp.sum(jnp.where(eidx == idx, before, 0.0), axis=0, keepdims=True) for idx in idxs],
        axis=0).astype(jnp.int32)
    lane = lax.broadcasted_iota(jnp.int32, (E, LANES), 1)
    tb = jnp.zeros((E, LANES), F32)
    for s in range(tm // sub):
        tb = jnp.where(lane == s, before[:, s * sub:s * sub + 1], tb)
    tb_ref[...] = tb
    base_ref[...] = base_ref[...] + jnp.sum(onehot, axis=1, keepdims=True)
    cnt_ref[...] = base_ref[...]


def _router(mixed, x2, wo, g_ffn, wr_t, b_router, n, tm, sub):
    row = lambda w: pl.BlockSpec((tm, w), lambda i: (i, 0))
    full = lambda a: pl.BlockSpec(a.shape, lambda i: (0,) * a.ndim, pipeline_mode=pl.Buffered(1))
    tok = pl.BlockSpec((TOPK, tm), lambda i: (0, i))
    return pl.pallas_call(
        functools.partial(_router_kernel, tm=tm, sub=sub),
        grid=(n // tm,),
        in_specs=[row(D), row(D), full(wo), full(g_ffn), full(wr_t), full(b_router)],
        out_specs=[
            row(D), row(D), tok, tok, tok,
            pl.BlockSpec((E, LANES), lambda i: (0, 0)), pl.BlockSpec((E, LANES), lambda i: (i, 0)),
        ],
        out_shape=[
            jax.ShapeDtypeStruct((n, D), F32), jax.ShapeDtypeStruct((n, D), BF16),
            jax.ShapeDtypeStruct((TOPK, n), jnp.int32), jax.ShapeDtypeStruct((TOPK, n), F32),
            jax.ShapeDtypeStruct((TOPK, n), jnp.int32), jax.ShapeDtypeStruct((E, LANES), F32),
            jax.ShapeDtypeStruct((n // tm * E, LANES), F32),
        ],
        scratch_shapes=[pltpu.VMEM((E, LANES), F32)],
        compiler_params=_params(("arbitrary",)),
        name="outproj_router",
    )(mixed, x2, wo, g_ffn, wr_t, b_router)


def _slot_kernel(pstart_ref, shift_ref, te_ref, rk_ref, slot_ref, *, tt, group):
    for g in range(group):
        tile = pl.program_id(0) * group + g
        te = te_ref[:, g * tt:(g + 1) * tt]
        s = rk_ref[:, g * tt:(g + 1) * tt]
        for e in range(E):
            s = s + jnp.where(te == e, pstart_ref[e] - shift_ref[tile, e], 0)
        slot_ref[:, g * tt:(g + 1) * tt] = s


def _slots(pstart, shift, top_e, rank, n, tt):
    group = min(8, n // tt)
    tok = pl.BlockSpec((TOPK, group * tt), lambda i, ps, sh: (0, i))
    grid_spec = pltpu.PrefetchScalarGridSpec(
        num_scalar_prefetch=2, grid=(n // (group * tt),), in_specs=[tok, tok], out_specs=tok)
    return pl.pallas_call(
        functools.partial(_slot_kernel, tt=tt, group=group), grid_spec=grid_spec,
        out_shape=jax.ShapeDtypeStruct((TOPK, n), jnp.int32), name="staging_slots",
    )(pstart, shift, top_e, rank)


def _prep_dispatch_kernel(pdst_ref, ntot_ref, nwr_ref, pb_ref, np_ref, ff_ref, lf_ref, cnt_ref, pad_ref, pst_ref,
                          slot_ref, hn_ref, wgu_ref, wdn_ref, g_ref, u_ref, d_ref, xg_ref,
                          stage_ref, carry_ref, sem, fsem, *, tt, n_disp, kcat):
    i = pl.program_id(0)

    def stage_piece(row):
        return stage_ref.at[pl.ds(pl.multiple_of(row, PIECE), PIECE), :]

    def out_piece(row):
        return xg_ref.at[pl.ds(pl.multiple_of(row, PIECE), PIECE), :]

    @pl.when(i == 0)
    def _():
        carry_ref[...] = jnp.zeros_like(carry_ref)

    @pl.when(i < n_disp)
    def _():
        slot = slot_ref[...]
        chunk = 512
        for r0 in range(0, kcat, chunk):
            row = lax.broadcasted_iota(jnp.int32, (chunk, tt), 0) + r0
            hit = row == slot[0:1, :]
            for k in range(1, TOPK):
                hit = hit | (row == slot[k:k + 1, :])
            sel = jnp.where(hit, 1.0, 0.0).astype(BF16)
            stage_ref[r0:r0 + chunk, :] = jnp.dot(sel, hn_ref[...], preferred_element_type=F32).astype(BF16)

        def per_expert(e, c):
            pieces = np_ref[i, e]

            @pl.when(pieces > 0)
            def _():
                first = pb_ref[i, e]
                last = first + (pieces - 1) * PIECE

                @pl.when(ff_ref[i, e] == 1)
                def _():
                    stage_piece(first)[...] = stage_piece(first)[...] + carry_ref[e]

                @pl.when(lf_ref[i, e] == 1)
                def _():
                    carry_ref[e] = stage_piece(last)[...]

            return c

        lax.fori_loop(0, E, per_expert, 0)

        def start(q, c):
            dst = pdst_ref[i, q]

            @pl.when(dst >= 0)
            def _():
                pltpu.make_async_copy(stage_piece(q * PIECE), out_piece(dst), sem).start()

            return c

        lax.fori_loop(0, ntot_ref[i], start, 0)

    chunk = 2 * LANES
    r = lax.broadcasted_iota(jnp.int32, (chunk, chunk), 0)
    c = lax.broadcasted_iota(jnp.int32, (chunk, chunk), 1)
    src = jnp.where(c < LANES, 2 * c, 2 * (c - LANES) + 1)
    perm = jnp.where(r == src, 1.0, 0.0).astype(BF16)
    for k in range(wgu_ref.shape[1] // chunk):
        w = wgu_ref[:, k * chunk:(k + 1) * chunk].astype(BF16)
        o = jnp.dot(w, perm, preferred_element_type=F32)
        g_ref[:, k * LANES:(k + 1) * LANES] = o[:, :LANES].astype(BF16)
        u_ref[:, k * LANES:(k + 1) * LANES] = o[:, LANES:].astype(BF16)
    d_ref[...] = wdn_ref[...].astype(BF16)

    @pl.when(i < n_disp)
    def _():
        def wait(q, c):
            pltpu.make_async_copy(stage_piece(0), out_piece(0), sem).wait()
            return c

        lax.fori_loop(0, nwr_ref[i], wait, 0)

    @pl.when(i == n_disp - 1)
    def _():
        stage_ref[0:PIECE, :] = jnp.zeros((PIECE, D), BF16)

        def per_expert(e, c):
            real = cnt_ref[e]
            tail = real % PIECE
            base = pst_ref[e]

            @pl.when(tail != 0)
            def _():
                cp = pltpu.make_async_copy(carry_ref.at[e], out_piece(base + real - tail), fsem)
                cp.start()
                cp.wait()

            zero_from = (real + PIECE - 1) // PIECE
            zero_to = pad_ref[e] // PIECE

            def zstart(z, c2):
                pltpu.make_async_copy(stage_piece(0), out_piece(base + z * PIECE), fsem).start()
                return c2

            def zwait(z, c2):
                pltpu.make_async_copy(stage_piece(0), out_piece(base + z * PIECE), fsem).wait()
                return c2

            lax.fori_loop(zero_from, zero_to, zstart, 0)
            lax.fori_loop(zero_from, zero_to, zwait, 0)
            return c

        lax.fori_loop(0, E, per_expert, 0)


def _prep_dispatch(tables, counts, padded, pstart, slot, hn, w_gate_up, w_down, n, p_rows, tt, kcat):
    tr = 512
    steps = E * (D // tr)
    n_disp = n // tt
    assert n_disp <= steps, "more token tiles than weight tiles"
    nsp = len(tables) + 3
    tile = lambda i, *_: (i // (D // tr), i % (D // tr), 0)
    tok = lambda i: jnp.minimum(i, n_disp - 1)
    wout = pl.BlockSpec((None, tr, DE), tile)
    grid_spec = pltpu.PrefetchScalarGridSpec(
        num_scalar_prefetch=nsp, grid=(steps,),
        in_specs=[
            pl.BlockSpec((TOPK, tt), lambda i, *_: (0, tok(i))),
            pl.BlockSpec((tt, D), lambda i, *_: (tok(i), 0)),
            pl.BlockSpec((None, tr, 2 * DE), tile),
            pl.BlockSpec((None, tr, D), tile),
        ],
        out_specs=[wout, wout, pl.BlockSpec((None, tr, D), tile), pl.BlockSpec(memory_space=pl.ANY)],
        scratch_shapes=[pltpu.VMEM((kcat, D), BF16), pltpu.VMEM((E, PIECE, D), BF16),
                        pltpu.SemaphoreType.DMA(()), pltpu.SemaphoreType.DMA(())],
    )
    return pl.pallas_call(
        functools.partial(_prep_dispatch_kernel, tt=tt, n_disp=n_disp, kcat=kcat), grid_spec=grid_spec,
        out_shape=[jax.ShapeDtypeStruct((E, D, DE), BF16), jax.ShapeDtypeStruct((E, D, DE), BF16),
                   jax.ShapeDtypeStruct((E, DE, D), BF16), jax.ShapeDtypeStruct((p_rows, D), BF16)],
        compiler_params=pltpu.CompilerParams(dimension_semantics=("arbitrary",), has_side_effects=True,
                                             vmem_limit_bytes=VMEM_LIMIT),
        name="weight_prep_dispatch",
    )(*tables, counts, padded, pstart, slot, hn, w_gate_up, w_down)


def _moe_kernel(be_ref, nu_ref, bv_ref, x_ref, wg_ref, wu_ref, wd_ref, bg_ref, bu_ref, bd_ref, y_ref,
                *, bm, sub, dsub):
    valid = bv_ref[pl.program_id(0)]

    def body(rows):
        x = x_ref[0:rows, :]
        acts = []

        def gate_up(s):
            sl = slice(s * sub, (s + 1) * sub)
            return (jnp.dot(x, wg_ref[:, sl], preferred_element_type=F32) + bg_ref[:, sl],
                    jnp.dot(x, wu_ref[:, sl], preferred_element_type=F32) + bu_ref[:, sl])

        def activate(s, gu):
            gate = jnp.minimum(gu[0], LIMIT)
            up = jnp.clip(gu[1], -LIMIT, LIMIT)
            acts.append((gate * jax.nn.sigmoid(ALPHA * gate) * (up + 1.0)).astype(BF16))

        _pipelined(DE // sub, gate_up, activate)
        act = jnp.concatenate(acts, axis=1)

        def down(c):
            return jnp.dot(act, wd_ref[:, c * dsub:(c + 1) * dsub], preferred_element_type=F32)

        def emit(c, part):
            sl = slice(c * dsub, (c + 1) * dsub)
            y_ref[0:rows, sl] = (part + bd_ref[:, sl]).astype(BF16)

        _pipelined(D // dsub, down, emit)
        if rows < bm:
            y_ref[rows:bm, :] = jnp.zeros((bm - rows, D), BF16)

    @pl.when(valid > bm // 2)
    def _():
        body(bm)

    @pl.when((valid > 0) & (valid <= bm // 2))
    def _():
        body(bm // 2)


def _moe(block_e, n_used, block_valid, xg, wg, wu, wd, bg, bu, bd, n_blocks, bm):
    def expert(blk, be, nu, bv):
        return (be[blk], 0, 0)

    def rows(blk, be, nu, bv):
        return (jnp.minimum(blk, nu[0] - 1), 0)

    single = lambda shape: pl.BlockSpec(shape, expert, pipeline_mode=pl.Buffered(1))
    double = lambda shape: pl.BlockSpec(shape, expert)
    grid_spec = pltpu.PrefetchScalarGridSpec(
        num_scalar_prefetch=3, grid=(n_blocks,),
        in_specs=[
            pl.BlockSpec((bm, D), rows),
            double((None, D, DE)), single((None, D, DE)), double((None, DE, D)),
            double((None, 1, DE)), double((None, 1, DE)), double((None, 1, D)),
        ],
        out_specs=pl.BlockSpec((bm, D), rows),
    )
    return pl.pallas_call(
        functools.partial(_moe_kernel, bm=bm, sub=512, dsub=256), grid_spec=grid_spec,
        out_shape=jax.ShapeDtypeStruct((n_blocks * bm, D), BF16),
        compiler_params=_params(("arbitrary",)),
        name="moe_experts",
    )(block_e, n_used, block_valid, xg, wg, wu, wd, bg, bu, bd)


def _combine_kernel(src_ref, tot_ref, y_ref, h_ref, slot_ref, tw_ref, gf_ref, o_ref, ycat_ref, sem,
                    *, tt, nt, kcat):
    i = pl.program_id(0)
    cur = i % 2

    def for_pieces(tile, s, fn):
        def body(q, c):
            fn(pltpu.make_async_copy(
                y_ref.at[pl.ds(pl.multiple_of(src_ref[tile, q], PIECE), PIECE), :],
                ycat_ref.at[s, pl.ds(pl.multiple_of(q * PIECE, PIECE), PIECE), :], sem.at[s]))
            return c

        lax.fori_loop(0, tot_ref[tile], body, 0)

    @pl.when(i == 0)
    def _():
        ycat_ref[...] = jnp.zeros_like(ycat_ref)
        for_pieces(0, 0, lambda cp: cp.start())

    @pl.when(i + 1 < nt)
    def _():
        for_pieces(i + 1, 1 - cur, lambda cp: cp.start())

    for_pieces(i, cur, lambda cp: cp.wait())

    def combine(k_rows):
        slot = slot_ref[...]
        tw = tw_ref[...]
        col = lax.broadcasted_iota(jnp.int32, (tt, k_rows), 1)
        sel = jnp.zeros((tt, k_rows), F32)
        for k in range(TOPK):
            sel = jnp.where(col == slot[:, k:k + 1], tw[:, k:k + 1], sel)
        moe = jnp.dot(sel.astype(BF16), ycat_ref[cur, 0:k_rows, :], preferred_element_type=F32)
        o_ref[...] = _rms(h_ref[...] + moe, gf_ref[...])

    k_short = kcat * 3 // 4
    used = tot_ref[i] * PIECE

    @pl.when(used <= k_short)
    def _():
        combine(k_short)

    @pl.when(used > k_short)
    def _():
        combine(kcat)


def _combine(src, total, y, h, slot, tw, g_final, n, tt, kcat):
    nt = n // tt
    grid_spec = pltpu.PrefetchScalarGridSpec(
        num_scalar_prefetch=2, grid=(nt,),
        in_specs=[
            pl.BlockSpec(memory_space=pl.ANY),
            pl.BlockSpec((tt, D), lambda i, a, b: (i, 0)),
            pl.BlockSpec((tt, TOPK), lambda i, a, b: (i, 0)),
            pl.BlockSpec((tt, TOPK), lambda i, a, b: (i, 0)),
            pl.BlockSpec((1, D), lambda i, a, b: (0, 0)),
        ],
        out_specs=pl.BlockSpec((tt, D), lambda i, a, b: (i, 0)),
        scratch_shapes=[pltpu.VMEM((2, kcat, D), BF16), pltpu.SemaphoreType.DMA((2,))],
    )
    return pl.pallas_call(
        functools.partial(_combine_kernel, tt=tt, nt=nt, kcat=kcat), grid_spec=grid_spec,
        out_shape=jax.ShapeDtypeStruct((n, D), F32),
        compiler_params=_params(("arbitrary",)),
        name="combine_final_norm",
    )(src, total, y, h, slot, tw, g_final)


def _permute_weights(w_in, w_uq, w_ukv, w_o_swa):
    widths = (HQ * DH, HKV * DH, HKV * DH, QR, KVR, DR, 2 * D)
    offs = [0]
    for w in widths:
        offs.append(offs[-1] + w)
    wq, wk, wv, wdq, wdkv, wkr, wgates = [w_in[:, offs[i]:offs[i + 1]] for i in range(7)]
    wq = wq.reshape(D, HKV, G, DH).transpose(0, 2, 1, 3).reshape(D, HQ * DH)
    pad = jnp.zeros((D, PROJ_COLS - COL_KR - DR), w_in.dtype)
    w_in_p = jnp.concatenate([wgates, wq, wdq, wdkv, wk, wv, wkr, pad], axis=1).astype(BF16)
    wo_a = w_o_swa.reshape(HKV, G, DH, D).transpose(1, 0, 2, 3).reshape(HQ * DH, D).astype(BF16)
    wuq = w_uq.reshape(QR, MH, DN + DR)
    wq_p = jnp.concatenate([wuq, jnp.zeros((QR, MH, 2 * LANES - DN - DR), w_uq.dtype)], axis=2)
    wq_p = wq_p.reshape(QR, MH * 2 * LANES).astype(BF16)
    wukv = w_ukv.reshape(KVR, MH, DN + DV)
    wk_p = wukv[:, :, :DN].reshape(KVR, MH * DN).astype(BF16)
    wv_p = wukv[:, :, DN:].reshape(KVR, MH * DV).astype(BF16)
    return w_in_p, wo_a, wq_p, wk_p, wv_p


def kernel(x, positions, g_attn, w_in, b_gate, sinks, g_q, w_uq, g_kv, w_ukv, w_o_swa, w_o_mla, w_out,
           g_ffn, w_router, b_router, w_gate_up, b_gate_up, w_down, b_down, g_final):
    b, s, _ = x.shape
    n = b * s
    assert w_in.shape[0] == 1, "single-layer block"
    x2 = x.reshape(n, D)

    w_in_p, wo_a, wq_p, wk_p, wv_p = _permute_weights(w_in[0], w_uq[0], w_ukv[0], w_o_swa[0])
    cos, sin = _rope_tables(positions, n)
    proj = _in_proj(x2, g_attn, w_in_p, n)
    qa, ka, qm, km, vm = _prep(proj, cos, sin, g_q, g_kv, wq_p, wk_p, wv_p, n)
    out_a = _swa(sinks[0], qa, ka, proj, b, s)
    out_b = _mla(qm, km, vm, b, s)

    mixed = _mix(out_a, out_b, proj, wo_a, w_o_mla[0].astype(BF16), b_gate, n)
    wr_t = w_router[0].T
    wr_top = _bf16_part(wr_t)
    wr_split = jnp.stack([wr_top.astype(BF16), (wr_t - wr_top).astype(BF16)])
    tm = min(n, 512)
    tt = min(n, 256)
    h, hn, top_e, top_w, rank, cnt, tile_base = _router(
        mixed, x2, w_out[0].astype(BF16), g_ffn, wr_split, b_router[0][:, None], n, tm, tt)

    bm = 512
    n_blocks = -(-(n * TOPK) // bm) + E
    counts = cnt[:, 0].astype(jnp.int32)
    padded = (counts + bm - 1) // bm * bm
    pend = jnp.cumsum(padded)
    pstart = pend - padded
    n_used = pend[-1] // bm
    blocks = jnp.arange(n_blocks, dtype=jnp.int32)
    block_e = jnp.minimum(jnp.sum(pend[None, :] <= (blocks * bm)[:, None], axis=1), E - 1).astype(jnp.int32)
    block_valid = jnp.clip(counts[block_e] - (blocks * bm - pstart[block_e]), 0, bm).astype(jnp.int32)
    block_valid = jnp.where(blocks < n_used, block_valid, 0)
    block_e = jnp.where(blocks < n_used, block_e, block_e[jnp.maximum(n_used - 1, 0)])

    nsub = tm // tt
    before = tile_base.reshape(n // tm, E, LANES)[:, :, :nsub].transpose(0, 2, 1).reshape(n // tt, E)
    before = before.astype(jnp.int32)
    after = jnp.concatenate([before[1:], counts[None, :]], axis=0)
    first = pstart[None, :] + before
    astart = first // PIECE * PIECE
    npieces = jnp.where(after > before, (first + (after - before) - astart + PIECE - 1) // PIECE, 0)
    cum = jnp.cumsum(npieces, axis=1)
    pbase = cum - npieces
    kcat = -(-(tt * TOPK + E * 2 * (PIECE - 1)) // (2 * LANES)) * (2 * LANES)
    shift = (astart - pbase * PIECE).astype(jnp.int32)
    q = jnp.arange(kcat // PIECE, dtype=jnp.int32)
    owner = jnp.sum(cum[:, None, :] <= q[None, :, None], axis=2)
    own = owner[:, :, None] == jnp.arange(E, dtype=jnp.int32)[None, None, :]
    piece_src = jnp.sum(jnp.where(own, shift[:, None, :], 0), axis=2) + q[None, :] * PIECE
    piece_src = jnp.where(owner < E, piece_src, 0).astype(jnp.int32)
    total = cum[:, -1].astype(jnp.int32)
    last = first + (after - before)
    starts_inside = ((npieces > 0) & (first % PIECE != 0)).astype(jnp.int32)
    ends_inside = ((npieces > 0) & (last % PIECE != 0)).astype(jnp.int32)
    held = jnp.sum(jnp.where(own & (q[None, :, None] == cum[:, None, :] - 1), ends_inside[:, None, :], 0), axis=2)
    piece_dst = jnp.where((owner < E) & (held == 0), piece_src, -1).astype(jnp.int32)
    written = (total - jnp.sum(ends_inside, axis=1)).astype(jnp.int32)
    tables = (piece_dst, total, written, (pbase * PIECE).astype(jnp.int32), npieces.astype(jnp.int32),
              starts_inside, ends_inside)

    slot = _slots(pstart, shift, top_e, rank, n, tt)
    wg, wu, wd, xg = _prep_dispatch(tables, counts, padded, pstart, slot, hn, w_gate_up[0], w_down[0], n,
                                    n_blocks * bm, tt, kcat)
    bgu = b_gate_up[0]
    y = _moe(block_e, n_used.reshape(1).astype(jnp.int32), block_valid, xg, wg, wu, wd,
             bgu[:, None, 0::2], bgu[:, None, 1::2], b_down[0][:, None, :], n_blocks, bm)
    out = _combine(piece_src, total, y, h, slot.T, top_w.T, g_final[None, :], n, tt, kcat)
    return out.reshape(b, s, D)
```

```python
import functools

import jax
import jax.numpy as jnp
from jax import lax
from jax.experimental import pallas as pl
from jax.experimental.pallas import tpu as pltpu

F32 = jnp.float32
BF16 = jnp.bfloat16
U32 = jnp.uint32

D = 2048
HQ, HKV, DH, WIN = 16, 2, 64, 128
G = HQ // HKV
MH, QR, KVR, DN, DR, DV = 8, 512, 256, 128, 64, 128
THETA = 10000.0
E, TOPK, DE = 32, 4, 2048
LIMIT, ALPHA = 7.0, 1.702
EPS = 1e-6
NEG = -1e30
LOG2E = 1.4426950408889634

LANES = 128
PIECE = 16
VMEM_LIMIT = 56 * 1024 * 1024

PROJ_COLS = 6400
COL_GATES, COL_Q, COL_DQ, COL_DKV, COL_K, COL_V, COL_KR = 0, 4096, 5120, 5632, 5888, 6016, 6144

NT = (((1,), (1,)), ((), ()))


def _params(sem, vmem=VMEM_LIMIT):
    return pltpu.CompilerParams(dimension_semantics=sem, vmem_limit_bytes=vmem)


def _rms(x, g):
    return x * lax.rsqrt(jnp.mean(x * x, axis=-1, keepdims=True) + EPS) * g


def _bf16_part(x):
    bits = lax.bitcast_convert_type(x, U32) & jnp.uint32(0xFFFF0000)
    return lax.bitcast_convert_type(bits, F32)


def _pipelined(n, produce, consume):
    nxt = produce(0)
    for i in range(n):
        cur = nxt
        if i + 1 < n:
            nxt = produce(i + 1)
        consume(i, cur)


def _rope128(v, cos, sin):
    lane = lax.broadcasted_iota(jnp.int32, v.shape, 1)
    rot = jnp.where((lane % DH) < (DH // 2), -pltpu.roll(v, LANES - DH // 2, 1), pltpu.roll(v, DH // 2, 1))
    return v * cos + rot * sin


def _rope_table_kernel(pos_ref, inv_ref, cos_ref, sin_ref):
    ang = pos_ref[...].astype(F32) * inv_ref[...]
    cos_ref[...] = jnp.cos(ang)
    sin_ref[...] = jnp.sin(ang)


def _rope_tables(positions, n):
    half = DH // 2
    per_row = LANES // half
    inv = jnp.power(THETA, -jnp.arange(half, dtype=F32) * 2.0 / DH)
    inv_row = jnp.tile(inv, per_row)[None, :]
    pos_rows = jnp.repeat(positions.reshape(n // per_row, per_row), half, axis=1)
    rows = n // per_row
    tr = min(rows, 1024)
    cos, sin = pl.pallas_call(
        _rope_table_kernel,
        grid=(rows // tr,),
        in_specs=[pl.BlockSpec((tr, LANES), lambda i: (i, 0)), pl.BlockSpec((1, LANES), lambda i: (0, 0))],
        out_specs=[pl.BlockSpec((tr, LANES), lambda i: (i, 0))] * 2,
        out_shape=[jax.ShapeDtypeStruct((rows, LANES), F32)] * 2,
        name="rope_tables",
    )(pos_rows, inv_row)
    cos = jnp.tile(cos.reshape(n, half), (1, per_row))
    sin = jnp.tile(sin.reshape(n, half), (1, per_row))
    return cos, sin


def _in_proj_kernel(x_ref, g_ref, w_ref, o_ref, xn_ref):
    @pl.when(pl.program_id(1) == 0)
    def _():
        xn_ref[...] = _rms(x_ref[...], g_ref[...]).astype(BF16)

    o_ref[...] = jnp.dot(xn_ref[...], w_ref[...], preferred_element_type=F32).astype(o_ref.dtype)


def _in_proj(x2, g_attn, w_in_p, n):
    tm = min(n, 1024)
    tn = 1280
    return pl.pallas_call(
        _in_proj_kernel,
        grid=(n // tm, PROJ_COLS // tn),
        in_specs=[
            pl.BlockSpec((tm, D), lambda i, j: (i, 0)),
            pl.BlockSpec((1, D), lambda i, j: (0, 0)),
            pl.BlockSpec((D, tn), lambda i, j: (0, j)),
        ],
        out_specs=pl.BlockSpec((tm, tn), lambda i, j: (i, j)),
        out_shape=jax.ShapeDtypeStruct((n, PROJ_COLS), BF16),
        scratch_shapes=[pltpu.VMEM((tm, D), BF16)],
        compiler_params=_params(("arbitrary", "arbitrary")),
        name="in_proj",
    )(x2, g_attn, w_in_p)


def _prep_kernel(q_ref, dq_ref, dkv_ref, k_ref, kr_ref, cos_ref, sin_ref, gq_ref, gkv_ref,
                 wq_ref, wk_ref, wv_ref, qa_ref, ka_ref, qm_ref, km_ref, vm_ref):
    cos = cos_ref[...]
    sin = sin_ref[...]
    swa_scale = DH ** -0.5 * LOG2E
    for c in range(HQ * DH // LANES):
        sl = slice(c * LANES, (c + 1) * LANES)
        qa_ref[:, sl] = (_rope128(q_ref[:, sl].astype(F32), cos, sin) * swa_scale).astype(BF16)
    ka_ref[...] = _rope128(k_ref[...].astype(F32), cos, sin).astype(BF16)

    mla_scale = (DN + DR) ** -0.5 * LOG2E
    cq = _rms(dq_ref[...].astype(F32), gq_ref[...]).astype(BF16)
    qb = jnp.dot(cq, wq_ref[...], preferred_element_type=F32)
    ckv = _rms(dkv_ref[...].astype(F32), gkv_ref[...]).astype(BF16)
    kn = jnp.dot(ckv, wk_ref[...], preferred_element_type=F32)
    vv = jnp.dot(ckv, wv_ref[...], preferred_element_type=F32)
    ones_col = jnp.where(lax.broadcasted_iota(jnp.int32, (vv.shape[0], LANES), 1) == 0, 1.0, 0.0).astype(BF16)
    kr = _rope128(kr_ref[...].astype(F32), cos, sin).astype(BF16)
    for h in range(MH):
        lo = h * 2 * LANES
        qm_ref[:, lo:lo + LANES] = (qb[:, lo:lo + LANES] * mla_scale).astype(BF16)
        qm_ref[:, lo + LANES:lo + 2 * LANES] = (
            _rope128(qb[:, lo + LANES:lo + 2 * LANES], cos, sin) * mla_scale).astype(BF16)
        km_ref[:, lo:lo + LANES] = kn[:, h * LANES:(h + 1) * LANES].astype(BF16)
        km_ref[:, lo + LANES:lo + 2 * LANES] = kr
        vm_ref[:, lo:lo + LANES] = vv[:, h * LANES:(h + 1) * LANES].astype(BF16)
        vm_ref[:, lo + LANES:lo + 2 * LANES] = ones_col


def _prep(proj, cos, sin, g_q, g_kv, wq_p, wk_p, wv_p, n):
    tm = min(n, 512)
    row = lambda w, cb: pl.BlockSpec((tm, w), lambda i: (i, cb))
    full = lambda a: pl.BlockSpec(a.shape, lambda i: (0, 0))
    return pl.pallas_call(
        _prep_kernel,
        grid=(n // tm,),
        in_specs=[
            row(HQ * DH, COL_Q // (HQ * DH)), row(QR, COL_DQ // QR), row(KVR, COL_DKV // KVR),
            row(LANES, COL_K // LANES), row(LANES, COL_KR // LANES),
            row(LANES, 0), row(LANES, 0), full(g_q), full(g_kv), full(wq_p), full(wk_p), full(wv_p),
        ],
        out_specs=[row(HQ * DH, 0), row(LANES, 0)] + [row(MH * 2 * LANES, 0)] * 3,
        out_shape=[jax.ShapeDtypeStruct((n, HQ * DH), BF16), jax.ShapeDtypeStruct((n, LANES), BF16)]
        + [jax.ShapeDtypeStruct((n, MH * 2 * LANES), BF16)] * 3,
        compiler_params=_params(("arbitrary",)),
        name="rope_mla_prep",
    )(proj, proj, proj, proj, proj, cos, sin, g_q, g_kv, wq_p, wk_p, wv_p)


def _swa_kernel(sinks_ref, q_ref, kc_ref, kp_ref, vc_ref, vp_ref, o_ref):
    blk = pl.program_id(1)
    q2 = jnp.concatenate([q_ref[:, g * LANES:(g + 1) * LANES] for g in range(G)], axis=0)
    k2 = jnp.concatenate([kp_ref[...], kc_ref[...]], axis=0)
    v2t = jnp.concatenate([vp_ref[...], vc_ref[...]], axis=0).astype(F32).T
    lane = lax.broadcasted_iota(jnp.int32, (2 * WIN, LANES), 1)
    vrow = lax.broadcasted_iota(jnp.int32, (LANES, 2 * WIN), 0)
    kj = lax.broadcasted_iota(jnp.int32, (2 * WIN, WIN), 0)
    qi = lax.broadcasted_iota(jnp.int32, (2 * WIN, WIN), 1)
    valid = (kj > qi) & (kj <= qi + WIN) & ((kj >= WIN) | (blk > 0))
    outs = []

    def scores(h):
        kz = jnp.where((lane >= h * DH) & (lane < (h + 1) * DH), k2, jnp.zeros_like(k2))
        return lax.dot_general(kz, q2, NT, preferred_element_type=F32)

    def attend(h, s):
        vz = jnp.where((vrow >= h * DH) & (vrow < (h + 1) * DH), v2t, 0.0).astype(BF16)
        ps, invs = [], []
        for g in range(G):
            sg = jnp.where(valid, s[:, g * WIN:(g + 1) * WIN], NEG)
            sink = sinks_ref[h * G + g] * LOG2E
            m = jnp.maximum(jnp.max(sg, axis=0, keepdims=True), sink)
            p = jnp.exp2(sg - m)
            l = jnp.sum(p, axis=0, keepdims=True) + jnp.exp2(sink - m)
            ps.append(p.astype(BF16))
            invs.append(1.0 / l)
        o_h = jnp.dot(vz, jnp.concatenate(ps, axis=1), preferred_element_type=F32)
        outs.append(o_h * jnp.concatenate(invs, axis=1))

    _pipelined(HKV, scores, attend)
    acc = outs[0] + outs[1]
    for g in range(G):
        o_ref[:, g * LANES:(g + 1) * LANES] = acc[:, g * WIN:(g + 1) * WIN].T.astype(BF16)


def _swa(sinks, qa, ka, proj, b, s):
    nb = s // WIN
    cur = lambda cb: pl.BlockSpec((WIN, LANES), lambda bi, n, sk: (bi * nb + n, cb))
    prev = lambda cb: pl.BlockSpec((WIN, LANES), lambda bi, n, sk: (bi * nb + jnp.maximum(n - 1, 0), cb))
    grid_spec = pltpu.PrefetchScalarGridSpec(
        num_scalar_prefetch=1,
        grid=(b, nb),
        in_specs=[
            pl.BlockSpec((WIN, HQ * DH), lambda bi, n, sk: (bi * nb + n, 0)),
            cur(0), prev(0), cur(COL_V // LANES), prev(COL_V // LANES),
        ],
        out_specs=pl.BlockSpec((WIN, HQ * DH), lambda bi, n, sk: (bi * nb + n, 0)),
    )
    return pl.pallas_call(
        _swa_kernel,
        grid_spec=grid_spec,
        out_shape=jax.ShapeDtypeStruct((b * s, HQ * DH), BF16),
        compiler_params=_params(("arbitrary", "arbitrary")),
        name="swa_attention",
    )(sinks, qa, ka, ka, proj, proj)


def _mla_kernel(q_ref, k_ref, v_ref, o_ref, *, seq, tq):
    row = lax.broadcasted_iota(jnp.int32, (tq, tq), 0)
    col = lax.broadcasted_iota(jnp.int32, (tq, tq), 1)
    causal = col <= row
    nq = seq // tq

    def scores(i):
        kv = (i + 1) * tq
        q = q_ref[i * tq:(i + 1) * tq, :]
        s = lax.dot_general(q, k_ref[0:kv, :], NT, preferred_element_type=F32)
        diag = jnp.where(causal, s[:, kv - tq:kv], NEG)
        return diag if i == 0 else jnp.concatenate([s[:, 0:kv - tq], diag], axis=1)

    def finish(i, s):
        kv = (i + 1) * tq
        m = jnp.max(s, axis=-1, keepdims=True)
        p = jnp.exp2(s - m).astype(BF16)
        ov = jnp.dot(p, v_ref[0:kv, :], preferred_element_type=F32)
        o_ref[i * tq:(i + 1) * tq, :] = (ov[:, 0:DV] * (1.0 / ov[:, DV:DV + 1])).astype(BF16)

    _pipelined(nq, scores, finish)


def _mla(qm, km, vm, b, s):
    tq = min(s, 256)
    return pl.pallas_call(
        functools.partial(_mla_kernel, seq=s, tq=tq),
        grid=(b, MH),
        in_specs=[
            pl.BlockSpec((s, 2 * LANES), lambda bi, h: (bi, h)),
            pl.BlockSpec((s, 2 * LANES), lambda bi, h: (bi, h)),
            pl.BlockSpec((s, 2 * LANES), lambda bi, h: (bi, h)),
        ],
        out_specs=pl.BlockSpec((s, DV), lambda bi, h: (bi, h)),
        out_shape=jax.ShapeDtypeStruct((b * s, MH * DV), BF16),
        compiler_params=_params(("arbitrary", "arbitrary")),
        name="mla_attention",
    )(qm, km, vm)


def _mix_kernel(oa_ref, ob_ref, ga_ref, gb_ref, wa_ref, wb_ref, ba_ref, bb_ref, o_ref):
    ya = jnp.dot(oa_ref[...], wa_ref[...], preferred_element_type=F32)
    yb = jnp.dot(ob_ref[...], wb_ref[...], preferred_element_type=F32)
    ga = jax.nn.sigmoid(ga_ref[...].astype(F32) + ba_ref[...])
    gb = jax.nn.sigmoid(gb_ref[...].astype(F32) + bb_ref[...])
    o_ref[...] = (ga * ya + gb * yb).astype(BF16)


def _mix(out_a, out_b, proj, wa, wb, b_gate, n):
    tm = min(n, 1024)
    tn = 1024
    nj = D // tn
    return pl.pallas_call(
        _mix_kernel,
        grid=(n // tm, nj),
        in_specs=[
            pl.BlockSpec((tm, HQ * DH), lambda i, j: (i, 0)),
            pl.BlockSpec((tm, MH * DV), lambda i, j: (i, 0)),
            pl.BlockSpec((tm, tn), lambda i, j: (i, j)),
            pl.BlockSpec((tm, tn), lambda i, j: (i, nj + j)),
            pl.BlockSpec((HQ * DH, tn), lambda i, j: (0, j)),
            pl.BlockSpec((MH * DV, tn), lambda i, j: (0, j)),
            pl.BlockSpec((1, tn), lambda i, j: (0, j)),
            pl.BlockSpec((1, tn), lambda i, j: (0, nj + j)),
        ],
        out_specs=pl.BlockSpec((tm, tn), lambda i, j: (i, j)),
        out_shape=jax.ShapeDtypeStruct((n, D), BF16),
        compiler_params=_params(("arbitrary", "arbitrary")),
        name="gated_mix",
    )(out_a, out_b, proj, proj, wa, wb, b_gate, b_gate)


def _router_kernel(mx_ref, x_ref, wo_ref, gf_ref, wr_ref, br_ref,
                   h_ref, hn_ref, te_ref, tw_ref, rk_ref, cnt_ref, tb_ref, base_ref, *, tm, sub):
    @pl.when(pl.program_id(0) == 0)
    def _():
        base_ref[...] = jnp.zeros_like(base_ref)

    h = x_ref[...] + jnp.dot(mx_ref[...], wo_ref[...], preferred_element_type=F32)
    h_ref[...] = h
    hn = _rms(h, gf_ref[...])
    hn_ref[...] = hn.astype(BF16)

    hn_top = _bf16_part(hn)
    hn_hi = hn_top.astype(BF16)
    hn_lo = (hn - hn_top).astype(BF16)
    logits = (lax.dot_general(wr_ref[0], hn_hi, NT, preferred_element_type=F32)
              + lax.dot_general(wr_ref[0], hn_lo, NT, preferred_element_type=F32)
              + lax.dot_general(wr_ref[1], hn_hi, NT, preferred_element_type=F32)) + br_ref[...]
    eidx = lax.broadcasted_iota(jnp.int32, (E, tm), 0)
    vals, idxs = [], []
    l = logits
    for _ in range(TOPK):
        m = jnp.max(l, axis=0, keepdims=True)
        idx = jnp.min(jnp.where(l == m, eidx, E), axis=0, keepdims=True)
        vals.append(m)
        idxs.append(idx)
        l = jnp.where(eidx == idx, -jnp.inf, l)
    ex = [jnp.exp(v - vals[0]) for v in vals]
    tot = ex[0] + ex[1] + ex[2] + ex[3]
    tw_ref[...] = jnp.concatenate([e / tot for e in ex], axis=0)
    te_ref[...] = jnp.concatenate(idxs, axis=0)

    onehot = jnp.zeros((E, tm), F32)
    for idx in idxs:
        onehot = onehot + jnp.where(eidx == idx, 1.0, 0.0)
    r = lax.broadcasted_iota(jnp.int32, (tm, tm), 0)
    c = lax.broadcasted_iota(jnp.int32, (tm, tm), 1)
    earlier = jnp.where(r < c, 1.0, 0.0).astype(BF16)
    before = jnp.dot(onehot.astype(BF16), earlier, preferred_element_type=F32) + base_ref[:, 0:1]
    rk_ref[...] = jnp.concatenate(
        [jnp.sum(jnp.where(eidx == idx, before, 0.0), axis=0, keepdims=True) for idx in idxs],
        axis=0).astype(jnp.int32)
    lane = lax.broadcasted_iota(jnp.int32, (E, LANES), 1)
    tb = jnp.zeros((E, LANES), F32)
    for s in range(tm // sub):
        tb = jnp.where(lane == s, before[:, s * sub:s * sub + 1], tb)
    tb_ref[...] = tb
    base_ref[...] = base_ref[...] + jnp.sum(onehot, axis=1, keepdims=True)
    cnt_ref[...] = base_ref[...]


def _router(mixed, x2, wo, g_ffn, wr_t, b_router, n, tm, sub):
    row = lambda w: pl.BlockSpec((tm, w), lambda i: (i, 0))
    full = lambda a: pl.BlockSpec(a.shape, lambda i: (0,) * a.ndim, pipeline_mode=pl.Buffered(1))
    tok = pl.BlockSpec((TOPK, tm), lambda i: (0, i))
    return pl.pallas_call(
        functools.partial(_router_kernel, tm=tm, sub=sub),
        grid=(n // tm,),
        in_specs=[row(D), row(D), full(wo), full(g_ffn), full(wr_t), full(b_router)],
        out_specs=[
            row(D), row(D), tok, tok, tok,
            pl.BlockSpec((E, LANES), lambda i: (0, 0)), pl.BlockSpec((E, LANES), lambda i: (i, 0)),
        ],
        out_shape=[
            jax.ShapeDtypeStruct((n, D), F32), jax.ShapeDtypeStruct((n, D), BF16),
            jax.ShapeDtypeStruct((TOPK, n), jnp.int32), jax.ShapeDtypeStruct((TOPK, n), F32),
            jax.ShapeDtypeStruct((TOPK, n), jnp.int32), jax.ShapeDtypeStruct((E, LANES), F32),
            jax.ShapeDtypeStruct((n // tm * E, LANES), F32),
        ],
        scratch_shapes=[pltpu.VMEM((E, LANES), F32)],
        compiler_params=_params(("arbitrary",)),
        name="outproj_router",
    )(mixed, x2, wo, g_ffn, wr_t, b_router)


def _slot_kernel(pstart_ref, shift_ref, te_ref, rk_ref, slot_ref, *, tt, group):
    for g in range(group):
        tile = pl.program_id(0) * group + g
        te = te_ref[:, g * tt:(g + 1) * tt]
        s = rk_ref[:, g * tt:(g + 1) * tt]
        for e in range(E):
            s = s + jnp.where(te == e, pstart_ref[e] - shift_ref[tile, e], 0)
        slot_ref[:, g * tt:(g + 1) * tt] = s


def _slots(pstart, shift, top_e, rank, n, tt):
    group = min(8, n // tt)
    tok = pl.BlockSpec((TOPK, group * tt), lambda i, ps, sh: (0, i))
    grid_spec = pltpu.PrefetchScalarGridSpec(
        num_scalar_prefetch=2, grid=(n // (group * tt),), in_specs=[tok, tok], out_specs=tok)
    return pl.pallas_call(
        functools.partial(_slot_kernel, tt=tt, group=group), grid_spec=grid_spec,
        out_shape=jax.ShapeDtypeStruct((TOPK, n), jnp.int32), name="staging_slots",
    )(pstart, shift, top_e, rank)


def _prep_dispatch_kernel(pdst_ref, ntot_ref, nwr_ref, pb_ref, np_ref, ff_ref, lf_ref, cnt_ref, pad_ref, pst_ref,
                          slot_ref, hn_ref, wgu_ref, wdn_ref, g_ref, u_ref, d_ref, xg_ref,
                          stage_ref, carry_ref, sem, fsem, *, tt, n_disp, kcat):
    i = pl.program_id(0)

    def stage_piece(row):
        return stage_ref.at[pl.ds(pl.multiple_of(row, PIECE), PIECE), :]

    def out_piece(row):
        return xg_ref.at[pl.ds(pl.multiple_of(row, PIECE), PIECE), :]

    @pl.when(i == 0)
    def _():
        carry_ref[...] = jnp.zeros_like(carry_ref)

    @pl.when(i < n_disp)
    def _():
        slot = slot_ref[...]
        chunk = 512
        for r0 in range(0, kcat, chunk):
            row = lax.broadcasted_iota(jnp.int32, (chunk, tt), 0) + r0
            hit = row == slot[0:1, :]
            for k in range(1, TOPK):
                hit = hit | (row == slot[k:k + 1, :])
            sel = jnp.where(hit, 1.0, 0.0).astype(BF16)
            stage_ref[r0:r0 + chunk, :] = jnp.dot(sel, hn_ref[...], preferred_element_type=F32).astype(BF16)

        def per_expert(e, c):
            pieces = np_ref[i, e]

            @pl.when(pieces > 0)
            def _():
                first = pb_ref[i, e]
                last = first + (pieces - 1) * PIECE

                @pl.when(ff_ref[i, e] == 1)
                def _():
                    stage_piece(first)[...] = stage_piece(first)[...] + carry_ref[e]

                @pl.when(lf_ref[i, e] == 1)
                def _():
                    carry_ref[e] = stage_piece(last)[...]

            return c

        lax.fori_loop(0, E, per_expert, 0)

        def start(q, c):
            dst = pdst_ref[i, q]

            @pl.when(dst >= 0)
            def _():
                pltpu.make_async_copy(stage_piece(q * PIECE), out_piece(dst), sem).start()

            return c

        lax.fori_loop(0, ntot_ref[i], start, 0)

    chunk = 2 * LANES
    r = lax.broadcasted_iota(jnp.int32, (chunk, chunk), 0)
    c = lax.broadcasted_iota(jnp.int32, (chunk, chunk), 1)
    src = jnp.where(c < LANES, 2 * c, 2 * (c - LANES) + 1)
    perm = jnp.where(r == src, 1.0, 0.0).astype(BF16)
    for k in range(wgu_ref.shape[1] // chunk):
        w = wgu_ref[:, k * chunk:(k + 1) * chunk].astype(BF16)
        o = jnp.dot(w, perm, preferred_element_type=F32)
        g_ref[:, k * LANES:(k + 1) * LANES] = o[:, :LANES].astype(BF16)
        u_ref[:, k * LANES:(k + 1) * LANES] = o[:, LANES:].astype(BF16)
    d_ref[...] = wdn_ref[...].astype(BF16)

    @pl.when(i < n_disp)
    def _():
        def wait(q, c):
            pltpu.make_async_copy(stage_piece(0), out_piece(0), sem).wait()
            return c

        lax.fori_loop(0, nwr_ref[i], wait, 0)

    @pl.when(i == n_disp - 1)
    def _():
        stage_ref[0:PIECE, :] = jnp.zeros((PIECE, D), BF16)

        def per_expert(e, c):
            real = cnt_ref[e]
            tail = real % PIECE
            base = pst_ref[e]

            @pl.when(tail != 0)
            def _():
                cp = pltpu.make_async_copy(carry_ref.at[e], out_piece(base + real - tail), fsem)
                cp.start()
                cp.wait()

            zero_from = (real + PIECE - 1) // PIECE
            zero_to = pad_ref[e] // PIECE

            def zstart(z, c2):
                pltpu.make_async_copy(stage_piece(0), out_piece(base + z * PIECE), fsem).start()
                return c2

            def zwait(z, c2):
                pltpu.make_async_copy(stage_piece(0), out_piece(base + z * PIECE), fsem).wait()
                return c2

            lax.fori_loop(zero_from, zero_to, zstart, 0)
            lax.fori_loop(zero_from, zero_to, zwait, 0)
            return c

        lax.fori_loop(0, E, per_expert, 0)


def _prep_dispatch(tables, counts, padded, pstart, slot, hn, w_gate_up, w_down, n, p_rows, tt, kcat):
    tr = 512
    steps = E * (D // tr)
    n_disp = n // tt
    assert n_disp <= steps, "more token tiles than weight tiles"
    nsp = len(tables) + 3
    tile = lambda i, *_: (i // (D // tr), i % (D // tr), 0)
    tok = lambda i: jnp.minimum(i, n_disp - 1)
    wout = pl.BlockSpec((None, tr, DE), tile)
    grid_spec = pltpu.PrefetchScalarGridSpec(
        num_scalar_prefetch=nsp, grid=(steps,),
        in_specs=[
            pl.BlockSpec((TOPK, tt), lambda i, *_: (0, tok(i))),
            pl.BlockSpec((tt, D), lambda i, *_: (tok(i), 0)),
            pl.BlockSpec((None, tr, 2 * DE), tile),
            pl.BlockSpec((None, tr, D), tile),
        ],
        out_specs=[wout, wout, pl.BlockSpec((None, tr, D), tile), pl.BlockSpec(memory_space=pl.ANY)],
        scratch_shapes=[pltpu.VMEM((kcat, D), BF16), pltpu.VMEM((E, PIECE, D), BF16),
                        pltpu.SemaphoreType.DMA(()), pltpu.SemaphoreType.DMA(())],
    )
    return pl.pallas_call(
        functools.partial(_prep_dispatch_kernel, tt=tt, n_disp=n_disp, kcat=kcat), grid_spec=grid_spec,
        out_shape=[jax.ShapeDtypeStruct((E, D, DE), BF16), jax.ShapeDtypeStruct((E, D, DE), BF16),
                   jax.ShapeDtypeStruct((E, DE, D), BF16), jax.ShapeDtypeStruct((p_rows, D), BF16)],
        compiler_params=pltpu.CompilerParams(dimension_semantics=("arbitrary",), has_side_effects=True,
                                             vmem_limit_bytes=VMEM_LIMIT),
        name="weight_prep_dispatch",
    )(*tables, counts, padded, pstart, slot, hn, w_gate_up, w_down)


def _moe_kernel(be_ref, nu_ref, bv_ref, x_ref, wg_ref, wu_ref, wd_ref, bg_ref, bu_ref, bd_ref, y_ref,
                *, bm, sub, dsub):
    valid = bv_ref[pl.program_id(0)]

    def body(rows):
        x = x_ref[0:rows, :]
        acts = []

        def gate_up(s):
            sl = slice(s * sub, (s + 1) * sub)
            return (jnp.dot(x, wg_ref[:, sl], preferred_element_type=F32) + bg_ref[:, sl],
                    jnp.dot(x, wu_ref[:, sl], preferred_element_type=F32) + bu_ref[:, sl])

        def activate(s, gu):
            gate = jnp.minimum(gu[0], LIMIT)
            up = jnp.clip(gu[1], -LIMIT, LIMIT)
            acts.append((gate * jax.nn.sigmoid(ALPHA * gate) * (up + 1.0)).astype(BF16))

        _pipelined(DE // sub, gate_up, activate)
        act = jnp.concatenate(acts, axis=1)

        def down(c):
            return jnp.dot(act, wd_ref[:, c * dsub:(c + 1) * dsub], preferred_element_type=F32)

        def emit(c, part):
            sl = slice(c * dsub, (c + 1) * dsub)
            y_ref[0:rows, sl] = (part + bd_ref[:, sl]).astype(BF16)

        _pipelined(D // dsub, down, emit)
        if rows < bm:
            y_ref[rows:bm, :] = jnp.zeros((bm - rows, D), BF16)

    @pl.when(valid > bm // 2)
    def _():
        body(bm)

    @pl.when((valid > 0) & (valid <= bm // 2))
    def _():
        body(bm // 2)


def _moe(block_e, n_used, block_valid, xg, wg, wu, wd, bg, bu, bd, n_blocks, bm):
    def expert(blk, be, nu, bv):
        return (be[blk], 0, 0)

    def rows(blk, be, nu, bv):
        return (jnp.minimum(blk, nu[0] - 1), 0)

    single = lambda shape: pl.BlockSpec(shape, expert, pipeline_mode=pl.Buffered(1))
    double = lambda shape: pl.BlockSpec(shape, expert)
    grid_spec = pltpu.PrefetchScalarGridSpec(
        num_scalar_prefetch=3, grid=(n_blocks,),
        in_specs=[
            pl.BlockSpec((bm, D), rows),
            double((None, D, DE)), single((None, D, DE)), double((None, DE, D)),
            double((None, 1, DE)), double((None, 1, DE)), double((None, 1, D)),
        ],
        out_specs=pl.BlockSpec((bm, D), rows),
    )
    return pl.pallas_call(
        functools.partial(_moe_kernel, bm=bm, sub=512, dsub=256), grid_spec=grid_spec,
        out_shape=jax.ShapeDtypeStruct((n_blocks * bm, D), BF16),
        compiler_params=_params(("arbitrary",)),
        name="moe_experts",
    )(block_e, n_used, block_valid, xg, wg, wu, wd, bg, bu, bd)


def _combine_kernel(src_ref, tot_ref, y_ref, h_ref, slot_ref, tw_ref, gf_ref, o_ref, ycat_ref, sem,
                    *, tt, nt, kcat):
    i = pl.program_id(0)
    cur = i % 2

    def for_pieces(tile, s, fn):
        def body(q, c):
            fn(pltpu.make_async_copy(
                y_ref.at[pl.ds(pl.multiple_of(src_ref[tile, q], PIECE), PIECE), :],
                ycat_ref.at[s, pl.ds(pl.multiple_of(q * PIECE, PIECE), PIECE), :], sem.at[s]))
            return c

        lax.fori_loop(0, tot_ref[tile], body, 0)

    @pl.when(i == 0)
    def _():
        ycat_ref[...] = jnp.zeros_like(ycat_ref)
        for_pieces(0, 0, lambda cp: cp.start())

    @pl.when(i + 1 < nt)
    def _():
        for_pieces(i + 1, 1 - cur, lambda cp: cp.start())

    for_pieces(i, cur, lambda cp: cp.wait())

    def combine(k_rows):
        slot = slot_ref[...]
        tw = tw_ref[...]
        col = lax.broadcasted_iota(jnp.int32, (tt, k_rows), 1)
        sel = jnp.zeros((tt, k_rows), F32)
        for k in range(TOPK):
            sel = jnp.where(col == slot[:, k:k + 1], tw[:, k:k + 1], sel)
        moe = jnp.dot(sel.astype(BF16), ycat_ref[cur, 0:k_rows, :], preferred_element_type=F32)
        o_ref[...] = _rms(h_ref[...] + moe, gf_ref[...])

    k_short = kcat * 3 // 4
    used = tot_ref[i] * PIECE

    @pl.when(used <= k_short)
    def _():
        combine(k_short)

    @pl.when(used > k_short)
    def _():
        combine(kcat)


def _combine(src, total, y, h, slot, tw, g_final, n, tt, kcat):
    nt = n // tt
    grid_spec = pltpu.PrefetchScalarGridSpec(
        num_scalar_prefetch=2, grid=(nt,),
        in_specs=[
            pl.BlockSpec(memory_space=pl.ANY),
            pl.BlockSpec((tt, D), lambda i, a, b: (i, 0)),
            pl.BlockSpec((tt, TOPK), lambda i, a, b: (i, 0)),
            pl.BlockSpec((tt, TOPK), lambda i, a, b: (i, 0)),
            pl.BlockSpec((1, D), lambda i, a, b: (0, 0)),
        ],
        out_specs=pl.BlockSpec((tt, D), lambda i, a, b: (i, 0)),
        scratch_shapes=[pltpu.VMEM((2, kcat, D), BF16), pltpu.SemaphoreType.DMA((2,))],
    )
    return pl.pallas_call(
        functools.partial(_combine_kernel, tt=tt, nt=nt, kcat=kcat), grid_spec=grid_spec,
        out_shape=jax.ShapeDtypeStruct((n, D), F32),
        compiler_params=_params(("arbitrary",)),
        name="combine_final_norm",
    )(src, total, y, h, slot, tw, g_final)


def _permute_weights(w_in, w_uq, w_ukv, w_o_swa):
    widths = (HQ * DH, HKV * DH, HKV * DH, QR, KVR, DR, 2 * D)
    offs = [0]
    for w in widths:
        offs.append(offs[-1] + w)
    wq, wk, wv, wdq, wdkv, wkr, wgates = [w_in[:, offs[i]:offs[i + 1]] for i in range(7)]
    wq = wq.reshape(D, HKV, G, DH).transpose(0, 2, 1, 3).reshape(D, HQ * DH)
    pad = jnp.zeros((D, PROJ_COLS - COL_KR - DR), w_in.dtype)
    w_in_p = jnp.concatenate([wgates, wq, wdq, wdkv, wk, wv, wkr, pad], axis=1).astype(BF16)
    wo_a = w_o_swa.reshape(HKV, G, DH, D).transpose(1, 0, 2, 3).reshape(HQ * DH, D).astype(BF16)
    wuq = w_uq.reshape(QR, MH, DN + DR)
    wq_p = jnp.concatenate([wuq, jnp.zeros((QR, MH, 2 * LANES - DN - DR), w_uq.dtype)], axis=2)
    wq_p = wq_p.reshape(QR, MH * 2 * LANES).astype(BF16)
    wukv = w_ukv.reshape(KVR, MH, DN + DV)
    wk_p = wukv[:, :, :DN].reshape(KVR, MH * DN).astype(BF16)
    wv_p = wukv[:, :, DN:].reshape(KVR, MH * DV).astype(BF16)
    return w_in_p, wo_a, wq_p, wk_p, wv_p


def kernel(x, positions, g_attn, w_in, b_gate, sinks, g_q, w_uq, g_kv, w_ukv, w_o_swa, w_o_mla, w_out,
           g_ffn, w_router, b_router, w_gate_up, b_gate_up, w_down, b_down, g_final):
    b, s, _ = x.shape
    n = b * s
    assert w_in.shape[0] == 1, "single-layer block"
    x2 = x.reshape(n, D)

    w_in_p, wo_a, wq_p, wk_p, wv_p = _permute_weights(w_in[0], w_uq[0], w_ukv[0], w_o_swa[0])
    cos, sin = _rope_tables(positions, n)
    proj = _in_proj(x2, g_attn, w_in_p, n)
    qa, ka, qm, km, vm = _prep(proj, cos, sin, g_q, g_kv, wq_p, wk_p, wv_p, n)
    out_a = _swa(sinks[0], qa, ka, proj, b, s)
    out_b = _mla(qm, km, vm, b, s)

    mixed = _mix(out_a, out_b, proj, wo_a, w_o_mla[0].astype(BF16), b_gate, n)
    wr_t = w_router[0].T
    wr_top = _bf16_part(wr_t)
    wr_split = jnp.stack([wr_top.astype(BF16), (wr_t - wr_top).astype(BF16)])
    tm = min(n, 512)
    tt = min(n, 256)
    h, hn, top_e, top_w, rank, cnt, tile_base = _router(
        mixed, x2, w_out[0].astype(BF16), g_ffn, wr_split, b_router[0][:, None], n, tm, tt)

    bm = 512
    n_blocks = -(-(n * TOPK) // bm) + E
    counts = cnt[:, 0].astype(jnp.int32)
    padded = (counts + bm - 1) // bm * bm
    pend = jnp.cumsum(padded)
    pstart = pend - padded
    n_used = pend[-1] // bm
    blocks = jnp.arange(n_blocks, dtype=jnp.int32)
    block_e = jnp.minimum(jnp.sum(pend[None, :] <= (blocks * bm)[:, None], axis=1), E - 1).astype(jnp.int32)
    block_valid = jnp.clip(counts[block_e] - (blocks * bm - pstart[block_e]), 0, bm).astype(jnp.int32)
    block_valid = jnp.where(blocks < n_used, block_valid, 0)
    block_e = jnp.where(blocks < n_used, block_e, block_e[jnp.maximum(n_used - 1, 0)])

    nsub = tm // tt
    before = tile_base.reshape(n // tm, E, LANES)[:, :, :nsub].transpose(0, 2, 1).reshape(n // tt, E)
    before = before.astype(jnp.int32)
    after = jnp.concatenate([before[1:], counts[None, :]], axis=0)
    first = pstart[None, :] + before
    astart = first // PIECE * PIECE
    npieces = jnp.where(after > before, (first + (after - before) - astart + PIECE - 1) // PIECE, 0)
    cum = jnp.cumsum(npieces, axis=1)
    pbase = cum - npieces
    kcat = -(-(tt * TOPK + E * 2 * (PIECE - 1)) // (2 * LANES)) * (2 * LANES)
    shift = (astart - pbase * PIECE).astype(jnp.int32)
    q = jnp.arange(kcat // PIECE, dtype=jnp.int32)
    owner = jnp.sum(cum[:, None, :] <= q[None, :, None], axis=2)
    own = owner[:, :, None] == jnp.arange(E, dtype=jnp.int32)[None, None, :]
    piece_src = jnp.sum(jnp.where(own, shift[:, None, :], 0), axis=2) + q[None, :] * PIECE
    piece_src = jnp.where(owner < E, piece_src, 0).astype(jnp.int32)
    total = cum[:, -1].astype(jnp.int32)
    last = first + (after - before)
    starts_inside = ((npieces > 0) & (first % PIECE != 0)).astype(jnp.int32)
    ends_inside = ((npieces > 0) & (last % PIECE != 0)).astype(jnp.int32)
    held = jnp.sum(jnp.where(own & (q[None, :, None] == cum[:, None, :] - 1), ends_inside[:, None, :], 0), axis=2)
    piece_dst = jnp.where((owner < E) & (held == 0), piece_src, -1).astype(jnp.int32)
    written = (total - jnp.sum(ends_inside, axis=1)).astype(jnp.int32)
    tables = (piece_dst, total, written, (pbase * PIECE).astype(jnp.int32), npieces.astype(jnp.int32),
              starts_inside, ends_inside)

    slot = _slots(pstart, shift, top_e, rank, n, tt)
    wg, wu, wd, xg = _prep_dispatch(tables, counts, padded, pstart, slot, hn, w_gate_up[0], w_down[0], n,
                                    n_blocks * bm, tt, kcat)
    bgu = b_gate_up[0]
    y = _moe(block_e, n_used.reshape(1).astype(jnp.int32), block_valid, xg, wg, wu, wd,
             bgu[:, None, 0::2], bgu[:, None, 1::2], b_down[0][:, None, :], n_blocks, bm)
    out = _combine(piece_src, total, y, h, slot.T, top_w.T, g_final[None, :], n, tt, kcat)
    return out.reshape(b, s, D)
```

```python
import functools

import jax
import jax.numpy as jnp
from jax import lax
from jax.experimental import pallas as pl
from jax.experimental.pallas import tpu as pltpu

F32 = jnp.float32
BF16 = jnp.bfloat16
U32 = jnp.uint32

D = 2048
HQ, HKV, DH, WIN = 16, 2, 64, 128
G = HQ // HKV
MH, QR, KVR, DN, DR, DV = 8, 512, 256, 128, 64, 128
THETA = 10000.0
E, TOPK, DE = 32, 4, 2048
LIMIT, ALPHA = 7.0, 1.702
EPS = 1e-6
NEG = -1e30
LOG2E = 1.4426950408889634

LANES = 128
PIECE = 16
VMEM_LIMIT = 56 * 1024 * 1024

PROJ_COLS = 6400
COL_GATES, COL_Q, COL_DQ, COL_DKV, COL_K, COL_V, COL_KR = 0, 4096, 5120, 5632, 5888, 6016, 6144

NT = (((1,), (1,)), ((), ()))


def _params(sem, vmem=VMEM_LIMIT):
    return pltpu.CompilerParams(dimension_semantics=sem, vmem_limit_bytes=vmem)


def _rms(x, g):
    return x * lax.rsqrt(jnp.mean(x * x, axis=-1, keepdims=True) + EPS) * g


def _bf16_part(x):
    bits = lax.bitcast_convert_type(x, U32) & jnp.uint32(0xFFFF0000)
    return lax.bitcast_convert_type(bits, F32)


def _pipelined(n, produce, consume):
    nxt = produce(0)
    for i in range(n):
        cur = nxt
        if i + 1 < n:
            nxt = produce(i + 1)
        consume(i, cur)


def _rope128(v, cos, sin):
    lane = lax.broadcasted_iota(jnp.int32, v.shape, 1)
    rot = jnp.where((lane % DH) < (DH // 2), -pltpu.roll(v, LANES - DH // 2, 1), pltpu.roll(v, DH // 2, 1))
    return v * cos + rot * sin


def _rope_table_kernel(pos_ref, inv_ref, cos_ref, sin_ref):
    ang = pos_ref[...].astype(F32) * inv_ref[...]
    cos_ref[...] = jnp.cos(ang)
    sin_ref[...] = jnp.sin(ang)


def _rope_tables(positions, n):
    half = DH // 2
    per_row = LANES // half
    inv = jnp.power(THETA, -jnp.arange(half, dtype=F32) * 2.0 / DH)
    inv_row = jnp.tile(inv, per_row)[None, :]
    pos_rows = jnp.repeat(positions.reshape(n // per_row, per_row), half, axis=1)
    rows = n // per_row
    tr = min(rows, 1024)
    cos, sin = pl.pallas_call(
        _rope_table_kernel,
        grid=(rows // tr,),
        in_specs=[pl.BlockSpec((tr, LANES), lambda i: (i, 0)), pl.BlockSpec((1, LANES), lambda i: (0, 0))],
        out_specs=[pl.BlockSpec((tr, LANES), lambda i: (i, 0))] * 2,
        out_shape=[jax.ShapeDtypeStruct((rows, LANES), F32)] * 2,
        name="rope_tables",
    )(pos_rows, inv_row)
    cos = jnp.tile(cos.reshape(n, half), (1, per_row))
    sin = jnp.tile(sin.reshape(n, half), (1, per_row))
    return cos, sin


def _in_proj_kernel(x_ref, g_ref, w_ref, o_ref, xn_ref):
    @pl.when(pl.program_id(1) == 0)
    def _():
        xn_ref[...] = _rms(x_ref[...], g_ref[...]).astype(BF16)

    o_ref[...] = jnp.dot(xn_ref[...], w_ref[...], preferred_element_type=F32).astype(o_ref.dtype)


def _in_proj(x2, g_attn, w_in_p, n):
    tm = min(n, 1024)
    tn = 1280
    return pl.pallas_call(
        _in_proj_kernel,
        grid=(n // tm, PROJ_COLS // tn),
        in_specs=[
            pl.BlockSpec((tm, D), lambda i, j: (i, 0)),
            pl.BlockSpec((1, D), lambda i, j: (0, 0)),
            pl.BlockSpec((D, tn), lambda i, j: (0, j)),
        ],
        out_specs=pl.BlockSpec((tm, tn), lambda i, j: (i, j)),
        out_shape=jax.ShapeDtypeStruct((n, PROJ_COLS), BF16),
        scratch_shapes=[pltpu.VMEM((tm, D), BF16)],
        compiler_params=_params(("arbitrary", "arbitrary")),
        name="in_proj",
    )(x2, g_attn, w_in_p)


def _prep_kernel(q_ref, dq_ref, dkv_ref, k_ref, kr_ref, cos_ref, sin_ref, gq_ref, gkv_ref,
                 wq_ref, wk_ref, wv_ref, qa_ref, ka_ref, qm_ref, km_ref, vm_ref):
    cos = cos_ref[...]
    sin = sin_ref[...]
    swa_scale = DH ** -0.5 * LOG2E
    for c in range(HQ * DH // LANES):
        sl = slice(c * LANES, (c + 1) * LANES)
        qa_ref[:, sl] = (_rope128(q_ref[:, sl].astype(F32), cos, sin) * swa_scale).astype(BF16)
    ka_ref[...] = _rope128(k_ref[...].astype(F32), cos, sin).astype(BF16)

    mla_scale = (DN + DR) ** -0.5 * LOG2E
    cq = _rms(dq_ref[...].astype(F32), gq_ref[...]).astype(BF16)
    qb = jnp.dot(cq, wq_ref[...], preferred_element_type=F32)
    ckv = _rms(dkv_ref[...].astype(F32), gkv_ref[...]).astype(BF16)
    kn = jnp.dot(ckv, wk_ref[...], preferred_element_type=F32)
    vv = jnp.dot(ckv, wv_ref[...], preferred_element_type=F32)
    ones_col = jnp.where(lax.broadcasted_iota(jnp.int32, (vv.shape[0], LANES), 1) == 0, 1.0, 0.0).astype(BF16)
    kr = _rope128(kr_ref[...].astype(F32), cos, sin).astype(BF16)
    for h in range(MH):
        lo = h * 2 * LANES
        qm_ref[:, lo:lo + LANES] = (qb[:, lo:lo + LANES] * mla_scale).astype(BF16)
        qm_ref[:, lo + LANES:lo + 2 * LANES] = (
            _rope128(qb[:, lo + LANES:lo + 2 * LANES], cos, sin) * mla_scale).astype(BF16)
        km_ref[:, lo:lo + LANES] = kn[:, h * LANES:(h + 1) * LANES].astype(BF16)
        km_ref[:, lo + LANES:lo + 2 * LANES] = kr
        vm_ref[:, lo:lo + LANES] = vv[:, h * LANES:(h + 1) * LANES].astype(BF16)
        vm_ref[:, lo + LANES:lo + 2 * LANES] = ones_col


def _prep(proj, cos, sin, g_q, g_kv, wq_p, wk_p, wv_p, n):
    tm = min(n, 512)
    row = lambda w, cb: pl.BlockSpec((tm, w), lambda i: (i, cb))
    full = lambda a: pl.BlockSpec(a.shape, lambda i: (0, 0))
    return pl.pallas_call(
        _prep_kernel,
        grid=(n // tm,),
        in_specs=[
            row(HQ * DH, COL_Q // (HQ * DH)), row(QR, COL_DQ // QR), row(KVR, COL_DKV // KVR),
            row(LANES, COL_K // LANES), row(LANES, COL_KR // LANES),
            row(LANES, 0), row(LANES, 0), full(g_q), full(g_kv), full(wq_p), full(wk_p), full(wv_p),
        ],
        out_specs=[row(HQ * DH, 0), row(LANES, 0)] + [row(MH * 2 * LANES, 0)] * 3,
        out_shape=[jax.ShapeDtypeStruct((n, HQ * DH), BF16), jax.ShapeDtypeStruct((n, LANES), BF16)]
        + [jax.ShapeDtypeStruct((n, MH * 2 * LANES), BF16)] * 3,
        compiler_params=_params(("arbitrary",)),
        name="rope_mla_prep",
    )(proj, proj, proj, proj, proj, cos, sin, g_q, g_kv, wq_p, wk_p, wv_p)


def _swa_kernel(sinks_ref, q_ref, kc_ref, kp_ref, vc_ref, vp_ref, o_ref, *, nblk):
    first_blk = pl.program_id(1) * nblk
    k_all = jnp.concatenate([kp_ref[...], kc_ref[...]], axis=0)
    v_all_t = jnp.concatenate([vp_ref[...], vc_ref[...]], axis=0).astype(F32).T
    lane = lax.broadcasted_iota(jnp.int32, (2 * WIN, LANES), 1)
    vrow = lax.broadcasted_iota(jnp.int32, (LANES, 2 * WIN), 0)
    kj = lax.broadcasted_iota(jnp.int32, (2 * WIN, WIN), 0)
    qi = lax.broadcasted_iota(jnp.int32, (2 * WIN, WIN), 1)
    band = (kj > qi) & (kj <= qi + WIN)
    stages = [(blk, h) for blk in range(nblk) for h in range(HKV)]
    outs = []

    def scores(i):
        blk, h = stages[i]
        q = q_ref[blk * WIN:(blk + 1) * WIN, :]
        q2 = jnp.concatenate([q[:, g * LANES:(g + 1) * LANES] for g in range(G)], axis=0)
        k2 = k_all[blk * WIN:(blk + 2) * WIN]
        kz = jnp.where((lane >= h * DH) & (lane < (h + 1) * DH), k2, jnp.zeros_like(k2))
        return lax.dot_general(kz, q2, NT, preferred_element_type=F32)

    def attend(i, s):
        blk, h = stages[i]
        valid = band & ((kj >= WIN) | (first_blk + blk > 0))
        v2t = v_all_t[:, blk * WIN:(blk + 2) * WIN]
        vz = jnp.where((vrow >= h * DH) & (vrow < (h + 1) * DH), v2t, 0.0).astype(BF16)
        ps, invs = [], []
        for g in range(G):
            sg = jnp.where(valid, s[:, g * WIN:(g + 1) * WIN], NEG)
            sink = sinks_ref[h * G + g] * LOG2E
            m = jnp.maximum(jnp.max(sg, axis=0, keepdims=True), sink)
            p = jnp.exp2(sg - m)
            l = jnp.sum(p, axis=0, keepdims=True) + jnp.exp2(sink - m)
            ps.append(p.astype(BF16))
            invs.append(1.0 / l)
        o_h = jnp.dot(vz, jnp.concatenate(ps, axis=1), preferred_element_type=F32)
        outs.append(o_h * jnp.concatenate(invs, axis=1))

    _pipelined(len(stages), scores, attend)
    for blk in range(nblk):
        acc = outs[blk * HKV]
        for h in range(1, HKV):
            acc = acc + outs[blk * HKV + h]
        for g in range(G):
            o_ref[blk * WIN:(blk + 1) * WIN, g * LANES:(g + 1) * LANES] = (
                acc[:, g * WIN:(g + 1) * WIN].T.astype(BF16))


def _swa(sinks, qa, ka, proj, b, s):
    nb = s // WIN
    nblk = 4 if nb % 4 == 0 else 1
    ng = nb // nblk
    cur = lambda cb: pl.BlockSpec((nblk * WIN, LANES), lambda bi, n, sk: (bi * ng + n, cb))
    prev = lambda cb: pl.BlockSpec(
        (WIN, LANES), lambda bi, n, sk: (bi * nb + jnp.maximum(n * nblk - 1, 0), cb))
    grid_spec = pltpu.PrefetchScalarGridSpec(
        num_scalar_prefetch=1,
        grid=(b, ng),
        in_specs=[
            pl.BlockSpec((nblk * WIN, HQ * DH), lambda bi, n, sk: (bi * ng + n, 0)),
            cur(0), prev(0), cur(COL_V // LANES), prev(COL_V // LANES),
        ],
        out_specs=pl.BlockSpec((nblk * WIN, HQ * DH), lambda bi, n, sk: (bi * ng + n, 0)),
    )
    return pl.pallas_call(
        functools.partial(_swa_kernel, nblk=nblk),
        grid_spec=grid_spec,
        out_shape=jax.ShapeDtypeStruct((b * s, HQ * DH), BF16),
        compiler_params=_params(("arbitrary", "arbitrary")),
        name="swa_attention",
    )(sinks, qa, ka, ka, proj, proj)


def _mla_kernel(q_ref, k_ref, v_ref, o_ref, *, seq, tq):
    row = lax.broadcasted_iota(jnp.int32, (tq, tq), 0)
    col = lax.broadcasted_iota(jnp.int32, (tq, tq), 1)
    causal = col <= row
    nq = seq // tq

    def scores(i):
        kv = (i + 1) * tq
        q = q_ref[i * tq:(i + 1) * tq, :]
        s = lax.dot_general(q, k_ref[0:kv, :], NT, preferred_element_type=F32)
        diag = jnp.where(causal, s[:, kv - tq:kv], NEG)
        return diag if i == 0 else jnp.concatenate([s[:, 0:kv - tq], diag], axis=1)

    def finish(i, s):
        kv = (i + 1) * tq
        m = jnp.max(s, axis=-1, keepdims=True)
        p = jnp.exp2(s - m).astype(BF16)
        ov = jnp.dot(p, v_ref[0:kv, :], preferred_element_type=F32)
        o_ref[i * tq:(i + 1) * tq, :] = (ov[:, 0:DV] * (1.0 / ov[:, DV:DV + 1])).astype(BF16)

    _pipelined(nq, scores, finish)


def _mla(qm, km, vm, b, s):
    tq = min(s, 256)
    return pl.pallas_call(
        functools.partial(_mla_kernel, seq=s, tq=tq),
        grid=(b, MH),
        in_specs=[
            pl.BlockSpec((s, 2 * LANES), lambda bi, h: (bi, h)),
            pl.BlockSpec((s, 2 * LANES), lambda bi, h: (bi, h)),
            pl.BlockSpec((s, 2 * LANES), lambda bi, h: (bi, h)),
        ],
        out_specs=pl.BlockSpec((s, DV), lambda bi, h: (bi, h)),
        out_shape=jax.ShapeDtypeStruct((b * s, MH * DV), BF16),
        compiler_params=_params(("arbitrary", "arbitrary")),
        name="mla_attention",
    )(qm, km, vm)


def _mix_kernel(oa_ref, ob_ref, ga_ref, gb_ref, wa_ref, wb_ref, ba_ref, bb_ref, o_ref):
    ya = jnp.dot(oa_ref[...], wa_ref[...], preferred_element_type=F32)
    yb = jnp.dot(ob_ref[...], wb_ref[...], preferred_element_type=F32)
    ga = jax.nn.sigmoid(ga_ref[...].astype(F32) + ba_ref[...])
    gb = jax.nn.sigmoid(gb_ref[...].astype(F32) + bb_ref[...])
    o_ref[...] = (ga * ya + gb * yb).astype(BF16)


def _mix(out_a, out_b, proj, wa, wb, b_gate, n):
    tm = min(n, 1024)
    tn = 1024
    nj = D // tn
    return pl.pallas_call(
        _mix_kernel,
        grid=(n // tm, nj),
        in_specs=[
            pl.BlockSpec((tm, HQ * DH), lambda i, j: (i, 0)),
            pl.BlockSpec((tm, MH * DV), lambda i, j: (i, 0)),
            pl.BlockSpec((tm, tn), lambda i, j: (i, j)),
            pl.BlockSpec((tm, tn), lambda i, j: (i, nj + j)),
            pl.BlockSpec((HQ * DH, tn), lambda i, j: (0, j)),
            pl.BlockSpec((MH * DV, tn), lambda i, j: (0, j)),
            pl.BlockSpec((1, tn), lambda i, j: (0, j)),
            pl.BlockSpec((1, tn), lambda i, j: (0, nj + j)),
        ],
        out_specs=pl.BlockSpec((tm, tn), lambda i, j: (i, j)),
        out_shape=jax.ShapeDtypeStruct((n, D), BF16),
        compiler_params=_params(("arbitrary", "arbitrary")),
        name="gated_mix",
    )(out_a, out_b, proj, proj, wa, wb, b_gate, b_gate)


def _router_kernel(mx_ref, x_ref, wo_ref, gf_ref, wr_ref, br_ref,
                   h_ref, hn_ref, te_ref, tw_ref, rk_ref, cnt_ref, tb_ref, base_ref, *, tm, sub):
    @pl.when(pl.program_id(0) == 0)
    def _():
        base_ref[...] = jnp.zeros_like(base_ref)

    h = x_ref[...] + jnp.dot(mx_ref[...], wo_ref[...], preferred_element_type=F32)
    h_ref[...] = h
    hn = _rms(h, gf_ref[...])
    hn_ref[...] = hn.astype(BF16)

    hn_top = _bf16_part(hn)
    hn_hi = hn_top.astype(BF16)
    hn_lo = (hn - hn_top).astype(BF16)
    logits = (lax.dot_general(wr_ref[0], hn_hi, NT, preferred_element_type=F32)
              + lax.dot_general(wr_ref[0], hn_lo, NT, preferred_element_type=F32)
              + lax.dot_general(wr_ref[1], hn_hi, NT, preferred_element_type=F32)) + br_ref[...]
    eidx = lax.broadcasted_iota(jnp.int32, (E, tm), 0)
    vals, idxs = [], []
    l = logits
    for _ in range(TOPK):
        m = jnp.max(l, axis=0, keepdims=True)
        idx = jnp.min(jnp.where(l == m, eidx, E), axis=0, keepdims=True)
        vals.append(m)
        idxs.append(idx)
        l = jnp.where(eidx == idx, -jnp.inf, l)
    ex = [jnp.exp(v - vals[0]) for v in vals]
    tot = ex[0] + ex[1] + ex[2] + ex[3]
    tw_ref[...] = jnp.concatenate([e / tot for e in ex], axis=0)
    te_ref[...] = jnp.concatenate(idxs, axis=0)

    onehot = jnp.zeros((E, tm), F32)
    for idx in idxs:
        onehot = onehot + jnp.where(eidx == idx, 1.0, 0.0)
    r = lax.broadcasted_iota(jnp.int32, (tm, tm), 0)
    c = lax.broadcasted_iota(jnp.int32, (tm, tm), 1)
    earlier = jnp.where(r < c, 1.0, 0.0).astype(BF16)
    before = jnp.dot(onehot.astype(BF16), earlier, preferred_element_type=F32) + base_ref[:, 0:1]
    rk_ref[...] = jnp.concatenate(
        [jnp.sum(jnp.where(eidx == idx, before, 0.0), axis=0, keepdims=True) for idx in idxs],
        axis=0).astype(jnp.int32)
    lane = lax.broadcasted_iota(jnp.int32, (E, LANES), 1)
    tb = jnp.zeros((E, LANES), F32)
    for s in range(tm // sub):
        tb = jnp.where(lane == s, before[:, s * sub:s * sub + 1], tb)
    tb_ref[...] = tb
    base_ref[...] = base_ref[...] + jnp.sum(onehot, axis=1, keepdims=True)
    cnt_ref[...] = base_ref[...]


def _router(mixed, x2, wo, g_ffn, wr_t, b_router, n, tm, sub):
    row = lambda w: pl.BlockSpec((tm, w), lambda i: (i, 0))
    full = lambda a: pl.BlockSpec(a.shape, lambda i: (0,) * a.ndim, pipeline_mode=pl.Buffered(1))
    tok = pl.BlockSpec((TOPK, tm), lambda i: (0, i))
    return pl.pallas_call(
        functools.partial(_router_kernel, tm=tm, sub=sub),
        grid=(n // tm,),
        in_specs=[row(D), row(D), full(wo), full(g_ffn), full(wr_t), full(b_router)],
        out_specs=[
            row(D), row(D), tok, tok, tok,
            pl.BlockSpec((E, LANES), lambda i: (0, 0)), pl.BlockSpec((E, LANES), lambda i: (i, 0)),
        ],
        out_shape=[
            jax.ShapeDtypeStruct((n, D), F32), jax.ShapeDtypeStruct((n, D), BF16),
            jax.ShapeDtypeStruct((TOPK, n), jnp.int32), jax.ShapeDtypeStruct((TOPK, n), F32),
            jax.ShapeDtypeStruct((TOPK, n), jnp.int32), jax.ShapeDtypeStruct((E, LANES), F32),
            jax.ShapeDtypeStruct((n // tm * E, LANES), F32),
        ],
        scratch_shapes=[pltpu.VMEM((E, LANES), F32)],
        compiler_params=_params(("arbitrary",)),
        name="outproj_router",
    )(mixed, x2, wo, g_ffn, wr_t, b_router)


def _slot_kernel(pstart_ref, shift_ref, te_ref, rk_ref, slot_ref, *, tt, group):
    for g in range(group):
        tile = pl.program_id(0) * group + g
        te = te_ref[:, g * tt:(g + 1) * tt]
        s = rk_ref[:, g * tt:(g + 1) * tt]
        for e in range(E):
            s = s + jnp.where(te == e, pstart_ref[e] - shift_ref[tile, e], 0)
        slot_ref[:, g * tt:(g + 1) * tt] = s


def _slots(pstart, shift, top_e, rank, n, tt):
    group = min(8, n // tt)
    tok = pl.BlockSpec((TOPK, group * tt), lambda i, ps, sh: (0, i))
    grid_spec = pltpu.PrefetchScalarGridSpec(
        num_scalar_prefetch=2, grid=(n // (group * tt),), in_specs=[tok, tok], out_specs=tok)
    return pl.pallas_call(
        functools.partial(_slot_kernel, tt=tt, group=group), grid_spec=grid_spec,
        out_shape=jax.ShapeDtypeStruct((TOPK, n), jnp.int32), name="staging_slots",
    )(pstart, shift, top_e, rank)


def _prep_dispatch_kernel(pdst_ref, ntot_ref, nwr_ref, pb_ref, np_ref, ff_ref, lf_ref, cnt_ref, pad_ref, pst_ref,
                          slot_ref, hn_ref, wgu_ref, wdn_ref, g_ref, u_ref, d_ref, xg_ref,
                          stage_ref, carry_ref, sem, fsem, *, tt, n_disp, kcat):
    i = pl.program_id(0)

    def stage_piece(row):
        return stage_ref.at[pl.ds(pl.multiple_of(row, PIECE), PIECE), :]

    def out_piece(row):
        return xg_ref.at[pl.ds(pl.multiple_of(row, PIECE), PIECE), :]

    @pl.when(i == 0)
    def _():
        carry_ref[...] = jnp.zeros_like(carry_ref)

    @pl.when(i < n_disp)
    def _():
        slot = slot_ref[...]
        chunk = 512
        for r0 in range(0, kcat, chunk):
            row = lax.broadcasted_iota(jnp.int32, (chunk, tt), 0) + r0
            hit = row == slot[0:1, :]
            for k in range(1, TOPK):
                hit = hit | (row == slot[k:k + 1, :])
            sel = jnp.where(hit, 1.0, 0.0).astype(BF16)
            stage_ref[r0:r0 + chunk, :] = jnp.dot(sel, hn_ref[...], preferred_element_type=F32).astype(BF16)

        def per_expert(e, c):
            pieces = np_ref[i, e]

            @pl.when(pieces > 0)
            def _():
                first = pb_ref[i, e]
                last = first + (pieces - 1) * PIECE

                @pl.when(ff_ref[i, e] == 1)
                def _():
                    stage_piece(first)[...] = stage_piece(first)[...] + carry_ref[e]

                @pl.when(lf_ref[i, e] == 1)
                def _():
                    carry_ref[e] = stage_piece(last)[...]

            return c

        lax.fori_loop(0, E, per_expert, 0)

        def start(q, c):
            dst = pdst_ref[i, q]

            @pl.when(dst >= 0)
            def _():
                pltpu.make_async_copy(stage_piece(q * PIECE), out_piece(dst), sem).start()

            return c

        lax.fori_loop(0, ntot_ref[i], start, 0)

    chunk = 2 * LANES
    r = lax.broadcasted_iota(jnp.int32, (chunk, chunk), 0)
    c = lax.broadcasted_iota(jnp.int32, (chunk, chunk), 1)
    src = jnp.where(c < LANES, 2 * c, 2 * (c - LANES) + 1)
    perm = jnp.where(r == src, 1.0, 0.0).astype(BF16)
    for k in range(wgu_ref.shape[1] // chunk):
        w = wgu_ref[:, k * chunk:(k + 1) * chunk].astype(BF16)
        o = jnp.dot(w, perm, preferred_element_type=F32)
        g_ref[:, k * LANES:(k + 1) * LANES] = o[:, :LANES].astype(BF16)
        u_ref[:, k * LANES:(k + 1) * LANES] = o[:, LANES:].astype(BF16)
    d_ref[...] = wdn_ref[...].astype(BF16)

    @pl.when(i < n_disp)
    def _():
        def wait(q, c):
            pltpu.make_async_copy(stage_piece(0), out_piece(0), sem).wait()
            return c

        lax.fori_loop(0, nwr_ref[i], wait, 0)

    @pl.when(i == n_disp - 1)
    def _():
        stage_ref[0:PIECE, :] = jnp.zeros((PIECE, D), BF16)

        def per_expert(e, c):
            real = cnt_ref[e]
            tail = real % PIECE
            base = pst_ref[e]

            @pl.when(tail != 0)
            def _():
                cp = pltpu.make_async_copy(carry_ref.at[e], out_piece(base + real - tail), fsem)
                cp.start()
                cp.wait()

            zero_from = (real + PIECE - 1) // PIECE
            zero_to = pad_ref[e] // PIECE

            def zstart(z, c2):
                pltpu.make_async_copy(stage_piece(0), out_piece(base + z * PIECE), fsem).start()
                return c2

            def zwait(z, c2):
                pltpu.make_async_copy(stage_piece(0), out_piece(base + z * PIECE), fsem).wait()
                return c2

            lax.fori_loop(zero_from, zero_to, zstart, 0)
            lax.fori_loop(zero_from, zero_to, zwait, 0)
            return c

        lax.fori_loop(0, E, per_expert, 0)


def _prep_dispatch(tables, counts, padded, pstart, slot, hn, w_gate_up, w_down, n, p_rows, tt, kcat):
    tr = 512
    steps = E * (D // tr)
    n_disp = n // tt
    assert n_disp <= steps, "more token tiles than weight tiles"
    nsp = len(tables) + 3
    tile = lambda i, *_: (i // (D // tr), i % (D // tr), 0)
    tok = lambda i: jnp.minimum(i, n_disp - 1)
    wout = pl.BlockSpec((None, tr, DE), tile)
    grid_spec = pltpu.PrefetchScalarGridSpec(
        num_scalar_prefetch=nsp, grid=(steps,),
        in_specs=[
            pl.BlockSpec((TOPK, tt), lambda i, *_: (0, tok(i))),
            pl.BlockSpec((tt, D), lambda i, *_: (tok(i), 0)),
            pl.BlockSpec((None, tr, 2 * DE), tile),
            pl.BlockSpec((None, tr, D), tile),
        ],
        out_specs=[wout, wout, pl.BlockSpec((None, tr, D), tile), pl.BlockSpec(memory_space=pl.ANY)],
        scratch_shapes=[pltpu.VMEM((kcat, D), BF16), pltpu.VMEM((E, PIECE, D), BF16),
                        pltpu.SemaphoreType.DMA(()), pltpu.SemaphoreType.DMA(())],
    )
    return pl.pallas_call(
        functools.partial(_prep_dispatch_kernel, tt=tt, n_disp=n_disp, kcat=kcat), grid_spec=grid_spec,
        out_shape=[jax.ShapeDtypeStruct((E, D, DE), BF16), jax.ShapeDtypeStruct((E, D, DE), BF16),
                   jax.ShapeDtypeStruct((E, DE, D), BF16), jax.ShapeDtypeStruct((p_rows, D), BF16)],
        compiler_params=pltpu.CompilerParams(dimension_semantics=("arbitrary",), has_side_effects=True,
                                             vmem_limit_bytes=VMEM_LIMIT),
        name="weight_prep_dispatch",
    )(*tables, counts, padded, pstart, slot, hn, w_gate_up, w_down)


def _moe_kernel(be_ref, nu_ref, bv_ref, x_ref, wg_ref, wu_ref, wd_ref, bg_ref, bu_ref, bd_ref, y_ref,
                *, bm, sub, dsub):
    valid = bv_ref[pl.program_id(0)]

    def body(rows):
        x = x_ref[0:rows, :]
        acts = []

        def gate_up(s):
            sl = slice(s * sub, (s + 1) * sub)
            return (jnp.dot(x, wg_ref[:, sl], preferred_element_type=F32) + bg_ref[:, sl],
                    jnp.dot(x, wu_ref[:, sl], preferred_element_type=F32) + bu_ref[:, sl])

        def activate(s, gu):
            gate = jnp.minimum(gu[0], LIMIT)
            up = jnp.clip(gu[1], -LIMIT, LIMIT)
            acts.append((gate * jax.nn.sigmoid(ALPHA * gate) * (up + 1.0)).astype(BF16))

        _pipelined(DE // sub, gate_up, activate)
        act = jnp.concatenate(acts, axis=1)

        def down(c):
            return jnp.dot(act, wd_ref[:, c * dsub:(c + 1) * dsub], preferred_element_type=F32)

        def emit(c, part):
            sl = slice(c * dsub, (c + 1) * dsub)
            y_ref[0:rows, sl] = (part + bd_ref[:, sl]).astype(BF16)

        _pipelined(D // dsub, down, emit)
        if rows < bm:
            y_ref[rows:bm, :] = jnp.zeros((bm - rows, D), BF16)

    @pl.when(valid > bm // 2)
    def _():
        body(bm)

    @pl.when((valid > 0) & (valid <= bm // 2))
    def _():
        body(bm // 2)


def _moe(block_e, n_used, block_valid, xg, wg, wu, wd, bg, bu, bd, n_blocks, bm):
    def expert(blk, be, nu, bv):
        return (be[blk], 0, 0)

    def rows(blk, be, nu, bv):
        return (jnp.minimum(blk, nu[0] - 1), 0)

    single = lambda shape: pl.BlockSpec(shape, expert, pipeline_mode=pl.Buffered(1))
    double = lambda shape: pl.BlockSpec(shape, expert)
    grid_spec = pltpu.PrefetchScalarGridSpec(
        num_scalar_prefetch=3, grid=(n_blocks,),
        in_specs=[
            pl.BlockSpec((bm, D), rows),
            double((None, D, DE)), single((None, D, DE)), double((None, DE, D)),
            double((None, 1, DE)), double((None, 1, DE)), double((None, 1, D)),
        ],
        out_specs=pl.BlockSpec((bm, D), rows),
    )
    return pl.pallas_call(
        functools.partial(_moe_kernel, bm=bm, sub=512, dsub=256), grid_spec=grid_spec,
        out_shape=jax.ShapeDtypeStruct((n_blocks * bm, D), BF16),
        compiler_params=_params(("arbitrary",)),
        name="moe_experts",
    )(block_e, n_used, block_valid, xg, wg, wu, wd, bg, bu, bd)


def _combine_kernel(src_ref, tot_ref, y_ref, h_ref, slot_ref, tw_ref, gf_ref, o_ref, ycat_ref, sem,
                    *, tt, nt, kcat):
    i = pl.program_id(0)
    cur = i % 2

    def for_pieces(tile, s, fn):
        def body(q, c):
            fn(pltpu.make_async_copy(
                y_ref.at[pl.ds(pl.multiple_of(src_ref[tile, q], PIECE), PIECE), :],
                ycat_ref.at[s, pl.ds(pl.multiple_of(q * PIECE, PIECE), PIECE), :], sem.at[s]))
            return c

        lax.fori_loop(0, tot_ref[tile], body, 0)

    @pl.when(i == 0)
    def _():
        ycat_ref[...] = jnp.zeros_like(ycat_ref)
        for_pieces(0, 0, lambda cp: cp.start())

    @pl.when(i + 1 < nt)
    def _():
        for_pieces(i + 1, 1 - cur, lambda cp: cp.start())

    for_pieces(i, cur, lambda cp: cp.wait())

    def combine(k_rows):
        slot = slot_ref[...]
        tw = tw_ref[...]
        col = lax.broadcasted_iota(jnp.int32, (tt, k_rows), 1)
        sel = jnp.zeros((tt, k_rows), F32)
        for k in range(TOPK):
            sel = jnp.where(col == slot[:, k:k + 1], tw[:, k:k + 1], sel)
        moe = jnp.dot(sel.astype(BF16), ycat_ref[cur, 0:k_rows, :], preferred_element_type=F32)
        o_ref[...] = _rms(h_ref[...] + moe, gf_ref[...])

    k_short = kcat * 3 // 4
    used = tot_ref[i] * PIECE

    @pl.when(used <= k_short)
    def _():
        combine(k_short)

    @pl.when(used > k_short)
    def _():
        combine(kcat)


def _combine(src, total, y, h, slot, tw, g_final, n, tt, kcat):
    nt = n // tt
    grid_spec = pltpu.PrefetchScalarGridSpec(
        num_scalar_prefetch=2, grid=(nt,),
        in_specs=[
            pl.BlockSpec(memory_space=pl.ANY),
            pl.BlockSpec((tt, D), lambda i, a, b: (i, 0)),
            pl.BlockSpec((tt, TOPK), lambda i, a, b: (i, 0)),
            pl.BlockSpec((tt, TOPK), lambda i, a, b: (i, 0)),
            pl.BlockSpec((1, D), lambda i, a, b: (0, 0)),
        ],
        out_specs=pl.BlockSpec((tt, D), lambda i, a, b: (i, 0)),
        scratch_shapes=[pltpu.VMEM((2, kcat, D), BF16), pltpu.SemaphoreType.DMA((2,))],
    )
    return pl.pallas_call(
        functools.partial(_combine_kernel, tt=tt, nt=nt, kcat=kcat), grid_spec=grid_spec,
        out_shape=jax.ShapeDtypeStruct((n, D), F32),
        compiler_params=_params(("arbitrary",)),
        name="combine_final_norm",
    )(src, total, y, h, slot, tw, g_final)


def _permute_weights(w_in, w_uq, w_ukv, w_o_swa):
    widths = (HQ * DH, HKV * DH, HKV * DH, QR, KVR, DR, 2 * D)
    offs = [0]
    for w in widths:
        offs.append(offs[-1] + w)
    wq, wk, wv, wdq, wdkv, wkr, wgates = [w_in[:, offs[i]:offs[i + 1]] for i in range(7)]
    wq = wq.reshape(D, HKV, G, DH).transpose(0, 2, 1, 3).reshape(D, HQ * DH)
    pad = jnp.zeros((D, PROJ_COLS - COL_KR - DR), w_in.dtype)
    w_in_p = jnp.concatenate([wgates, wq, wdq, wdkv, wk, wv, wkr, pad], axis=1).astype(BF16)
    wo_a = w_o_swa.reshape(HKV, G, DH, D).transpose(1, 0, 2, 3).reshape(HQ * DH, D).astype(BF16)
    wuq = w_uq.reshape(QR, MH, DN + DR)
    wq_p = jnp.concatenate([wuq, jnp.zeros((QR, MH, 2 * LANES - DN - DR), w_uq.dtype)], axis=2)
    wq_p = wq_p.reshape(QR, MH * 2 * LANES).astype(BF16)
    wukv = w_ukv.reshape(KVR, MH, DN + DV)
    wk_p = wukv[:, :, :DN].reshape(KVR, MH * DN).astype(BF16)
    wv_p = wukv[:, :, DN:].reshape(KVR, MH * DV).astype(BF16)
    return w_in_p, wo_a, wq_p, wk_p, wv_p


def kernel(x, positions, g_attn, w_in, b_gate, sinks, g_q, w_uq, g_kv, w_ukv, w_o_swa, w_o_mla, w_out,
           g_ffn, w_router, b_router, w_gate_up, b_gate_up, w_down, b_down, g_final):
    b, s, _ = x.shape
    n = b * s
    assert w_in.shape[0] == 1, "single-layer block"
    x2 = x.reshape(n, D)

    w_in_p, wo_a, wq_p, wk_p, wv_p = _permute_weights(w_in[0], w_uq[0], w_ukv[0], w_o_swa[0])
    cos, sin = _rope_tables(positions, n)
    proj = _in_proj(x2, g_attn, w_in_p, n)
    qa, ka, qm, km, vm = _prep(proj, cos, sin, g_q, g_kv, wq_p, wk_p, wv_p, n)
    out_a = _swa(sinks[0], qa, ka, proj, b, s)
    out_b = _mla(qm, km, vm, b, s)

    mixed = _mix(out_a, out_b, proj, wo_a, w_o_mla[0].astype(BF16), b_gate, n)
    wr_t = w_router[0].T
    wr_top = _bf16_part(wr_t)
    wr_split = jnp.stack([wr_top.astype(BF16), (wr_t - wr_top).astype(BF16)])
    tm = min(n, 512)
    tt = min(n, 256)
    h, hn, top_e, top_w, rank, cnt, tile_base = _router(
        mixed, x2, w_out[0].astype(BF16), g_ffn, wr_split, b_router[0][:, None], n, tm, tt)

    bm = 512
    n_blocks = -(-(n * TOPK) // bm) + E
    counts = cnt[:, 0].astype(jnp.int32)
    padded = (counts + bm - 1) // bm * bm
    pend = jnp.cumsum(padded)
    pstart = pend - padded
    n_used = pend[-1] // bm
    blocks = jnp.arange(n_blocks, dtype=jnp.int32)
    block_e = jnp.minimum(jnp.sum(pend[None, :] <= (blocks * bm)[:, None], axis=1), E - 1).astype(jnp.int32)
    block_valid = jnp.clip(counts[block_e] - (blocks * bm - pstart[block_e]), 0, bm).astype(jnp.int32)
    block_valid = jnp.where(blocks < n_used, block_valid, 0)
    block_e = jnp.where(blocks < n_used, block_e, block_e[jnp.maximum(n_used - 1, 0)])

    nsub = tm // tt
    before = tile_base.reshape(n // tm, E, LANES)[:, :, :nsub].transpose(0, 2, 1).reshape(n // tt, E)
    before = before.astype(jnp.int32)
    after = jnp.concatenate([before[1:], counts[None, :]], axis=0)
    first = pstart[None, :] + before
    astart = first // PIECE * PIECE
    npieces = jnp.where(after > before, (first + (after - before) - astart + PIECE - 1) // PIECE, 0)
    cum = jnp.cumsum(npieces, axis=1)
    pbase = cum - npieces
    kcat = -(-(tt * TOPK + E * 2 * (PIECE - 1)) // (2 * LANES)) * (2 * LANES)
    shift = (astart - pbase * PIECE).astype(jnp.int32)
    q = jnp.arange(kcat // PIECE, dtype=jnp.int32)
    owner = jnp.sum(cum[:, None, :] <= q[None, :, None], axis=2)
    own = owner[:, :, None] == jnp.arange(E, dtype=jnp.int32)[None, None, :]
    piece_src = jnp.sum(jnp.where(own, shift[:, None, :], 0), axis=2) + q[None, :] * PIECE
    piece_src = jnp.where(owner < E, piece_src, 0).astype(jnp.int32)
    total = cum[:, -1].astype(jnp.int32)
    last = first + (after - before)
    starts_inside = ((npieces > 0) & (first % PIECE != 0)).astype(jnp.int32)
    ends_inside = ((npieces > 0) & (last % PIECE != 0)).astype(jnp.int32)
    held = jnp.sum(jnp.where(own & (q[None, :, None] == cum[:, None, :] - 1), ends_inside[:, None, :], 0), axis=2)
    piece_dst = jnp.where((owner < E) & (held == 0), piece_src, -1).astype(jnp.int32)
    written = (total - jnp.sum(ends_inside, axis=1)).astype(jnp.int32)
    tables = (piece_dst, total, written, (pbase * PIECE).astype(jnp.int32), npieces.astype(jnp.int32),
              starts_inside, ends_inside)

    slot = _slots(pstart, shift, top_e, rank, n, tt)
    wg, wu, wd, xg = _prep_dispatch(tables, counts, padded, pstart, slot, hn, w_gate_up[0], w_down[0], n,
                                    n_blocks * bm, tt, kcat)
    bgu = b_gate_up[0]
    y = _moe(block_e, n_used.reshape(1).astype(jnp.int32), block_valid, xg, wg, wu, wd,
             bgu[:, None, 0::2], bgu[:, None, 1::2], b_down[0][:, None, :], n_blocks, bm)
    out = _combine(piece_src, total, y, h, slot.T, top_w.T, g_final[None, :], n, tt, kcat)
    return out.reshape(b, s, D)
```

```python
import functools

import jax
import jax.numpy as jnp
from jax import lax
from jax.experimental import pallas as pl
from jax.experimental.pallas import tpu as pltpu

F32 = jnp.float32
BF16 = jnp.bfloat16
U32 = jnp.uint32

D = 2048
HQ, HKV, DH, WIN = 16, 2, 64, 128
G = HQ // HKV
MH, QR, KVR, DN, DR, DV = 8, 512, 256, 128, 64, 128
THETA = 10000.0
E, TOPK, DE = 32, 4, 2048
LIMIT, ALPHA = 7.0, 1.702
EPS = 1e-6
NEG = -1e30
LOG2E = 1.4426950408889634

LANES = 128
PIECE = 16
VMEM_LIMIT = 56 * 1024 * 1024

PROJ_COLS = 6400
COL_GATES, COL_Q, COL_DQ, COL_DKV, COL_K, COL_V, COL_KR = 0, 4096, 5120, 5632, 5888, 6016, 6144

NT = (((1,), (1,)), ((), ()))


def _params(sem, vmem=VMEM_LIMIT):
    return pltpu.CompilerParams(dimension_semantics=sem, vmem_limit_bytes=vmem)


def _rms(x, g):
    return x * lax.rsqrt(jnp.mean(x * x, axis=-1, keepdims=True) + EPS) * g


def _bf16_part(x):
    bits = lax.bitcast_convert_type(x, U32) & jnp.uint32(0xFFFF0000)
    return lax.bitcast_convert_type(bits, F32)


def _pipelined(n, produce, consume):
    nxt = produce(0)
    for i in range(n):
        cur = nxt
        if i + 1 < n:
            nxt = produce(i + 1)
        consume(i, cur)


def _rope128(v, cos, sin):
    lane = lax.broadcasted_iota(jnp.int32, v.shape, 1)
    rot = jnp.where((lane % DH) < (DH // 2), -pltpu.roll(v, LANES - DH // 2, 1), pltpu.roll(v, DH // 2, 1))
    return v * cos + rot * sin


def _rope_table_kernel(pos_ref, inv_ref, cos_ref, sin_ref):
    ang = pos_ref[...].astype(F32) * inv_ref[...]
    cos_ref[...] = jnp.cos(ang)
    sin_ref[...] = jnp.sin(ang)


def _rope_tables(positions, n):
    half = DH // 2
    per_row = LANES // half
    inv = jnp.power(THETA, -jnp.arange(half, dtype=F32) * 2.0 / DH)
    inv_row = jnp.tile(inv, per_row)[None, :]
    pos_rows = jnp.repeat(positions.reshape(n // per_row, per_row), half, axis=1)
    rows = n // per_row
    tr = min(rows, 1024)
    cos, sin = pl.pallas_call(
        _rope_table_kernel,
        grid=(rows // tr,),
        in_specs=[pl.BlockSpec((tr, LANES), lambda i: (i, 0)), pl.BlockSpec((1, LANES), lambda i: (0, 0))],
        out_specs=[pl.BlockSpec((tr, LANES), lambda i: (i, 0))] * 2,
        out_shape=[jax.ShapeDtypeStruct((rows, LANES), F32)] * 2,
        name="rope_tables",
    )(pos_rows, inv_row)
    cos = jnp.tile(cos.reshape(n, half), (1, per_row))
    sin = jnp.tile(sin.reshape(n, half), (1, per_row))
    return cos, sin


def _in_proj_kernel(x_ref, g_ref, w_ref, o_ref, xn_ref):
    @pl.when(pl.program_id(1) == 0)
    def _():
        xn_ref[...] = _rms(x_ref[...], g_ref[...]).astype(BF16)

    o_ref[...] = jnp.dot(xn_ref[...], w_ref[...], preferred_element_type=F32).astype(o_ref.dtype)


def _in_proj(x2, g_attn, w_in_p, n):
    tm = min(n, 1024)
    tn = 1280
    return pl.pallas_call(
        _in_proj_kernel,
        grid=(n // tm, PROJ_COLS // tn),
        in_specs=[
            pl.BlockSpec((tm, D), lambda i, j: (i, 0)),
            pl.BlockSpec((1, D), lambda i, j: (0, 0)),
            pl.BlockSpec((D, tn), lambda i, j: (0, j)),
        ],
        out_specs=pl.BlockSpec((tm, tn), lambda i, j: (i, j)),
        out_shape=jax.ShapeDtypeStruct((n, PROJ_COLS), BF16),
        scratch_shapes=[pltpu.VMEM((tm, D), BF16)],
        compiler_params=_params(("arbitrary", "arbitrary")),
        name="in_proj",
    )(x2, g_attn, w_in_p)


def _prep_kernel(q_ref, dq_ref, dkv_ref, k_ref, kr_ref, cos_ref, sin_ref, gq_ref, gkv_ref,
                 wq_ref, wk_ref, wv_ref, qa_ref, ka_ref, qm_ref, km_ref, vm_ref):
    cos = cos_ref[...]
    sin = sin_ref[...]
    swa_scale = DH ** -0.5 * LOG2E
    for c in range(HQ * DH // LANES):
        sl = slice(c * LANES, (c + 1) * LANES)
        qa_ref[:, sl] = (_rope128(q_ref[:, sl].astype(F32), cos, sin) * swa_scale).astype(BF16)
    ka_ref[...] = _rope128(k_ref[...].astype(F32), cos, sin).astype(BF16)

    mla_scale = (DN + DR) ** -0.5 * LOG2E
    cq = _rms(dq_ref[...].astype(F32), gq_ref[...]).astype(BF16)
    qb = jnp.dot(cq, wq_ref[...], preferred_element_type=F32)
    ckv = _rms(dkv_ref[...].astype(F32), gkv_ref[...]).astype(BF16)
    kn = jnp.dot(ckv, wk_ref[...], preferred_element_type=F32)
    vv = jnp.dot(ckv, wv_ref[...], preferred_element_type=F32)
    ones_col = jnp.where(lax.broadcasted_iota(jnp.int32, (vv.shape[0], LANES), 1) == 0, 1.0, 0.0).astype(BF16)
    kr = _rope128(kr_ref[...].astype(F32), cos, sin).astype(BF16)
    for h in range(MH):
        lo = h * 2 * LANES
        qm_ref[:, lo:lo + LANES] = (qb[:, lo:lo + LANES] * mla_scale).astype(BF16)
        qm_ref[:, lo + LANES:lo + 2 * LANES] = (
            _rope128(qb[:, lo + LANES:lo + 2 * LANES], cos, sin) * mla_scale).astype(BF16)
        km_ref[:, lo:lo + LANES] = kn[:, h * LANES:(h + 1) * LANES].astype(BF16)
        km_ref[:, lo + LANES:lo + 2 * LANES] = kr
        vm_ref[:, lo:lo + LANES] = vv[:, h * LANES:(h + 1) * LANES].astype(BF16)
        vm_ref[:, lo + LANES:lo + 2 * LANES] = ones_col


def _prep(proj, cos, sin, g_q, g_kv, wq_p, wk_p, wv_p, n):
    tm = min(n, 512)
    row = lambda w, cb: pl.BlockSpec((tm, w), lambda i: (i, cb))
    full = lambda a: pl.BlockSpec(a.shape, lambda i: (0, 0))
    return pl.pallas_call(
        _prep_kernel,
        grid=(n // tm,),
        in_specs=[
            row(HQ * DH, COL_Q // (HQ * DH)), row(QR, COL_DQ // QR), row(KVR, COL_DKV // KVR),
            row(LANES, COL_K // LANES), row(LANES, COL_KR // LANES),
            row(LANES, 0), row(LANES, 0), full(g_q), full(g_kv), full(wq_p), full(wk_p), full(wv_p),
        ],
        out_specs=[row(HQ * DH, 0), row(LANES, 0)] + [row(MH * 2 * LANES, 0)] * 3,
        out_shape=[jax.ShapeDtypeStruct((n, HQ * DH), BF16), jax.ShapeDtypeStruct((n, LANES), BF16)]
        + [jax.ShapeDtypeStruct((n, MH * 2 * LANES), BF16)] * 3,
        compiler_params=_params(("arbitrary",)),
        name="rope_mla_prep",
    )(proj, proj, proj, proj, proj, cos, sin, g_q, g_kv, wq_p, wk_p, wv_p)


def _swa_kernel(sinks_ref, q_ref, kc_ref, kp_ref, vc_ref, vp_ref, o_ref, *, nblk):
    first_blk = pl.program_id(1) * nblk
    k_all = jnp.concatenate([kp_ref[...], kc_ref[...]], axis=0)
    v_all_t = jnp.concatenate([vp_ref[...], vc_ref[...]], axis=0).astype(F32).T
    lane = lax.broadcasted_iota(jnp.int32, (2 * WIN, LANES), 1)
    vrow = lax.broadcasted_iota(jnp.int32, (LANES, 2 * WIN), 0)
    kj = lax.broadcasted_iota(jnp.int32, (2 * WIN, WIN), 0)
    qi = lax.broadcasted_iota(jnp.int32, (2 * WIN, WIN), 1)
    band = (kj > qi) & (kj <= qi + WIN)
    stages = [(blk, h) for blk in range(nblk) for h in range(HKV)]
    outs = []

    def scores(i):
        blk, h = stages[i]
        q = q_ref[blk * WIN:(blk + 1) * WIN, :]
        q2 = jnp.concatenate([q[:, g * LANES:(g + 1) * LANES] for g in range(G)], axis=0)
        k2 = k_all[blk * WIN:(blk + 2) * WIN]
        kz = jnp.where((lane >= h * DH) & (lane < (h + 1) * DH), k2, jnp.zeros_like(k2))
        return lax.dot_general(kz, q2, NT, preferred_element_type=F32)

    def attend(i, s):
        blk, h = stages[i]
        valid = band & ((kj >= WIN) | (first_blk + blk > 0))
        v2t = v_all_t[:, blk * WIN:(blk + 2) * WIN]
        vz = jnp.where((vrow >= h * DH) & (vrow < (h + 1) * DH), v2t, 0.0).astype(BF16)
        ps, invs = [], []
        for g in range(G):
            sg = jnp.where(valid, s[:, g * WIN:(g + 1) * WIN], NEG)
            sink = sinks_ref[h * G + g] * LOG2E
            m = jnp.maximum(jnp.max(sg, axis=0, keepdims=True), sink)
            p = jnp.exp2(sg - m)
            l = jnp.sum(p, axis=0, keepdims=True) + jnp.exp2(sink - m)
            ps.append(p.astype(BF16))
            invs.append(1.0 / l)
        o_h = jnp.dot(vz, jnp.concatenate(ps, axis=1), preferred_element_type=F32)
        outs.append(o_h * jnp.concatenate(invs, axis=1))

    _pipelined(len(stages), scores, attend)
    for blk in range(nblk):
        acc = outs[blk * HKV]
        for h in range(1, HKV):
            acc = acc + outs[blk * HKV + h]
        for g in range(G):
            o_ref[blk * WIN:(blk + 1) * WIN, g * LANES:(g + 1) * LANES] = (
                acc[:, g * WIN:(g + 1) * WIN].T.astype(BF16))


def _swa(sinks, qa, ka, proj, b, s):
    nb = s // WIN
    nblk = 4 if nb % 4 == 0 else 1
    ng = nb // nblk
    cur = lambda cb: pl.BlockSpec((nblk * WIN, LANES), lambda bi, n, sk: (bi * ng + n, cb))
    prev = lambda cb: pl.BlockSpec(
        (WIN, LANES), lambda bi, n, sk: (bi * nb + jnp.maximum(n * nblk - 1, 0), cb))
    grid_spec = pltpu.PrefetchScalarGridSpec(
        num_scalar_prefetch=1,
        grid=(b, ng),
        in_specs=[
            pl.BlockSpec((nblk * WIN, HQ * DH), lambda bi, n, sk: (bi * ng + n, 0)),
            cur(0), prev(0), cur(COL_V // LANES), prev(COL_V // LANES),
        ],
        out_specs=pl.BlockSpec((nblk * WIN, HQ * DH), lambda bi, n, sk: (bi * ng + n, 0)),
    )
    return pl.pallas_call(
        functools.partial(_swa_kernel, nblk=nblk),
        grid_spec=grid_spec,
        out_shape=jax.ShapeDtypeStruct((b * s, HQ * DH), BF16),
        compiler_params=_params(("arbitrary", "arbitrary")),
        name="swa_attention",
    )(sinks, qa, ka, ka, proj, proj)


def _mla_kernel(q_ref, k_ref, v_ref, o_ref, *, seq, tq, nh):
    row = lax.broadcasted_iota(jnp.int32, (tq, tq), 0)
    col = lax.broadcasted_iota(jnp.int32, (tq, tq), 1)
    causal = col <= row
    nq = seq // tq
    hw = 2 * LANES

    def scores(t):
        h, i = divmod(t, nq)
        kv = (i + 1) * tq
        q = q_ref[i * tq:(i + 1) * tq, h * hw:(h + 1) * hw]
        s = lax.dot_general(q, k_ref[0:kv, h * hw:(h + 1) * hw], NT, preferred_element_type=F32)
        diag = jnp.where(causal, s[:, kv - tq:kv], NEG)
        return diag if i == 0 else jnp.concatenate([s[:, 0:kv - tq], diag], axis=1)

    def finish(t, s):
        h, i = divmod(t, nq)
        kv = (i + 1) * tq
        m = jnp.max(s, axis=-1, keepdims=True)
        p = jnp.exp2(s - m).astype(BF16)
        ov = jnp.dot(p, v_ref[0:kv, h * hw:(h + 1) * hw], preferred_element_type=F32)
        o_ref[i * tq:(i + 1) * tq, h * DV:(h + 1) * DV] = (ov[:, 0:DV] * (1.0 / ov[:, DV:DV + 1])).astype(BF16)

    _pipelined(nh * nq, scores, finish)


def _mla(qm, km, vm, b, s):
    tq = min(s, 256)
    nh = 2
    heads = pl.BlockSpec((s, nh * 2 * LANES), lambda bi, h: (bi, h))
    return pl.pallas_call(
        functools.partial(_mla_kernel, seq=s, tq=tq, nh=nh),
        grid=(b, MH // nh),
        in_specs=[heads, heads, heads],
        out_specs=pl.BlockSpec((s, nh * DV), lambda bi, h: (bi, h)),
        out_shape=jax.ShapeDtypeStruct((b * s, MH * DV), BF16),
        compiler_params=_params(("arbitrary", "arbitrary")),
        name="mla_attention",
    )(qm, km, vm)


def _mix_kernel(oa_ref, ob_ref, ga_ref, gb_ref, wa_ref, wb_ref, ba_ref, bb_ref, o_ref):
    ya = jnp.dot(oa_ref[...], wa_ref[...], preferred_element_type=F32)
    yb = jnp.dot(ob_ref[...], wb_ref[...], preferred_element_type=F32)
    ga = jax.nn.sigmoid(ga_ref[...].astype(F32) + ba_ref[...])
    gb = jax.nn.sigmoid(gb_ref[...].astype(F32) + bb_ref[...])
    o_ref[...] = (ga * ya + gb * yb).astype(BF16)


def _mix(out_a, out_b, proj, wa, wb, b_gate, n):
    tm = min(n, 1024)
    tn = 1024
    nj = D // tn
    return pl.pallas_call(
        _mix_kernel,
        grid=(n // tm, nj),
        in_specs=[
            pl.BlockSpec((tm, HQ * DH), lambda i, j: (i, 0)),
            pl.BlockSpec((tm, MH * DV), lambda i, j: (i, 0)),
            pl.BlockSpec((tm, tn), lambda i, j: (i, j)),
            pl.BlockSpec((tm, tn), lambda i, j: (i, nj + j)),
            pl.BlockSpec((HQ * DH, tn), lambda i, j: (0, j)),
            pl.BlockSpec((MH * DV, tn), lambda i, j: (0, j)),
            pl.BlockSpec((1, tn), lambda i, j: (0, j)),
            pl.BlockSpec((1, tn), lambda i, j: (0, nj + j)),
        ],
        out_specs=pl.BlockSpec((tm, tn), lambda i, j: (i, j)),
        out_shape=jax.ShapeDtypeStruct((n, D), BF16),
        compiler_params=_params(("arbitrary", "arbitrary")),
        name="gated_mix",
    )(out_a, out_b, proj, proj, wa, wb, b_gate, b_gate)


def _router_kernel(mx_ref, x_ref, wo_ref, gf_ref, wr_ref, br_ref,
                   h_ref, hn_ref, te_ref, tw_ref, rk_ref, cnt_ref, tb_ref, base_ref, *, tm, sub):
    @pl.when(pl.program_id(0) == 0)
    def _():
        base_ref[...] = jnp.zeros_like(base_ref)

    h = x_ref[...] + jnp.dot(mx_ref[...], wo_ref[...], preferred_element_type=F32)
    h_ref[...] = h
    hn = _rms(h, gf_ref[...])
    hn_ref[...] = hn.astype(BF16)

    hn_top = _bf16_part(hn)
    hn_hi = hn_top.astype(BF16)
    hn_lo = (hn - hn_top).astype(BF16)
    logits = (lax.dot_general(wr_ref[0], hn_hi, NT, preferred_element_type=F32)
              + lax.dot_general(wr_ref[0], hn_lo, NT, preferred_element_type=F32)
              + lax.dot_general(wr_ref[1], hn_hi, NT, preferred_element_type=F32)) + br_ref[...]
    eidx = lax.broadcasted_iota(jnp.int32, (E, tm), 0)
    vals, idxs = [], []
    l = logits
    for _ in range(TOPK):
        m = jnp.max(l, axis=0, keepdims=True)
        idx = jnp.min(jnp.where(l == m, eidx, E), axis=0, keepdims=True)
        vals.append(m)
        idxs.append(idx)
        l = jnp.where(eidx == idx, -jnp.inf, l)
    ex = [jnp.exp(v - vals[0]) for v in vals]
    tot = ex[0] + ex[1] + ex[2] + ex[3]
    tw_ref[...] = jnp.concatenate([e / tot for e in ex], axis=0)
    te_ref[...] = jnp.concatenate(idxs, axis=0)

    onehot = jnp.zeros((E, tm), F32)
    for idx in idxs:
        onehot = onehot + jnp.where(eidx == idx, 1.0, 0.0)
    r = lax.broadcasted_iota(jnp.int32, (tm, tm), 0)
    c = lax.broadcasted_iota(jnp.int32, (tm, tm), 1)
    earlier = jnp.where(r < c, 1.0, 0.0).astype(BF16)
    before = jnp.dot(onehot.astype(BF16), earlier, preferred_element_type=F32) + base_ref[:, 0:1]
    rk_ref[...] = jnp.concatenate(
        [jnp.sum(jnp.where(eidx == idx, before, 0.0), axis=0, keepdims=True) for idx in idxs],
        axis=0).astype(jnp.int32)
    lane = lax.broadcasted_iota(jnp.int32, (E, LANES), 1)
    tb = jnp.zeros((E, LANES), F32)
    for s in range(tm // sub):
        tb = jnp.where(lane == s, before[:, s * sub:s * sub + 1], tb)
    tb_ref[...] = tb
    base_ref[...] = base_ref[...] + jnp.sum(onehot, axis=1, keepdims=True)
    cnt_ref[...] = base_ref[...]


def _router(mixed, x2, wo, g_ffn, wr_t, b_router, n, tm, sub):
    row = lambda w: pl.BlockSpec((tm, w), lambda i: (i, 0))
    full = lambda a: pl.BlockSpec(a.shape, lambda i: (0,) * a.ndim, pipeline_mode=pl.Buffered(1))
    tok = pl.BlockSpec((TOPK, tm), lambda i: (0, i))
    return pl.pallas_call(
        functools.partial(_router_kernel, tm=tm, sub=sub),
        grid=(n // tm,),
        in_specs=[row(D), row(D), full(wo), full(g_ffn), full(wr_t), full(b_router)],
        out_specs=[
            row(D), row(D), tok, tok, tok,
            pl.BlockSpec((E, LANES), lambda i: (0, 0)), pl.BlockSpec((E, LANES), lambda i: (i, 0)),
        ],
        out_shape=[
            jax.ShapeDtypeStruct((n, D), F32), jax.ShapeDtypeStruct((n, D), BF16),
            jax.ShapeDtypeStruct((TOPK, n), jnp.int32), jax.ShapeDtypeStruct((TOPK, n), F32),
            jax.ShapeDtypeStruct((TOPK, n), jnp.int32), jax.ShapeDtypeStruct((E, LANES), F32),
            jax.ShapeDtypeStruct((n // tm * E, LANES), F32),
        ],
        scratch_shapes=[pltpu.VMEM((E, LANES), F32)],
        compiler_params=_params(("arbitrary",)),
        name="outproj_router",
    )(mixed, x2, wo, g_ffn, wr_t, b_router)


def _slot_kernel(pstart_ref, shift_ref, te_ref, rk_ref, slot_ref, *, tt, group):
    for g in range(group):
        tile = pl.program_id(0) * group + g
        te = te_ref[:, g * tt:(g + 1) * tt]
        s = rk_ref[:, g * tt:(g + 1) * tt]
        for e in range(E):
            s = s + jnp.where(te == e, pstart_ref[e] - shift_ref[tile, e], 0)
        slot_ref[:, g * tt:(g + 1) * tt] = s


def _slots(pstart, shift, top_e, rank, n, tt):
    group = min(8, n // tt)
    tok = pl.BlockSpec((TOPK, group * tt), lambda i, ps, sh: (0, i))
    grid_spec = pltpu.PrefetchScalarGridSpec(
        num_scalar_prefetch=2, grid=(n // (group * tt),), in_specs=[tok, tok], out_specs=tok)
    return pl.pallas_call(
        functools.partial(_slot_kernel, tt=tt, group=group), grid_spec=grid_spec,
        out_shape=jax.ShapeDtypeStruct((TOPK, n), jnp.int32), name="staging_slots",
    )(pstart, shift, top_e, rank)


def _prep_dispatch_kernel(pdst_ref, ntot_ref, nwr_ref, pb_ref, np_ref, ff_ref, lf_ref, cnt_ref, pad_ref, pst_ref,
                          slot_ref, hn_ref, wgu_ref, wdn_ref, g_ref, u_ref, d_ref, xg_ref,
                          stage_ref, carry_ref, sem, fsem, *, tt, n_disp, kcat):
    i = pl.program_id(0)

    def stage_piece(row):
        return stage_ref.at[pl.ds(pl.multiple_of(row, PIECE), PIECE), :]

    def out_piece(row):
        return xg_ref.at[pl.ds(pl.multiple_of(row, PIECE), PIECE), :]

    @pl.when(i == 0)
    def _():
        carry_ref[...] = jnp.zeros_like(carry_ref)

    @pl.when(i < n_disp)
    def _():
        slot = slot_ref[...]
        chunk = 512
        for r0 in range(0, kcat, chunk):
            row = lax.broadcasted_iota(jnp.int32, (chunk, tt), 0) + r0
            hit = row == slot[0:1, :]
            for k in range(1, TOPK):
                hit = hit | (row == slot[k:k + 1, :])
            sel = jnp.where(hit, 1.0, 0.0).astype(BF16)
            stage_ref[r0:r0 + chunk, :] = jnp.dot(sel, hn_ref[...], preferred_element_type=F32).astype(BF16)

        def per_expert(e, c):
            pieces = np_ref[i, e]

            @pl.when(pieces > 0)
            def _():
                first = pb_ref[i, e]
                last = first + (pieces - 1) * PIECE

                @pl.when(ff_ref[i, e] == 1)
                def _():
                    stage_piece(first)[...] = stage_piece(first)[...] + carry_ref[e]

                @pl.when(lf_ref[i, e] == 1)
                def _():
                    carry_ref[e] = stage_piece(last)[...]

            return c

        lax.fori_loop(0, E, per_expert, 0)

        def start(q, c):
            dst = pdst_ref[i, q]

            @pl.when(dst >= 0)
            def _():
                pltpu.make_async_copy(stage_piece(q * PIECE), out_piece(dst), sem).start()

            return c

        lax.fori_loop(0, ntot_ref[i], start, 0)

    chunk = 2 * LANES
    r = lax.broadcasted_iota(jnp.int32, (chunk, chunk), 0)
    c = lax.broadcasted_iota(jnp.int32, (chunk, chunk), 1)
    src = jnp.where(c < LANES, 2 * c, 2 * (c - LANES) + 1)
    perm = jnp.where(r == src, 1.0, 0.0).astype(BF16)
    for k in range(wgu_ref.shape[1] // chunk):
        w = wgu_ref[:, k * chunk:(k + 1) * chunk].astype(BF16)
        o = jnp.dot(w, perm, preferred_element_type=F32)
        g_ref[:, k * LANES:(k + 1) * LANES] = o[:, :LANES].astype(BF16)
        u_ref[:, k * LANES:(k + 1) * LANES] = o[:, LANES:].astype(BF16)
    d_ref[...] = wdn_ref[...].astype(BF16)

    @pl.when(i < n_disp)
    def _():
        def wait(q, c):
            pltpu.make_async_copy(stage_piece(0), out_piece(0), sem).wait()
            return c

        lax.fori_loop(0, nwr_ref[i], wait, 0)

    @pl.when(i == n_disp - 1)
    def _():
        stage_ref[0:PIECE, :] = jnp.zeros((PIECE, D), BF16)

        def per_expert(e, c):
            real = cnt_ref[e]
            tail = real % PIECE
            base = pst_ref[e]

            @pl.when(tail != 0)
            def _():
                cp = pltpu.make_async_copy(carry_ref.at[e], out_piece(base + real - tail), fsem)
                cp.start()
                cp.wait()

            zero_from = (real + PIECE - 1) // PIECE
            zero_to = pad_ref[e] // PIECE

            def zstart(z, c2):
                pltpu.make_async_copy(stage_piece(0), out_piece(base + z * PIECE), fsem).start()
                return c2

            def zwait(z, c2):
                pltpu.make_async_copy(stage_piece(0), out_piece(base + z * PIECE), fsem).wait()
                return c2

            lax.fori_loop(zero_from, zero_to, zstart, 0)
            lax.fori_loop(zero_from, zero_to, zwait, 0)
            return c

        lax.fori_loop(0, E, per_expert, 0)


def _prep_dispatch(tables, counts, padded, pstart, slot, hn, w_gate_up, w_down, n, p_rows, tt, kcat):
    tr = 512
    steps = E * (D // tr)
    n_disp = n // tt
    assert n_disp <= steps, "more token tiles than weight tiles"
    nsp = len(tables) + 3
    tile = lambda i, *_: (i // (D // tr), i % (D // tr), 0)
    tok = lambda i: jnp.minimum(i, n_disp - 1)
    wout = pl.BlockSpec((None, tr, DE), tile)
    grid_spec = pltpu.PrefetchScalarGridSpec(
        num_scalar_prefetch=nsp, grid=(steps,),
        in_specs=[
            pl.BlockSpec((TOPK, tt), lambda i, *_: (0, tok(i))),
            pl.BlockSpec((tt, D), lambda i, *_: (tok(i), 0)),
            pl.BlockSpec((None, tr, 2 * DE), tile),
            pl.BlockSpec((None, tr, D), tile),
        ],
        out_specs=[wout, wout, pl.BlockSpec((None, tr, D), tile), pl.BlockSpec(memory_space=pl.ANY)],
        scratch_shapes=[pltpu.VMEM((kcat, D), BF16), pltpu.VMEM((E, PIECE, D), BF16),
                        pltpu.SemaphoreType.DMA(()), pltpu.SemaphoreType.DMA(())],
    )
    return pl.pallas_call(
        functools.partial(_prep_dispatch_kernel, tt=tt, n_disp=n_disp, kcat=kcat), grid_spec=grid_spec,
        out_shape=[jax.ShapeDtypeStruct((E, D, DE), BF16), jax.ShapeDtypeStruct((E, D, DE), BF16),
                   jax.ShapeDtypeStruct((E, DE, D), BF16), jax.ShapeDtypeStruct((p_rows, D), BF16)],
        compiler_params=pltpu.CompilerParams(dimension_semantics=("arbitrary",), has_side_effects=True,
                                             vmem_limit_bytes=VMEM_LIMIT),
        name="weight_prep_dispatch",
    )(*tables, counts, padded, pstart, slot, hn, w_gate_up, w_down)


def _moe_kernel(be_ref, nu_ref, bv_ref, x_ref, wg_ref, wu_ref, wd_ref, bg_ref, bu_ref, bd_ref, y_ref,
                *, bm, sub, dsub):
    valid = bv_ref[pl.program_id(0)]

    def body(rows):
        x = x_ref[0:rows, :]
        acts = []

        def gate_up(s):
            sl = slice(s * sub, (s + 1) * sub)
            return (jnp.dot(x, wg_ref[:, sl], preferred_element_type=F32) + bg_ref[:, sl],
                    jnp.dot(x, wu_ref[:, sl], preferred_element_type=F32) + bu_ref[:, sl])

        def activate(s, gu):
            gate = jnp.minimum(gu[0], LIMIT)
            up = jnp.clip(gu[1], -LIMIT, LIMIT)
            acts.append((gate * jax.nn.sigmoid(ALPHA * gate) * (up + 1.0)).astype(BF16))

        _pipelined(DE // sub, gate_up, activate)
        act = jnp.concatenate(acts, axis=1)

        def down(c):
            return jnp.dot(act, wd_ref[:, c * dsub:(c + 1) * dsub], preferred_element_type=F32)

        def emit(c, part):
            sl = slice(c * dsub, (c + 1) * dsub)
            y_ref[0:rows, sl] = (part + bd_ref[:, sl]).astype(BF16)

        _pipelined(D // dsub, down, emit)
        if rows < bm:
            y_ref[rows:bm, :] = jnp.zeros((bm - rows, D), BF16)

    @pl.when(valid > bm // 2)
    def _():
        body(bm)

    @pl.when((valid > 0) & (valid <= bm // 2))
    def _():
        body(bm // 2)


def _moe(block_e, n_used, block_valid, xg, wg, wu, wd, bg, bu, bd, n_blocks, bm):
    def expert(blk, be, nu, bv):
        return (be[blk], 0, 0)

    def rows(blk, be, nu, bv):
        return (jnp.minimum(blk, nu[0] - 1), 0)

    single = lambda shape: pl.BlockSpec(shape, expert, pipeline_mode=pl.Buffered(1))
    double = lambda shape: pl.BlockSpec(shape, expert)
    grid_spec = pltpu.PrefetchScalarGridSpec(
        num_scalar_prefetch=3, grid=(n_blocks,),
        in_specs=[
            pl.BlockSpec((bm, D), rows),
            double((None, D, DE)), single((None, D, DE)), double((None, DE, D)),
            double((None, 1, DE)), double((None, 1, DE)), double((None, 1, D)),
        ],
        out_specs=pl.BlockSpec((bm, D), rows),
    )
    return pl.pallas_call(
        functools.partial(_moe_kernel, bm=bm, sub=512, dsub=256), grid_spec=grid_spec,
        out_shape=jax.ShapeDtypeStruct((n_blocks * bm, D), BF16),
        compiler_params=_params(("arbitrary",)),
        name="moe_experts",
    )(block_e, n_used, block_valid, xg, wg, wu, wd, bg, bu, bd)


def _combine_kernel(src_ref, tot_ref, y_ref, h_ref, slot_ref, tw_ref, gf_ref, o_ref, ycat_ref, sem,
                    *, tt, nt, kcat):
    i = pl.program_id(0)
    cur = i % 2

    def for_pieces(tile, s, fn):
        def body(q, c):
            fn(pltpu.make_async_copy(
                y_ref.at[pl.ds(pl.multiple_of(src_ref[tile, q], PIECE), PIECE), :],
                ycat_ref.at[s, pl.ds(pl.multiple_of(q * PIECE, PIECE), PIECE), :], sem.at[s]))
            return c

        lax.fori_loop(0, tot_ref[tile], body, 0)

    @pl.when(i == 0)
    def _():
        ycat_ref[...] = jnp.zeros_like(ycat_ref)
        for_pieces(0, 0, lambda cp: cp.start())

    @pl.when(i + 1 < nt)
    def _():
        for_pieces(i + 1, 1 - cur, lambda cp: cp.start())

    for_pieces(i, cur, lambda cp: cp.wait())

    def combine(k_rows):
        slot = slot_ref[...]
        tw = tw_ref[...]
        col = lax.broadcasted_iota(jnp.int32, (tt, k_rows), 1)
        sel = jnp.zeros((tt, k_rows), F32)
        for k in range(TOPK):
            sel = jnp.where(col == slot[:, k:k + 1], tw[:, k:k + 1], sel)
        moe = jnp.dot(sel.astype(BF16), ycat_ref[cur, 0:k_rows, :], preferred_element_type=F32)
        o_ref[...] = _rms(h_ref[...] + moe, gf_ref[...])

    k_short = kcat * 3 // 4
    used = tot_ref[i] * PIECE

    @pl.when(used <= k_short)
    def _():
        combine(k_short)

    @pl.when(used > k_short)
    def _():
        combine(kcat)


def _combine(src, total, y, h, slot, tw, g_final, n, tt, kcat):
    nt = n // tt
    grid_spec = pltpu.PrefetchScalarGridSpec(
        num_scalar_prefetch=2, grid=(nt,),
        in_specs=[
            pl.BlockSpec(memory_space=pl.ANY),
            pl.BlockSpec((tt, D), lambda i, a, b: (i, 0)),
            pl.BlockSpec((tt, TOPK), lambda i, a, b: (i, 0)),
            pl.BlockSpec((tt, TOPK), lambda i, a, b: (i, 0)),
            pl.BlockSpec((1, D), lambda i, a, b: (0, 0)),
        ],
        out_specs=pl.BlockSpec((tt, D), lambda i, a, b: (i, 0)),
        scratch_shapes=[pltpu.VMEM((2, kcat, D), BF16), pltpu.SemaphoreType.DMA((2,))],
    )
    return pl.pallas_call(
        functools.partial(_combine_kernel, tt=tt, nt=nt, kcat=kcat), grid_spec=grid_spec,
        out_shape=jax.ShapeDtypeStruct((n, D), F32),
        compiler_params=_params(("arbitrary",)),
        name="combine_final_norm",
    )(src, total, y, h, slot, tw, g_final)


def _permute_weights(w_in, w_uq, w_ukv, w_o_swa):
    widths = (HQ * DH, HKV * DH, HKV * DH, QR, KVR, DR, 2 * D)
    offs = [0]
    for w in widths:
        offs.append(offs[-1] + w)
    wq, wk, wv, wdq, wdkv, wkr, wgates = [w_in[:, offs[i]:offs[i + 1]] for i in range(7)]
    wq = wq.reshape(D, HKV, G, DH).transpose(0, 2, 1, 3).reshape(D, HQ * DH)
    pad = jnp.zeros((D, PROJ_COLS - COL_KR - DR), w_in.dtype)
    w_in_p = jnp.concatenate([wgates, wq, wdq, wdkv, wk, wv, wkr, pad], axis=1).astype(BF16)
    wo_a = w_o_swa.reshape(HKV, G, DH, D).transpose(1, 0, 2, 3).reshape(HQ * DH, D).astype(BF16)
    wuq = w_uq.reshape(QR, MH, DN + DR)
    wq_p = jnp.concatenate([wuq, jnp.zeros((QR, MH, 2 * LANES - DN - DR), w_uq.dtype)], axis=2)
    wq_p = wq_p.reshape(QR, MH * 2 * LANES).astype(BF16)
    wukv = w_ukv.reshape(KVR, MH, DN + DV)
    wk_p = wukv[:, :, :DN].reshape(KVR, MH * DN).astype(BF16)
    wv_p = wukv[:, :, DN:].reshape(KVR, MH * DV).astype(BF16)
    return w_in_p, wo_a, wq_p, wk_p, wv_p


def kernel(x, positions, g_attn, w_in, b_gate, sinks, g_q, w_uq, g_kv, w_ukv, w_o_swa, w_o_mla, w_out,
           g_ffn, w_router, b_router, w_gate_up, b_gate_up, w_down, b_down, g_final):
    b, s, _ = x.shape
    n = b * s
    assert w_in.shape[0] == 1, "single-layer block"
    x2 = x.reshape(n, D)

    w_in_p, wo_a, wq_p, wk_p, wv_p = _permute_weights(w_in[0], w_uq[0], w_ukv[0], w_o_swa[0])
    cos, sin = _rope_tables(positions, n)
    proj = _in_proj(x2, g_attn, w_in_p, n)
    qa, ka, qm, km, vm = _prep(proj, cos, sin, g_q, g_kv, wq_p, wk_p, wv_p, n)
    out_a = _swa(sinks[0], qa, ka, proj, b, s)
    out_b = _mla(qm, km, vm, b, s)

    mixed = _mix(out_a, out_b, proj, wo_a, w_o_mla[0].astype(BF16), b_gate, n)
    wr_t = w_router[0].T
    wr_top = _bf16_part(wr_t)
    wr_split = jnp.stack([wr_top.astype(BF16), (wr_t - wr_top).astype(BF16)])
    tm = min(n, 512)
    tt = min(n, 256)
    h, hn, top_e, top_w, rank, cnt, tile_base = _router(
        mixed, x2, w_out[0].astype(BF16), g_ffn, wr_split, b_router[0][:, None], n, tm, tt)

    bm = 512
    n_blocks = -(-(n * TOPK) // bm) + E
    counts = cnt[:, 0].astype(jnp.int32)
    padded = (counts + bm - 1) // bm * bm
    pend = jnp.cumsum(padded)
    pstart = pend - padded
    n_used = pend[-1] // bm
    blocks = jnp.arange(n_blocks, dtype=jnp.int32)
    block_e = jnp.minimum(jnp.sum(pend[None, :] <= (blocks * bm)[:, None], axis=1), E - 1).astype(jnp.int32)
    block_valid = jnp.clip(counts[block_e] - (blocks * bm - pstart[block_e]), 0, bm).astype(jnp.int32)
    block_valid = jnp.where(blocks < n_used, block_valid, 0)
    block_e = jnp.where(blocks < n_used, block_e, block_e[jnp.maximum(n_used - 1, 0)])

    nsub = tm // tt
    before = tile_base.reshape(n // tm, E, LANES)[:, :, :nsub].transpose(0, 2, 1).reshape(n // tt, E)
    before = before.astype(jnp.int32)
    after = jnp.concatenate([before[1:], counts[None, :]], axis=0)
    first = pstart[None, :] + before
    astart = first // PIECE * PIECE
    npieces = jnp.where(after > before, (first + (after - before) - astart + PIECE - 1) // PIECE, 0)
    cum = jnp.cumsum(npieces, axis=1)
    pbase = cum - npieces
    kcat = -(-(tt * TOPK + E * 2 * (PIECE - 1)) // (2 * LANES)) * (2 * LANES)
    shift = (astart - pbase * PIECE).astype(jnp.int32)
    q = jnp.arange(kcat // PIECE, dtype=jnp.int32)
    owner = jnp.sum(cum[:, None, :] <= q[None, :, None], axis=2)
    own = owner[:, :, None] == jnp.arange(E, dtype=jnp.int32)[None, None, :]
    piece_src = jnp.sum(jnp.where(own, shift[:, None, :], 0), axis=2) + q[None, :] * PIECE
    piece_src = jnp.where(owner < E, piece_src, 0).astype(jnp.int32)
    total = cum[:, -1].astype(jnp.int32)
    last = first + (after - before)
    starts_inside = ((npieces > 0) & (first % PIECE != 0)).astype(jnp.int32)
    ends_inside = ((npieces > 0) & (last % PIECE != 0)).astype(jnp.int32)
    held = jnp.sum(jnp.where(own & (q[None, :, None] == cum[:, None, :] - 1), ends_inside[:, None, :], 0), axis=2)
    piece_dst = jnp.where((owner < E) & (held == 0), piece_src, -1).astype(jnp.int32)
    written = (total - jnp.sum(ends_inside, axis=1)).astype(jnp.int32)
    tables = (piece_dst, total, written, (pbase * PIECE).astype(jnp.int32), npieces.astype(jnp.int32),
              starts_inside, ends_inside)

    slot = _slots(pstart, shift, top_e, rank, n, tt)
    wg, wu, wd, xg = _prep_dispatch(tables, counts, padded, pstart, slot, hn, w_gate_up[0], w_down[0], n,
                                    n_blocks * bm, tt, kcat)
    bgu = b_gate_up[0]
    y = _moe(block_e, n_used.reshape(1).astype(jnp.int32), block_valid, xg, wg, wu, wd,
             bgu[:, None, 0::2], bgu[:, None, 1::2], b_down[0][:, None, :], n_blocks, bm)
    out = _combine(piece_src, total, y, h, slot.T, top_w.T, g_final[None, :], n, tt, kcat)
    return out.reshape(b, s, D)
```
